```python
import math
import jax, jax.numpy as jnp
from jax import lax
import numpy as np

D_MODEL = 1024
BATCH = 8
SEQ = 2048
DEPTH = 1

MEM_LEN = 256
EPS = 1e-6
DA_HEADS = 4
DA_QK_DIM = 64
DA_V_DIM = 2 * DA_QK_DIM
DA_WIDTH = DA_HEADS * DA_V_DIM
QK_WIDTH = DA_HEADS * 2 * DA_QK_DIM
POOL_WINDOWS = (2, 4, 8, 16)
POOL_GROUPS = len(POOL_WINDOWS)
POOL_WIDTH = D_MODEL - DA_WIDTH
POOL_GROUP_DIM = POOL_WIDTH // POOL_GROUPS
MIX_WIDTH = DA_WIDTH + POOL_WIDTH
IN_WIDTH = 2 * QK_WIDTH + DA_WIDTH + POOL_WIDTH
ROPE_THETA = 500000.0
ROPE_DIM = DA_QK_DIM // 4
BLOCK_Q = 128
X_HEADS = 4
X_HEAD_DIM = D_MODEL // X_HEADS
D_FF = -(-(8 * D_MODEL) // (3 * 256)) * 256
def _lambda_init(layer_idx):
    return 0.8 - 0.6 * math.exp(-0.3 * (layer_idx - 1))

kernel_name = "hybrid_diffattn_pool_memxattn_swiglu"


def rmsnorm(x, g):
    xf = x.astype(jnp.float32)
    y = xf * lax.rsqrt(jnp.mean(xf * xf, axis=-1, keepdims=True) + EPS)
    return (y * g.astype(jnp.float32)).astype(x.dtype)


def rope_tables(positions, dtype):
    inv_freq = ROPE_THETA ** (-jnp.arange(0, ROPE_DIM, 2, dtype=jnp.float32) / ROPE_DIM)
    ang = positions.astype(jnp.float32)[..., None] * inv_freq
    return jnp.cos(ang).astype(dtype), jnp.sin(ang).astype(dtype)


def apply_partial_rope(t, cos, sin):
    cos = cos[:, :, None, None, :]
    sin = sin[:, :, None, None, :]
    half = ROPE_DIM // 2
    t1 = t[..., :half]
    t2 = t[..., half:ROPE_DIM]
    rest = t[..., ROPE_DIM:]
    return jnp.concatenate([t1 * cos - t2 * sin, t2 * cos + t1 * sin, rest], axis=-1)


def diff_attention(q, k, v, lam):
    seq = q.shape[3]
    scale = DA_QK_DIM ** -0.5
    outs = []
    for i in range(seq // BLOCK_Q):
        q0 = i * BLOCK_Q
        kend = q0 + BLOCK_Q
        qb = q[:, :, :, q0:kend]
        kb = k[:, :, :, :kend]
        vb = v[:, :, :kend]
        s = jnp.einsum('bhmqd,bhmkd->bhmqk', qb, kb).astype(jnp.float32) * scale
        mask = (q0 + jnp.arange(BLOCK_Q))[:, None] >= jnp.arange(kend)[None, :]
        s = jnp.where(mask, s, -jnp.inf)
        p = jax.nn.softmax(s, axis=-1)
        a = p[:, :, 0] - lam * p[:, :, 1]
        outs.append(jnp.einsum('bhqk,bhkd->bhqd', a.astype(v.dtype), vb))
    return jnp.concatenate(outs, axis=2)


def causal_multiscale_pool(u):
    b, s, _ = u.shape
    uf = u.astype(jnp.float32).reshape(b, s, POOL_GROUPS, POOL_GROUP_DIM)
    c = jnp.cumsum(uf, axis=1)
    t1 = jnp.arange(1, s + 1, dtype=jnp.float32)
    groups = []
    for g, w in enumerate(POOL_WINDOWS):
        cg = c[:, :, g]
        shifted = jnp.pad(cg, ((0, 0), (w, 0), (0, 0)))[:, :s]
        count = jnp.minimum(t1, float(w))[None, :, None]
        groups.append((cg - shifted) / count - uf[:, :, g])
    return jnp.stack(groups, axis=2).astype(u.dtype)


def setup_inputs(seed: int = 0) -> dict:
    key = jax.random.key(seed)
    ks = jax.random.split(key, 32)
    f32 = jnp.float32

    def w(k, shape, fan_in):
        return jax.random.normal(k, shape, f32) * (fan_in ** -0.5)

    def gain(k, n):
        return 1.0 + 0.02 * jax.random.normal(k, (n,), f32)

    return {
        "x": jax.random.normal(ks[0], (BATCH, SEQ, D_MODEL), f32),
        "mem": jax.random.normal(ks[1], (BATCH, MEM_LEN, D_MODEL), f32),
        "positions": jnp.broadcast_to(jnp.arange(SEQ, dtype=jnp.int32), (BATCH, SEQ)),
        "g_mix_pre": gain(ks[2], D_MODEL),
        "w_in": w(ks[3], (D_MODEL, IN_WIDTH), D_MODEL),
        "lambda_q1": 0.1 * jax.random.normal(ks[4], (DA_QK_DIM,), f32),
        "lambda_k1": 0.1 * jax.random.normal(ks[5], (DA_QK_DIM,), f32),
        "lambda_q2": 0.1 * jax.random.normal(ks[6], (DA_QK_DIM,), f32),
        "lambda_k2": 0.1 * jax.random.normal(ks[7], (DA_QK_DIM,), f32),
        "g_subln": gain(ks[8], DA_V_DIM),
        "w_pool": w(ks[9], (POOL_GROUPS, POOL_GROUP_DIM, POOL_GROUP_DIM), POOL_GROUP_DIM),
        "pool_scale": 1.0 + 0.1 * jax.random.normal(ks[10], (POOL_WIDTH,), f32),
        "w_out": w(ks[11], (MIX_WIDTH, D_MODEL), MIX_WIDTH),
        "g_mix_post": gain(ks[12], D_MODEL),
        "g_x_pre": gain(ks[13], D_MODEL),
        "g_mem": gain(ks[14], D_MODEL),
        "w_xq": w(ks[15], (D_MODEL, X_HEADS * X_HEAD_DIM), D_MODEL),
        "w_xkv": w(ks[16], (D_MODEL, 2 * X_HEADS * X_HEAD_DIM), D_MODEL),
        "w_xo": w(ks[17], (X_HEADS * X_HEAD_DIM, D_MODEL), X_HEADS * X_HEAD_DIM),
        "g_x_post": gain(ks[18], D_MODEL),
        "g_ffn_pre": gain(ks[19], D_MODEL),
        "w_gate": w(ks[20], (D_MODEL, D_FF), D_MODEL),
        "w_up": w(ks[21], (D_MODEL, D_FF), D_MODEL),
        "w_down": w(ks[22], (D_FF, D_MODEL), D_FF),
        "g_ffn_post": gain(ks[23], D_MODEL),
    }


def reference(x, mem, positions, g_mix_pre, w_in, lambda_q1, lambda_k1, lambda_q2,
              lambda_k2, g_subln, w_pool, pool_scale, w_out, g_mix_post, g_x_pre,
              g_mem, w_xq, w_xkv, w_xo, g_x_post, g_ffn_pre, w_gate, w_up, w_down,
              g_ffn_post):
    b, s, _ = x.shape
    cos, sin = rope_tables(positions, x.dtype)
    mem_n = rmsnorm(mem, g_mem)

    for layer in range(DEPTH):
        lam_init = _lambda_init(layer + 1)

        h = rmsnorm(x, g_mix_pre)
        proj = h @ w_in
        q = proj[..., :QK_WIDTH].reshape(b, s, DA_HEADS, 2, DA_QK_DIM)
        k = proj[..., QK_WIDTH:2 * QK_WIDTH].reshape(b, s, DA_HEADS, 2, DA_QK_DIM)
        v = proj[..., 2 * QK_WIDTH:2 * QK_WIDTH + DA_WIDTH].reshape(b, s, DA_HEADS, DA_V_DIM)
        u = proj[..., 2 * QK_WIDTH + DA_WIDTH:]

        q = apply_partial_rope(q, cos, sin).transpose(0, 2, 3, 1, 4)
        k = apply_partial_rope(k, cos, sin).transpose(0, 2, 3, 1, 4)
        v = v.transpose(0, 2, 1, 3)
        lam = (jnp.exp(jnp.sum(lambda_q1.astype(jnp.float32) * lambda_k1.astype(jnp.float32)))
               - jnp.exp(jnp.sum(lambda_q2.astype(jnp.float32) * lambda_k2.astype(jnp.float32)))
               + lam_init)
        da = diff_attention(q, k, v, lam)
        da = rmsnorm(da, g_subln) * (1.0 - lam_init)
        da = da.transpose(0, 2, 1, 3).reshape(b, s, DA_WIDTH)

        pooled = causal_multiscale_pool(u)
        po = jnp.einsum('bsgc,gcd->bsgd', pooled, w_pool).reshape(b, s, POOL_WIDTH)
        po = po * pool_scale

        mix = jnp.concatenate([da, po], axis=-1) @ w_out
        x = x + rmsnorm(mix, g_mix_post)

        hq = rmsnorm(x, g_x_pre)
        xq = (hq @ w_xq).reshape(b, s, X_HEADS, X_HEAD_DIM)
        kv = (mem_n @ w_xkv).reshape(b, MEM_LEN, 2, X_HEADS, X_HEAD_DIM)
        sc = jnp.einsum('bshd,bmhd->bhsm', xq, kv[:, :, 0]).astype(jnp.float32) * (X_HEAD_DIM ** -0.5)
        pm = jax.nn.softmax(sc, axis=-1).astype(x.dtype)
        xo = jnp.einsum('bhsm,bmhd->bshd', pm, kv[:, :, 1]).reshape(b, s, X_HEADS * X_HEAD_DIM)
        x = x + rmsnorm(xo @ w_xo, g_x_post)

        hf = rmsnorm(x, g_ffn_pre)
        ff = (jax.nn.silu(hf @ w_gate) * (hf @ w_up)) @ w_down
        x = x + rmsnorm(ff, g_ffn_post)

    return x
```

```python
import functools
import math

import jax
import jax.numpy as jnp
from jax import lax
from jax.experimental import pallas as pl
from jax.experimental.pallas import tpu as pltpu

F32 = jnp.float32
BF16 = jnp.bfloat16

D_MODEL = 1024
MEM_LEN = 256
EPS = 1e-6
DA_HEADS = 4
DA_QK_DIM = 64
DA_V_DIM = 2 * DA_QK_DIM
DA_WIDTH = DA_HEADS * DA_V_DIM
QK_WIDTH = DA_HEADS * 2 * DA_QK_DIM
POOL_WINDOWS = (2, 4, 8, 16)
POOL_WIDTH = D_MODEL - DA_WIDTH
POOL_GROUP_DIM = POOL_WIDTH // len(POOL_WINDOWS)
ROPE_THETA = 500000.0
ROPE_DIM = DA_QK_DIM // 4
ROPE_HALF = ROPE_DIM // 2
X_HEADS = 4
X_HEAD_DIM = D_MODEL // X_HEADS
D_FF = -(-(8 * D_MODEL) // (3 * 256)) * 256
LAMBDA_INIT = 0.8 - 0.6 * math.exp(-0.3 * 0)

LANES = 128
POOL_HALO = 16
VMEM_LIMIT = 56 * 1024 * 1024

TOKEN_TILE = 512
ATTN_TILE = 256
FF_CHUNKS = ((0, 1536), (1536, 2816))


def _rms(x, g):
    return x * lax.rsqrt(jnp.mean(x * x, axis=-1, keepdims=True) + EPS) * g


def _const_spec(shape):
    zeros = (0,) * len(shape)
    return pl.BlockSpec(shape, lambda *_: zeros, pipeline_mode=pl.Buffered(1))


def _kv_kernel(mem_ref, g_ref, w_ref, kv_ref):
    mn = _rms(mem_ref[0], g_ref[...]).astype(BF16)
    kv_ref[0] = jnp.dot(mn, w_ref[...], preferred_element_type=F32).astype(BF16)


def _kv_call(mem, g_mem, w_xkv):
    b, m, d = mem.shape
    n = w_xkv.shape[1]
    return pl.pallas_call(
        _kv_kernel,
        grid=(b,),
        in_specs=[pl.BlockSpec((1, m, d), lambda i: (i, 0, 0)),
                  _const_spec((1, d)), _const_spec((d, n))],
        out_specs=pl.BlockSpec((1, m, n), lambda i: (i, 0, 0)),
        out_shape=jax.ShapeDtypeStruct((b, m, n), BF16),
        compiler_params=pltpu.CompilerParams(
            dimension_semantics=("arbitrary",), vmem_limit_bytes=VMEM_LIMIT),
        name="kv_proj",
    )(mem, g_mem, w_xkv)


def _inproj_kernel(x_ref, g_ref, w_ref, c_ref, sa_ref, sb_ref, wp_ref, ps_ref,
                   q_ref, k_ref, v_ref, po_ref, halo_ref, *, tm):
    i = pl.program_id(1)
    h = _rms(x_ref[0], g_ref[...]).astype(BF16)

    c, sa, sb = c_ref[0], sa_ref[0], sb_ref[0]

    def rope(t):
        return t * c + pltpu.roll(t, LANES - ROPE_HALF, 1) * sa + pltpu.roll(t, ROPE_HALF, 1) * sb

    qk_scale = DA_QK_DIM ** -0.5
    pq = jnp.dot(h, w_ref[:, 0:QK_WIDTH], preferred_element_type=F32)
    for j in range(DA_HEADS):
        sl = slice(j * LANES, (j + 1) * LANES)
        q_ref[0, :, sl] = (rope(pq[:, sl]) * qk_scale).astype(BF16)
    pk = jnp.dot(h, w_ref[:, QK_WIDTH:2 * QK_WIDTH], preferred_element_type=F32)
    for j in range(DA_HEADS):
        sl = slice(j * LANES, (j + 1) * LANES)
        k_ref[0, :, sl] = rope(pk[:, sl]).astype(BF16)
    v_ref[0] = jnp.dot(h, w_ref[:, 2 * QK_WIDTH:2 * QK_WIDTH + DA_WIDTH],
                       preferred_element_type=F32).astype(BF16)
    u = jnp.dot(h, w_ref[:, 2 * QK_WIDTH + DA_WIDTH:], preferred_element_type=F32)

    @pl.when(i == 0)
    def _():
        halo_ref[...] = jnp.zeros_like(halo_ref)

    prev = halo_ref[...]
    halo_ref[...] = u[tm - POOL_HALO:, :]
    tpos = i * tm + lax.broadcasted_iota(jnp.int32, (tm, 1), 0)
    for g, w in enumerate(POOL_WINDOWS):
        sl = slice(g * POOL_GROUP_DIM, (g + 1) * POOL_GROUP_DIM)
        ug = u[:, sl]
        s = jnp.concatenate([prev[:, sl], ug], axis=0)
        d = 1
        while d < w:
            s = s + pltpu.roll(s, d, 0)
            d *= 2
        inv_count = 1.0 / jnp.minimum(tpos + 1, w).astype(F32)
        pooled = s[POOL_HALO:, :] * inv_count - ug
        po = jnp.dot(pooled.astype(BF16), wp_ref[g], preferred_element_type=F32) * ps_ref[:, sl]
        po_ref[0, :, sl] = po.astype(BF16)


def _inproj_call(x, g, w_in, c, sa, sb, w_pool, pool_scale, tm):
    b, s, d = x.shape
    tok = lambda n: pl.BlockSpec((1, tm, n), lambda bi, i: (bi, i, 0))
    out = jax.ShapeDtypeStruct((b, s, QK_WIDTH), BF16)
    return pl.pallas_call(
        functools.partial(_inproj_kernel, tm=tm),
        grid=(b, s // tm),
        in_specs=[tok(d), _const_spec((1, d)), _const_spec(w_in.shape),
                  tok(LANES), tok(LANES), tok(LANES),
                  _const_spec(w_pool.shape), _const_spec((1, POOL_WIDTH))],
        out_specs=[tok(QK_WIDTH), tok(QK_WIDTH), tok(DA_WIDTH), tok(POOL_WIDTH)],
        out_shape=[out, out, out, out],
        scratch_shapes=[pltpu.VMEM((POOL_HALO, POOL_WIDTH), F32)],
        compiler_params=pltpu.CompilerParams(
            dimension_semantics=("arbitrary", "arbitrary"), vmem_limit_bytes=VMEM_LIMIT),
        name="inproj",
    )(x, g, w_in, c, sa, sb, w_pool, pool_scale)


def _attn_kernel(q_ref, k_ref, v_ref, lq1_ref, lk1_ref, lq2_ref, lk2_ref, gs_ref, o_ref, *, t):
    i = pl.program_id(2)
    q = q_ref[0]
    lane = lax.broadcasted_iota(jnp.int32, q.shape, 1)
    zero = jnp.zeros_like(q)
    qs = jnp.concatenate([jnp.where(lane < DA_QK_DIM, q, zero),
                          jnp.where(lane >= DA_QK_DIM, q, zero)], axis=0)

    def step(j, carry, diagonal):
        m, l, acc = carry
        start = pl.multiple_of(j * t, t)
        kj = k_ref[0, pl.ds(start, t), :]
        vj = v_ref[0, pl.ds(start, t), :]
        s = lax.dot_general(qs, kj, (((1,), (1,)), ((), ())), preferred_element_type=F32)
        if diagonal:
            row = lax.broadcasted_iota(jnp.int32, (t, t), 0)
            col = lax.broadcasted_iota(jnp.int32, (t, t), 1)
            keep = jnp.concatenate([col <= row, col <= row], axis=0)
            s = jnp.where(keep, s, -jnp.inf)
        m_new = jnp.maximum(m, jnp.max(s, axis=-1, keepdims=True))
        p = jnp.exp(s - m_new)
        alpha = jnp.exp(m - m_new)
        l = alpha * l + jnp.sum(p, axis=-1, keepdims=True)
        acc = alpha * acc + jnp.dot(p.astype(BF16), vj, preferred_element_type=F32)
        return m_new, l, acc

    init = (jnp.full((2 * t, 1), -jnp.inf, F32), jnp.zeros((2 * t, 1), F32),
            jnp.zeros((2 * t, DA_V_DIM), F32))
    carry = lax.fori_loop(0, i, lambda j, cr: step(j, cr, False), init)
    _, l, acc = step(i, carry, True)

    o = acc * (1.0 / l)
    lam = (jnp.exp(jnp.sum(lq1_ref[...] * lk1_ref[...], axis=-1, keepdims=True))
           - jnp.exp(jnp.sum(lq2_ref[...] * lk2_ref[...], axis=-1, keepdims=True))
           + LAMBDA_INIT)
    da = o[:t] - lam * o[t:]
    o_ref[0] = (_rms(da, gs_ref[...]) * (1.0 - LAMBDA_INIT)).astype(BF16)


def _attn_call(q, k, v, lq1, lk1, lq2, lk2, g_subln, t):
    b, s, _ = q.shape
    qspec = pl.BlockSpec((1, t, DA_V_DIM), lambda bi, h, i: (bi, i, h))
    kvspec = pl.BlockSpec((1, s, DA_V_DIM), lambda bi, h, i: (bi, 0, h))
    vec = _const_spec((1, DA_QK_DIM))
    return pl.pallas_call(
        functools.partial(_attn_kernel, t=t),
        grid=(b, DA_HEADS, s // t),
        in_specs=[qspec, kvspec, kvspec, vec, vec, vec, vec, _const_spec((1, DA_V_DIM))],
        out_specs=qspec,
        out_shape=jax.ShapeDtypeStruct((b, s, DA_WIDTH), BF16),
        compiler_params=pltpu.CompilerParams(
            dimension_semantics=("arbitrary", "arbitrary", "arbitrary"),
            vmem_limit_bytes=VMEM_LIMIT),
        name="diff_attn",
    )(q, k, v, lq1, lk1, lq2, lk2, g_subln)


def _mixout_kernel(da_ref, po_ref, x_ref, wo_ref, gpost_ref, gxpre_ref, wxq_ref, kv_ref, wxo_ref,
                   gxpost_ref, o_ref):
    mix = (jnp.dot(da_ref[0], wo_ref[0:DA_WIDTH, :], preferred_element_type=F32)
           + jnp.dot(po_ref[0], wo_ref[DA_WIDTH:, :], preferred_element_type=F32))
    x1 = x_ref[0] + _rms(mix, gpost_ref[...])

    hq = _rms(x1, gxpre_ref[...]).astype(BF16)
    x_scale = X_HEAD_DIM ** -0.5
    xq = (jnp.dot(hq, wxq_ref[...], preferred_element_type=F32) * x_scale).astype(BF16)
    heads = []
    for h in range(X_HEADS):
        sl = slice(h * X_HEAD_DIM, (h + 1) * X_HEAD_DIM)
        kh = kv_ref[0, :, sl]
        vh = kv_ref[0, :, D_MODEL + h * X_HEAD_DIM:D_MODEL + (h + 1) * X_HEAD_DIM]
        sc = lax.dot_general(xq[:, sl], kh, (((1,), (1,)), ((), ())), preferred_element_type=F32)
        p = jnp.exp(sc - jnp.max(sc, axis=-1, keepdims=True))
        pm = (p * (1.0 / jnp.sum(p, axis=-1, keepdims=True))).astype(BF16)
        heads.append(jnp.dot(pm, vh, preferred_element_type=F32).astype(BF16))
    xo = jnp.concatenate(heads, axis=-1)
    y = jnp.dot(xo, wxo_ref[...], preferred_element_type=F32)
    o_ref[0] = x1 + _rms(y, gxpost_ref[...])


def _mixout_call(da, po, x, w_out, g_mix_post, g_x_pre, w_xq, kv, w_xo, g_x_post, tm):
    b, s, d = x.shape
    tok = lambda n: pl.BlockSpec((1, tm, n), lambda bi, i: (bi, i, 0))
    gvec = _const_spec((1, d))
    return pl.pallas_call(
        _mixout_kernel,
        grid=(b, s // tm),
        in_specs=[tok(DA_WIDTH), tok(POOL_WIDTH), tok(d), _const_spec(w_out.shape), gvec, gvec,
                  _const_spec(w_xq.shape),
                  pl.BlockSpec((1,) + kv.shape[1:], lambda bi, i: (bi, 0, 0)),
                  _const_spec(w_xo.shape), gvec],
        out_specs=tok(d),
        out_shape=jax.ShapeDtypeStruct((b, s, d), F32),
        compiler_params=pltpu.CompilerParams(
            dimension_semantics=("arbitrary", "arbitrary"), vmem_limit_bytes=VMEM_LIMIT),
        name="mixout_xattn",
    )(da, po, x, w_out, g_mix_post, g_x_pre, w_xq, kv, w_xo, g_x_post)


def _ffn_kernel(x_ref, gpre_ref, wg_ref, wu_ref, wd_ref, gpost_ref, o_ref):
    x = x_ref[0]
    hf = _rms(x, gpre_ref[...]).astype(BF16)
    ff = None
    for lo, hi in FF_CHUNKS:
        gate = jnp.dot(hf, wg_ref[:, lo:hi], preferred_element_type=F32)
        up = jnp.dot(hf, wu_ref[:, lo:hi], preferred_element_type=F32)
        act = (gate * (1.0 / (1.0 + jnp.exp(-gate))) * up).astype(BF16)
        part = jnp.dot(act, wd_ref[lo:hi, :], preferred_element_type=F32)
        ff = part if ff is None else ff + part
    o_ref[0] = x + _rms(ff, gpost_ref[...])


def _ffn_call(x, g_pre, w_gate, w_up, w_down, g_post, tm):
    b, s, d = x.shape
    tok = pl.BlockSpec((1, tm, d), lambda bi, i: (bi, i, 0))
    gvec = _const_spec((1, d))
    return pl.pallas_call(
        _ffn_kernel,
        grid=(b, s // tm),
        in_specs=[tok, gvec, _const_spec(w_gate.shape), _const_spec(w_up.shape),
                  _const_spec(w_down.shape), gvec],
        out_specs=tok,
        out_shape=jax.ShapeDtypeStruct((b, s, d), F32),
        compiler_params=pltpu.CompilerParams(
            dimension_semantics=("arbitrary", "arbitrary"), vmem_limit_bytes=VMEM_LIMIT),
        name="swiglu",
    )(x, g_pre, w_gate, w_up, w_down, g_post)


def _rope_lane_tables(positions):
    inv_freq = ROPE_THETA ** (-jnp.arange(0, ROPE_DIM, 2, dtype=F32) / ROPE_DIM)
    ang = positions.astype(F32)[..., None] * inv_freq
    cos, sin = jnp.cos(ang), jnp.sin(ang)
    rest = DA_QK_DIM - ROPE_DIM
    ones = jnp.ones(cos.shape[:-1] + (rest,), F32)
    z_half = jnp.zeros_like(cos)
    z_rest = jnp.zeros_like(ones)
    c = jnp.concatenate([cos, cos, ones], axis=-1)
    sa = jnp.concatenate([-sin, z_half, z_rest], axis=-1)
    sb = jnp.concatenate([z_half, sin, z_rest], axis=-1)
    dup = lambda t: jnp.concatenate([t, t], axis=-1)
    return dup(c), dup(sa), dup(sb)


def kernel(x, mem, positions, g_mix_pre, w_in, lambda_q1, lambda_k1, lambda_q2, lambda_k2,
           g_subln, w_pool, pool_scale, w_out, g_mix_post, g_x_pre, g_mem, w_xq, w_xkv, w_xo,
           g_x_post, g_ffn_pre, w_gate, w_up, w_down, g_ffn_post):
    b, s, d = x.shape
    assert d == D_MODEL and s % TOKEN_TILE == 0 and s % ATTN_TILE == 0
    row = lambda a: a.reshape(1, -1).astype(F32)
    bf = lambda a: a.astype(BF16)

    c, sa, sb = _rope_lane_tables(positions)
    kv = _kv_call(mem, row(g_mem), bf(w_xkv))
    q, k, v, po = _inproj_call(x, row(g_mix_pre), bf(w_in), c, sa, sb, bf(w_pool),
                               row(pool_scale), TOKEN_TILE)
    da = _attn_call(q, k, v, row(lambda_q1), row(lambda_k1), row(lambda_q2), row(lambda_k2),
                    row(g_subln), ATTN_TILE)
    x2 = _mixout_call(da, po, x, bf(w_out), row(g_mix_post), row(g_x_pre), bf(w_xq), kv,
                      bf(w_xo), row(g_x_post), TOKEN_TILE)
    return _ffn_call(x2, row(g_ffn_pre), bf(w_gate), bf(w_up), bf(w_down), row(g_ffn_post),
                     TOKEN_TILE)
```

```python
import functools
import math

import jax
import jax.numpy as jnp
from jax import lax
from jax.experimental import pallas as pl
from jax.experimental.pallas import tpu as pltpu

F32 = jnp.float32
BF16 = jnp.bfloat16

D_MODEL = 1024
MEM_LEN = 256
EPS = 1e-6
DA_HEADS = 4
DA_QK_DIM = 64
DA_V_DIM = 2 * DA_QK_DIM
DA_WIDTH = DA_HEADS * DA_V_DIM
QK_WIDTH = DA_HEADS * 2 * DA_QK_DIM
POOL_WINDOWS = (2, 4, 8, 16)
POOL_WIDTH = D_MODEL - DA_WIDTH
POOL_GROUP_DIM = POOL_WIDTH // len(POOL_WINDOWS)
ROPE_THETA = 500000.0
ROPE_DIM = DA_QK_DIM // 4
ROPE_HALF = ROPE_DIM // 2
X_HEADS = 4
X_HEAD_DIM = D_MODEL // X_HEADS
D_FF = -(-(8 * D_MODEL) // (3 * 256)) * 256
LAMBDA_INIT = 0.8 - 0.6 * math.exp(-0.3 * 0)

LANES = 128
POOL_HALO = 16
VMEM_LIMIT = 56 * 1024 * 1024

TOKEN_TILE = 512
ATTN_TILE = 256
FF_CHUNKS = ((0, 1536), (1536, 2816))


def _rms(x, g):
    return x * lax.rsqrt(jnp.mean(x * x, axis=-1, keepdims=True) + EPS) * g


def _const_spec(shape):
    zeros = (0,) * len(shape)
    return pl.BlockSpec(shape, lambda *_: zeros, pipeline_mode=pl.Buffered(1))


def _kv_kernel(mem_ref, g_ref, w_ref, kv_ref):
    mn = _rms(mem_ref[0], g_ref[...]).astype(BF16)
    kv_ref[0] = jnp.dot(mn, w_ref[...], preferred_element_type=F32).astype(BF16)


def _kv_call(mem, g_mem, w_xkv):
    b, m, d = mem.shape
    n = w_xkv.shape[1]
    return pl.pallas_call(
        _kv_kernel,
        grid=(b,),
        in_specs=[pl.BlockSpec((1, m, d), lambda i: (i, 0, 0)),
                  _const_spec((1, d)), _const_spec((d, n))],
        out_specs=pl.BlockSpec((1, m, n), lambda i: (i, 0, 0)),
        out_shape=jax.ShapeDtypeStruct((b, m, n), BF16),
        compiler_params=pltpu.CompilerParams(
            dimension_semantics=("arbitrary",), vmem_limit_bytes=VMEM_LIMIT),
        name="kv_proj",
    )(mem, g_mem, w_xkv)


def _inproj_kernel(x_ref, g_ref, w_ref, c_ref, sa_ref, sb_ref, wp_ref, ps_ref,
                   q_ref, k_ref, v_ref, po_ref, halo_ref, *, tm):
    i = pl.program_id(1)
    h = _rms(x_ref[0], g_ref[...]).astype(BF16)

    c, sa, sb = c_ref[0], sa_ref[0], sb_ref[0]

    def rope(t):
        return t * c + pltpu.roll(t, LANES - ROPE_HALF, 1) * sa + pltpu.roll(t, ROPE_HALF, 1) * sb

    qk_scale = DA_QK_DIM ** -0.5
    pq = jnp.dot(h, w_ref[:, 0:QK_WIDTH], preferred_element_type=F32)
    for j in range(DA_HEADS):
        sl = slice(j * LANES, (j + 1) * LANES)
        q_ref[0, :, sl] = (rope(pq[:, sl]) * qk_scale).astype(BF16)
    pk = jnp.dot(h, w_ref[:, QK_WIDTH:2 * QK_WIDTH], preferred_element_type=F32)
    for j in range(DA_HEADS):
        sl = slice(j * LANES, (j + 1) * LANES)
        k_ref[0, :, sl] = rope(pk[:, sl]).astype(BF16)
    v_ref[0] = jnp.dot(h, w_ref[:, 2 * QK_WIDTH:2 * QK_WIDTH + DA_WIDTH],
                       preferred_element_type=F32).astype(BF16)
    u = jnp.dot(h, w_ref[:, 2 * QK_WIDTH + DA_WIDTH:], preferred_element_type=F32)

    @pl.when(i == 0)
    def _():
        halo_ref[...] = jnp.zeros_like(halo_ref)

    prev = halo_ref[...]
    halo_ref[...] = u[tm - POOL_HALO:, :]
    tpos = i * tm + lax.broadcasted_iota(jnp.int32, (tm, 1), 0)
    for g, w in enumerate(POOL_WINDOWS):
        sl = slice(g * POOL_GROUP_DIM, (g + 1) * POOL_GROUP_DIM)
        ug = u[:, sl]
        s = jnp.concatenate([prev[:, sl], ug], axis=0)
        d = 1
        while d < w:
            s = s + pltpu.roll(s, d, 0)
            d *= 2
        inv_count = 1.0 / jnp.minimum(tpos + 1, w).astype(F32)
        pooled = s[POOL_HALO:, :] * inv_count - ug
        po = jnp.dot(pooled.astype(BF16), wp_ref[g], preferred_element_type=F32) * ps_ref[:, sl]
        po_ref[0, :, sl] = po.astype(BF16)


def _inproj_call(x, g, w_in, c, sa, sb, w_pool, pool_scale, tm):
    b, s, d = x.shape
    tok = lambda n: pl.BlockSpec((1, tm, n), lambda bi, i: (bi, i, 0))
    out = jax.ShapeDtypeStruct((b, s, QK_WIDTH), BF16)
    return pl.pallas_call(
        functools.partial(_inproj_kernel, tm=tm),
        grid=(b, s // tm),
        in_specs=[tok(d), _const_spec((1, d)), _const_spec(w_in.shape),
                  tok(LANES), tok(LANES), tok(LANES),
                  _const_spec(w_pool.shape), _const_spec((1, POOL_WIDTH))],
        out_specs=[tok(QK_WIDTH), tok(QK_WIDTH), tok(DA_WIDTH), tok(POOL_WIDTH)],
        out_shape=[out, out, out, out],
        scratch_shapes=[pltpu.VMEM((POOL_HALO, POOL_WIDTH), F32)],
        compiler_params=pltpu.CompilerParams(
            dimension_semantics=("arbitrary", "arbitrary"), vmem_limit_bytes=VMEM_LIMIT),
        name="inproj",
    )(x, g, w_in, c, sa, sb, w_pool, pool_scale)


def _attn_kernel(q_ref, k_ref, v_ref, lq1_ref, lk1_ref, lq2_ref, lk2_ref, gs_ref, o_ref,
                 m_ref, l_ref, acc_ref, qs_ref, s0_ref, s1_ref, p0_ref, p1_ref, *, t, nq):
    m_ref[...] = jnp.full_like(m_ref, -jnp.inf)
    l_ref[...] = jnp.zeros_like(l_ref)
    acc_ref[...] = jnp.zeros_like(acc_ref)

    lane = lax.broadcasted_iota(jnp.int32, (t, DA_V_DIM), 1)
    key = lax.broadcasted_iota(jnp.int32, (t, t), 0)
    qry = lax.broadcasted_iota(jnp.int32, (t, t), 1)
    causal = jnp.concatenate([key <= qry, key <= qry], axis=1)

    def rows(i):
        return pl.ds(pl.multiple_of(i * t, t), t)

    def stack_maps(i, _):
        q = q_ref[0, rows(i), :]
        zero = jnp.zeros_like(q)
        qs_ref[i] = jnp.concatenate([jnp.where(lane < DA_QK_DIM, q, zero),
                                     jnp.where(lane >= DA_QK_DIM, q, zero)], axis=0)
        return 0

    lax.fori_loop(0, nq, stack_maps, 0)

    def scores(i, j):
        return lax.dot_general(k_ref[0, rows(j), :], qs_ref[i], (((1,), (1,)), ((), ())),
                               preferred_element_type=F32)

    def softmax(i, s_ref, p_ref, diagonal):
        s = s_ref[...]
        if diagonal:
            s = jnp.where(causal, s, -jnp.inf)
        m = m_ref[i]
        m_new = jnp.maximum(m, jnp.max(s, axis=0, keepdims=True))
        p = jnp.exp(s - m_new)
        alpha = jnp.exp(m - m_new)
        l_ref[i] = alpha * l_ref[i] + jnp.sum(p, axis=0, keepdims=True)
        m_ref[i] = m_new
        p_ref[...] = p.astype(BF16)
        return alpha

    def accumulate(i, j, p_ref, alpha):
        pv = lax.dot_general(v_ref[0, rows(j), :], p_ref[...], (((0,), (0,)), ((), ())),
                             preferred_element_type=F32)
        acc_ref[i] = alpha * acc_ref[i] + pv

    s_bufs, p_bufs = (s0_ref, s1_ref), (p0_ref, p1_ref)

    def pipeline(n_units, unit, diagonal):
        assert n_units % 2 == 0 and n_units >= 2

        def stage(n, slot, alpha_prev):
            s_bufs[1 - slot][...] = scores(*unit(n + 1))
            alpha = softmax(unit(n)[0], s_bufs[slot], p_bufs[slot], diagonal)
            accumulate(*unit(n - 1), p_bufs[1 - slot], alpha_prev)
            return alpha

        s0_ref[...] = scores(*unit(0))
        s1_ref[...] = scores(*unit(1))
        alpha_first = softmax(unit(0)[0], s0_ref, p0_ref, diagonal)

        def body(h, alpha_prev):
            return stage(2 * h + 2, 0, stage(2 * h + 1, 1, alpha_prev))

        alpha_even = lax.fori_loop(0, n_units // 2 - 1, body, alpha_first)
        alpha_last = softmax(unit(n_units - 1)[0], s1_ref, p1_ref, diagonal)
        accumulate(*unit(n_units - 2), p0_ref, alpha_even)
        accumulate(*unit(n_units - 1), p1_ref, alpha_last)

    def below_diagonal(n):
        n = jnp.asarray(n, jnp.int32)
        i = 1 + sum((n >= k * (k + 1) // 2).astype(jnp.int32) for k in range(1, nq - 1))
        return i, n - i * (i - 1) // 2

    pipeline(nq * (nq - 1) // 2, below_diagonal, False)
    pipeline(nq, lambda n: (n, n), True)

    lam = (jnp.exp(jnp.sum(lq1_ref[...] * lk1_ref[...], axis=-1, keepdims=True))
           - jnp.exp(jnp.sum(lq2_ref[...] * lk2_ref[...], axis=-1, keepdims=True))
           + LAMBDA_INIT)
    gain = gs_ref[...] * (1.0 - LAMBDA_INIT)

    def finish(i, _):
        o = acc_ref[i] * (1.0 / l_ref[i])
        da = o[:, :t] - lam * o[:, t:]
        inv = lax.rsqrt(jnp.mean(da * da, axis=0, keepdims=True) + EPS)
        o_ref[0, rows(i), :] = (da * inv * gain).T.astype(BF16)
        return 0

    lax.fori_loop(0, nq, finish, 0)


def _attn_call(q, k, v, lq1, lk1, lq2, lk2, g_subln_col, t):
    b, s, _ = q.shape
    nq = s // t
    head = pl.BlockSpec((1, s, DA_V_DIM), lambda bi, h: (bi, 0, h))
    vec = _const_spec((1, DA_QK_DIM))
    return pl.pallas_call(
        functools.partial(_attn_kernel, t=t, nq=nq),
        grid=(b, DA_HEADS),
        in_specs=[head, head, head, vec, vec, vec, vec, _const_spec((DA_V_DIM, 1))],
        out_specs=head,
        out_shape=jax.ShapeDtypeStruct((b, s, DA_WIDTH), BF16),
        scratch_shapes=[pltpu.VMEM((nq, 1, 2 * t), F32), pltpu.VMEM((nq, 1, 2 * t), F32),
                        pltpu.VMEM((nq, DA_V_DIM, 2 * t), F32),
                        pltpu.VMEM((nq, 2 * t, DA_V_DIM), BF16),
                        pltpu.VMEM((t, 2 * t), F32), pltpu.VMEM((t, 2 * t), F32),
                        pltpu.VMEM((t, 2 * t), BF16), pltpu.VMEM((t, 2 * t), BF16)],
        compiler_params=pltpu.CompilerParams(
            dimension_semantics=("arbitrary", "arbitrary"), vmem_limit_bytes=VMEM_LIMIT),
        name="diff_attn",
    )(q, k, v, lq1, lk1, lq2, lk2, g_subln_col)


def _mixout_kernel(da_ref, po_ref, x_ref, wo_ref, gpost_ref, gxpre_ref, wxq_ref, kv_ref, wxo_ref,
                   gxpost_ref, o_ref):
    mix = (jnp.dot(da_ref[0], wo_ref[0:DA_WIDTH, :], preferred_element_type=F32)
           + jnp.dot(po_ref[0], wo_ref[DA_WIDTH:, :], preferred_element_type=F32))
    x1 = x_ref[0] + _rms(mix, gpost_ref[...])

    hq = _rms(x1, gxpre_ref[...]).astype(BF16)
    x_scale = X_HEAD_DIM ** -0.5
    xq = (jnp.dot(hq, wxq_ref[...], preferred_element_type=F32) * x_scale).astype(BF16)
    heads = []
    for h in range(X_HEADS):
        sl = slice(h * X_HEAD_DIM, (h + 1) * X_HEAD_DIM)
        kh = kv_ref[0, :, sl]
        vh = kv_ref[0, :, D_MODEL + h * X_HEAD_DIM:D_MODEL + (h + 1) * X_HEAD_DIM]
        sc = lax.dot_general(xq[:, sl], kh, (((1,), (1,)), ((), ())), preferred_element_type=F32)
        p = jnp.exp(sc - jnp.max(sc, axis=-1, keepdims=True))
        pm = (p * (1.0 / jnp.sum(p, axis=-1, keepdims=True))).astype(BF16)
        heads.append(jnp.dot(pm, vh, preferred_element_type=F32).astype(BF16))
    xo = jnp.concatenate(heads, axis=-1)
    y = jnp.dot(xo, wxo_ref[...], preferred_element_type=F32)
    o_ref[0] = x1 + _rms(y, gxpost_ref[...])


def _mixout_call(da, po, x, w_out, g_mix_post, g_x_pre, w_xq, kv, w_xo, g_x_post, tm):
    b, s, d = x.shape
    tok = lambda n: pl.BlockSpec((1, tm, n), lambda bi, i: (bi, i, 0))
    gvec = _const_spec((1, d))
    return pl.pallas_call(
        _mixout_kernel,
        grid=(b, s // tm),
        in_specs=[tok(DA_WIDTH), tok(POOL_WIDTH), tok(d), _const_spec(w_out.shape), gvec, gvec,
                  _const_spec(w_xq.shape),
                  pl.BlockSpec((1,) + kv.shape[1:], lambda bi, i: (bi, 0, 0)),
                  _const_spec(w_xo.shape), gvec],
        out_specs=tok(d),
        out_shape=jax.ShapeDtypeStruct((b, s, d), F32),
        compiler_params=pltpu.CompilerParams(
            dimension_semantics=("arbitrary", "arbitrary"), vmem_limit_bytes=VMEM_LIMIT),
        name="mixout_xattn",
    )(da, po, x, w_out, g_mix_post, g_x_pre, w_xq, kv, w_xo, g_x_post)


def _ffn_kernel(x_ref, gpre_ref, wg_ref, wu_ref, wd_ref, gpost_ref, o_ref):
    x = x_ref[0]
    hf = _rms(x, gpre_ref[...]).astype(BF16)
    ff = None
    for lo, hi in FF_CHUNKS:
        gate = jnp.dot(hf, wg_ref[:, lo:hi], preferred_element_type=F32)
        up = jnp.dot(hf, wu_ref[:, lo:hi], preferred_element_type=F32)
        act = (gate * (1.0 / (1.0 + jnp.exp(-gate))) * up).astype(BF16)
        part = jnp.dot(act, wd_ref[lo:hi, :], preferred_element_type=F32)
        ff = part if ff is None else ff + part
    o_ref[0] = x + _rms(ff, gpost_ref[...])


def _ffn_call(x, g_pre, w_gate, w_up, w_down, g_post, tm):
    b, s, d = x.shape
    tok = pl.BlockSpec((1, tm, d), lambda bi, i: (bi, i, 0))
    gvec = _const_spec((1, d))
    return pl.pallas_call(
        _ffn_kernel,
        grid=(b, s // tm),
        in_specs=[tok, gvec, _const_spec(w_gate.shape), _const_spec(w_up.shape),
                  _const_spec(w_down.shape), gvec],
        out_specs=tok,
        out_shape=jax.ShapeDtypeStruct((b, s, d), F32),
        compiler_params=pltpu.CompilerParams(
            dimension_semantics=("arbitrary", "arbitrary"), vmem_limit_bytes=VMEM_LIMIT),
        name="swiglu",
    )(x, g_pre, w_gate, w_up, w_down, g_post)


def _rope_lane_tables(positions):
    inv_freq = ROPE_THETA ** (-jnp.arange(0, ROPE_DIM, 2, dtype=F32) / ROPE_DIM)
    ang = positions.astype(F32)[..., None] * inv_freq
    cos, sin = jnp.cos(ang), jnp.sin(ang)
    rest = DA_QK_DIM - ROPE_DIM
    ones = jnp.ones(cos.shape[:-1] + (rest,), F32)
    z_half = jnp.zeros_like(cos)
    z_rest = jnp.zeros_like(ones)
    c = jnp.concatenate([cos, cos, ones], axis=-1)
    sa = jnp.concatenate([-sin, z_half, z_rest], axis=-1)
    sb = jnp.concatenate([z_half, sin, z_rest], axis=-1)
    dup = lambda t: jnp.concatenate([t, t], axis=-1)
    return dup(c), dup(sa), dup(sb)


def kernel(x, mem, positions, g_mix_pre, w_in, lambda_q1, lambda_k1, lambda_q2, lambda_k2,
           g_subln, w_pool, pool_scale, w_out, g_mix_post, g_x_pre, g_mem, w_xq, w_xkv, w_xo,
           g_x_post, g_ffn_pre, w_gate, w_up, w_down, g_ffn_post):
    b, s, d = x.shape
    assert d == D_MODEL and s % TOKEN_TILE == 0 and s % ATTN_TILE == 0
    row = lambda a: a.reshape(1, -1).astype(F32)
    bf = lambda a: a.astype(BF16)

    c, sa, sb = _rope_lane_tables(positions)
    kv = _kv_call(mem, row(g_mem), bf(w_xkv))
    q, k, v, po = _inproj_call(x, row(g_mix_pre), bf(w_in), c, sa, sb, bf(w_pool),
                               row(pool_scale), TOKEN_TILE)
    da = _attn_call(q, k, v, row(lambda_q1), row(lambda_k1), row(lambda_q2), row(lambda_k2),
                    g_subln.reshape(-1, 1).astype(F32), ATTN_TILE)
    x2 = _mixout_call(da, po, x, bf(w_out), row(g_mix_post), row(g_x_pre), bf(w_xq), kv,
                      bf(w_xo), row(g_x_post), TOKEN_TILE)
    return _ffn_call(x2, row(g_ffn_pre), bf(w_gate), bf(w_up), bf(w_down), row(g_ffn_post),
                     TOKEN_TILE)
```

```python
import functools
import math

import jax
import jax.numpy as jnp
from jax import lax
from jax.experimental import pallas as pl
from jax.experimental.pallas import tpu as pltpu

F32 = jnp.float32
BF16 = jnp.bfloat16

D_MODEL = 1024
MEM_LEN = 256
EPS = 1e-6
DA_HEADS = 4
DA_QK_DIM = 64
DA_V_DIM = 2 * DA_QK_DIM
DA_WIDTH = DA_HEADS * DA_V_DIM
QK_WIDTH = DA_HEADS * 2 * DA_QK_DIM
POOL_WINDOWS = (2, 4, 8, 16)
POOL_WIDTH = D_MODEL - DA_WIDTH
POOL_GROUP_DIM = POOL_WIDTH // len(POOL_WINDOWS)
ROPE_THETA = 500000.0
ROPE_DIM = DA_QK_DIM // 4
ROPE_HALF = ROPE_DIM // 2
X_HEADS = 4
X_HEAD_DIM = D_MODEL // X_HEADS
D_FF = -(-(8 * D_MODEL) // (3 * 256)) * 256
LAMBDA_INIT = 0.8 - 0.6 * math.exp(-0.3 * 0)

LANES = 128
POOL_HALO = 16
VMEM_LIMIT = 56 * 1024 * 1024

TOKEN_TILE = 512
ATTN_TILE = 256
FF_CHUNKS = ((0, 1536), (1536, 2816))


def _rms(x, g):
    return x * lax.rsqrt(jnp.mean(x * x, axis=-1, keepdims=True) + EPS) * g


def _const_spec(shape):
    zeros = (0,) * len(shape)
    return pl.BlockSpec(shape, lambda *_: zeros, pipeline_mode=pl.Buffered(1))


def _kv_kernel(mem_ref, g_ref, w_ref, kv_ref):
    mn = _rms(mem_ref[0], g_ref[...]).astype(BF16)
    kv_ref[0] = jnp.dot(mn, w_ref[...], preferred_element_type=F32).astype(BF16)


def _kv_call(mem, g_mem, w_xkv):
    b, m, d = mem.shape
    n = w_xkv.shape[1]
    return pl.pallas_call(
        _kv_kernel,
        grid=(b,),
        in_specs=[pl.BlockSpec((1, m, d), lambda i: (i, 0, 0)),
                  _const_spec((1, d)), _const_spec((d, n))],
        out_specs=pl.BlockSpec((1, m, n), lambda i: (i, 0, 0)),
        out_shape=jax.ShapeDtypeStruct((b, m, n), BF16),
        compiler_params=pltpu.CompilerParams(
            dimension_semantics=("arbitrary",), vmem_limit_bytes=VMEM_LIMIT),
        name="kv_proj",
    )(mem, g_mem, w_xkv)


def _inproj_kernel(x_ref, g_ref, w_ref, pos_ref, freq_ref, wp_ref, ps_ref,
                   q_ref, k_ref, v_ref, po_ref, halo_ref, *, tm):
    i = pl.program_id(1)
    h = _rms(x_ref[0], g_ref[...]).astype(BF16)

    ang = pos_ref[0] * freq_ref[0:1, :]
    c, sn = jnp.cos(ang), jnp.sin(ang)
    sa, sb = sn * freq_ref[1:2, :], sn * freq_ref[2:3, :]

    def rope(t):
        return t * c + pltpu.roll(t, LANES - ROPE_HALF, 1) * sa + pltpu.roll(t, ROPE_HALF, 1) * sb

    qk_scale = DA_QK_DIM ** -0.5
    pq = jnp.dot(h, w_ref[:, 0:QK_WIDTH], preferred_element_type=F32)
    for j in range(DA_HEADS):
        sl = slice(j * LANES, (j + 1) * LANES)
        q_ref[0, :, sl] = (rope(pq[:, sl]) * qk_scale).astype(BF16)
    pk = jnp.dot(h, w_ref[:, QK_WIDTH:2 * QK_WIDTH], preferred_element_type=F32)
    for j in range(DA_HEADS):
        sl = slice(j * LANES, (j + 1) * LANES)
        k_ref[0, :, sl] = rope(pk[:, sl]).astype(BF16)
    v_ref[0] = jnp.dot(h, w_ref[:, 2 * QK_WIDTH:2 * QK_WIDTH + DA_WIDTH],
                       preferred_element_type=F32).astype(BF16)
    u = jnp.dot(h, w_ref[:, 2 * QK_WIDTH + DA_WIDTH:], preferred_element_type=F32)

    @pl.when(i == 0)
    def _():
        halo_ref[...] = jnp.zeros_like(halo_ref)

    prev = halo_ref[...]
    halo_ref[...] = u[tm - POOL_HALO:, :]
    tpos = i * tm + lax.broadcasted_iota(jnp.int32, (tm, 1), 0)
    for g, w in enumerate(POOL_WINDOWS):
        sl = slice(g * POOL_GROUP_DIM, (g + 1) * POOL_GROUP_DIM)
        ug = u[:, sl]
        s = jnp.concatenate([prev[:, sl], ug], axis=0)
        d = 1
        while d < w:
            s = s + pltpu.roll(s, d, 0)
            d *= 2
        inv_count = 1.0 / jnp.minimum(tpos + 1, w).astype(F32)
        pooled = s[POOL_HALO:, :] * inv_count - ug
        po = jnp.dot(pooled.astype(BF16), wp_ref[g], preferred_element_type=F32) * ps_ref[:, sl]
        po_ref[0, :, sl] = po.astype(BF16)


def _inproj_call(x, g, w_in, pos, freq, w_pool, pool_scale, tm):
    b, s, d = x.shape
    tok = lambda n: pl.BlockSpec((1, tm, n), lambda bi, i: (bi, i, 0))
    out = jax.ShapeDtypeStruct((b, s, QK_WIDTH), BF16)
    return pl.pallas_call(
        functools.partial(_inproj_kernel, tm=tm),
        grid=(b, s // tm),
        in_specs=[tok(d), _const_spec((1, d)), _const_spec(w_in.shape),
                  tok(1), _const_spec(freq.shape),
                  _const_spec(w_pool.shape), _const_spec((1, POOL_WIDTH))],
        out_specs=[tok(QK_WIDTH), tok(QK_WIDTH), tok(DA_WIDTH), tok(POOL_WIDTH)],
        out_shape=[out, out, out, out],
        scratch_shapes=[pltpu.VMEM((POOL_HALO, POOL_WIDTH), F32)],
        compiler_params=pltpu.CompilerParams(
            dimension_semantics=("arbitrary", "arbitrary"), vmem_limit_bytes=VMEM_LIMIT),
        name="inproj",
    )(x, g, w_in, pos, freq, w_pool, pool_scale)


def _attn_kernel(q_ref, k_ref, v_ref, lq1_ref, lk1_ref, lq2_ref, lk2_ref, gs_ref, o_ref,
                 m_ref, l_ref, acc_ref, qs_ref, s0_ref, s1_ref, p0_ref, p1_ref, *, t, nq):
    m_ref[...] = jnp.full_like(m_ref, -jnp.inf)
    l_ref[...] = jnp.zeros_like(l_ref)
    acc_ref[...] = jnp.zeros_like(acc_ref)

    lane = lax.broadcasted_iota(jnp.int32, (t, DA_V_DIM), 1)
    key = lax.broadcasted_iota(jnp.int32, (t, t), 0)
    qry = lax.broadcasted_iota(jnp.int32, (t, t), 1)
    causal = jnp.concatenate([key <= qry, key <= qry], axis=1)

    def rows(i):
        return pl.ds(pl.multiple_of(i * t, t), t)

    def stack_maps(i, _):
        q = q_ref[0, rows(i), :]
        zero = jnp.zeros_like(q)
        qs_ref[i] = jnp.concatenate([jnp.where(lane < DA_QK_DIM, q, zero),
                                     jnp.where(lane >= DA_QK_DIM, q, zero)], axis=0)
        return 0

    lax.fori_loop(0, nq, stack_maps, 0)

    def scores(i, j):
        return lax.dot_general(k_ref[0, rows(j), :], qs_ref[i], (((1,), (1,)), ((), ())),
                               preferred_element_type=F32)

    def softmax(i, s_ref, p_ref, diagonal):
        s = s_ref[...]
        if diagonal:
            s = jnp.where(causal, s, -jnp.inf)
        m = m_ref[i]
        m_new = jnp.maximum(m, jnp.max(s, axis=0, keepdims=True))
        p = jnp.exp(s - m_new)
        alpha = jnp.exp(m - m_new)
        l_ref[i] = alpha * l_ref[i] + jnp.sum(p, axis=0, keepdims=True)
        m_ref[i] = m_new
        p_ref[...] = p.astype(BF16)
        return alpha

    def accumulate(i, j, p_ref, alpha):
        pv = lax.dot_general(v_ref[0, rows(j), :], p_ref[...], (((0,), (0,)), ((), ())),
                             preferred_element_type=F32)
        acc_ref[i] = alpha * acc_ref[i] + pv

    s_bufs, p_bufs = (s0_ref, s1_ref), (p0_ref, p1_ref)

    def pipeline(n_units, unit, diagonal):
        assert n_units % 2 == 0 and n_units >= 2

        def stage(n, slot, alpha_prev):
            s_bufs[1 - slot][...] = scores(*unit(n + 1))
            alpha = softmax(unit(n)[0], s_bufs[slot], p_bufs[slot], diagonal)
            accumulate(*unit(n - 1), p_bufs[1 - slot], alpha_prev)
            return alpha

        s0_ref[...] = scores(*unit(0))
        s1_ref[...] = scores(*unit(1))
        alpha_first = softmax(unit(0)[0], s0_ref, p0_ref, diagonal)

        def body(h, alpha_prev):
            return stage(2 * h + 2, 0, stage(2 * h + 1, 1, alpha_prev))

        alpha_even = lax.fori_loop(0, n_units // 2 - 1, body, alpha_first)
        alpha_last = softmax(unit(n_units - 1)[0], s1_ref, p1_ref, diagonal)
        accumulate(*unit(n_units - 2), p0_ref, alpha_even)
        accumulate(*unit(n_units - 1), p1_ref, alpha_last)

    def below_diagonal(n):
        n = jnp.asarray(n, jnp.int32)
        i = 1 + sum((n >= k * (k + 1) // 2).astype(jnp.int32) for k in range(1, nq - 1))
        return i, n - i * (i - 1) // 2

    pipeline(nq * (nq - 1) // 2, below_diagonal, False)
    pipeline(nq, lambda n: (n, n), True)

    lam = (jnp.exp(jnp.sum(lq1_ref[...] * lk1_ref[...], axis=-1, keepdims=True))
           - jnp.exp(jnp.sum(lq2_ref[...] * lk2_ref[...], axis=-1, keepdims=True))
           + LAMBDA_INIT)
    gain = gs_ref[...] * (1.0 - LAMBDA_INIT)

    def finish(i, _):
        o = acc_ref[i] * (1.0 / l_ref[i])
        da = o[:, :t] - lam * o[:, t:]
        inv = lax.rsqrt(jnp.mean(da * da, axis=0, keepdims=True) + EPS)
        o_ref[0, rows(i), :] = (da * inv * gain).T.astype(BF16)
        return 0

    lax.fori_loop(0, nq, finish, 0)


def _attn_call(q, k, v, lq1, lk1, lq2, lk2, g_subln_col, t):
    b, s, _ = q.shape
    nq = s // t
    head = pl.BlockSpec((1, s, DA_V_DIM), lambda bi, h: (bi, 0, h))
    vec = _const_spec((1, DA_QK_DIM))
    return pl.pallas_call(
        functools.partial(_attn_kernel, t=t, nq=nq),
        grid=(b, DA_HEADS),
        in_specs=[head, head, head, vec, vec, vec, vec, _const_spec((DA_V_DIM, 1))],
        out_specs=head,
        out_shape=jax.ShapeDtypeStruct((b, s, DA_WIDTH), BF16),
        scratch_shapes=[pltpu.VMEM((nq, 1, 2 * t), F32), pltpu.VMEM((nq, 1, 2 * t), F32),
                        pltpu.VMEM((nq, DA_V_DIM, 2 * t), F32),
                        pltpu.VMEM((nq, 2 * t, DA_V_DIM), BF16),
                        pltpu.VMEM((t, 2 * t), F32), pltpu.VMEM((t, 2 * t), F32),
                        pltpu.VMEM((t, 2 * t), BF16), pltpu.VMEM((t, 2 * t), BF16)],
        compiler_params=pltpu.CompilerParams(
            dimension_semantics=("arbitrary", "arbitrary"), vmem_limit_bytes=VMEM_LIMIT),
        name="diff_attn",
    )(q, k, v, lq1, lk1, lq2, lk2, g_subln_col)


def _mixout_kernel(da_ref, po_ref, x_ref, wo_ref, gpost_ref, gxpre_ref, wxq_ref, kv_ref, wxo_ref,
                   gxpost_ref, o_ref):
    mix = (jnp.dot(da_ref[0], wo_ref[0:DA_WIDTH, :], preferred_element_type=F32)
           + jnp.dot(po_ref[0], wo_ref[DA_WIDTH:, :], preferred_element_type=F32))
    x1 = x_ref[0] + _rms(mix, gpost_ref[...])

    hq = _rms(x1, gxpre_ref[...]).astype(BF16)
    x_scale = X_HEAD_DIM ** -0.5
    xq = (jnp.dot(hq, wxq_ref[...], preferred_element_type=F32) * x_scale).astype(BF16)
    heads = []
    for h in range(X_HEADS):
        sl = slice(h * X_HEAD_DIM, (h + 1) * X_HEAD_DIM)
        kh = kv_ref[0, :, sl]
        vh = kv_ref[0, :, D_MODEL + h * X_HEAD_DIM:D_MODEL + (h + 1) * X_HEAD_DIM]
        sc = lax.dot_general(xq[:, sl], kh, (((1,), (1,)), ((), ())), preferred_element_type=F32)
        p = jnp.exp(sc - jnp.max(sc, axis=-1, keepdims=True))
        pm = (p * (1.0 / jnp.sum(p, axis=-1, keepdims=True))).astype(BF16)
        heads.append(jnp.dot(pm, vh, preferred_element_type=F32).astype(BF16))
    xo = jnp.concatenate(heads, axis=-1)
    y = jnp.dot(xo, wxo_ref[...], preferred_element_type=F32)
    o_ref[0] = x1 + _rms(y, gxpost_ref[...])


def _mixout_call(da, po, x, w_out, g_mix_post, g_x_pre, w_xq, kv, w_xo, g_x_post, tm):
    b, s, d = x.shape
    tok = lambda n: pl.BlockSpec((1, tm, n), lambda bi, i: (bi, i, 0))
    gvec = _const_spec((1, d))
    return pl.pallas_call(
        _mixout_kernel,
        grid=(b, s // tm),
        in_specs=[tok(DA_WIDTH), tok(POOL_WIDTH), tok(d), _const_spec(w_out.shape), gvec, gvec,
                  _const_spec(w_xq.shape),
                  pl.BlockSpec((1,) + kv.shape[1:], lambda bi, i: (bi, 0, 0)),
                  _const_spec(w_xo.shape), gvec],
        out_specs=tok(d),
        out_shape=jax.ShapeDtypeStruct((b, s, d), F32),
        compiler_params=pltpu.CompilerParams(
            dimension_semantics=("arbitrary", "arbitrary"), vmem_limit_bytes=VMEM_LIMIT),
        name="mixout_xattn",
    )(da, po, x, w_out, g_mix_post, g_x_pre, w_xq, kv, w_xo, g_x_post)


def _ffn_kernel(x_ref, gpre_ref, wg_ref, wu_ref, wd_ref, gpost_ref, o_ref):
    x = x_ref[0]
    hf = _rms(x, gpre_ref[...]).astype(BF16)
    ff = None
    for lo, hi in FF_CHUNKS:
        gate = jnp.dot(hf, wg_ref[:, lo:hi], preferred_element_type=F32)
        up = jnp.dot(hf, wu_ref[:, lo:hi], preferred_element_type=F32)
        act = (gate * (1.0 / (1.0 + jnp.exp(-gate))) * up).astype(BF16)
        part = jnp.dot(act, wd_ref[lo:hi, :], preferred_element_type=F32)
        ff = part if ff is None else ff + part
    o_ref[0] = x + _rms(ff, gpost_ref[...])


def _ffn_call(x, g_pre, w_gate, w_up, w_down, g_post, tm):
    b, s, d = x.shape
    tok = pl.BlockSpec((1, tm, d), lambda bi, i: (bi, i, 0))
    gvec = _const_spec((1, d))
    return pl.pallas_call(
        _ffn_kernel,
        grid=(b, s // tm),
        in_specs=[tok, gvec, _const_spec(w_gate.shape), _const_spec(w_up.shape),
                  _const_spec(w_down.shape), gvec],
        out_specs=tok,
        out_shape=jax.ShapeDtypeStruct((b, s, d), F32),
        compiler_params=pltpu.CompilerParams(
            dimension_semantics=("arbitrary", "arbitrary"), vmem_limit_bytes=VMEM_LIMIT),
        name="swiglu",
    )(x, g_pre, w_gate, w_up, w_down, g_post)


def _rope_lane_constants():
    inv_freq = ROPE_THETA ** (-jnp.arange(0, ROPE_DIM, 2, dtype=F32) / ROPE_DIM)
    zero_half = jnp.zeros((ROPE_HALF,), F32)
    one_half = jnp.ones((ROPE_HALF,), F32)
    rest = jnp.zeros((DA_QK_DIM - ROPE_DIM,), F32)
    per_map = jnp.stack([jnp.concatenate([inv_freq, inv_freq, rest]),
                         jnp.concatenate([-one_half, zero_half, rest]),
                         jnp.concatenate([zero_half, one_half, rest])])
    rows = jnp.concatenate([per_map, per_map], axis=-1)
    return jnp.concatenate([rows, jnp.zeros((5, LANES), F32)], axis=0)


def kernel(x, mem, positions, g_mix_pre, w_in, lambda_q1, lambda_k1, lambda_q2, lambda_k2,
           g_subln, w_pool, pool_scale, w_out, g_mix_post, g_x_pre, g_mem, w_xq, w_xkv, w_xo,
           g_x_post, g_ffn_pre, w_gate, w_up, w_down, g_ffn_post):
    b, s, d = x.shape
    assert d == D_MODEL and s % TOKEN_TILE == 0 and s % ATTN_TILE == 0
    row = lambda a: a.reshape(1, -1).astype(F32)
    bf = lambda a: a.astype(BF16)

    pos = positions.astype(F32).reshape(b, s, 1)
    kv = _kv_call(mem, row(g_mem), bf(w_xkv))
    q, k, v, po = _inproj_call(x, row(g_mix_pre), bf(w_in), pos, _rope_lane_constants(),
                               bf(w_pool), row(pool_scale), TOKEN_TILE)
    da = _attn_call(q, k, v, row(lambda_q1), row(lambda_k1), row(lambda_q2), row(lambda_k2),
                    g_subln.reshape(-1, 1).astype(F32), ATTN_TILE)
    x2 = _mixout_call(da, po, x, bf(w_out), row(g_mix_post), row(g_x_pre), bf(w_xq), kv,
                      bf(w_xo), row(g_x_post), TOKEN_TILE)
    return _ffn_call(x2, row(g_ffn_pre), bf(w_gate), bf(w_up), bf(w_down), row(g_ffn_post),
                     TOKEN_TILE)
```

```python
import functools
import math

import jax
import jax.numpy as jnp
from jax import lax
from jax.experimental import pallas as pl
from jax.experimental.pallas import tpu as pltpu

F32 = jnp.float32
BF16 = jnp.bfloat16

D_MODEL = 1024
MEM_LEN = 256
EPS = 1e-6
DA_HEADS = 4
DA_QK_DIM = 64
DA_V_DIM = 2 * DA_QK_DIM
DA_WIDTH = DA_HEADS * DA_V_DIM
QK_WIDTH = DA_HEADS * 2 * DA_QK_DIM
POOL_WINDOWS = (2, 4, 8, 16)
POOL_WIDTH = D_MODEL - DA_WIDTH
POOL_GROUP_DIM = POOL_WIDTH // len(POOL_WINDOWS)
ROPE_THETA = 500000.0
ROPE_DIM = DA_QK_DIM // 4
ROPE_HALF = ROPE_DIM // 2
X_HEADS = 4
X_HEAD_DIM = D_MODEL // X_HEADS
D_FF = -(-(8 * D_MODEL) // (3 * 256)) * 256
LAMBDA_INIT = 0.8 - 0.6 * math.exp(-0.3 * 0)
LOG2_E = math.log2(math.e)

LANES = 128
POOL_HALO = 16
VMEM_LIMIT = 56 * 1024 * 1024

TOKEN_TILE = 512
ATTN_TILE = 256
FF_CHUNKS = ((0, 1536), (1536, 2816))


def _rms(x, g):
    return x * lax.rsqrt(jnp.mean(x * x, axis=-1, keepdims=True) + EPS) * g


def _const_spec(shape):
    zeros = (0,) * len(shape)
    return pl.BlockSpec(shape, lambda *_: zeros, pipeline_mode=pl.Buffered(1))


def _kv_kernel(mem_ref, g_ref, w_ref, kv_ref):
    mn = _rms(mem_ref[0], g_ref[...]).astype(BF16)
    kv_ref[0] = jnp.dot(mn, w_ref[...], preferred_element_type=F32).astype(BF16)


def _kv_call(mem, g_mem, w_xkv):
    b, m, d = mem.shape
    n = w_xkv.shape[1]
    return pl.pallas_call(
        _kv_kernel,
        grid=(b,),
        in_specs=[pl.BlockSpec((1, m, d), lambda i: (i, 0, 0)),
                  _const_spec((1, d)), _const_spec((d, n))],
        out_specs=pl.BlockSpec((1, m, n), lambda i: (i, 0, 0)),
        out_shape=jax.ShapeDtypeStruct((b, m, n), BF16),
        compiler_params=pltpu.CompilerParams(
            dimension_semantics=("arbitrary",), vmem_limit_bytes=VMEM_LIMIT),
        name="kv_proj",
    )(mem, g_mem, w_xkv)


def _inproj_kernel(x_ref, g_ref, w_ref, pos_ref, freq_ref, wp_ref, ps_ref,
                   q_ref, k_ref, v_ref, po_ref, halo_ref, *, tm):
    i = pl.program_id(1)
    h = _rms(x_ref[0], g_ref[...]).astype(BF16)

    ang = pos_ref[0] * freq_ref[0:1, :]
    c, sn = jnp.cos(ang), jnp.sin(ang)
    sa, sb = sn * freq_ref[1:2, :], sn * freq_ref[2:3, :]

    def rope(t):
        return t * c + pltpu.roll(t, LANES - ROPE_HALF, 1) * sa + pltpu.roll(t, ROPE_HALF, 1) * sb

    qk_scale = DA_QK_DIM ** -0.5 * LOG2_E
    pq = jnp.dot(h, w_ref[:, 0:QK_WIDTH], preferred_element_type=F32)
    for j in range(DA_HEADS):
        sl = slice(j * LANES, (j + 1) * LANES)
        q_ref[0, :, sl] = (rope(pq[:, sl]) * qk_scale).astype(BF16)
    pk = jnp.dot(h, w_ref[:, QK_WIDTH:2 * QK_WIDTH], preferred_element_type=F32)
    for j in range(DA_HEADS):
        sl = slice(j * LANES, (j + 1) * LANES)
        k_ref[0, :, sl] = rope(pk[:, sl]).astype(BF16)
    v_ref[0] = jnp.dot(h, w_ref[:, 2 * QK_WIDTH:2 * QK_WIDTH + DA_WIDTH],
                       preferred_element_type=F32).astype(BF16)
    u = jnp.dot(h, w_ref[:, 2 * QK_WIDTH + DA_WIDTH:], preferred_element_type=F32)

    @pl.when(i == 0)
    def _():
        halo_ref[...] = jnp.zeros_like(halo_ref)

    prev = halo_ref[...]
    halo_ref[...] = u[tm - POOL_HALO:, :]
    tpos = i * tm + lax.broadcasted_iota(jnp.int32, (tm, 1), 0)
    for g, w in enumerate(POOL_WINDOWS):
        sl = slice(g * POOL_GROUP_DIM, (g + 1) * POOL_GROUP_DIM)
        ug = u[:, sl]
        s = jnp.concatenate([prev[:, sl], ug], axis=0)
        d = 1
        while d < w:
            s = s + pltpu.roll(s, d, 0)
            d *= 2
        inv_count = 1.0 / jnp.minimum(tpos + 1, w).astype(F32)
        pooled = s[POOL_HALO:, :] * inv_count - ug
        po = jnp.dot(pooled.astype(BF16), wp_ref[g], preferred_element_type=F32) * ps_ref[:, sl]
        po_ref[0, :, sl] = po.astype(BF16)


def _inproj_call(x, g, w_in, pos, freq, w_pool, pool_scale, tm):
    b, s, d = x.shape
    tok = lambda n: pl.BlockSpec((1, tm, n), lambda bi, i: (bi, i, 0))
    out = jax.ShapeDtypeStruct((b, s, QK_WIDTH), BF16)
    return pl.pallas_call(
        functools.partial(_inproj_kernel, tm=tm),
        grid=(b, s // tm),
        in_specs=[tok(d), _const_spec((1, d)), _const_spec(w_in.shape),
                  tok(1), _const_spec(freq.shape),
                  _const_spec(w_pool.shape), _const_spec((1, POOL_WIDTH))],
        out_specs=[tok(QK_WIDTH), tok(QK_WIDTH), tok(DA_WIDTH), tok(POOL_WIDTH)],
        out_shape=[out, out, out, out],
        scratch_shapes=[pltpu.VMEM((POOL_HALO, POOL_WIDTH), F32)],
        compiler_params=pltpu.CompilerParams(
            dimension_semantics=("arbitrary", "arbitrary"), vmem_limit_bytes=VMEM_LIMIT),
        name="inproj",
    )(x, g, w_in, pos, freq, w_pool, pool_scale)


def _attn_kernel(q_ref, k_ref, v_ref, lq1_ref, lk1_ref, lq2_ref, lk2_ref, gs_ref, o_ref,
                 m_ref, l_ref, acc_ref, qs_ref, s0_ref, s1_ref, p0_ref, p1_ref, *, t, nq):
    lane = lax.broadcasted_iota(jnp.int32, (t, DA_V_DIM), 1)
    key = lax.broadcasted_iota(jnp.int32, (t, t), 0)
    qry = lax.broadcasted_iota(jnp.int32, (t, t), 1)
    causal = jnp.concatenate([key <= qry, key <= qry], axis=1)

    def rows(i):
        return pl.ds(pl.multiple_of(i * t, t), t)

    def stack_maps(i, _):
        q = q_ref[0, rows(i), :]
        zero = jnp.zeros_like(q)
        qs_ref[i] = jnp.concatenate([jnp.where(lane < DA_QK_DIM, q, zero),
                                     jnp.where(lane >= DA_QK_DIM, q, zero)], axis=0)
        return 0

    lax.fori_loop(0, nq, stack_maps, 0)

    def scores(i, j):
        return lax.dot_general(k_ref[0, rows(j), :], qs_ref[i], (((1,), (1,)), ((), ())),
                               preferred_element_type=F32)

    def softmax(i, s_ref, p_ref, diagonal):
        s = s_ref[...]
        if diagonal:
            s = jnp.where(causal, s, -jnp.inf)
            m_new = jnp.max(s, axis=0, keepdims=True)
            p = jnp.exp2(s - m_new)
            alpha = None
            l_ref[i] = jnp.sum(p, axis=0, keepdims=True)
        else:
            m = m_ref[i]
            m_new = jnp.maximum(m, jnp.max(s, axis=0, keepdims=True))
            p = jnp.exp2(s - m_new)
            alpha = jnp.exp2(m - m_new)
            l_ref[i] = alpha * l_ref[i] + jnp.sum(p, axis=0, keepdims=True)
        m_ref[i] = m_new
        p_ref[...] = p.astype(BF16)
        return alpha

    def accumulate(i, j, p_ref, alpha):
        pv = lax.dot_general(v_ref[0, rows(j), :], p_ref[...], (((0,), (0,)), ((), ())),
                             preferred_element_type=F32)
        acc_ref[i] = pv if alpha is None else alpha * acc_ref[i] + pv

    s_bufs, p_bufs = (s0_ref, s1_ref), (p0_ref, p1_ref)

    def pipeline(n_units, unit, diagonal):
        assert n_units % 2 == 0 and n_units >= 2

        def stage(n, slot, alpha_prev):
            s_bufs[1 - slot][...] = scores(*unit(n + 1))
            alpha = softmax(unit(n)[0], s_bufs[slot], p_bufs[slot], diagonal)
            accumulate(*unit(n - 1), p_bufs[1 - slot], alpha_prev)
            return alpha

        s0_ref[...] = scores(*unit(0))
        s1_ref[...] = scores(*unit(1))
        alpha_first = softmax(unit(0)[0], s0_ref, p0_ref, diagonal)

        def body(h, alpha_prev):
            if diagonal:
                stage(2 * h + 2, 0, stage(2 * h + 1, 1, None))
                return alpha_prev
            return stage(2 * h + 2, 0, stage(2 * h + 1, 1, alpha_prev))

        alpha_even = lax.fori_loop(0, n_units // 2 - 1, body, 0 if diagonal else alpha_first)
        alpha_last = softmax(unit(n_units - 1)[0], s1_ref, p1_ref, diagonal)
        accumulate(*unit(n_units - 2), p0_ref, None if diagonal else alpha_even)
        accumulate(*unit(n_units - 1), p1_ref, alpha_last)

    def below_diagonal(n):
        n = jnp.asarray(n, jnp.int32)
        i = 1 + sum((n >= k * (k + 1) // 2).astype(jnp.int32) for k in range(1, nq - 1))
        return i, n - i * (i - 1) // 2

    pipeline(nq, lambda n: (n, n), True)
    pipeline(nq * (nq - 1) // 2, below_diagonal, False)

    lam = (jnp.exp(jnp.sum(lq1_ref[...] * lk1_ref[...], axis=-1, keepdims=True))
           - jnp.exp(jnp.sum(lq2_ref[...] * lk2_ref[...], axis=-1, keepdims=True))
           + LAMBDA_INIT)
    gain = gs_ref[...] * (1.0 - LAMBDA_INIT)
    for i in range(nq):
        o = acc_ref[i] * (1.0 / l_ref[i])
        da = o[:, :t] - lam * o[:, t:]
        inv = lax.rsqrt(jnp.mean(da * da, axis=0, keepdims=True) + EPS)
        o_ref[0, i * t:(i + 1) * t, :] = (da * inv * gain).T.astype(BF16)


def _attn_call(q, k, v, lq1, lk1, lq2, lk2, g_subln_col, t):
    b, s, _ = q.shape
    nq = s // t
    head = pl.BlockSpec((1, s, DA_V_DIM), lambda bi, h: (bi, 0, h))
    vec = _const_spec((1, DA_QK_DIM))
    return pl.pallas_call(
        functools.partial(_attn_kernel, t=t, nq=nq),
        grid=(b, DA_HEADS),
        in_specs=[head, head, head, vec, vec, vec, vec, _const_spec((DA_V_DIM, 1))],
        out_specs=head,
        out_shape=jax.ShapeDtypeStruct((b, s, DA_WIDTH), BF16),
        scratch_shapes=[pltpu.VMEM((nq, 1, 2 * t), F32), pltpu.VMEM((nq, 1, 2 * t), F32),
                        pltpu.VMEM((nq, DA_V_DIM, 2 * t), F32),
                        pltpu.VMEM((nq, 2 * t, DA_V_DIM), BF16),
                        pltpu.VMEM((t, 2 * t), F32), pltpu.VMEM((t, 2 * t), F32),
                        pltpu.VMEM((t, 2 * t), BF16), pltpu.VMEM((t, 2 * t), BF16)],
        compiler_params=pltpu.CompilerParams(
            dimension_semantics=("arbitrary", "arbitrary"), vmem_limit_bytes=VMEM_LIMIT),
        name="diff_attn",
    )(q, k, v, lq1, lk1, lq2, lk2, g_subln_col)


def _mixout_kernel(da_ref, po_ref, x_ref, wo_ref, gpost_ref, gxpre_ref, wxq_ref, kv_ref, wxo_ref,
                   gxpost_ref, o_ref):
    mix = (jnp.dot(da_ref[0], wo_ref[0:DA_WIDTH, :], preferred_element_type=F32)
           + jnp.dot(po_ref[0], wo_ref[DA_WIDTH:, :], preferred_element_type=F32))
    x1 = x_ref[0] + _rms(mix, gpost_ref[...])

    hq = _rms(x1, gxpre_ref[...]).astype(BF16)
    x_scale = X_HEAD_DIM ** -0.5
    xq = (jnp.dot(hq, wxq_ref[...], preferred_element_type=F32) * x_scale).astype(BF16)
    heads = []
    for h in range(X_HEADS):
        sl = slice(h * X_HEAD_DIM, (h + 1) * X_HEAD_DIM)
        kh = kv_ref[0, :, sl]
        vh = kv_ref[0, :, D_MODEL + h * X_HEAD_DIM:D_MODEL + (h + 1) * X_HEAD_DIM]
        sc = lax.dot_general(xq[:, sl], kh, (((1,), (1,)), ((), ())), preferred_element_type=F32)
        p = jnp.exp(sc - jnp.max(sc, axis=-1, keepdims=True))
        pm = (p * (1.0 / jnp.sum(p, axis=-1, keepdims=True))).astype(BF16)
        heads.append(jnp.dot(pm, vh, preferred_element_type=F32).astype(BF16))
    xo = jnp.concatenate(heads, axis=-1)
    y = jnp.dot(xo, wxo_ref[...], preferred_element_type=F32)
    o_ref[0] = x1 + _rms(y, gxpost_ref[...])


def _mixout_call(da, po, x, w_out, g_mix_post, g_x_pre, w_xq, kv, w_xo, g_x_post, tm):
    b, s, d = x.shape
    tok = lambda n: pl.BlockSpec((1, tm, n), lambda bi, i: (bi, i, 0))
    gvec = _const_spec((1, d))
    return pl.pallas_call(
        _mixout_kernel,
        grid=(b, s // tm),
        in_specs=[tok(DA_WIDTH), tok(POOL_WIDTH), tok(d), _const_spec(w_out.shape), gvec, gvec,
                  _const_spec(w_xq.shape),
                  pl.BlockSpec((1,) + kv.shape[1:], lambda bi, i: (bi, 0, 0)),
                  _const_spec(w_xo.shape), gvec],
        out_specs=tok(d),
        out_shape=jax.ShapeDtypeStruct((b, s, d), F32),
        compiler_params=pltpu.CompilerParams(
            dimension_semantics=("arbitrary", "arbitrary"), vmem_limit_bytes=VMEM_LIMIT),
        name="mixout_xattn",
    )(da, po, x, w_out, g_mix_post, g_x_pre, w_xq, kv, w_xo, g_x_post)


def _ffn_kernel(x_ref, gpre_ref, wg_ref, wu_ref, wd_ref, gpost_ref, o_ref):
    x = x_ref[0]
    hf = _rms(x, gpre_ref[...]).astype(BF16)
    ff = None
    for lo, hi in FF_CHUNKS:
        gate = jnp.dot(hf, wg_ref[:, lo:hi], preferred_element_type=F32)
        up = jnp.dot(hf, wu_ref[:, lo:hi], preferred_element_type=F32)
        act = (gate * (1.0 / (1.0 + jnp.exp(-gate))) * up).astype(BF16)
        part = jnp.dot(act, wd_ref[lo:hi, :], preferred_element_type=F32)
        ff = part if ff is None else ff + part
    o_ref[0] = x + _rms(ff, gpost_ref[...])


def _ffn_call(x, g_pre, w_gate, w_up, w_down, g_post, tm):
    b, s, d = x.shape
    tok = pl.BlockSpec((1, tm, d), lambda bi, i: (bi, i, 0))
    gvec = _const_spec((1, d))
    return pl.pallas_call(
        _ffn_kernel,
        grid=(b, s // tm),
        in_specs=[tok, gvec, _const_spec(w_gate.shape), _const_spec(w_up.shape),
                  _const_spec(w_down.shape), gvec],
        out_specs=tok,
        out_shape=jax.ShapeDtypeStruct((b, s, d), F32),
        compiler_params=pltpu.CompilerParams(
            dimension_semantics=("arbitrary", "arbitrary"), vmem_limit_bytes=VMEM_LIMIT),
        name="swiglu",
    )(x, g_pre, w_gate, w_up, w_down, g_post)


def _rope_lane_constants():
    inv_freq = ROPE_THETA ** (-jnp.arange(0, ROPE_DIM, 2, dtype=F32) / ROPE_DIM)
    zero_half = jnp.zeros((ROPE_HALF,), F32)
    one_half = jnp.ones((ROPE_HALF,), F32)
    rest = jnp.zeros((DA_QK_DIM - ROPE_DIM,), F32)
    per_map = jnp.stack([jnp.concatenate([inv_freq, inv_freq, rest]),
                         jnp.concatenate([-one_half, zero_half, rest]),
                         jnp.concatenate([zero_half, one_half, rest])])
    rows = jnp.concatenate([per_map, per_map], axis=-1)
    return jnp.concatenate([rows, jnp.zeros((5, LANES), F32)], axis=0)


def kernel(x, mem, positions, g_mix_pre, w_in, lambda_q1, lambda_k1, lambda_q2, lambda_k2,
           g_subln, w_pool, pool_scale, w_out, g_mix_post, g_x_pre, g_mem, w_xq, w_xkv, w_xo,
           g_x_post, g_ffn_pre, w_gate, w_up, w_down, g_ffn_post):
    b, s, d = x.shape
    assert d == D_MODEL and s % TOKEN_TILE == 0 and s % ATTN_TILE == 0
    row = lambda a: a.reshape(1, -1).astype(F32)
    bf = lambda a: a.astype(BF16)

    pos = positions.astype(F32).reshape(b, s, 1)
    kv = _kv_call(mem, row(g_mem), bf(w_xkv))
    q, k, v, po = _inproj_call(x, row(g_mix_pre), bf(w_in), pos, _rope_lane_constants(),
                               bf(w_pool), row(pool_scale), TOKEN_TILE)
    da = _attn_call(q, k, v, row(lambda_q1), row(lambda_k1), row(lambda_q2), row(lambda_k2),
                    g_subln.reshape(-1, 1).astype(F32), ATTN_TILE)
    x2 = _mixout_call(da, po, x, bf(w_out), row(g_mix_post), row(g_x_pre), bf(w_xq), kv,
                      bf(w_xo), row(g_x_post), TOKEN_TILE)
    return _ffn_call(x2, row(g_ffn_pre), bf(w_gate), bf(w_up), bf(w_down), row(g_ffn_post),
                     TOKEN_TILE)
```

```python
import functools
import math

import jax
import jax.numpy as jnp
from jax import lax
from jax.experimental import pallas as pl
from jax.experimental.pallas import tpu as pltpu

F32 = jnp.float32
BF16 = jnp.bfloat16

D_MODEL = 1024
MEM_LEN = 256
EPS = 1e-6
DA_HEADS = 4
DA_QK_DIM = 64
DA_V_DIM = 2 * DA_QK_DIM
DA_WIDTH = DA_HEADS * DA_V_DIM
QK_WIDTH = DA_HEADS * 2 * DA_QK_DIM
POOL_WINDOWS = (2, 4, 8, 16)
POOL_WIDTH = D_MODEL - DA_WIDTH
POOL_GROUP_DIM = POOL_WIDTH // len(POOL_WINDOWS)
ROPE_THETA = 500000.0
ROPE_DIM = DA_QK_DIM // 4
ROPE_HALF = ROPE_DIM // 2
X_HEADS = 4
X_HEAD_DIM = D_MODEL // X_HEADS
D_FF = -(-(8 * D_MODEL) // (3 * 256)) * 256
LAMBDA_INIT = 0.8 - 0.6 * math.exp(-0.3 * 0)
LOG2_E = math.log2(math.e)

LANES = 128
POOL_HALO = 16
VMEM_LIMIT = 56 * 1024 * 1024

TOKEN_TILE = 512
ATTN_TILE = 256
FF_CHUNKS = ((0, 1536), (1536, 2816))


def _rms(x, g):
    return x * lax.rsqrt(jnp.mean(x * x, axis=-1, keepdims=True) + EPS) * g


def _const_spec(shape):
    zeros = (0,) * len(shape)
    return pl.BlockSpec(shape, lambda *_: zeros, pipeline_mode=pl.Buffered(1))


def _kv_kernel(mem_ref, g_ref, w_ref, kv_ref):
    mn = _rms(mem_ref[0], g_ref[...]).astype(BF16)
    kv_ref[0] = jnp.dot(mn, w_ref[...], preferred_element_type=F32).astype(BF16)


def _kv_call(mem, g_mem, w_xkv):
    b, m, d = mem.shape
    n = w_xkv.shape[1]
    return pl.pallas_call(
        _kv_kernel,
        grid=(b,),
        in_specs=[pl.BlockSpec((1, m, d), lambda i: (i, 0, 0)),
                  _const_spec((1, d)), _const_spec((d, n))],
        out_specs=pl.BlockSpec((1, m, n), lambda i: (i, 0, 0)),
        out_shape=jax.ShapeDtypeStruct((b, m, n), BF16),
        compiler_params=pltpu.CompilerParams(
            dimension_semantics=("arbitrary",), vmem_limit_bytes=VMEM_LIMIT),
        name="kv_proj",
    )(mem, g_mem, w_xkv)


def _inproj_kernel(x_ref, g_ref, w_ref, pos_ref, freq_ref, wp_ref, ps_ref,
                   q_ref, k_ref, v_ref, po_ref, halo_ref, *, tm):
    i = pl.program_id(1)
    h = _rms(x_ref[0], g_ref[...]).astype(BF16)

    ang = pos_ref[0] * freq_ref[0:1, :]
    c, sn = jnp.cos(ang), jnp.sin(ang)
    sa, sb = sn * freq_ref[1:2, :], sn * freq_ref[2:3, :]

    def rope(t):
        return t * c + pltpu.roll(t, LANES - ROPE_HALF, 1) * sa + pltpu.roll(t, ROPE_HALF, 1) * sb

    qk_scale = DA_QK_DIM ** -0.5 * LOG2_E
    pq = jnp.dot(h, w_ref[:, 0:QK_WIDTH], preferred_element_type=F32)
    for j in range(DA_HEADS):
        sl = slice(j * LANES, (j + 1) * LANES)
        q_ref[j] = (rope(pq[:, sl]) * qk_scale).astype(BF16)
    pk = jnp.dot(h, w_ref[:, QK_WIDTH:2 * QK_WIDTH], preferred_element_type=F32)
    for j in range(DA_HEADS):
        sl = slice(j * LANES, (j + 1) * LANES)
        k_ref[j] = rope(pk[:, sl]).astype(BF16)
    pv = jnp.dot(h, w_ref[:, 2 * QK_WIDTH:2 * QK_WIDTH + DA_WIDTH], preferred_element_type=F32)
    for j in range(DA_HEADS):
        v_ref[j] = pv[:, j * DA_V_DIM:(j + 1) * DA_V_DIM].astype(BF16)
    u = jnp.dot(h, w_ref[:, 2 * QK_WIDTH + DA_WIDTH:], preferred_element_type=F32)

    @pl.when(i == 0)
    def _():
        halo_ref[...] = jnp.zeros_like(halo_ref)

    prev = halo_ref[...]
    halo_ref[...] = u[tm - POOL_HALO:, :]
    tpos = i * tm + lax.broadcasted_iota(jnp.int32, (tm, 1), 0)
    for g, w in enumerate(POOL_WINDOWS):
        sl = slice(g * POOL_GROUP_DIM, (g + 1) * POOL_GROUP_DIM)
        ug = u[:, sl]
        s = jnp.concatenate([prev[:, sl], ug], axis=0)
        d = 1
        while d < w:
            s = s + pltpu.roll(s, d, 0)
            d *= 2
        inv_count = 1.0 / jnp.minimum(tpos + 1, w).astype(F32)
        pooled = s[POOL_HALO:, :] * inv_count - ug
        po = jnp.dot(pooled.astype(BF16), wp_ref[g], preferred_element_type=F32) * ps_ref[:, sl]
        po_ref[0, :, sl] = po.astype(BF16)


def _inproj_call(x, g, w_in, pos, freq, w_pool, pool_scale, tm):
    b, s, d = x.shape
    tok = lambda n: pl.BlockSpec((1, tm, n), lambda bi, i: (bi, i, 0))
    heads = pl.BlockSpec((None, DA_HEADS, tm, DA_V_DIM), lambda bi, i: (bi, 0, i, 0))
    head_major = jax.ShapeDtypeStruct((b, DA_HEADS, s, DA_V_DIM), BF16)
    return pl.pallas_call(
        functools.partial(_inproj_kernel, tm=tm),
        grid=(b, s // tm),
        in_specs=[tok(d), _const_spec((1, d)), _const_spec(w_in.shape),
                  tok(1), _const_spec(freq.shape),
                  _const_spec(w_pool.shape), _const_spec((1, POOL_WIDTH))],
        out_specs=[heads, heads, heads, tok(POOL_WIDTH)],
        out_shape=[head_major, head_major, head_major,
                   jax.ShapeDtypeStruct((b, s, POOL_WIDTH), BF16)],
        scratch_shapes=[pltpu.VMEM((POOL_HALO, POOL_WIDTH), F32)],
        compiler_params=pltpu.CompilerParams(
            dimension_semantics=("arbitrary", "arbitrary"), vmem_limit_bytes=VMEM_LIMIT),
        name="inproj",
    )(x, g, w_in, pos, freq, w_pool, pool_scale)


def _attn_kernel(q_ref, k_ref, v_ref, lq1_ref, lk1_ref, lq2_ref, lk2_ref, gs_ref, o_ref,
                 m_ref, l_ref, acc_ref, qs_ref, s0_ref, s1_ref, p0_ref, p1_ref, *, t, nq):
    lane = lax.broadcasted_iota(jnp.int32, (t, DA_V_DIM), 1)
    key = lax.broadcasted_iota(jnp.int32, (t, t), 0)
    qry = lax.broadcasted_iota(jnp.int32, (t, t), 1)
    causal = jnp.concatenate([key <= qry, key <= qry], axis=1)

    def rows(i):
        return pl.ds(pl.multiple_of(i * t, t), t)

    def stack_maps(i, _):
        q = q_ref[rows(i), :]
        zero = jnp.zeros_like(q)
        qs_ref[i] = jnp.concatenate([jnp.where(lane < DA_QK_DIM, q, zero),
                                     jnp.where(lane >= DA_QK_DIM, q, zero)], axis=0)
        return 0

    lax.fori_loop(0, nq, stack_maps, 0)

    def scores(i, j):
        return lax.dot_general(k_ref[rows(j), :], qs_ref[i], (((1,), (1,)), ((), ())),
                               preferred_element_type=F32)

    def softmax(i, s_ref, p_ref, diagonal):
        s = s_ref[...]
        if diagonal:
            s = jnp.where(causal, s, -jnp.inf)
            m_new = jnp.max(s, axis=0, keepdims=True)
            p = jnp.exp2(s - m_new)
            alpha = None
            l_ref[i] = jnp.sum(p, axis=0, keepdims=True)
        else:
            m = m_ref[i]
            m_new = jnp.maximum(m, jnp.max(s, axis=0, keepdims=True))
            p = jnp.exp2(s - m_new)
            alpha = jnp.exp2(m - m_new)
            l_ref[i] = alpha * l_ref[i] + jnp.sum(p, axis=0, keepdims=True)
        m_ref[i] = m_new
        p_ref[...] = p.astype(BF16)
        return alpha

    def accumulate(i, j, p_ref, alpha):
        pv = lax.dot_general(v_ref[rows(j), :], p_ref[...], (((0,), (0,)), ((), ())),
                             preferred_element_type=F32)
        acc_ref[i] = pv if alpha is None else alpha * acc_ref[i] + pv

    s_bufs, p_bufs = (s0_ref, s1_ref), (p0_ref, p1_ref)

    def pipeline(n_units, unit, diagonal):
        assert n_units % 2 == 0 and n_units >= 2

        def stage(n, slot, alpha_prev):
            s_bufs[1 - slot][...] = scores(*unit(n + 1))
            alpha = softmax(unit(n)[0], s_bufs[slot], p_bufs[slot], diagonal)
            accumulate(*unit(n - 1), p_bufs[1 - slot], alpha_prev)
            return alpha

        s0_ref[...] = scores(*unit(0))
        s1_ref[...] = scores(*unit(1))
        alpha_first = softmax(unit(0)[0], s0_ref, p0_ref, diagonal)

        def body(h, alpha_prev):
            if diagonal:
                stage(2 * h + 2, 0, stage(2 * h + 1, 1, None))
                return alpha_prev
            return stage(2 * h + 2, 0, stage(2 * h + 1, 1, alpha_prev))

        alpha_even = lax.fori_loop(0, n_units // 2 - 1, body, 0 if diagonal else alpha_first)
        alpha_last = softmax(unit(n_units - 1)[0], s1_ref, p1_ref, diagonal)
        accumulate(*unit(n_units - 2), p0_ref, None if diagonal else alpha_even)
        accumulate(*unit(n_units - 1), p1_ref, alpha_last)

    def below_diagonal(n):
        n = jnp.asarray(n, jnp.int32)
        i = 1 + sum((n >= k * (k + 1) // 2).astype(jnp.int32) for k in range(1, nq - 1))
        return i, n - i * (i - 1) // 2

    pipeline(nq, lambda n: (n, n), True)
    pipeline(nq * (nq - 1) // 2, below_diagonal, False)

    lam = (jnp.exp(jnp.sum(lq1_ref[...] * lk1_ref[...], axis=-1, keepdims=True))
           - jnp.exp(jnp.sum(lq2_ref[...] * lk2_ref[...], axis=-1, keepdims=True))
           + LAMBDA_INIT)
    gain = gs_ref[...] * (1.0 - LAMBDA_INIT)
    for i in range(nq):
        o = acc_ref[i] * (1.0 / l_ref[i])
        da = o[:, :t] - lam * o[:, t:]
        inv = lax.rsqrt(jnp.mean(da * da, axis=0, keepdims=True) + EPS)
        o_ref[i * t:(i + 1) * t, :] = (da * inv * gain).T.astype(BF16)


def _attn_call(q, k, v, lq1, lk1, lq2, lk2, g_subln_col, t):
    b, _, s, _ = q.shape
    nq = s // t
    head = pl.BlockSpec((None, None, s, DA_V_DIM), lambda bi, h: (bi, h, 0, 0))
    vec = _const_spec((1, DA_QK_DIM))
    return pl.pallas_call(
        functools.partial(_attn_kernel, t=t, nq=nq),
        grid=(b, DA_HEADS),
        in_specs=[head, head, head, vec, vec, vec, vec, _const_spec((DA_V_DIM, 1))],
        out_specs=head,
        out_shape=jax.ShapeDtypeStruct((b, DA_HEADS, s, DA_V_DIM), BF16),
        scratch_shapes=[pltpu.VMEM((nq, 1, 2 * t), F32), pltpu.VMEM((nq, 1, 2 * t), F32),
                        pltpu.VMEM((nq, DA_V_DIM, 2 * t), F32),
                        pltpu.VMEM((nq, 2 * t, DA_V_DIM), BF16),
                        pltpu.VMEM((t, 2 * t), F32), pltpu.VMEM((t, 2 * t), F32),
                        pltpu.VMEM((t, 2 * t), BF16), pltpu.VMEM((t, 2 * t), BF16)],
        compiler_params=pltpu.CompilerParams(
            dimension_semantics=("arbitrary", "arbitrary"), vmem_limit_bytes=VMEM_LIMIT),
        name="diff_attn",
    )(q, k, v, lq1, lk1, lq2, lk2, g_subln_col)


def _mixout_kernel(da_ref, po_ref, x_ref, wo_ref, gpost_ref, gxpre_ref, wxq_ref, kv_ref, wxo_ref,
                   gxpost_ref, o_ref):
    da = jnp.concatenate([da_ref[h] for h in range(DA_HEADS)], axis=-1)
    mix = (jnp.dot(da, wo_ref[0:DA_WIDTH, :], preferred_element_type=F32)
           + jnp.dot(po_ref[0], wo_ref[DA_WIDTH:, :], preferred_element_type=F32))
    x1 = x_ref[0] + _rms(mix, gpost_ref[...])

    hq = _rms(x1, gxpre_ref[...]).astype(BF16)
    x_scale = X_HEAD_DIM ** -0.5
    xq = (jnp.dot(hq, wxq_ref[...], preferred_element_type=F32) * x_scale).astype(BF16)
    heads = []
    for h in range(X_HEADS):
        sl = slice(h * X_HEAD_DIM, (h + 1) * X_HEAD_DIM)
        kh = kv_ref[0, :, sl]
        vh = kv_ref[0, :, D_MODEL + h * X_HEAD_DIM:D_MODEL + (h + 1) * X_HEAD_DIM]
        sc = lax.dot_general(xq[:, sl], kh, (((1,), (1,)), ((), ())), preferred_element_type=F32)
        p = jnp.exp(sc - jnp.max(sc, axis=-1, keepdims=True))
        pm = (p * (1.0 / jnp.sum(p, axis=-1, keepdims=True))).astype(BF16)
        heads.append(jnp.dot(pm, vh, preferred_element_type=F32).astype(BF16))
    xo = jnp.concatenate(heads, axis=-1)
    y = jnp.dot(xo, wxo_ref[...], preferred_element_type=F32)
    o_ref[0] = x1 + _rms(y, gxpost_ref[...])


def _mixout_call(da, po, x, w_out, g_mix_post, g_x_pre, w_xq, kv, w_xo, g_x_post, tm):
    b, s, d = x.shape
    tok = lambda n: pl.BlockSpec((1, tm, n), lambda bi, i: (bi, i, 0))
    gvec = _const_spec((1, d))
    return pl.pallas_call(
        _mixout_kernel,
        grid=(b, s // tm),
        in_specs=[pl.BlockSpec((None, DA_HEADS, tm, DA_V_DIM), lambda bi, i: (bi, 0, i, 0)),
                  tok(POOL_WIDTH), tok(d), _const_spec(w_out.shape), gvec, gvec,
                  _const_spec(w_xq.shape),
                  pl.BlockSpec((1,) + kv.shape[1:], lambda bi, i: (bi, 0, 0)),
                  _const_spec(w_xo.shape), gvec],
        out_specs=tok(d),
        out_shape=jax.ShapeDtypeStruct((b, s, d), F32),
        compiler_params=pltpu.CompilerParams(
            dimension_semantics=("arbitrary", "arbitrary"), vmem_limit_bytes=VMEM_LIMIT),
        name="mixout_xattn",
    )(da, po, x, w_out, g_mix_post, g_x_pre, w_xq, kv, w_xo, g_x_post)


def _ffn_kernel(x_ref, gpre_ref, wg_ref, wu_ref, wd_ref, gpost_ref, o_ref):
    x = x_ref[0]
    hf = _rms(x, gpre_ref[...]).astype(BF16)
    ff = None
    for lo, hi in FF_CHUNKS:
        gate = jnp.dot(hf, wg_ref[:, lo:hi], preferred_element_type=F32)
        up = jnp.dot(hf, wu_ref[:, lo:hi], preferred_element_type=F32)
        act = (gate * (1.0 / (1.0 + jnp.exp(-gate))) * up).astype(BF16)
        part = jnp.dot(act, wd_ref[lo:hi, :], preferred_element_type=F32)
        ff = part if ff is None else ff + part
    o_ref[0] = x + _rms(ff, gpost_ref[...])


def _ffn_call(x, g_pre, w_gate, w_up, w_down, g_post, tm):
    b, s, d = x.shape
    tok = pl.BlockSpec((1, tm, d), lambda bi, i: (bi, i, 0))
    gvec = _const_spec((1, d))
    return pl.pallas_call(
        _ffn_kernel,
        grid=(b, s // tm),
        in_specs=[tok, gvec, _const_spec(w_gate.shape), _const_spec(w_up.shape),
                  _const_spec(w_down.shape), gvec],
        out_specs=tok,
        out_shape=jax.ShapeDtypeStruct((b, s, d), F32),
        compiler_params=pltpu.CompilerParams(
            dimension_semantics=("arbitrary", "arbitrary"), vmem_limit_bytes=VMEM_LIMIT),
        name="swiglu",
    )(x, g_pre, w_gate, w_up, w_down, g_post)


def _rope_lane_constants():
    inv_freq = ROPE_THETA ** (-jnp.arange(0, ROPE_DIM, 2, dtype=F32) / ROPE_DIM)
    zero_half = jnp.zeros((ROPE_HALF,), F32)
    one_half = jnp.ones((ROPE_HALF,), F32)
    rest = jnp.zeros((DA_QK_DIM - ROPE_DIM,), F32)
    per_map = jnp.stack([jnp.concatenate([inv_freq, inv_freq, rest]),
                         jnp.concatenate([-one_half, zero_half, rest]),
                         jnp.concatenate([zero_half, one_half, rest])])
    rows = jnp.concatenate([per_map, per_map], axis=-1)
    return jnp.concatenate([rows, jnp.zeros((5, LANES), F32)], axis=0)


def kernel(x, mem, positions, g_mix_pre, w_in, lambda_q1, lambda_k1, lambda_q2, lambda_k2,
           g_subln, w_pool, pool_scale, w_out, g_mix_post, g_x_pre, g_mem, w_xq, w_xkv, w_xo,
           g_x_post, g_ffn_pre, w_gate, w_up, w_down, g_ffn_post):
    b, s, d = x.shape
    assert d == D_MODEL and s % TOKEN_TILE == 0 and s % ATTN_TILE == 0
    row = lambda a: a.reshape(1, -1).astype(F32)
    bf = lambda a: a.astype(BF16)

    pos = positions.astype(F32).reshape(b, s, 1)
    kv = _kv_call(mem, row(g_mem), bf(w_xkv))
    q, k, v, po = _inproj_call(x, row(g_mix_pre), bf(w_in), pos, _rope_lane_constants(),
                               bf(w_pool), row(pool_scale), TOKEN_TILE)
    da = _attn_call(q, k, v, row(lambda_q1), row(lambda_k1), row(lambda_q2), row(lambda_k2),
                    g_subln.reshape(-1, 1).astype(F32), ATTN_TILE)
    x2 = _mixout_call(da, po, x, bf(w_out), row(g_mix_post), row(g_x_pre), bf(w_xq), kv,
                      bf(w_xo), row(g_x_post), TOKEN_TILE)
    return _ffn_call(x2, row(g_ffn_pre), bf(w_gate), bf(w_up), bf(w_down), row(g_ffn_post),
                     TOKEN_TILE)
```

```python
import functools
import math

import jax
import jax.numpy as jnp
from jax import lax
from jax.experimental import pallas as pl
from jax.experimental.pallas import tpu as pltpu

F32 = jnp.float32
BF16 = jnp.bfloat16

D_MODEL = 1024
MEM_LEN = 256
EPS = 1e-6
DA_HEADS = 4
DA_QK_DIM = 64
DA_V_DIM = 2 * DA_QK_DIM
DA_WIDTH = DA_HEADS * DA_V_DIM
QK_WIDTH = DA_HEADS * 2 * DA_QK_DIM
POOL_WINDOWS = (2, 4, 8, 16)
POOL_WIDTH = D_MODEL - DA_WIDTH
POOL_GROUP_DIM = POOL_WIDTH // len(POOL_WINDOWS)
ROPE_THETA = 500000.0
ROPE_DIM = DA_QK_DIM // 4
ROPE_HALF = ROPE_DIM // 2
X_HEADS = 4
X_HEAD_DIM = D_MODEL // X_HEADS
D_FF = -(-(8 * D_MODEL) // (3 * 256)) * 256
LAMBDA_INIT = 0.8 - 0.6 * math.exp(-0.3 * 0)
LOG2_E = math.log2(math.e)

LANES = 128
POOL_HALO = 16
VMEM_LIMIT = 56 * 1024 * 1024

TOKEN_TILE = 512
ATTN_TILE = 512
FF_CHUNKS = ((0, 1536), (1536, 2816))


def _rms(x, g):
    return x * lax.rsqrt(jnp.mean(x * x, axis=-1, keepdims=True) + EPS) * g


def _const_spec(shape):
    zeros = (0,) * len(shape)
    return pl.BlockSpec(shape, lambda *_: zeros, pipeline_mode=pl.Buffered(1))


def _kv_kernel(mem_ref, g_ref, w_ref, kv_ref):
    mn = _rms(mem_ref[0], g_ref[...]).astype(BF16)
    kv_ref[0] = jnp.dot(mn, w_ref[...], preferred_element_type=F32).astype(BF16)


def _kv_call(mem, g_mem, w_xkv):
    b, m, d = mem.shape
    n = w_xkv.shape[1]
    return pl.pallas_call(
        _kv_kernel,
        grid=(b,),
        in_specs=[pl.BlockSpec((1, m, d), lambda i: (i, 0, 0)),
                  _const_spec((1, d)), _const_spec((d, n))],
        out_specs=pl.BlockSpec((1, m, n), lambda i: (i, 0, 0)),
        out_shape=jax.ShapeDtypeStruct((b, m, n), BF16),
        compiler_params=pltpu.CompilerParams(
            dimension_semantics=("arbitrary",), vmem_limit_bytes=VMEM_LIMIT),
        name="kv_proj",
    )(mem, g_mem, w_xkv)


def _inproj_kernel(x_ref, g_ref, w_ref, pos_ref, freq_ref, wp_ref, ps_ref,
                   q_ref, k_ref, v_ref, po_ref, halo_ref, *, tm):
    i = pl.program_id(1)
    h = _rms(x_ref[0], g_ref[...]).astype(BF16)

    ang = pos_ref[0] * freq_ref[0:1, :]
    c, sn = jnp.cos(ang), jnp.sin(ang)
    sa, sb = sn * freq_ref[1:2, :], sn * freq_ref[2:3, :]

    def rope(t):
        return t * c + pltpu.roll(t, LANES - ROPE_HALF, 1) * sa + pltpu.roll(t, ROPE_HALF, 1) * sb

    qk_scale = DA_QK_DIM ** -0.5 * LOG2_E
    pq = jnp.dot(h, w_ref[:, 0:QK_WIDTH], preferred_element_type=F32)
    for j in range(DA_HEADS):
        sl = slice(j * LANES, (j + 1) * LANES)
        q_ref[j] = (rope(pq[:, sl]) * qk_scale).astype(BF16)
    pk = jnp.dot(h, w_ref[:, QK_WIDTH:2 * QK_WIDTH], preferred_element_type=F32)
    for j in range(DA_HEADS):
        sl = slice(j * LANES, (j + 1) * LANES)
        k_ref[j] = rope(pk[:, sl]).astype(BF16)
    pv = jnp.dot(h, w_ref[:, 2 * QK_WIDTH:2 * QK_WIDTH + DA_WIDTH], preferred_element_type=F32)
    for j in range(DA_HEADS):
        v_ref[j] = pv[:, j * DA_V_DIM:(j + 1) * DA_V_DIM].astype(BF16)
    u = jnp.dot(h, w_ref[:, 2 * QK_WIDTH + DA_WIDTH:], preferred_element_type=F32)

    @pl.when(i == 0)
    def _():
        halo_ref[...] = jnp.zeros_like(halo_ref)

    prev = halo_ref[...]
    halo_ref[...] = u[tm - POOL_HALO:, :]
    tpos = i * tm + lax.broadcasted_iota(jnp.int32, (tm, 1), 0)
    for g, w in enumerate(POOL_WINDOWS):
        sl = slice(g * POOL_GROUP_DIM, (g + 1) * POOL_GROUP_DIM)
        ug = u[:, sl]
        s = jnp.concatenate([prev[:, sl], ug], axis=0)
        d = 1
        while d < w:
            s = s + pltpu.roll(s, d, 0)
            d *= 2
        inv_count = 1.0 / jnp.minimum(tpos + 1, w).astype(F32)
        pooled = s[POOL_HALO:, :] * inv_count - ug
        po = jnp.dot(pooled.astype(BF16), wp_ref[g], preferred_element_type=F32) * ps_ref[:, sl]
        po_ref[0, :, sl] = po.astype(BF16)


def _inproj_call(x, g, w_in, pos, freq, w_pool, pool_scale, tm):
    b, s, d = x.shape
    tok = lambda n: pl.BlockSpec((1, tm, n), lambda bi, i: (bi, i, 0))
    heads = pl.BlockSpec((None, DA_HEADS, tm, DA_V_DIM), lambda bi, i: (bi, 0, i, 0))
    head_major = jax.ShapeDtypeStruct((b, DA_HEADS, s, DA_V_DIM), BF16)
    return pl.pallas_call(
        functools.partial(_inproj_kernel, tm=tm),
        grid=(b, s // tm),
        in_specs=[tok(d), _const_spec((1, d)), _const_spec(w_in.shape),
                  tok(1), _const_spec(freq.shape),
                  _const_spec(w_pool.shape), _const_spec((1, POOL_WIDTH))],
        out_specs=[heads, heads, heads, tok(POOL_WIDTH)],
        out_shape=[head_major, head_major, head_major,
                   jax.ShapeDtypeStruct((b, s, POOL_WIDTH), BF16)],
        scratch_shapes=[pltpu.VMEM((POOL_HALO, POOL_WIDTH), F32)],
        compiler_params=pltpu.CompilerParams(
            dimension_semantics=("arbitrary", "arbitrary"), vmem_limit_bytes=VMEM_LIMIT),
        name="inproj",
    )(x, g, w_in, pos, freq, w_pool, pool_scale)


def _attn_kernel(q_ref, k_ref, v_ref, lq1_ref, lk1_ref, lq2_ref, lk2_ref, gs_ref, o_ref,
                 m_ref, l_ref, acc_ref, qs_ref, s0_ref, s1_ref, p0_ref, p1_ref, *, t, nq):
    lane = lax.broadcasted_iota(jnp.int32, (t, DA_V_DIM), 1)
    key = lax.broadcasted_iota(jnp.int32, (t, t), 0)
    qry = lax.broadcasted_iota(jnp.int32, (t, t), 1)
    causal = jnp.concatenate([key <= qry, key <= qry], axis=1)

    def rows(i):
        return pl.ds(pl.multiple_of(i * t, t), t)

    def stack_maps(i, _):
        q = q_ref[rows(i), :]
        zero = jnp.zeros_like(q)
        qs_ref[i] = jnp.concatenate([jnp.where(lane < DA_QK_DIM, q, zero),
                                     jnp.where(lane >= DA_QK_DIM, q, zero)], axis=0)
        return 0

    lax.fori_loop(0, nq, stack_maps, 0)

    def scores(i, j):
        return lax.dot_general(k_ref[rows(j), :], qs_ref[i], (((1,), (1,)), ((), ())),
                               preferred_element_type=F32)

    def softmax(i, s_ref, p_ref, diagonal):
        s = s_ref[...]
        if diagonal:
            s = jnp.where(causal, s, -jnp.inf)
            m_new = jnp.max(s, axis=0, keepdims=True)
            p = jnp.exp2(s - m_new)
            alpha = None
            l_ref[i] = jnp.sum(p, axis=0, keepdims=True)
        else:
            m = m_ref[i]
            m_new = jnp.maximum(m, jnp.max(s, axis=0, keepdims=True))
            p = jnp.exp2(s - m_new)
            alpha = jnp.exp2(m - m_new)
            l_ref[i] = alpha * l_ref[i] + jnp.sum(p, axis=0, keepdims=True)
        m_ref[i] = m_new
        p_ref[...] = p.astype(BF16)
        return alpha

    def accumulate(i, j, p_ref, alpha):
        pv = lax.dot_general(v_ref[rows(j), :], p_ref[...], (((0,), (0,)), ((), ())),
                             preferred_element_type=F32)
        acc_ref[i] = pv if alpha is None else alpha * acc_ref[i] + pv

    s_bufs, p_bufs = (s0_ref, s1_ref), (p0_ref, p1_ref)

    def pipeline(n_units, unit, diagonal):
        assert n_units % 2 == 0 and n_units >= 2

        def stage(n, slot, alpha_prev):
            s_bufs[1 - slot][...] = scores(*unit(n + 1))
            alpha = softmax(unit(n)[0], s_bufs[slot], p_bufs[slot], diagonal)
            accumulate(*unit(n - 1), p_bufs[1 - slot], alpha_prev)
            return alpha

        s0_ref[...] = scores(*unit(0))
        s1_ref[...] = scores(*unit(1))
        alpha_first = softmax(unit(0)[0], s0_ref, p0_ref, diagonal)

        def body(h, alpha_prev):
            if diagonal:
                stage(2 * h + 2, 0, stage(2 * h + 1, 1, None))
                return alpha_prev
            return stage(2 * h + 2, 0, stage(2 * h + 1, 1, alpha_prev))

        alpha_even = lax.fori_loop(0, n_units // 2 - 1, body, 0 if diagonal else alpha_first)
        alpha_last = softmax(unit(n_units - 1)[0], s1_ref, p1_ref, diagonal)
        accumulate(*unit(n_units - 2), p0_ref, None if diagonal else alpha_even)
        accumulate(*unit(n_units - 1), p1_ref, alpha_last)

    def below_diagonal(n):
        n = jnp.asarray(n, jnp.int32)
        i = 1 + sum((n >= k * (k + 1) // 2).astype(jnp.int32) for k in range(1, nq - 1))
        return i, n - i * (i - 1) // 2

    pipeline(nq, lambda n: (n, n), True)
    pipeline(nq * (nq - 1) // 2, below_diagonal, False)

    lam = (jnp.exp(jnp.sum(lq1_ref[...] * lk1_ref[...], axis=-1, keepdims=True))
           - jnp.exp(jnp.sum(lq2_ref[...] * lk2_ref[...], axis=-1, keepdims=True))
           + LAMBDA_INIT)
    gain = gs_ref[...] * (1.0 - LAMBDA_INIT)
    for i in range(nq):
        o = acc_ref[i] * (1.0 / l_ref[i])
        da = o[:, :t] - lam * o[:, t:]
        inv = lax.rsqrt(jnp.mean(da * da, axis=0, keepdims=True) + EPS)
        o_ref[i * t:(i + 1) * t, :] = (da * inv * gain).T.astype(BF16)


def _attn_call(q, k, v, lq1, lk1, lq2, lk2, g_subln_col, t):
    b, _, s, _ = q.shape
    nq = s // t
    head = pl.BlockSpec((None, None, s, DA_V_DIM), lambda bi, h: (bi, h, 0, 0))
    vec = _const_spec((1, DA_QK_DIM))
    return pl.pallas_call(
        functools.partial(_attn_kernel, t=t, nq=nq),
        grid=(b, DA_HEADS),
        in_specs=[head, head, head, vec, vec, vec, vec, _const_spec((DA_V_DIM, 1))],
        out_specs=head,
        out_shape=jax.ShapeDtypeStruct((b, DA_HEADS, s, DA_V_DIM), BF16),
        scratch_shapes=[pltpu.VMEM((nq, 1, 2 * t), F32), pltpu.VMEM((nq, 1, 2 * t), F32),
                        pltpu.VMEM((nq, DA_V_DIM, 2 * t), F32),
                        pltpu.VMEM((nq, 2 * t, DA_V_DIM), BF16),
                        pltpu.VMEM((t, 2 * t), F32), pltpu.VMEM((t, 2 * t), F32),
                        pltpu.VMEM((t, 2 * t), BF16), pltpu.VMEM((t, 2 * t), BF16)],
        compiler_params=pltpu.CompilerParams(
            dimension_semantics=("arbitrary", "arbitrary"), vmem_limit_bytes=VMEM_LIMIT),
        name="diff_attn",
    )(q, k, v, lq1, lk1, lq2, lk2, g_subln_col)


def _mixout_kernel(da_ref, po_ref, x_ref, wo_ref, gpost_ref, gxpre_ref, wxq_ref, kv_ref, wxo_ref,
                   gxpost_ref, o_ref):
    da = jnp.concatenate([da_ref[h] for h in range(DA_HEADS)], axis=-1)
    mix = (jnp.dot(da, wo_ref[0:DA_WIDTH, :], preferred_element_type=F32)
           + jnp.dot(po_ref[0], wo_ref[DA_WIDTH:, :], preferred_element_type=F32))
    x1 = x_ref[0] + _rms(mix, gpost_ref[...])

    hq = _rms(x1, gxpre_ref[...]).astype(BF16)
    x_scale = X_HEAD_DIM ** -0.5
    xq = (jnp.dot(hq, wxq_ref[...], preferred_element_type=F32) * x_scale).astype(BF16)
    heads = []
    for h in range(X_HEADS):
        sl = slice(h * X_HEAD_DIM, (h + 1) * X_HEAD_DIM)
        kh = kv_ref[0, :, sl]
        vh = kv_ref[0, :, D_MODEL + h * X_HEAD_DIM:D_MODEL + (h + 1) * X_HEAD_DIM]
        sc = lax.dot_general(xq[:, sl], kh, (((1,), (1,)), ((), ())), preferred_element_type=F32)
        p = jnp.exp(sc - jnp.max(sc, axis=-1, keepdims=True))
        pm = (p * (1.0 / jnp.sum(p, axis=-1, keepdims=True))).astype(BF16)
        heads.append(jnp.dot(pm, vh, preferred_element_type=F32).astype(BF16))
    xo = jnp.concatenate(heads, axis=-1)
    y = jnp.dot(xo, wxo_ref[...], preferred_element_type=F32)
    o_ref[0] = x1 + _rms(y, gxpost_ref[...])


def _mixout_call(da, po, x, w_out, g_mix_post, g_x_pre, w_xq, kv, w_xo, g_x_post, tm):
    b, s, d = x.shape
    tok = lambda n: pl.BlockSpec((1, tm, n), lambda bi, i: (bi, i, 0))
    gvec = _const_spec((1, d))
    return pl.pallas_call(
        _mixout_kernel,
        grid=(b, s // tm),
        in_specs=[pl.BlockSpec((None, DA_HEADS, tm, DA_V_DIM), lambda bi, i: (bi, 0, i, 0)),
                  tok(POOL_WIDTH), tok(d), _const_spec(w_out.shape), gvec, gvec,
                  _const_spec(w_xq.shape),
                  pl.BlockSpec((1,) + kv.shape[1:], lambda bi, i: (bi, 0, 0)),
                  _const_spec(w_xo.shape), gvec],
        out_specs=tok(d),
        out_shape=jax.ShapeDtypeStruct((b, s, d), F32),
        compiler_params=pltpu.CompilerParams(
            dimension_semantics=("arbitrary", "arbitrary"), vmem_limit_bytes=VMEM_LIMIT),
        name="mixout_xattn",
    )(da, po, x, w_out, g_mix_post, g_x_pre, w_xq, kv, w_xo, g_x_post)


def _ffn_kernel(x_ref, gpre_ref, wg_ref, wu_ref, wd_ref, gpost_ref, o_ref):
    x = x_ref[0]
    hf = _rms(x, gpre_ref[...]).astype(BF16)
    ff = None
    for lo, hi in FF_CHUNKS:
        gate = jnp.dot(hf, wg_ref[:, lo:hi], preferred_element_type=F32)
        up = jnp.dot(hf, wu_ref[:, lo:hi], preferred_element_type=F32)
        act = (gate * (1.0 / (1.0 + jnp.exp(-gate))) * up).astype(BF16)
        part = jnp.dot(act, wd_ref[lo:hi, :], preferred_element_type=F32)
        ff = part if ff is None else ff + part
    o_ref[0] = x + _rms(ff, gpost_ref[...])


def _ffn_call(x, g_pre, w_gate, w_up, w_down, g_post, tm):
    b, s, d = x.shape
    tok = pl.BlockSpec((1, tm, d), lambda bi, i: (bi, i, 0))
    gvec = _const_spec((1, d))
    return pl.pallas_call(
        _ffn_kernel,
        grid=(b, s // tm),
        in_specs=[tok, gvec, _const_spec(w_gate.shape), _const_spec(w_up.shape),
                  _const_spec(w_down.shape), gvec],
        out_specs=tok,
        out_shape=jax.ShapeDtypeStruct((b, s, d), F32),
        compiler_params=pltpu.CompilerParams(
            dimension_semantics=("arbitrary", "arbitrary"), vmem_limit_bytes=VMEM_LIMIT),
        name="swiglu",
    )(x, g_pre, w_gate, w_up, w_down, g_post)


def _rope_lane_constants():
    inv_freq = ROPE_THETA ** (-jnp.arange(0, ROPE_DIM, 2, dtype=F32) / ROPE_DIM)
    zero_half = jnp.zeros((ROPE_HALF,), F32)
    one_half = jnp.ones((ROPE_HALF,), F32)
    rest = jnp.zeros((DA_QK_DIM - ROPE_DIM,), F32)
    per_map = jnp.stack([jnp.concatenate([inv_freq, inv_freq, rest]),
                         jnp.concatenate([-one_half, zero_half, rest]),
                         jnp.concatenate([zero_half, one_half, rest])])
    rows = jnp.concatenate([per_map, per_map], axis=-1)
    return jnp.concatenate([rows, jnp.zeros((5, LANES), F32)], axis=0)


def kernel(x, mem, positions, g_mix_pre, w_in, lambda_q1, lambda_k1, lambda_q2, lambda_k2,
           g_subln, w_pool, pool_scale, w_out, g_mix_post, g_x_pre, g_mem, w_xq, w_xkv, w_xo,
           g_x_post, g_ffn_pre, w_gate, w_up, w_down, g_ffn_post):
    b, s, d = x.shape
    assert d == D_MODEL and s % TOKEN_TILE == 0 and s % ATTN_TILE == 0
    row = lambda a: a.reshape(1, -1).astype(F32)
    bf = lambda a: a.astype(BF16)

    pos = positions.astype(F32).reshape(b, s, 1)
    kv = _kv_call(mem, row(g_mem), bf(w_xkv))
    q, k, v, po = _inproj_call(x, row(g_mix_pre), bf(w_in), pos, _rope_lane_constants(),
                               bf(w_pool), row(pool_scale), TOKEN_TILE)
    da = _attn_call(q, k, v, row(lambda_q1), row(lambda_k1), row(lambda_q2), row(lambda_k2),
                    g_subln.reshape(-1, 1).astype(F32), ATTN_TILE)
    x2 = _mixout_call(da, po, x, bf(w_out), row(g_mix_post), row(g_x_pre), bf(w_xq), kv,
                      bf(w_xo), row(g_x_post), TOKEN_TILE)
    return _ffn_call(x2, row(g_ffn_pre), bf(w_gate), bf(w_up), bf(w_down), row(g_ffn_post),
                     TOKEN_TILE)
```

```python
import functools
import math

import jax
import jax.numpy as jnp
from jax import lax
from jax.experimental import pallas as pl
from jax.experimental.pallas import tpu as pltpu

F32 = jnp.float32
BF16 = jnp.bfloat16

D_MODEL = 1024
MEM_LEN = 256
EPS = 1e-6
DA_HEADS = 4
DA_QK_DIM = 64
DA_V_DIM = 2 * DA_QK_DIM
DA_WIDTH = DA_HEADS * DA_V_DIM
QK_WIDTH = DA_HEADS * 2 * DA_QK_DIM
POOL_WINDOWS = (2, 4, 8, 16)
POOL_WIDTH = D_MODEL - DA_WIDTH
POOL_GROUP_DIM = POOL_WIDTH // len(POOL_WINDOWS)
ROPE_THETA = 500000.0
ROPE_DIM = DA_QK_DIM // 4
ROPE_HALF = ROPE_DIM // 2
X_HEADS = 4
X_HEAD_DIM = D_MODEL // X_HEADS
D_FF = -(-(8 * D_MODEL) // (3 * 256)) * 256
LAMBDA_INIT = 0.8 - 0.6 * math.exp(-0.3 * 0)
LOG2_E = math.log2(math.e)

LANES = 128
POOL_HALO = 16
VMEM_LIMIT = 56 * 1024 * 1024

TOKEN_TILE = 512
ATTN_TILE = 512
FF_CHUNKS = ((0, 1536), (1536, 2816))
WEIGHT_CHUNKS = 8


def _rms(x, g):
    return x * lax.rsqrt(jnp.mean(x * x, axis=-1, keepdims=True) + EPS) * g


def _const_spec(shape):
    zeros = (0,) * len(shape)
    return pl.BlockSpec(shape, lambda *_: zeros, pipeline_mode=pl.Buffered(1))


_HBM = pl.BlockSpec(memory_space=pl.ANY)


def _stage_scratch(chunk_rows, cols):
    return [pltpu.VMEM((2, chunk_rows, cols), F32), pltpu.SemaphoreType.DMA((2,))]


def _load_weight(w_hbm, w_vmem, stage_ref, sem_ref):
    rows = stage_ref.shape[1]
    n = w_hbm.shape[0] // rows
    assert n * rows == w_hbm.shape[0] and stage_ref.shape[2] == w_hbm.shape[1]

    def copy(c):
        return pltpu.make_async_copy(w_hbm.at[pl.ds(c * rows, rows), :], stage_ref.at[c % 2],
                                     sem_ref.at[c % 2])

    copy(0).start()
    for c in range(n):
        if c + 1 < n:
            copy(c + 1).start()
        copy(c).wait()
        w_vmem[c * rows:(c + 1) * rows, :] = stage_ref[c % 2].astype(BF16)


def _first_step(grid_rank):
    first = pl.program_id(0) == 0
    for axis in range(1, grid_rank):
        first = first & (pl.program_id(axis) == 0)
    return first


def _kv_kernel(mem_ref, g_ref, w_hbm, kv_ref, w_ref, stage, sem):
    @pl.when(_first_step(1))
    def _():
        _load_weight(w_hbm, w_ref, stage, sem)

    mn = _rms(mem_ref[0], g_ref[...]).astype(BF16)
    kv_ref[0] = jnp.dot(mn, w_ref[...], preferred_element_type=F32).astype(BF16)


def _kv_call(mem, g_mem, w_xkv):
    b, m, d = mem.shape
    n = w_xkv.shape[1]
    return pl.pallas_call(
        _kv_kernel,
        grid=(b,),
        in_specs=[pl.BlockSpec((1, m, d), lambda i: (i, 0, 0)),
                  _const_spec((1, d)), _HBM],
        out_specs=pl.BlockSpec((1, m, n), lambda i: (i, 0, 0)),
        out_shape=jax.ShapeDtypeStruct((b, m, n), BF16),
        scratch_shapes=[pltpu.VMEM((d, n), BF16), *_stage_scratch(d // WEIGHT_CHUNKS, n)],
        compiler_params=pltpu.CompilerParams(
            dimension_semantics=("arbitrary",), vmem_limit_bytes=VMEM_LIMIT),
        name="kv_proj",
    )(mem, g_mem, w_xkv)


def _inproj_kernel(x_ref, g_ref, w_hbm, pos_ref, freq_ref, wp_ref, ps_ref,
                   q_ref, k_ref, v_ref, po_ref, halo_ref, w_ref, stage, sem, *, tm):
    @pl.when(_first_step(2))
    def _():
        _load_weight(w_hbm, w_ref, stage, sem)

    i = pl.program_id(1)
    h = _rms(x_ref[0], g_ref[...]).astype(BF16)

    ang = pos_ref[0] * freq_ref[0:1, :]
    c, sn = jnp.cos(ang), jnp.sin(ang)
    sa, sb = sn * freq_ref[1:2, :], sn * freq_ref[2:3, :]

    def rope(t):
        return t * c + pltpu.roll(t, LANES - ROPE_HALF, 1) * sa + pltpu.roll(t, ROPE_HALF, 1) * sb

    qk_scale = DA_QK_DIM ** -0.5 * LOG2_E
    pq = jnp.dot(h, w_ref[:, 0:QK_WIDTH], preferred_element_type=F32)
    for j in range(DA_HEADS):
        sl = slice(j * LANES, (j + 1) * LANES)
        q_ref[j] = (rope(pq[:, sl]) * qk_scale).astype(BF16)
    pk = jnp.dot(h, w_ref[:, QK_WIDTH:2 * QK_WIDTH], preferred_element_type=F32)
    for j in range(DA_HEADS):
        sl = slice(j * LANES, (j + 1) * LANES)
        k_ref[j] = rope(pk[:, sl]).astype(BF16)
    pv = jnp.dot(h, w_ref[:, 2 * QK_WIDTH:2 * QK_WIDTH + DA_WIDTH], preferred_element_type=F32)
    for j in range(DA_HEADS):
        v_ref[j] = pv[:, j * DA_V_DIM:(j + 1) * DA_V_DIM].astype(BF16)
    u = jnp.dot(h, w_ref[:, 2 * QK_WIDTH + DA_WIDTH:], preferred_element_type=F32)

    @pl.when(i == 0)
    def _():
        halo_ref[...] = jnp.zeros_like(halo_ref)

    prev = halo_ref[...]
    halo_ref[...] = u[tm - POOL_HALO:, :]
    tpos = i * tm + lax.broadcasted_iota(jnp.int32, (tm, 1), 0)
    for g, w in enumerate(POOL_WINDOWS):
        sl = slice(g * POOL_GROUP_DIM, (g + 1) * POOL_GROUP_DIM)
        ug = u[:, sl]
        s = jnp.concatenate([prev[:, sl], ug], axis=0)
        d = 1
        while d < w:
            s = s + pltpu.roll(s, d, 0)
            d *= 2
        inv_count = 1.0 / jnp.minimum(tpos + 1, w).astype(F32)
        pooled = s[POOL_HALO:, :] * inv_count - ug
        po = jnp.dot(pooled.astype(BF16), wp_ref[g], preferred_element_type=F32) * ps_ref[:, sl]
        po_ref[0, :, sl] = po.astype(BF16)


def _inproj_call(x, g, w_in, pos, freq, w_pool, pool_scale, tm):
    b, s, d = x.shape
    tok = lambda n: pl.BlockSpec((1, tm, n), lambda bi, i: (bi, i, 0))
    heads = pl.BlockSpec((None, DA_HEADS, tm, DA_V_DIM), lambda bi, i: (bi, 0, i, 0))
    head_major = jax.ShapeDtypeStruct((b, DA_HEADS, s, DA_V_DIM), BF16)
    return pl.pallas_call(
        functools.partial(_inproj_kernel, tm=tm),
        grid=(b, s // tm),
        in_specs=[tok(d), _const_spec((1, d)), _HBM,
                  tok(1), _const_spec(freq.shape),
                  _const_spec(w_pool.shape), _const_spec((1, POOL_WIDTH))],
        out_specs=[heads, heads, heads, tok(POOL_WIDTH)],
        out_shape=[head_major, head_major, head_major,
                   jax.ShapeDtypeStruct((b, s, POOL_WIDTH), BF16)],
        scratch_shapes=[pltpu.VMEM((POOL_HALO, POOL_WIDTH), F32), pltpu.VMEM(w_in.shape, BF16),
                        *_stage_scratch(d // WEIGHT_CHUNKS, w_in.shape[1])],
        compiler_params=pltpu.CompilerParams(
            dimension_semantics=("arbitrary", "arbitrary"), vmem_limit_bytes=VMEM_LIMIT),
        name="inproj",
    )(x, g, w_in, pos, freq, w_pool, pool_scale)


def _attn_kernel(q_ref, k_ref, v_ref, lq1_ref, lk1_ref, lq2_ref, lk2_ref, gs_ref, o_ref,
                 m_ref, l_ref, acc_ref, qs_ref, s0_ref, s1_ref, p0_ref, p1_ref, *, t, nq):
    lane = lax.broadcasted_iota(jnp.int32, (t, DA_V_DIM), 1)
    key = lax.broadcasted_iota(jnp.int32, (t, t), 0)
    qry = lax.broadcasted_iota(jnp.int32, (t, t), 1)
    causal = jnp.concatenate([key <= qry, key <= qry], axis=1)

    def rows(i):
        return pl.ds(pl.multiple_of(i * t, t), t)

    def stack_maps(i, _):
        q = q_ref[rows(i), :]
        zero = jnp.zeros_like(q)
        qs_ref[i] = jnp.concatenate([jnp.where(lane < DA_QK_DIM, q, zero),
                                     jnp.where(lane >= DA_QK_DIM, q, zero)], axis=0)
        return 0

    lax.fori_loop(0, nq, stack_maps, 0)

    def scores(i, j):
        return lax.dot_general(k_ref[rows(j), :], qs_ref[i], (((1,), (1,)), ((), ())),
                               preferred_element_type=F32)

    def softmax(i, s_ref, p_ref, diagonal):
        s = s_ref[...]
        if diagonal:
            s = jnp.where(causal, s, -jnp.inf)
            m_new = jnp.max(s, axis=0, keepdims=True)
            p = jnp.exp2(s - m_new)
            alpha = None
            l_ref[i] = jnp.sum(p, axis=0, keepdims=True)
        else:
            m = m_ref[i]
            m_new = jnp.maximum(m, jnp.max(s, axis=0, keepdims=True))
            p = jnp.exp2(s - m_new)
            alpha = jnp.exp2(m - m_new)
            l_ref[i] = alpha * l_ref[i] + jnp.sum(p, axis=0, keepdims=True)
        m_ref[i] = m_new
        p_ref[...] = p.astype(BF16)
        return alpha

    def accumulate(i, j, p_ref, alpha):
        pv = lax.dot_general(v_ref[rows(j), :], p_ref[...], (((0,), (0,)), ((), ())),
                             preferred_element_type=F32)
        acc_ref[i] = pv if alpha is None else alpha * acc_ref[i] + pv

    s_bufs, p_bufs = (s0_ref, s1_ref), (p0_ref, p1_ref)

    def pipeline(n_units, unit, diagonal):
        assert n_units % 2 == 0 and n_units >= 2

        def stage(n, slot, alpha_prev):
            s_bufs[1 - slot][...] = scores(*unit(n + 1))
            alpha = softmax(unit(n)[0], s_bufs[slot], p_bufs[slot], diagonal)
            accumulate(*unit(n - 1), p_bufs[1 - slot], alpha_prev)
            return alpha

        s0_ref[...] = scores(*unit(0))
        s1_ref[...] = scores(*unit(1))
        alpha_first = softmax(unit(0)[0], s0_ref, p0_ref, diagonal)

        def body(h, alpha_prev):
            if diagonal:
                stage(2 * h + 2, 0, stage(2 * h + 1, 1, None))
                return alpha_prev
            return stage(2 * h + 2, 0, stage(2 * h + 1, 1, alpha_prev))

        alpha_even = lax.fori_loop(0, n_units // 2 - 1, body, 0 if diagonal else alpha_first)
        alpha_last = softmax(unit(n_units - 1)[0], s1_ref, p1_ref, diagonal)
        accumulate(*unit(n_units - 2), p0_ref, None if diagonal else alpha_even)
        accumulate(*unit(n_units - 1), p1_ref, alpha_last)

    def below_diagonal(n):
        n = jnp.asarray(n, jnp.int32)
        i = 1 + sum((n >= k * (k + 1) // 2).astype(jnp.int32) for k in range(1, nq - 1))
        return i, n - i * (i - 1) // 2

    pipeline(nq, lambda n: (n, n), True)
    pipeline(nq * (nq - 1) // 2, below_diagonal, False)

    lam = (jnp.exp(jnp.sum(lq1_ref[...] * lk1_ref[...], axis=-1, keepdims=True))
           - jnp.exp(jnp.sum(lq2_ref[...] * lk2_ref[...], axis=-1, keepdims=True))
           + LAMBDA_INIT)
    gain = gs_ref[...] * (1.0 - LAMBDA_INIT)
    for i in range(nq):
        o = acc_ref[i] * (1.0 / l_ref[i])
        da = o[:, :t] - lam * o[:, t:]
        inv = lax.rsqrt(jnp.mean(da * da, axis=0, keepdims=True) + EPS)
        o_ref[i * t:(i + 1) * t, :] = (da * inv * gain).T.astype(BF16)


def _attn_call(q, k, v, lq1, lk1, lq2, lk2, g_subln_col, t):
    b, _, s, _ = q.shape
    nq = s // t
    head = pl.BlockSpec((None, None, s, DA_V_DIM), lambda bi, h: (bi, h, 0, 0))
    vec = _const_spec((1, DA_QK_DIM))
    return pl.pallas_call(
        functools.partial(_attn_kernel, t=t, nq=nq),
        grid=(b, DA_HEADS),
        in_specs=[head, head, head, vec, vec, vec, vec, _const_spec((DA_V_DIM, 1))],
        out_specs=head,
        out_shape=jax.ShapeDtypeStruct((b, DA_HEADS, s, DA_V_DIM), BF16),
        scratch_shapes=[pltpu.VMEM((nq, 1, 2 * t), F32), pltpu.VMEM((nq, 1, 2 * t), F32),
                        pltpu.VMEM((nq, DA_V_DIM, 2 * t), F32),
                        pltpu.VMEM((nq, 2 * t, DA_V_DIM), BF16),
                        pltpu.VMEM((t, 2 * t), F32), pltpu.VMEM((t, 2 * t), F32),
                        pltpu.VMEM((t, 2 * t), BF16), pltpu.VMEM((t, 2 * t), BF16)],
        compiler_params=pltpu.CompilerParams(
            dimension_semantics=("arbitrary", "arbitrary"), vmem_limit_bytes=VMEM_LIMIT),
        name="diff_attn",
    )(q, k, v, lq1, lk1, lq2, lk2, g_subln_col)


def _mixout_kernel(da_ref, po_ref, x_ref, wo_hbm, gpost_ref, gxpre_ref, wxq_hbm, kv_ref, wxo_hbm,
                   gxpost_ref, o_ref, wo_ref, wxq_ref, wxo_ref, stage, sem):
    @pl.when(_first_step(2))
    def _():
        _load_weight(wo_hbm, wo_ref, stage, sem)
        _load_weight(wxq_hbm, wxq_ref, stage, sem)
        _load_weight(wxo_hbm, wxo_ref, stage, sem)

    da = jnp.concatenate([da_ref[h] for h in range(DA_HEADS)], axis=-1)
    mix = (jnp.dot(da, wo_ref[0:DA_WIDTH, :], preferred_element_type=F32)
           + jnp.dot(po_ref[0], wo_ref[DA_WIDTH:, :], preferred_element_type=F32))
    x1 = x_ref[0] + _rms(mix, gpost_ref[...])

    hq = _rms(x1, gxpre_ref[...]).astype(BF16)
    x_scale = X_HEAD_DIM ** -0.5
    xq = (jnp.dot(hq, wxq_ref[...], preferred_element_type=F32) * x_scale).astype(BF16)
    heads = []
    for h in range(X_HEADS):
        sl = slice(h * X_HEAD_DIM, (h + 1) * X_HEAD_DIM)
        kh = kv_ref[0, :, sl]
        vh = kv_ref[0, :, D_MODEL + h * X_HEAD_DIM:D_MODEL + (h + 1) * X_HEAD_DIM]
        sc = lax.dot_general(xq[:, sl], kh, (((1,), (1,)), ((), ())), preferred_element_type=F32)
        p = jnp.exp(sc - jnp.max(sc, axis=-1, keepdims=True))
        pm = (p * (1.0 / jnp.sum(p, axis=-1, keepdims=True))).astype(BF16)
        heads.append(jnp.dot(pm, vh, preferred_element_type=F32).astype(BF16))
    xo = jnp.concatenate(heads, axis=-1)
    y = jnp.dot(xo, wxo_ref[...], preferred_element_type=F32)
    o_ref[0] = x1 + _rms(y, gxpost_ref[...])


def _mixout_call(da, po, x, w_out, g_mix_post, g_x_pre, w_xq, kv, w_xo, g_x_post, tm):
    b, s, d = x.shape
    tok = lambda n: pl.BlockSpec((1, tm, n), lambda bi, i: (bi, i, 0))
    gvec = _const_spec((1, d))
    return pl.pallas_call(
        _mixout_kernel,
        grid=(b, s // tm),
        in_specs=[pl.BlockSpec((None, DA_HEADS, tm, DA_V_DIM), lambda bi, i: (bi, 0, i, 0)),
                  tok(POOL_WIDTH), tok(d), _HBM, gvec, gvec, _HBM,
                  pl.BlockSpec((1,) + kv.shape[1:], lambda bi, i: (bi, 0, 0)),
                  _HBM, gvec],
        out_specs=tok(d),
        out_shape=jax.ShapeDtypeStruct((b, s, d), F32),
        scratch_shapes=[pltpu.VMEM(w_out.shape, BF16), pltpu.VMEM(w_xq.shape, BF16),
                        pltpu.VMEM(w_xo.shape, BF16), *_stage_scratch(d // WEIGHT_CHUNKS, d)],
        compiler_params=pltpu.CompilerParams(
            dimension_semantics=("arbitrary", "arbitrary"), vmem_limit_bytes=VMEM_LIMIT),
        name="mixout_xattn",
    )(da, po, x, w_out, g_mix_post, g_x_pre, w_xq, kv, w_xo, g_x_post)


def _ffn_kernel(x_ref, gpre_ref, wg_hbm, wu_hbm, wd_hbm, gpost_ref, o_ref,
                wg_ref, wu_ref, wd_ref, wide_stage, wide_sem, tall_stage, tall_sem):
    @pl.when(_first_step(2))
    def _():
        _load_weight(wg_hbm, wg_ref, wide_stage, wide_sem)
        _load_weight(wu_hbm, wu_ref, wide_stage, wide_sem)
        _load_weight(wd_hbm, wd_ref, tall_stage, tall_sem)

    x = x_ref[0]
    hf = _rms(x, gpre_ref[...]).astype(BF16)
    ff = None
    for lo, hi in FF_CHUNKS:
        gate = jnp.dot(hf, wg_ref[:, lo:hi], preferred_element_type=F32)
        up = jnp.dot(hf, wu_ref[:, lo:hi], preferred_element_type=F32)
        act = (gate * (1.0 / (1.0 + jnp.exp(-gate))) * up).astype(BF16)
        part = jnp.dot(act, wd_ref[lo:hi, :], preferred_element_type=F32)
        ff = part if ff is None else ff + part
    o_ref[0] = x + _rms(ff, gpost_ref[...])


def _ffn_call(x, g_pre, w_gate, w_up, w_down, g_post, tm):
    b, s, d = x.shape
    tok = pl.BlockSpec((1, tm, d), lambda bi, i: (bi, i, 0))
    gvec = _const_spec((1, d))
    return pl.pallas_call(
        _ffn_kernel,
        grid=(b, s // tm),
        in_specs=[tok, gvec, _HBM, _HBM, _HBM, gvec],
        out_specs=tok,
        out_shape=jax.ShapeDtypeStruct((b, s, d), F32),
        scratch_shapes=[pltpu.VMEM(w_gate.shape, BF16), pltpu.VMEM(w_up.shape, BF16),
                        pltpu.VMEM(w_down.shape, BF16),
                        *_stage_scratch(d // WEIGHT_CHUNKS, w_gate.shape[1]),
                        *_stage_scratch(w_down.shape[0] // WEIGHT_CHUNKS, d)],
        compiler_params=pltpu.CompilerParams(
            dimension_semantics=("arbitrary", "arbitrary"), vmem_limit_bytes=VMEM_LIMIT),
        name="swiglu",
    )(x, g_pre, w_gate, w_up, w_down, g_post)


def _rope_lane_constants():
    inv_freq = ROPE_THETA ** (-jnp.arange(0, ROPE_DIM, 2, dtype=F32) / ROPE_DIM)
    zero_half = jnp.zeros((ROPE_HALF,), F32)
    one_half = jnp.ones((ROPE_HALF,), F32)
    rest = jnp.zeros((DA_QK_DIM - ROPE_DIM,), F32)
    per_map = jnp.stack([jnp.concatenate([inv_freq, inv_freq, rest]),
                         jnp.concatenate([-one_half, zero_half, rest]),
                         jnp.concatenate([zero_half, one_half, rest])])
    rows = jnp.concatenate([per_map, per_map], axis=-1)
    return jnp.concatenate([rows, jnp.zeros((5, LANES), F32)], axis=0)


def kernel(x, mem, positions, g_mix_pre, w_in, lambda_q1, lambda_k1, lambda_q2, lambda_k2,
           g_subln, w_pool, pool_scale, w_out, g_mix_post, g_x_pre, g_mem, w_xq, w_xkv, w_xo,
           g_x_post, g_ffn_pre, w_gate, w_up, w_down, g_ffn_post):
    b, s, d = x.shape
    assert d == D_MODEL and s % TOKEN_TILE == 0 and s % ATTN_TILE == 0
    row = lambda a: a.reshape(1, -1).astype(F32)
    bf = lambda a: a.astype(BF16)

    pos = positions.astype(F32).reshape(b, s, 1)
    kv = _kv_call(mem, row(g_mem), w_xkv)
    q, k, v, po = _inproj_call(x, row(g_mix_pre), w_in, pos, _rope_lane_constants(),
                               bf(w_pool), row(pool_scale), TOKEN_TILE)
    da = _attn_call(q, k, v, row(lambda_q1), row(lambda_k1), row(lambda_q2), row(lambda_k2),
                    g_subln.reshape(-1, 1).astype(F32), ATTN_TILE)
    x2 = _mixout_call(da, po, x, w_out, row(g_mix_post), row(g_x_pre), w_xq, kv, w_xo,
                      row(g_x_post), TOKEN_TILE)
    return _ffn_call(x2, row(g_ffn_pre), w_gate, w_up, w_down, row(g_ffn_post), TOKEN_TILE)
```

```python
import functools
import math

import jax
import jax.numpy as jnp
from jax import lax
from jax.experimental import pallas as pl
from jax.experimental.pallas import tpu as pltpu

F32 = jnp.float32
BF16 = jnp.bfloat16

D_MODEL = 1024
MEM_LEN = 256
EPS = 1e-6
DA_HEADS = 4
DA_QK_DIM = 64
DA_V_DIM = 2 * DA_QK_DIM
DA_WIDTH = DA_HEADS * DA_V_DIM
QK_WIDTH = DA_HEADS * 2 * DA_QK_DIM
POOL_WINDOWS = (2, 4, 8, 16)
POOL_WIDTH = D_MODEL - DA_WIDTH
POOL_GROUP_DIM = POOL_WIDTH // len(POOL_WINDOWS)
ROPE_THETA = 500000.0
ROPE_DIM = DA_QK_DIM // 4
ROPE_HALF = ROPE_DIM // 2
X_HEADS = 4
X_HEAD_DIM = D_MODEL // X_HEADS
D_FF = -(-(8 * D_MODEL) // (3 * 256)) * 256
LAMBDA_INIT = 0.8 - 0.6 * math.exp(-0.3 * 0)
LOG2_E = math.log2(math.e)

LANES = 128
POOL_HALO = 16
VMEM_LIMIT = 56 * 1024 * 1024

TOKEN_TILE = 512
ATTN_TILE = 512
FF_CHUNKS = ((0, 1536), (1536, 2816))
STAGE_BYTES = 4 * 1024 * 1024


def _rms(x, g):
    return x * lax.rsqrt(jnp.mean(x * x, axis=-1, keepdims=True) + EPS) * g


def _const_spec(shape):
    zeros = (0,) * len(shape)
    return pl.BlockSpec(shape, lambda *_: zeros, pipeline_mode=pl.Buffered(1))


_HBM = pl.BlockSpec(memory_space=pl.ANY)


def _stage_scratch(rows, cols):
    chunks = -(-(rows * cols * 4) // STAGE_BYTES)
    while rows % chunks or (rows // chunks) % 16:
        chunks += 1
    return [pltpu.VMEM((2, rows // chunks, cols), F32), pltpu.SemaphoreType.DMA((2,))]


def _load_weights(weights, stage_ref, sem_ref):
    rows = stage_ref.shape[1]
    jobs = []
    for w_hbm, w_vmem in weights:
        assert w_hbm.shape[0] % rows == 0 and stage_ref.shape[2] == w_hbm.shape[1]
        jobs += [(w_hbm, w_vmem, r) for r in range(0, w_hbm.shape[0], rows)]

    def copy(k):
        w_hbm, _, r = jobs[k]
        return pltpu.make_async_copy(w_hbm.at[pl.ds(r, rows), :], stage_ref.at[k % 2], sem_ref.at[k % 2])

    copy(0).start()
    for k, (_, w_vmem, r) in enumerate(jobs):
        if k + 1 < len(jobs):
            copy(k + 1).start()
        copy(k).wait()
        w_vmem[r:r + rows, :] = stage_ref[k % 2].astype(BF16)


def _first_step(grid_rank):
    first = pl.program_id(0) == 0
    for axis in range(1, grid_rank):
        first = first & (pl.program_id(axis) == 0)
    return first


def _kv_kernel(mem_ref, g_ref, w_hbm, kv_ref, w_ref, stage, sem):
    @pl.when(_first_step(1))
    def _():
        _load_weights([(w_hbm, w_ref)], stage, sem)

    mn = _rms(mem_ref[0], g_ref[...]).astype(BF16)
    kv_ref[0] = jnp.dot(mn, w_ref[...], preferred_element_type=F32).astype(BF16)


def _kv_call(mem, g_mem, w_xkv):
    b, m, d = mem.shape
    n = w_xkv.shape[1]
    return pl.pallas_call(
        _kv_kernel,
        grid=(b,),
        in_specs=[pl.BlockSpec((1, m, d), lambda i: (i, 0, 0)),
                  _const_spec((1, d)), _HBM],
        out_specs=pl.BlockSpec((1, m, n), lambda i: (i, 0, 0)),
        out_shape=jax.ShapeDtypeStruct((b, m, n), BF16),
        scratch_shapes=[pltpu.VMEM((d, n), BF16), *_stage_scratch(d, n)],
        compiler_params=pltpu.CompilerParams(
            dimension_semantics=("arbitrary",), vmem_limit_bytes=VMEM_LIMIT),
        name="kv_proj",
    )(mem, g_mem, w_xkv)


def _inproj_kernel(x_ref, g_ref, w_hbm, pos_ref, freq_ref, wp_ref, ps_ref,
                   q_ref, k_ref, v_ref, po_ref, halo_ref, w_ref, stage, sem, *, tm):
    @pl.when(_first_step(2))
    def _():
        _load_weights([(w_hbm, w_ref)], stage, sem)

    i = pl.program_id(1)
    h = _rms(x_ref[0], g_ref[...]).astype(BF16)

    ang = pos_ref[0] * freq_ref[0:1, :]
    c, sn = jnp.cos(ang), jnp.sin(ang)
    sa, sb = sn * freq_ref[1:2, :], sn * freq_ref[2:3, :]

    def rope(t):
        return t * c + pltpu.roll(t, LANES - ROPE_HALF, 1) * sa + pltpu.roll(t, ROPE_HALF, 1) * sb

    qk_scale = DA_QK_DIM ** -0.5 * LOG2_E
    pq = jnp.dot(h, w_ref[:, 0:QK_WIDTH], preferred_element_type=F32)
    for j in range(DA_HEADS):
        sl = slice(j * LANES, (j + 1) * LANES)
        q_ref[j] = (rope(pq[:, sl]) * qk_scale).astype(BF16)
    pk = jnp.dot(h, w_ref[:, QK_WIDTH:2 * QK_WIDTH], preferred_element_type=F32)
    for j in range(DA_HEADS):
        sl = slice(j * LANES, (j + 1) * LANES)
        k_ref[j] = rope(pk[:, sl]).astype(BF16)
    pv = jnp.dot(h, w_ref[:, 2 * QK_WIDTH:2 * QK_WIDTH + DA_WIDTH], preferred_element_type=F32)
    for j in range(DA_HEADS):
        v_ref[j] = pv[:, j * DA_V_DIM:(j + 1) * DA_V_DIM].astype(BF16)
    u = jnp.dot(h, w_ref[:, 2 * QK_WIDTH + DA_WIDTH:], preferred_element_type=F32)

    @pl.when(i == 0)
    def _():
        halo_ref[...] = jnp.zeros_like(halo_ref)

    prev = halo_ref[...]
    halo_ref[...] = u[tm - POOL_HALO:, :]
    tpos = i * tm + lax.broadcasted_iota(jnp.int32, (tm, 1), 0)
    for g, w in enumerate(POOL_WINDOWS):
        sl = slice(g * POOL_GROUP_DIM, (g + 1) * POOL_GROUP_DIM)
        ug = u[:, sl]
        s = jnp.concatenate([prev[:, sl], ug], axis=0)
        d = 1
        while d < w:
            s = s + pltpu.roll(s, d, 0)
            d *= 2
        inv_count = 1.0 / jnp.minimum(tpos + 1, w).astype(F32)
        pooled = s[POOL_HALO:, :] * inv_count - ug
        po = jnp.dot(pooled.astype(BF16), wp_ref[g], preferred_element_type=F32) * ps_ref[:, sl]
        po_ref[0, :, sl] = po.astype(BF16)


def _inproj_call(x, g, w_in, pos, freq, w_pool, pool_scale, tm):
    b, s, d = x.shape
    tok = lambda n: pl.BlockSpec((1, tm, n), lambda bi, i: (bi, i, 0))
    heads = pl.BlockSpec((None, DA_HEADS, tm, DA_V_DIM), lambda bi, i: (bi, 0, i, 0))
    head_major = jax.ShapeDtypeStruct((b, DA_HEADS, s, DA_V_DIM), BF16)
    return pl.pallas_call(
        functools.partial(_inproj_kernel, tm=tm),
        grid=(b, s // tm),
        in_specs=[tok(d), _const_spec((1, d)), _HBM,
                  tok(1), _const_spec(freq.shape),
                  _const_spec(w_pool.shape), _const_spec((1, POOL_WIDTH))],
        out_specs=[heads, heads, heads, tok(POOL_WIDTH)],
        out_shape=[head_major, head_major, head_major,
                   jax.ShapeDtypeStruct((b, s, POOL_WIDTH), BF16)],
        scratch_shapes=[pltpu.VMEM((POOL_HALO, POOL_WIDTH), F32), pltpu.VMEM(w_in.shape, BF16),
                        *_stage_scratch(d, w_in.shape[1])],
        compiler_params=pltpu.CompilerParams(
            dimension_semantics=("arbitrary", "arbitrary"), vmem_limit_bytes=VMEM_LIMIT),
        name="inproj",
    )(x, g, w_in, pos, freq, w_pool, pool_scale)


def _attn_kernel(q_ref, k_ref, v_ref, lq1_ref, lk1_ref, lq2_ref, lk2_ref, gs_ref, o_ref,
                 m_ref, l_ref, acc_ref, qs_ref, s0_ref, s1_ref, p0_ref, p1_ref, *, t, nq):
    lane = lax.broadcasted_iota(jnp.int32, (t, DA_V_DIM), 1)
    key = lax.broadcasted_iota(jnp.int32, (t, t), 0)
    qry = lax.broadcasted_iota(jnp.int32, (t, t), 1)
    causal = jnp.concatenate([key <= qry, key <= qry], axis=1)

    def rows(i):
        return pl.ds(pl.multiple_of(i * t, t), t)

    def stack_maps(i, _):
        q = q_ref[rows(i), :]
        zero = jnp.zeros_like(q)
        qs_ref[i] = jnp.concatenate([jnp.where(lane < DA_QK_DIM, q, zero),
                                     jnp.where(lane >= DA_QK_DIM, q, zero)], axis=0)
        return 0

    lax.fori_loop(0, nq, stack_maps, 0)

    def scores(i, j):
        return lax.dot_general(k_ref[rows(j), :], qs_ref[i], (((1,), (1,)), ((), ())),
                               preferred_element_type=F32)

    def softmax(i, s_ref, p_ref, diagonal):
        s = s_ref[...]
        if diagonal:
            s = jnp.where(causal, s, -jnp.inf)
            m_new = jnp.max(s, axis=0, keepdims=True)
            p = jnp.exp2(s - m_new)
            alpha = None
            l_ref[i] = jnp.sum(p, axis=0, keepdims=True)
        else:
            m = m_ref[i]
            m_new = jnp.maximum(m, jnp.max(s, axis=0, keepdims=True))
            p = jnp.exp2(s - m_new)
            alpha = jnp.exp2(m - m_new)
            l_ref[i] = alpha * l_ref[i] + jnp.sum(p, axis=0, keepdims=True)
        m_ref[i] = m_new
        p_ref[...] = p.astype(BF16)
        return alpha

    def accumulate(i, j, p_ref, alpha):
        pv = lax.dot_general(v_ref[rows(j), :], p_ref[...], (((0,), (0,)), ((), ())),
                             preferred_element_type=F32)
        acc_ref[i] = pv if alpha is None else alpha * acc_ref[i] + pv

    s_bufs, p_bufs = (s0_ref, s1_ref), (p0_ref, p1_ref)

    def pipeline(n_units, unit, diagonal):
        assert n_units % 2 == 0 and n_units >= 2

        def stage(n, slot, alpha_prev):
            s_bufs[1 - slot][...] = scores(*unit(n + 1))
            alpha = softmax(unit(n)[0], s_bufs[slot], p_bufs[slot], diagonal)
            accumulate(*unit(n - 1), p_bufs[1 - slot], alpha_prev)
            return alpha

        s0_ref[...] = scores(*unit(0))
        s1_ref[...] = scores(*unit(1))
        alpha_first = softmax(unit(0)[0], s0_ref, p0_ref, diagonal)

        def body(h, alpha_prev):
            if diagonal:
                stage(2 * h + 2, 0, stage(2 * h + 1, 1, None))
                return alpha_prev
            return stage(2 * h + 2, 0, stage(2 * h + 1, 1, alpha_prev))

        alpha_even = lax.fori_loop(0, n_units // 2 - 1, body, 0 if diagonal else alpha_first)
        alpha_last = softmax(unit(n_units - 1)[0], s1_ref, p1_ref, diagonal)
        accumulate(*unit(n_units - 2), p0_ref, None if diagonal else alpha_even)
        accumulate(*unit(n_units - 1), p1_ref, alpha_last)

    def below_diagonal(n):
        n = jnp.asarray(n, jnp.int32)
        i = 1 + sum((n >= k * (k + 1) // 2).astype(jnp.int32) for k in range(1, nq - 1))
        return i, n - i * (i - 1) // 2

    pipeline(nq, lambda n: (n, n), True)
    pipeline(nq * (nq - 1) // 2, below_diagonal, False)

    lam = (jnp.exp(jnp.sum(lq1_ref[...] * lk1_ref[...], axis=-1, keepdims=True))
           - jnp.exp(jnp.sum(lq2_ref[...] * lk2_ref[...], axis=-1, keepdims=True))
           + LAMBDA_INIT)
    gain = gs_ref[...] * (1.0 - LAMBDA_INIT)
    for i in range(nq):
        o = acc_ref[i] * (1.0 / l_ref[i])
        da = o[:, :t] - lam * o[:, t:]
        inv = lax.rsqrt(jnp.mean(da * da, axis=0, keepdims=True) + EPS)
        o_ref[i * t:(i + 1) * t, :] = (da * inv * gain).T.astype(BF16)


def _attn_call(q, k, v, lq1, lk1, lq2, lk2, g_subln_col, t):
    b, _, s, _ = q.shape
    nq = s // t
    head = pl.BlockSpec((None, None, s, DA_V_DIM), lambda bi, h: (bi, h, 0, 0))
    vec = _const_spec((1, DA_QK_DIM))
    return pl.pallas_call(
        functools.partial(_attn_kernel, t=t, nq=nq),
        grid=(b, DA_HEADS),
        in_specs=[head, head, head, vec, vec, vec, vec, _const_spec((DA_V_DIM, 1))],
        out_specs=head,
        out_shape=jax.ShapeDtypeStruct((b, DA_HEADS, s, DA_V_DIM), BF16),
        scratch_shapes=[pltpu.VMEM((nq, 1, 2 * t), F32), pltpu.VMEM((nq, 1, 2 * t), F32),
                        pltpu.VMEM((nq, DA_V_DIM, 2 * t), F32),
                        pltpu.VMEM((nq, 2 * t, DA_V_DIM), BF16),
                        pltpu.VMEM((t, 2 * t), F32), pltpu.VMEM((t, 2 * t), F32),
                        pltpu.VMEM((t, 2 * t), BF16), pltpu.VMEM((t, 2 * t), BF16)],
        compiler_params=pltpu.CompilerParams(
            dimension_semantics=("arbitrary", "arbitrary"), vmem_limit_bytes=VMEM_LIMIT),
        name="diff_attn",
    )(q, k, v, lq1, lk1, lq2, lk2, g_subln_col)


def _mixout_kernel(da_ref, po_ref, x_ref, wo_hbm, gpost_ref, gxpre_ref, wxq_hbm, kv_ref, wxo_hbm,
                   gxpost_ref, o_ref, wo_ref, wxq_ref, wxo_ref, stage, sem):
    @pl.when(_first_step(2))
    def _():
        _load_weights([(wo_hbm, wo_ref), (wxq_hbm, wxq_ref), (wxo_hbm, wxo_ref)], stage, sem)

    da = jnp.concatenate([da_ref[h] for h in range(DA_HEADS)], axis=-1)
    mix = (jnp.dot(da, wo_ref[0:DA_WIDTH, :], preferred_element_type=F32)
           + jnp.dot(po_ref[0], wo_ref[DA_WIDTH:, :], preferred_element_type=F32))
    x1 = x_ref[0] + _rms(mix, gpost_ref[...])

    hq = _rms(x1, gxpre_ref[...]).astype(BF16)
    x_scale = X_HEAD_DIM ** -0.5
    xq = (jnp.dot(hq, wxq_ref[...], preferred_element_type=F32) * x_scale).astype(BF16)
    heads = []
    for h in range(X_HEADS):
        sl = slice(h * X_HEAD_DIM, (h + 1) * X_HEAD_DIM)
        kh = kv_ref[0, :, sl]
        vh = kv_ref[0, :, D_MODEL + h * X_HEAD_DIM:D_MODEL + (h + 1) * X_HEAD_DIM]
        sc = lax.dot_general(xq[:, sl], kh, (((1,), (1,)), ((), ())), preferred_element_type=F32)
        p = jnp.exp(sc - jnp.max(sc, axis=-1, keepdims=True))
        pm = (p * (1.0 / jnp.sum(p, axis=-1, keepdims=True))).astype(BF16)
        heads.append(jnp.dot(pm, vh, preferred_element_type=F32).astype(BF16))
    xo = jnp.concatenate(heads, axis=-1)
    y = jnp.dot(xo, wxo_ref[...], preferred_element_type=F32)
    o_ref[0] = x1 + _rms(y, gxpost_ref[...])


def _mixout_call(da, po, x, w_out, g_mix_post, g_x_pre, w_xq, kv, w_xo, g_x_post, tm):
    b, s, d = x.shape
    tok = lambda n: pl.BlockSpec((1, tm, n), lambda bi, i: (bi, i, 0))
    gvec = _const_spec((1, d))
    return pl.pallas_call(
        _mixout_kernel,
        grid=(b, s // tm),
        in_specs=[pl.BlockSpec((None, DA_HEADS, tm, DA_V_DIM), lambda bi, i: (bi, 0, i, 0)),
                  tok(POOL_WIDTH), tok(d), _HBM, gvec, gvec, _HBM,
                  pl.BlockSpec((1,) + kv.shape[1:], lambda bi, i: (bi, 0, 0)),
                  _HBM, gvec],
        out_specs=tok(d),
        out_shape=jax.ShapeDtypeStruct((b, s, d), F32),
        scratch_shapes=[pltpu.VMEM(w_out.shape, BF16), pltpu.VMEM(w_xq.shape, BF16),
                        pltpu.VMEM(w_xo.shape, BF16), *_stage_scratch(d, d)],
        compiler_params=pltpu.CompilerParams(
            dimension_semantics=("arbitrary", "arbitrary"), vmem_limit_bytes=VMEM_LIMIT),
        name="mixout_xattn",
    )(da, po, x, w_out, g_mix_post, g_x_pre, w_xq, kv, w_xo, g_x_post)


def _ffn_kernel(x_ref, gpre_ref, wg_hbm, wu_hbm, wd_hbm, gpost_ref, o_ref,
                wg_ref, wu_ref, wd_ref, wide_stage, wide_sem, tall_stage, tall_sem):
    @pl.when(_first_step(2))
    def _():
        _load_weights([(wg_hbm, wg_ref), (wu_hbm, wu_ref)], wide_stage, wide_sem)
        _load_weights([(wd_hbm, wd_ref)], tall_stage, tall_sem)

    x = x_ref[0]
    hf = _rms(x, gpre_ref[...]).astype(BF16)
    ff = None
    for lo, hi in FF_CHUNKS:
        gate = jnp.dot(hf, wg_ref[:, lo:hi], preferred_element_type=F32)
        up = jnp.dot(hf, wu_ref[:, lo:hi], preferred_element_type=F32)
        act = (gate * (1.0 / (1.0 + jnp.exp(-gate))) * up).astype(BF16)
        part = jnp.dot(act, wd_ref[lo:hi, :], preferred_element_type=F32)
        ff = part if ff is None else ff + part
    o_ref[0] = x + _rms(ff, gpost_ref[...])


def _ffn_call(x, g_pre, w_gate, w_up, w_down, g_post, tm):
    b, s, d = x.shape
    tok = pl.BlockSpec((1, tm, d), lambda bi, i: (bi, i, 0))
    gvec = _const_spec((1, d))
    return pl.pallas_call(
        _ffn_kernel,
        grid=(b, s // tm),
        in_specs=[tok, gvec, _HBM, _HBM, _HBM, gvec],
        out_specs=tok,
        out_shape=jax.ShapeDtypeStruct((b, s, d), F32),
        scratch_shapes=[pltpu.VMEM(w_gate.shape, BF16), pltpu.VMEM(w_up.shape, BF16),
                        pltpu.VMEM(w_down.shape, BF16),
                        *_stage_scratch(d, w_gate.shape[1]),
                        *_stage_scratch(w_down.shape[0], d)],
        compiler_params=pltpu.CompilerParams(
            dimension_semantics=("arbitrary", "arbitrary"), vmem_limit_bytes=VMEM_LIMIT),
        name="swiglu",
    )(x, g_pre, w_gate, w_up, w_down, g_post)


def _rope_lane_constants():
    inv_freq = ROPE_THETA ** (-jnp.arange(0, ROPE_DIM, 2, dtype=F32) / ROPE_DIM)
    zero_half = jnp.zeros((ROPE_HALF,), F32)
    one_half = jnp.ones((ROPE_HALF,), F32)
    rest = jnp.zeros((DA_QK_DIM - ROPE_DIM,), F32)
    per_map = jnp.stack([jnp.concatenate([inv_freq, inv_freq, rest]),
                         jnp.concatenate([-one_half, zero_half, rest]),
                         jnp.concatenate([zero_half, one_half, rest])])
    rows = jnp.concatenate([per_map, per_map], axis=-1)
    return jnp.concatenate([rows, jnp.zeros((5, LANES), F32)], axis=0)


def kernel(x, mem, positions, g_mix_pre, w_in, lambda_q1, lambda_k1, lambda_q2, lambda_k2,
           g_subln, w_pool, pool_scale, w_out, g_mix_post, g_x_pre, g_mem, w_xq, w_xkv, w_xo,
           g_x_post, g_ffn_pre, w_gate, w_up, w_down, g_ffn_post):
    b, s, d = x.shape
    assert d == D_MODEL and s % TOKEN_TILE == 0 and s % ATTN_TILE == 0
    row = lambda a: a.reshape(1, -1).astype(F32)
    bf = lambda a: a.astype(BF16)

    pos = positions.astype(F32).reshape(b, s, 1)
    kv = _kv_call(mem, row(g_mem), w_xkv)
    q, k, v, po = _inproj_call(x, row(g_mix_pre), w_in, pos, _rope_lane_constants(),
                               bf(w_pool), row(pool_scale), TOKEN_TILE)
    da = _attn_call(q, k, v, row(lambda_q1), row(lambda_k1), row(lambda_q2), row(lambda_k2),
                    g_subln.reshape(-1, 1).astype(F32), ATTN_TILE)
    x2 = _mixout_call(da, po, x, w_out, row(g_mix_post), row(g_x_pre), w_xq, kv, w_xo,
                      row(g_x_post), TOKEN_TILE)
    return _ffn_call(x2, row(g_ffn_pre), w_gate, w_up, w_down, row(g_ffn_post), TOKEN_TILE)
```

```python
import functools
import math

import jax
import jax.numpy as jnp
from jax import lax
from jax.experimental import pallas as pl
from jax.experimental.pallas import tpu as pltpu

F32 = jnp.float32
BF16 = jnp.bfloat16

D_MODEL = 1024
MEM_LEN = 256
EPS = 1e-6
DA_HEADS = 4
DA_QK_DIM = 64
DA_V_DIM = 2 * DA_QK_DIM
DA_WIDTH = DA_HEADS * DA_V_DIM
QK_WIDTH = DA_HEADS * 2 * DA_QK_DIM
POOL_WINDOWS = (2, 4, 8, 16)
POOL_WIDTH = D_MODEL - DA_WIDTH
POOL_GROUP_DIM = POOL_WIDTH // len(POOL_WINDOWS)
ROPE_THETA = 500000.0
ROPE_DIM = DA_QK_DIM // 4
ROPE_HALF = ROPE_DIM // 2
X_HEADS = 4
X_HEAD_DIM = D_MODEL // X_HEADS
D_FF = -(-(8 * D_MODEL) // (3 * 256)) * 256
LAMBDA_INIT = 0.8 - 0.6 * math.exp(-0.3 * 0)
LOG2_E = math.log2(math.e)

LANES = 128
POOL_HALO = 16
VMEM_LIMIT = 56 * 1024 * 1024

TOKEN_TILE = 512
ATTN_TILE = 512
FF_CHUNKS = ((0, 1536), (1536, 2816))
STAGE_BYTES = 4 * 1024 * 1024


def _rms(x, g):
    return x * lax.rsqrt(jnp.mean(x * x, axis=-1, keepdims=True) + EPS) * g


def _const_spec(shape):
    zeros = (0,) * len(shape)
    return pl.BlockSpec(shape, lambda *_: zeros, pipeline_mode=pl.Buffered(1))


_HBM = pl.BlockSpec(memory_space=pl.ANY)


def _stage_scratch(rows, cols):
    chunks = -(-(rows * cols * 4) // STAGE_BYTES)
    while rows % chunks or (rows // chunks) % 16:
        chunks += 1
    return [pltpu.VMEM((2, rows // chunks, cols), F32), pltpu.SemaphoreType.DMA((2,))]


def _load_weights(weights, stage_ref, sem_ref):
    rows = stage_ref.shape[1]
    jobs = []
    for w_hbm, w_vmem in weights:
        assert w_hbm.shape[0] % rows == 0 and stage_ref.shape[2] == w_hbm.shape[1]
        jobs += [(w_hbm, w_vmem, r) for r in range(0, w_hbm.shape[0], rows)]

    def copy(k):
        w_hbm, _, r = jobs[k]
        return pltpu.make_async_copy(w_hbm.at[pl.ds(r, rows), :], stage_ref.at[k % 2], sem_ref.at[k % 2])

    copy(0).start()
    for k, (_, w_vmem, r) in enumerate(jobs):
        if k + 1 < len(jobs):
            copy(k + 1).start()
        copy(k).wait()
        w_vmem[r:r + rows, :] = stage_ref[k % 2].astype(BF16)


def _first_step(grid_rank):
    first = pl.program_id(0) == 0
    for axis in range(1, grid_rank):
        first = first & (pl.program_id(axis) == 0)
    return first


def _kv_kernel(mem_ref, g_ref, w_hbm, kv_ref, w_ref, stage, sem):
    @pl.when(_first_step(1))
    def _():
        _load_weights([(w_hbm, w_ref)], stage, sem)

    mn = _rms(mem_ref[0], g_ref[...]).astype(BF16)
    kv_ref[0] = jnp.dot(mn, w_ref[...], preferred_element_type=F32).astype(BF16)


def _kv_call(mem, g_mem, w_xkv):
    b, m, d = mem.shape
    n = w_xkv.shape[1]
    return pl.pallas_call(
        _kv_kernel,
        grid=(b,),
        in_specs=[pl.BlockSpec((1, m, d), lambda i: (i, 0, 0)),
                  _const_spec((1, d)), _HBM],
        out_specs=pl.BlockSpec((1, m, n), lambda i: (i, 0, 0)),
        out_shape=jax.ShapeDtypeStruct((b, m, n), BF16),
        scratch_shapes=[pltpu.VMEM((d, n), BF16), *_stage_scratch(d, n)],
        compiler_params=pltpu.CompilerParams(
            dimension_semantics=("arbitrary",), vmem_limit_bytes=VMEM_LIMIT),
        name="kv_proj",
    )(mem, g_mem, w_xkv)


def _inproj_kernel(x_ref, g_ref, w_hbm, pos_ref, freq_ref, wp_ref, ps_ref,
                   q_ref, k_ref, v_ref, po_ref, halo_ref, w_ref, stage, sem, *, tm):
    @pl.when(_first_step(2))
    def _():
        _load_weights([(w_hbm, w_ref)], stage, sem)

    i = pl.program_id(1)
    h = _rms(x_ref[0], g_ref[...]).astype(BF16)

    groups = LANES // ROPE_DIM
    rows = tm // groups
    lane = lax.broadcasted_iota(jnp.int32, (rows, LANES), 1)
    lane_group = lax.shift_right_logical(lane, int(math.log2(ROPE_DIM)))
    pos = pos_ref[0]
    packed = jnp.zeros((rows, LANES), F32)
    for g in range(groups):
        packed = jnp.where(lane_group == g, pos[g * rows:(g + 1) * rows], packed)
    ang = packed * freq_ref[0:1, :]
    cos8, sin8 = jnp.cos(ang), jnp.sin(ang)
    first_map = lane < DA_QK_DIM
    rotary = freq_ref[1:2, :] != 0.0

    def spread(packed_table, g):
        shift = (LANES - ROPE_DIM * g) % LANES
        lo = pltpu.roll(packed_table, shift, 1) if shift else packed_table
        hi_shift = (shift + DA_QK_DIM) % LANES
        hi = pltpu.roll(packed_table, hi_shift, 1) if hi_shift else packed_table
        return jnp.where(first_map, lo, hi)

    cos_t = jnp.concatenate([jnp.where(rotary, spread(cos8, g), 1.0) for g in range(groups)], axis=0)
    sin_t = jnp.concatenate([spread(sin8, g) * freq_ref[1:2, :] for g in range(groups)], axis=0)
    first_half = freq_ref[1:2, :] < 0.0

    def rope(t, c, s):
        partner = jnp.where(first_half, pltpu.roll(t, LANES - ROPE_HALF, 1), pltpu.roll(t, ROPE_HALF, 1))
        return t * c + partner * s

    qk_scale = DA_QK_DIM ** -0.5 * LOG2_E
    cos_q, sin_q = cos_t * qk_scale, sin_t * qk_scale
    pq = jnp.dot(h, w_ref[:, 0:QK_WIDTH], preferred_element_type=F32)
    for j in range(DA_HEADS):
        sl = slice(j * LANES, (j + 1) * LANES)
        q_ref[j] = rope(pq[:, sl], cos_q, sin_q).astype(BF16)
    pk = jnp.dot(h, w_ref[:, QK_WIDTH:2 * QK_WIDTH], preferred_element_type=F32)
    for j in range(DA_HEADS):
        sl = slice(j * LANES, (j + 1) * LANES)
        k_ref[j] = rope(pk[:, sl], cos_t, sin_t).astype(BF16)
    pv = jnp.dot(h, w_ref[:, 2 * QK_WIDTH:2 * QK_WIDTH + DA_WIDTH], preferred_element_type=F32)
    for j in range(DA_HEADS):
        v_ref[j] = pv[:, j * DA_V_DIM:(j + 1) * DA_V_DIM].astype(BF16)
    u = jnp.dot(h, w_ref[:, 2 * QK_WIDTH + DA_WIDTH:], preferred_element_type=F32)

    @pl.when(i == 0)
    def _():
        halo_ref[...] = jnp.zeros_like(halo_ref)

    prev = halo_ref[...]
    halo_ref[...] = u[tm - POOL_HALO:, :]
    tpos = i * tm + lax.broadcasted_iota(jnp.int32, (tm, 1), 0)
    for g, w in enumerate(POOL_WINDOWS):
        sl = slice(g * POOL_GROUP_DIM, (g + 1) * POOL_GROUP_DIM)
        ug = u[:, sl]
        s = jnp.concatenate([prev[:, sl], ug], axis=0)
        d = 1
        while d < w:
            s = s + pltpu.roll(s, d, 0)
            d *= 2
        inv_count = 1.0 / jnp.minimum(tpos + 1, w).astype(F32)
        pooled = s[POOL_HALO:, :] * inv_count - ug
        po = jnp.dot(pooled.astype(BF16), wp_ref[g], preferred_element_type=F32) * ps_ref[:, sl]
        po_ref[0, :, sl] = po.astype(BF16)


def _inproj_call(x, g, w_in, pos, freq, w_pool, pool_scale, tm):
    b, s, d = x.shape
    tok = lambda n: pl.BlockSpec((1, tm, n), lambda bi, i: (bi, i, 0))
    heads = pl.BlockSpec((None, DA_HEADS, tm, DA_V_DIM), lambda bi, i: (bi, 0, i, 0))
    head_major = jax.ShapeDtypeStruct((b, DA_HEADS, s, DA_V_DIM), BF16)
    return pl.pallas_call(
        functools.partial(_inproj_kernel, tm=tm),
        grid=(b, s // tm),
        in_specs=[tok(d), _const_spec((1, d)), _HBM,
                  tok(1), _const_spec(freq.shape),
                  _const_spec(w_pool.shape), _const_spec((1, POOL_WIDTH))],
        out_specs=[heads, heads, heads, tok(POOL_WIDTH)],
        out_shape=[head_major, head_major, head_major,
                   jax.ShapeDtypeStruct((b, s, POOL_WIDTH), BF16)],
        scratch_shapes=[pltpu.VMEM((POOL_HALO, POOL_WIDTH), F32), pltpu.VMEM(w_in.shape, BF16),
                        *_stage_scratch(d, w_in.shape[1])],
        compiler_params=pltpu.CompilerParams(
            dimension_semantics=("arbitrary", "arbitrary"), vmem_limit_bytes=VMEM_LIMIT),
        name="inproj",
    )(x, g, w_in, pos, freq, w_pool, pool_scale)


def _attn_kernel(q_ref, k_ref, v_ref, lq1_ref, lk1_ref, lq2_ref, lk2_ref, gs_ref, o_ref,
                 m_ref, l_ref, acc_ref, qs_ref, s0_ref, s1_ref, p0_ref, p1_ref, *, t, nq):
    lane = lax.broadcasted_iota(jnp.int32, (t, DA_V_DIM), 1)
    key = lax.broadcasted_iota(jnp.int32, (t, t), 0)
    qry = lax.broadcasted_iota(jnp.int32, (t, t), 1)
    causal = jnp.concatenate([key <= qry, key <= qry], axis=1)

    def rows(i):
        return pl.ds(pl.multiple_of(i * t, t), t)

    def stack_maps(i, _):
        q = q_ref[rows(i), :]
        zero = jnp.zeros_like(q)
        qs_ref[i] = jnp.concatenate([jnp.where(lane < DA_QK_DIM, q, zero),
                                     jnp.where(lane >= DA_QK_DIM, q, zero)], axis=0)
        return 0

    lax.fori_loop(0, nq, stack_maps, 0)

    def scores(i, j):
        return lax.dot_general(k_ref[rows(j), :], qs_ref[i], (((1,), (1,)), ((), ())),
                               preferred_element_type=F32)

    def softmax(i, s_ref, p_ref, diagonal):
        s = s_ref[...]
        if diagonal:
            s = jnp.where(causal, s, -jnp.inf)
            m_new = jnp.max(s, axis=0, keepdims=True)
            p = jnp.exp2(s - m_new)
            alpha = None
            l_ref[i] = jnp.sum(p, axis=0, keepdims=True)
        else:
            m = m_ref[i]
            m_new = jnp.maximum(m, jnp.max(s, axis=0, keepdims=True))
            p = jnp.exp2(s - m_new)
            alpha = jnp.exp2(m - m_new)
            l_ref[i] = alpha * l_ref[i] + jnp.sum(p, axis=0, keepdims=True)
        m_ref[i] = m_new
        p_ref[...] = p.astype(BF16)
        return alpha

    def accumulate(i, j, p_ref, alpha):
        pv = lax.dot_general(v_ref[rows(j), :], p_ref[...], (((0,), (0,)), ((), ())),
                             preferred_element_type=F32)
        acc_ref[i] = pv if alpha is None else alpha * acc_ref[i] + pv

    s_bufs, p_bufs = (s0_ref, s1_ref), (p0_ref, p1_ref)

    def pipeline(n_units, unit, diagonal):
        assert n_units % 2 == 0 and n_units >= 2

        def stage(n, slot, alpha_prev):
            s_bufs[1 - slot][...] = scores(*unit(n + 1))
            alpha = softmax(unit(n)[0], s_bufs[slot], p_bufs[slot], diagonal)
            accumulate(*unit(n - 1), p_bufs[1 - slot], alpha_prev)
            return alpha

        s0_ref[...] = scores(*unit(0))
        s1_ref[...] = scores(*unit(1))
        alpha_first = softmax(unit(0)[0], s0_ref, p0_ref, diagonal)

        def body(h, alpha_prev):
            if diagonal:
                stage(2 * h + 2, 0, stage(2 * h + 1, 1, None))
                return alpha_prev
            return stage(2 * h + 2, 0, stage(2 * h + 1, 1, alpha_prev))

        alpha_even = lax.fori_loop(0, n_units // 2 - 1, body, 0 if diagonal else alpha_first)
        alpha_last = softmax(unit(n_units - 1)[0], s1_ref, p1_ref, diagonal)
        accumulate(*unit(n_units - 2), p0_ref, None if diagonal else alpha_even)
        accumulate(*unit(n_units - 1), p1_ref, alpha_last)

    def below_diagonal(n):
        n = jnp.asarray(n, jnp.int32)
        i = 1 + sum((n >= k * (k + 1) // 2).astype(jnp.int32) for k in range(1, nq - 1))
        return i, n - i * (i - 1) // 2

    pipeline(nq, lambda n: (n, n), True)
    pipeline(nq * (nq - 1) // 2, below_diagonal, False)

    lam = (jnp.exp(jnp.sum(lq1_ref[...] * lk1_ref[...], axis=-1, keepdims=True))
           - jnp.exp(jnp.sum(lq2_ref[...] * lk2_ref[...], axis=-1, keepdims=True))
           + LAMBDA_INIT)
    gain = gs_ref[...] * (1.0 - LAMBDA_INIT)
    for i in range(nq):
        o = acc_ref[i] * (1.0 / l_ref[i])
        da = o[:, :t] - lam * o[:, t:]
        inv = lax.rsqrt(jnp.mean(da * da, axis=0, keepdims=True) + EPS)
        o_ref[i * t:(i + 1) * t, :] = (da * inv * gain).T.astype(BF16)


def _attn_call(q, k, v, lq1, lk1, lq2, lk2, g_subln_col, t):
    b, _, s, _ = q.shape
    nq = s // t
    head = pl.BlockSpec((None, None, s, DA_V_DIM), lambda bi, h: (bi, h, 0, 0))
    vec = _const_spec((1, DA_QK_DIM))
    return pl.pallas_call(
        functools.partial(_attn_kernel, t=t, nq=nq),
        grid=(b, DA_HEADS),
        in_specs=[head, head, head, vec, vec, vec, vec, _const_spec((DA_V_DIM, 1))],
        out_specs=head,
        out_shape=jax.ShapeDtypeStruct((b, DA_HEADS, s, DA_V_DIM), BF16),
        scratch_shapes=[pltpu.VMEM((nq, 1, 2 * t), F32), pltpu.VMEM((nq, 1, 2 * t), F32),
                        pltpu.VMEM((nq, DA_V_DIM, 2 * t), F32),
                        pltpu.VMEM((nq, 2 * t, DA_V_DIM), BF16),
                        pltpu.VMEM((t, 2 * t), F32), pltpu.VMEM((t, 2 * t), F32),
                        pltpu.VMEM((t, 2 * t), BF16), pltpu.VMEM((t, 2 * t), BF16)],
        compiler_params=pltpu.CompilerParams(
            dimension_semantics=("arbitrary", "arbitrary"), vmem_limit_bytes=VMEM_LIMIT),
        name="diff_attn",
    )(q, k, v, lq1, lk1, lq2, lk2, g_subln_col)


def _mixout_kernel(da_ref, po_ref, x_ref, wo_hbm, gpost_ref, gxpre_ref, wxq_hbm, kv_ref, wxo_hbm,
                   gxpost_ref, o_ref, wo_ref, wxq_ref, wxo_ref, stage, sem):
    @pl.when(_first_step(2))
    def _():
        _load_weights([(wo_hbm, wo_ref), (wxq_hbm, wxq_ref), (wxo_hbm, wxo_ref)], stage, sem)

    da = jnp.concatenate([da_ref[h] for h in range(DA_HEADS)], axis=-1)
    mix = (jnp.dot(da, wo_ref[0:DA_WIDTH, :], preferred_element_type=F32)
           + jnp.dot(po_ref[0], wo_ref[DA_WIDTH:, :], preferred_element_type=F32))
    x1 = x_ref[0] + _rms(mix, gpost_ref[...])

    hq = _rms(x1, gxpre_ref[...]).astype(BF16)
    x_scale = X_HEAD_DIM ** -0.5
    xq = (jnp.dot(hq, wxq_ref[...], preferred_element_type=F32) * x_scale).astype(BF16)
    heads = []
    for h in range(X_HEADS):
        sl = slice(h * X_HEAD_DIM, (h + 1) * X_HEAD_DIM)
        kh = kv_ref[0, :, sl]
        vh = kv_ref[0, :, D_MODEL + h * X_HEAD_DIM:D_MODEL + (h + 1) * X_HEAD_DIM]
        sc = lax.dot_general(xq[:, sl], kh, (((1,), (1,)), ((), ())), preferred_element_type=F32)
        p = jnp.exp(sc - jnp.max(sc, axis=-1, keepdims=True))
        pm = (p * (1.0 / jnp.sum(p, axis=-1, keepdims=True))).astype(BF16)
        heads.append(jnp.dot(pm, vh, preferred_element_type=F32).astype(BF16))
    xo = jnp.concatenate(heads, axis=-1)
    y = jnp.dot(xo, wxo_ref[...], preferred_element_type=F32)
    o_ref[0] = x1 + _rms(y, gxpost_ref[...])


def _mixout_call(da, po, x, w_out, g_mix_post, g_x_pre, w_xq, kv, w_xo, g_x_post, tm):
    b, s, d = x.shape
    tok = lambda n: pl.BlockSpec((1, tm, n), lambda bi, i: (bi, i, 0))
    gvec = _const_spec((1, d))
    return pl.pallas_call(
        _mixout_kernel,
        grid=(b, s // tm),
        in_specs=[pl.BlockSpec((None, DA_HEADS, tm, DA_V_DIM), lambda bi, i: (bi, 0, i, 0)),
                  tok(POOL_WIDTH), tok(d), _HBM, gvec, gvec, _HBM,
                  pl.BlockSpec((1,) + kv.shape[1:], lambda bi, i: (bi, 0, 0)),
                  _HBM, gvec],
        out_specs=tok(d),
        out_shape=jax.ShapeDtypeStruct((b, s, d), F32),
        scratch_shapes=[pltpu.VMEM(w_out.shape, BF16), pltpu.VMEM(w_xq.shape, BF16),
                        pltpu.VMEM(w_xo.shape, BF16), *_stage_scratch(d, d)],
        compiler_params=pltpu.CompilerParams(
            dimension_semantics=("arbitrary", "arbitrary"), vmem_limit_bytes=VMEM_LIMIT),
        name="mixout_xattn",
    )(da, po, x, w_out, g_mix_post, g_x_pre, w_xq, kv, w_xo, g_x_post)


def _ffn_kernel(x_ref, gpre_ref, wg_hbm, wu_hbm, wd_hbm, gpost_ref, o_ref,
                wg_ref, wu_ref, wd_ref, wide_stage, wide_sem, tall_stage, tall_sem):
    @pl.when(_first_step(2))
    def _():
        _load_weights([(wg_hbm, wg_ref), (wu_hbm, wu_ref)], wide_stage, wide_sem)
        _load_weights([(wd_hbm, wd_ref)], tall_stage, tall_sem)

    x = x_ref[0]
    hf = _rms(x, gpre_ref[...]).astype(BF16)
    ff = None
    for lo, hi in FF_CHUNKS:
        gate = jnp.dot(hf, wg_ref[:, lo:hi], preferred_element_type=F32)
        up = jnp.dot(hf, wu_ref[:, lo:hi], preferred_element_type=F32)
        act = (gate * (1.0 / (1.0 + jnp.exp(-gate))) * up).astype(BF16)
        part = jnp.dot(act, wd_ref[lo:hi, :], preferred_element_type=F32)
        ff = part if ff is None else ff + part
    o_ref[0] = x + _rms(ff, gpost_ref[...])


def _ffn_call(x, g_pre, w_gate, w_up, w_down, g_post, tm):
    b, s, d = x.shape
    tok = pl.BlockSpec((1, tm, d), lambda bi, i: (bi, i, 0))
    gvec = _const_spec((1, d))
    return pl.pallas_call(
        _ffn_kernel,
        grid=(b, s // tm),
        in_specs=[tok, gvec, _HBM, _HBM, _HBM, gvec],
        out_specs=tok,
        out_shape=jax.ShapeDtypeStruct((b, s, d), F32),
        scratch_shapes=[pltpu.VMEM(w_gate.shape, BF16), pltpu.VMEM(w_up.shape, BF16),
                        pltpu.VMEM(w_down.shape, BF16),
                        *_stage_scratch(d, w_gate.shape[1]),
                        *_stage_scratch(w_down.shape[0], d)],
        compiler_params=pltpu.CompilerParams(
            dimension_semantics=("arbitrary", "arbitrary"), vmem_limit_bytes=VMEM_LIMIT),
        name="swiglu",
    )(x, g_pre, w_gate, w_up, w_down, g_post)


def _rope_lane_constants():
    inv_freq = ROPE_THETA ** (-jnp.arange(0, ROPE_DIM, 2, dtype=F32) / ROPE_DIM)
    packed_freq = jnp.tile(inv_freq, LANES // ROPE_HALF)
    one_half = jnp.ones((ROPE_HALF,), F32)
    per_map = jnp.concatenate([-one_half, one_half, jnp.zeros((DA_QK_DIM - ROPE_DIM,), F32)])
    sign = jnp.concatenate([per_map, per_map])
    return jnp.concatenate([jnp.stack([packed_freq, sign]), jnp.zeros((6, LANES), F32)], axis=0)


def kernel(x, mem, positions, g_mix_pre, w_in, lambda_q1, lambda_k1, lambda_q2, lambda_k2,
           g_subln, w_pool, pool_scale, w_out, g_mix_post, g_x_pre, g_mem, w_xq, w_xkv, w_xo,
           g_x_post, g_ffn_pre, w_gate, w_up, w_down, g_ffn_post):
    b, s, d = x.shape
    assert d == D_MODEL and s % TOKEN_TILE == 0 and s % ATTN_TILE == 0
    row = lambda a: a.reshape(1, -1).astype(F32)
    bf = lambda a: a.astype(BF16)

    pos = positions.astype(F32).reshape(b, s, 1)
    kv = _kv_call(mem, row(g_mem), w_xkv)
    q, k, v, po = _inproj_call(x, row(g_mix_pre), w_in, pos, _rope_lane_constants(),
                               bf(w_pool), row(pool_scale), TOKEN_TILE)
    da = _attn_call(q, k, v, row(lambda_q1), row(lambda_k1), row(lambda_q2), row(lambda_k2),
                    g_subln.reshape(-1, 1).astype(F32), ATTN_TILE)
    x2 = _mixout_call(da, po, x, w_out, row(g_mix_post), row(g_x_pre), w_xq, kv, w_xo,
                      row(g_x_post), TOKEN_TILE)
    return _ffn_call(x2, row(g_ffn_pre), w_gate, w_up, w_down, row(g_ffn_post), TOKEN_TILE)
```

```python
import functools
import math

import jax
import jax.numpy as jnp
from jax import lax
from jax.experimental import pallas as pl
from jax.experimental.pallas import tpu as pltpu

F32 = jnp.float32
BF16 = jnp.bfloat16

D_MODEL = 1024
MEM_LEN = 256
EPS = 1e-6
DA_HEADS = 4
DA_QK_DIM = 64
DA_V_DIM = 2 * DA_QK_DIM
DA_WIDTH = DA_HEADS * DA_V_DIM
QK_WIDTH = DA_HEADS * 2 * DA_QK_DIM
POOL_WINDOWS = (2, 4, 8, 16)
POOL_WIDTH = D_MODEL - DA_WIDTH
POOL_GROUP_DIM = POOL_WIDTH // len(POOL_WINDOWS)
ROPE_THETA = 500000.0
ROPE_DIM = DA_QK_DIM // 4
ROPE_HALF = ROPE_DIM // 2
X_HEADS = 4
X_HEAD_DIM = D_MODEL // X_HEADS
D_FF = -(-(8 * D_MODEL) // (3 * 256)) * 256
LAMBDA_INIT = 0.8 - 0.6 * math.exp(-0.3 * 0)
LOG2_E = math.log2(math.e)

LANES = 128
POOL_HALO = 16
VMEM_LIMIT = 56 * 1024 * 1024

TOKEN_TILE = 512
ATTN_TILE = 512
FF_CHUNKS = ((0, 1536), (1536, 2816))
STAGE_BYTES = 4 * 1024 * 1024


def _inv_rms(x):
    return lax.rsqrt(jnp.mean(x * x, axis=-1, keepdims=True) + EPS)


def _rms(x, g):
    return x * _inv_rms(x) * g


def _prenorm_operand(x, g):
    return (x * g).astype(BF16), _inv_rms(x)


def _const_spec(shape):
    zeros = (0,) * len(shape)
    return pl.BlockSpec(shape, lambda *_: zeros, pipeline_mode=pl.Buffered(1))


_HBM = pl.BlockSpec(memory_space=pl.ANY)


def _stage_scratch(rows, cols):
    chunks = -(-(rows * cols * 4) // STAGE_BYTES)
    while rows % chunks or (rows // chunks) % 16:
        chunks += 1
    return [pltpu.VMEM((2, rows // chunks, cols), F32), pltpu.SemaphoreType.DMA((2,))]


def _load_weights(weights, stage_ref, sem_ref):
    rows = stage_ref.shape[1]
    jobs = []
    for w_hbm, w_vmem in weights:
        assert w_hbm.shape[0] % rows == 0 and stage_ref.shape[2] == w_hbm.shape[1]
        jobs += [(w_hbm, w_vmem, r) for r in range(0, w_hbm.shape[0], rows)]

    def copy(k):
        w_hbm, _, r = jobs[k]
        return pltpu.make_async_copy(w_hbm.at[pl.ds(r, rows), :], stage_ref.at[k % 2], sem_ref.at[k % 2])

    copy(0).start()
    for k, (_, w_vmem, r) in enumerate(jobs):
        if k + 1 < len(jobs):
            copy(k + 1).start()
        copy(k).wait()
        w_vmem[r:r + rows, :] = stage_ref[k % 2].astype(BF16)


def _first_step(grid_rank):
    first = pl.program_id(0) == 0
    for axis in range(1, grid_rank):
        first = first & (pl.program_id(axis) == 0)
    return first


def _kv_kernel(mem_ref, g_ref, w_hbm, kv_ref, w_ref, stage, sem):
    @pl.when(_first_step(1))
    def _():
        _load_weights([(w_hbm, w_ref)], stage, sem)

    mn = _rms(mem_ref[0], g_ref[...]).astype(BF16)
    kv_ref[0] = jnp.dot(mn, w_ref[...], preferred_element_type=F32).astype(BF16)


def _kv_call(mem, g_mem, w_xkv):
    b, m, d = mem.shape
    n = w_xkv.shape[1]
    return pl.pallas_call(
        _kv_kernel,
        grid=(b,),
        in_specs=[pl.BlockSpec((1, m, d), lambda i: (i, 0, 0)),
                  _const_spec((1, d)), _HBM],
        out_specs=pl.BlockSpec((1, m, n), lambda i: (i, 0, 0)),
        out_shape=jax.ShapeDtypeStruct((b, m, n), BF16),
        scratch_shapes=[pltpu.VMEM((d, n), BF16), *_stage_scratch(d, n)],
        compiler_params=pltpu.CompilerParams(
            dimension_semantics=("arbitrary",), vmem_limit_bytes=VMEM_LIMIT),
        name="kv_proj",
    )(mem, g_mem, w_xkv)


def _inproj_kernel(x_ref, g_ref, w_hbm, pos_ref, freq_ref, wp_ref, ps_ref,
                   q_ref, k_ref, v_ref, po_ref, halo_ref, w_ref, stage, sem, *, tm):
    @pl.when(_first_step(2))
    def _():
        _load_weights([(w_hbm, w_ref)], stage, sem)

    i = pl.program_id(1)
    h, inv = _prenorm_operand(x_ref[0], g_ref[...])

    groups = LANES // ROPE_DIM
    rows = tm // groups
    lane = lax.broadcasted_iota(jnp.int32, (rows, LANES), 1)
    lane_group = lax.shift_right_logical(lane, int(math.log2(ROPE_DIM)))
    pos = pos_ref[0]
    packed = jnp.zeros((rows, LANES), F32)
    for g in range(groups):
        packed = jnp.where(lane_group == g, pos[g * rows:(g + 1) * rows], packed)
    ang = packed * freq_ref[0:1, :]
    cos8, sin8 = jnp.cos(ang), jnp.sin(ang)
    first_map = lane < DA_QK_DIM
    rotary = freq_ref[1:2, :] != 0.0

    def spread(packed_table, g):
        shift = (LANES - ROPE_DIM * g) % LANES
        lo = pltpu.roll(packed_table, shift, 1) if shift else packed_table
        hi_shift = (shift + DA_QK_DIM) % LANES
        hi = pltpu.roll(packed_table, hi_shift, 1) if hi_shift else packed_table
        return jnp.where(first_map, lo, hi)

    cos_t = jnp.concatenate([jnp.where(rotary, spread(cos8, g), 1.0) for g in range(groups)], axis=0)
    sin_t = jnp.concatenate([spread(sin8, g) * freq_ref[1:2, :] for g in range(groups)], axis=0)
    first_half = freq_ref[1:2, :] < 0.0

    def rope(t, c, s):
        partner = jnp.where(first_half, pltpu.roll(t, LANES - ROPE_HALF, 1), pltpu.roll(t, ROPE_HALF, 1))
        return t * c + partner * s

    qk_scale = DA_QK_DIM ** -0.5 * LOG2_E
    cos_k, sin_k = cos_t * inv, sin_t * inv
    cos_q, sin_q = cos_k * qk_scale, sin_k * qk_scale
    pq = jnp.dot(h, w_ref[:, 0:QK_WIDTH], preferred_element_type=F32)
    for j in range(DA_HEADS):
        sl = slice(j * LANES, (j + 1) * LANES)
        q_ref[j] = rope(pq[:, sl], cos_q, sin_q).astype(BF16)
    pk = jnp.dot(h, w_ref[:, QK_WIDTH:2 * QK_WIDTH], preferred_element_type=F32)
    for j in range(DA_HEADS):
        sl = slice(j * LANES, (j + 1) * LANES)
        k_ref[j] = rope(pk[:, sl], cos_k, sin_k).astype(BF16)
    pv = jnp.dot(h, w_ref[:, 2 * QK_WIDTH:2 * QK_WIDTH + DA_WIDTH], preferred_element_type=F32)
    for j in range(DA_HEADS):
        v_ref[j] = (pv[:, j * DA_V_DIM:(j + 1) * DA_V_DIM] * inv).astype(BF16)
    u = jnp.dot(h, w_ref[:, 2 * QK_WIDTH + DA_WIDTH:], preferred_element_type=F32) * inv

    @pl.when(i == 0)
    def _():
        halo_ref[...] = jnp.zeros_like(halo_ref)

    prev = halo_ref[...]
    halo_ref[...] = u[tm - POOL_HALO:, :]
    tpos = i * tm + lax.broadcasted_iota(jnp.int32, (tm, 1), 0)
    for g, w in enumerate(POOL_WINDOWS):
        sl = slice(g * POOL_GROUP_DIM, (g + 1) * POOL_GROUP_DIM)
        ug = u[:, sl]
        s = jnp.concatenate([prev[:, sl], ug], axis=0)
        d = 1
        while d < w:
            s = s + pltpu.roll(s, d, 0)
            d *= 2
        inv_count = 1.0 / jnp.minimum(tpos + 1, w).astype(F32)
        pooled = s[POOL_HALO:, :] * inv_count - ug
        po = jnp.dot(pooled.astype(BF16), wp_ref[g], preferred_element_type=F32) * ps_ref[:, sl]
        po_ref[0, :, sl] = po.astype(BF16)


def _inproj_call(x, g, w_in, pos, freq, w_pool, pool_scale, tm):
    b, s, d = x.shape
    tok = lambda n: pl.BlockSpec((1, tm, n), lambda bi, i: (bi, i, 0))
    heads = pl.BlockSpec((None, DA_HEADS, tm, DA_V_DIM), lambda bi, i: (bi, 0, i, 0))
    head_major = jax.ShapeDtypeStruct((b, DA_HEADS, s, DA_V_DIM), BF16)
    return pl.pallas_call(
        functools.partial(_inproj_kernel, tm=tm),
        grid=(b, s // tm),
        in_specs=[tok(d), _const_spec((1, d)), _HBM,
                  tok(1), _const_spec(freq.shape),
                  _const_spec(w_pool.shape), _const_spec((1, POOL_WIDTH))],
        out_specs=[heads, heads, heads, tok(POOL_WIDTH)],
        out_shape=[head_major, head_major, head_major,
                   jax.ShapeDtypeStruct((b, s, POOL_WIDTH), BF16)],
        scratch_shapes=[pltpu.VMEM((POOL_HALO, POOL_WIDTH), F32), pltpu.VMEM(w_in.shape, BF16),
                        *_stage_scratch(d, w_in.shape[1])],
        compiler_params=pltpu.CompilerParams(
            dimension_semantics=("arbitrary", "arbitrary"), vmem_limit_bytes=VMEM_LIMIT),
        name="inproj",
    )(x, g, w_in, pos, freq, w_pool, pool_scale)


def _attn_kernel(q_ref, k_ref, v_ref, lq1_ref, lk1_ref, lq2_ref, lk2_ref, gs_ref, o_ref,
                 m_ref, l_ref, acc_ref, qs_ref, s0_ref, s1_ref, p0_ref, p1_ref, *, t, nq):
    lane = lax.broadcasted_iota(jnp.int32, (t, DA_V_DIM), 1)
    key = lax.broadcasted_iota(jnp.int32, (t, t), 0)
    qry = lax.broadcasted_iota(jnp.int32, (t, t), 1)
    causal = jnp.concatenate([key <= qry, key <= qry], axis=1)

    def rows(i):
        return pl.ds(pl.multiple_of(i * t, t), t)

    def stack_maps(i, _):
        q = q_ref[rows(i), :]
        zero = jnp.zeros_like(q)
        qs_ref[i] = jnp.concatenate([jnp.where(lane < DA_QK_DIM, q, zero),
                                     jnp.where(lane >= DA_QK_DIM, q, zero)], axis=0)
        return 0

    lax.fori_loop(0, nq, stack_maps, 0)

    def scores(i, j):
        return lax.dot_general(k_ref[rows(j), :], qs_ref[i], (((1,), (1,)), ((), ())),
                               preferred_element_type=F32)

    def softmax(i, s_ref, p_ref, diagonal):
        s = s_ref[...]
        if diagonal:
            s = jnp.where(causal, s, -jnp.inf)
            m_new = jnp.max(s, axis=0, keepdims=True)
            p = jnp.exp2(s - m_new)
            alpha = None
            l_ref[i] = jnp.sum(p, axis=0, keepdims=True)
        else:
            m = m_ref[i]
            m_new = jnp.maximum(m, jnp.max(s, axis=0, keepdims=True))
            p = jnp.exp2(s - m_new)
            alpha = jnp.exp2(m - m_new)
            l_ref[i] = alpha * l_ref[i] + jnp.sum(p, axis=0, keepdims=True)
        m_ref[i] = m_new
        p_ref[...] = p.astype(BF16)
        return alpha

    def accumulate(i, j, p_ref, alpha):
        pv = lax.dot_general(v_ref[rows(j), :], p_ref[...], (((0,), (0,)), ((), ())),
                             preferred_element_type=F32)
        acc_ref[i] = pv if alpha is None else alpha * acc_ref[i] + pv

    s_bufs, p_bufs = (s0_ref, s1_ref), (p0_ref, p1_ref)

    def pipeline(n_units, unit, diagonal):
        assert n_units % 2 == 0 and n_units >= 2

        def stage(n, slot, alpha_prev):
            s_bufs[1 - slot][...] = scores(*unit(n + 1))
            alpha = softmax(unit(n)[0], s_bufs[slot], p_bufs[slot], diagonal)
            accumulate(*unit(n - 1), p_bufs[1 - slot], alpha_prev)
            return alpha

        s0_ref[...] = scores(*unit(0))
        s1_ref[...] = scores(*unit(1))
        alpha_first = softmax(unit(0)[0], s0_ref, p0_ref, diagonal)

        def body(h, alpha_prev):
            if diagonal:
                stage(2 * h + 2, 0, stage(2 * h + 1, 1, None))
                return alpha_prev
            return stage(2 * h + 2, 0, stage(2 * h + 1, 1, alpha_prev))

        alpha_even = lax.fori_loop(0, n_units // 2 - 1, body, 0 if diagonal else alpha_first)
        alpha_last = softmax(unit(n_units - 1)[0], s1_ref, p1_ref, diagonal)
        accumulate(*unit(n_units - 2), p0_ref, None if diagonal else alpha_even)
        accumulate(*unit(n_units - 1), p1_ref, alpha_last)

    def below_diagonal(n):
        n = jnp.asarray(n, jnp.int32)
        i = 1 + sum((n >= k * (k + 1) // 2).astype(jnp.int32) for k in range(1, nq - 1))
        return i, n - i * (i - 1) // 2

    pipeline(nq, lambda n: (n, n), True)
    pipeline(nq * (nq - 1) // 2, below_diagonal, False)

    lam = (jnp.exp(jnp.sum(lq1_ref[...] * lk1_ref[...], axis=-1, keepdims=True))
           - jnp.exp(jnp.sum(lq2_ref[...] * lk2_ref[...], axis=-1, keepdims=True))
           + LAMBDA_INIT)
    gain = gs_ref[...] * (1.0 - LAMBDA_INIT)
    for i in range(nq):
        o = acc_ref[i] * (1.0 / l_ref[i])
        da = o[:, :t] - lam * o[:, t:]
        inv = lax.rsqrt(jnp.mean(da * da, axis=0, keepdims=True) + EPS)
        o_ref[i * t:(i + 1) * t, :] = (da * inv * gain).T.astype(BF16)


def _attn_call(q, k, v, lq1, lk1, lq2, lk2, g_subln_col, t):
    b, _, s, _ = q.shape
    nq = s // t
    head = pl.BlockSpec((None, None, s, DA_V_DIM), lambda bi, h: (bi, h, 0, 0))
    vec = _const_spec((1, DA_QK_DIM))
    return pl.pallas_call(
        functools.partial(_attn_kernel, t=t, nq=nq),
        grid=(b, DA_HEADS),
        in_specs=[head, head, head, vec, vec, vec, vec, _const_spec((DA_V_DIM, 1))],
        out_specs=head,
        out_shape=jax.ShapeDtypeStruct((b, DA_HEADS, s, DA_V_DIM), BF16),
        scratch_shapes=[pltpu.VMEM((nq, 1, 2 * t), F32), pltpu.VMEM((nq, 1, 2 * t), F32),
                        pltpu.VMEM((nq, DA_V_DIM, 2 * t), F32),
                        pltpu.VMEM((nq, 2 * t, DA_V_DIM), BF16),
                        pltpu.VMEM((t, 2 * t), F32), pltpu.VMEM((t, 2 * t), F32),
                        pltpu.VMEM((t, 2 * t), BF16), pltpu.VMEM((t, 2 * t), BF16)],
        compiler_params=pltpu.CompilerParams(
            dimension_semantics=("arbitrary", "arbitrary"), vmem_limit_bytes=VMEM_LIMIT),
        name="diff_attn",
    )(q, k, v, lq1, lk1, lq2, lk2, g_subln_col)


def _mixout_kernel(da_ref, po_ref, x_ref, wo_hbm, gpost_ref, gxpre_ref, wxq_hbm, kv_ref, wxo_hbm,
                   gxpost_ref, o_ref, wo_ref, wxq_ref, wxo_ref, stage, sem):
    @pl.when(_first_step(2))
    def _():
        _load_weights([(wo_hbm, wo_ref), (wxq_hbm, wxq_ref), (wxo_hbm, wxo_ref)], stage, sem)

    da = jnp.concatenate([da_ref[h] for h in range(DA_HEADS)], axis=-1)
    mix = (jnp.dot(da, wo_ref[0:DA_WIDTH, :], preferred_element_type=F32)
           + jnp.dot(po_ref[0], wo_ref[DA_WIDTH:, :], preferred_element_type=F32))
    x1 = x_ref[0] + _rms(mix, gpost_ref[...])

    hq, inv = _prenorm_operand(x1, gxpre_ref[...])
    x_scale = X_HEAD_DIM ** -0.5
    xq = (jnp.dot(hq, wxq_ref[...], preferred_element_type=F32) * (inv * x_scale)).astype(BF16)
    heads = []
    for h in range(X_HEADS):
        sl = slice(h * X_HEAD_DIM, (h + 1) * X_HEAD_DIM)
        kh = kv_ref[0, :, sl]
        vh = kv_ref[0, :, D_MODEL + h * X_HEAD_DIM:D_MODEL + (h + 1) * X_HEAD_DIM]
        sc = lax.dot_general(xq[:, sl], kh, (((1,), (1,)), ((), ())), preferred_element_type=F32)
        p = jnp.exp(sc - jnp.max(sc, axis=-1, keepdims=True))
        pm = (p * (1.0 / jnp.sum(p, axis=-1, keepdims=True))).astype(BF16)
        heads.append(jnp.dot(pm, vh, preferred_element_type=F32).astype(BF16))
    xo = jnp.concatenate(heads, axis=-1)
    y = jnp.dot(xo, wxo_ref[...], preferred_element_type=F32)
    o_ref[0] = x1 + _rms(y, gxpost_ref[...])


def _mixout_call(da, po, x, w_out, g_mix_post, g_x_pre, w_xq, kv, w_xo, g_x_post, tm):
    b, s, d = x.shape
    tok = lambda n: pl.BlockSpec((1, tm, n), lambda bi, i: (bi, i, 0))
    gvec = _const_spec((1, d))
    return pl.pallas_call(
        _mixout_kernel,
        grid=(b, s // tm),
        in_specs=[pl.BlockSpec((None, DA_HEADS, tm, DA_V_DIM), lambda bi, i: (bi, 0, i, 0)),
                  tok(POOL_WIDTH), tok(d), _HBM, gvec, gvec, _HBM,
                  pl.BlockSpec((1,) + kv.shape[1:], lambda bi, i: (bi, 0, 0)),
                  _HBM, gvec],
        out_specs=tok(d),
        out_shape=jax.ShapeDtypeStruct((b, s, d), F32),
        scratch_shapes=[pltpu.VMEM(w_out.shape, BF16), pltpu.VMEM(w_xq.shape, BF16),
                        pltpu.VMEM(w_xo.shape, BF16), *_stage_scratch(d, d)],
        compiler_params=pltpu.CompilerParams(
            dimension_semantics=("arbitrary", "arbitrary"), vmem_limit_bytes=VMEM_LIMIT),
        name="mixout_xattn",
    )(da, po, x, w_out, g_mix_post, g_x_pre, w_xq, kv, w_xo, g_x_post)


def _ffn_kernel(x_ref, gpre_ref, wg_hbm, wu_hbm, wd_hbm, gpost_ref, o_ref,
                wg_ref, wu_ref, wd_ref, wide_stage, wide_sem, tall_stage, tall_sem):
    @pl.when(_first_step(2))
    def _():
        _load_weights([(wg_hbm, wg_ref), (wu_hbm, wu_ref)], wide_stage, wide_sem)
        _load_weights([(wd_hbm, wd_ref)], tall_stage, tall_sem)

    x = x_ref[0]
    hf, inv = _prenorm_operand(x, gpre_ref[...])
    ff = None
    for lo, hi in FF_CHUNKS:
        gate = jnp.dot(hf, wg_ref[:, lo:hi], preferred_element_type=F32) * inv
        up = jnp.dot(hf, wu_ref[:, lo:hi], preferred_element_type=F32) * inv
        act = (gate * (1.0 / (1.0 + jnp.exp(-gate))) * up).astype(BF16)
        part = jnp.dot(act, wd_ref[lo:hi, :], preferred_element_type=F32)
        ff = part if ff is None else ff + part
    o_ref[0] = x + _rms(ff, gpost_ref[...])


def _ffn_call(x, g_pre, w_gate, w_up, w_down, g_post, tm):
    b, s, d = x.shape
    tok = pl.BlockSpec((1, tm, d), lambda bi, i: (bi, i, 0))
    gvec = _const_spec((1, d))
    return pl.pallas_call(
        _ffn_kernel,
        grid=(b, s // tm),
        in_specs=[tok, gvec, _HBM, _HBM, _HBM, gvec],
        out_specs=tok,
        out_shape=jax.ShapeDtypeStruct((b, s, d), F32),
        scratch_shapes=[pltpu.VMEM(w_gate.shape, BF16), pltpu.VMEM(w_up.shape, BF16),
                        pltpu.VMEM(w_down.shape, BF16),
                        *_stage_scratch(d, w_gate.shape[1]),
                        *_stage_scratch(w_down.shape[0], d)],
        compiler_params=pltpu.CompilerParams(
            dimension_semantics=("arbitrary", "arbitrary"), vmem_limit_bytes=VMEM_LIMIT),
        name="swiglu",
    )(x, g_pre, w_gate, w_up, w_down, g_post)


def _rope_lane_constants():
    inv_freq = ROPE_THETA ** (-jnp.arange(0, ROPE_DIM, 2, dtype=F32) / ROPE_DIM)
    packed_freq = jnp.tile(inv_freq, LANES // ROPE_HALF)
    one_half = jnp.ones((ROPE_HALF,), F32)
    per_map = jnp.concatenate([-one_half, one_half, jnp.zeros((DA_QK_DIM - ROPE_DIM,), F32)])
    sign = jnp.concatenate([per_map, per_map])
    return jnp.concatenate([jnp.stack([packed_freq, sign]), jnp.zeros((6, LANES), F32)], axis=0)


def kernel(x, mem, positions, g_mix_pre, w_in, lambda_q1, lambda_k1, lambda_q2, lambda_k2,
           g_subln, w_pool, pool_scale, w_out, g_mix_post, g_x_pre, g_mem, w_xq, w_xkv, w_xo,
           g_x_post, g_ffn_pre, w_gate, w_up, w_down, g_ffn_post):
    b, s, d = x.shape
    assert d == D_MODEL and s % TOKEN_TILE == 0 and s % ATTN_TILE == 0
    row = lambda a: a.reshape(1, -1).astype(F32)
    bf = lambda a: a.astype(BF16)

    pos = positions.astype(F32).reshape(b, s, 1)
    kv = _kv_call(mem, row(g_mem), w_xkv)
    q, k, v, po = _inproj_call(x, row(g_mix_pre), w_in, pos, _rope_lane_constants(),
                               bf(w_pool), row(pool_scale), TOKEN_TILE)
    da = _attn_call(q, k, v, row(lambda_q1), row(lambda_k1), row(lambda_q2), row(lambda_k2),
                    g_subln.reshape(-1, 1).astype(F32), ATTN_TILE)
    x2 = _mixout_call(da, po, x, w_out, row(g_mix_post), row(g_x_pre), w_xq, kv, w_xo,
                      row(g_x_post), TOKEN_TILE)
    return _ffn_call(x2, row(g_ffn_pre), w_gate, w_up, w_down, row(g_ffn_post), TOKEN_TILE)
```

```python
import functools
import math

import jax
import jax.numpy as jnp
from jax import lax
from jax.experimental import pallas as pl
from jax.experimental.pallas import tpu as pltpu

F32 = jnp.float32
BF16 = jnp.bfloat16

D_MODEL = 1024
MEM_LEN = 256
EPS = 1e-6
DA_HEADS = 4
DA_QK_DIM = 64
DA_V_DIM = 2 * DA_QK_DIM
DA_WIDTH = DA_HEADS * DA_V_DIM
QK_WIDTH = DA_HEADS * 2 * DA_QK_DIM
POOL_WINDOWS = (2, 4, 8, 16)
POOL_WIDTH = D_MODEL - DA_WIDTH
POOL_GROUP_DIM = POOL_WIDTH // len(POOL_WINDOWS)
ROPE_THETA = 500000.0
ROPE_DIM = DA_QK_DIM // 4
ROPE_HALF = ROPE_DIM // 2
X_HEADS = 4
X_HEAD_DIM = D_MODEL // X_HEADS
D_FF = -(-(8 * D_MODEL) // (3 * 256)) * 256
LAMBDA_INIT = 0.8 - 0.6 * math.exp(-0.3 * 0)
LOG2_E = math.log2(math.e)

LANES = 128
POOL_HALO = 16
VMEM_LIMIT = 56 * 1024 * 1024

TOKEN_TILE = 512
ATTN_TILE = 512
FF_CHUNKS = ((0, 1536), (1536, 2816))
STAGE_BYTES = 4 * 1024 * 1024


def _inv_rms(x):
    return lax.rsqrt(jnp.mean(x * x, axis=-1, keepdims=True) + EPS)


def _rms(x, g):
    return x * _inv_rms(x) * g


def _prenorm_operand(x, g):
    return (x * g).astype(BF16), _inv_rms(x)


def _const_spec(shape):
    zeros = (0,) * len(shape)
    return pl.BlockSpec(shape, lambda *_: zeros, pipeline_mode=pl.Buffered(1))


_HBM = pl.BlockSpec(memory_space=pl.ANY)


def _stage_scratch(rows, cols):
    chunks = -(-(rows * cols * 4) // STAGE_BYTES)
    while rows % chunks or (rows // chunks) % 16:
        chunks += 1
    return [pltpu.VMEM((2, rows // chunks, cols), F32), pltpu.SemaphoreType.DMA((2,))]


def _load_weights(weights, stage_ref, sem_ref):
    rows = stage_ref.shape[1]
    jobs = []
    for w_hbm, w_vmem in weights:
        assert w_hbm.shape[0] % rows == 0 and stage_ref.shape[2] == w_hbm.shape[1]
        jobs += [(w_hbm, w_vmem, r) for r in range(0, w_hbm.shape[0], rows)]

    def copy(k):
        w_hbm, _, r = jobs[k]
        return pltpu.make_async_copy(w_hbm.at[pl.ds(r, rows), :], stage_ref.at[k % 2], sem_ref.at[k % 2])

    copy(0).start()
    for k, (_, w_vmem, r) in enumerate(jobs):
        if k + 1 < len(jobs):
            copy(k + 1).start()
        copy(k).wait()
        w_vmem[r:r + rows, :] = stage_ref[k % 2].astype(BF16)


def _first_step(grid_rank):
    first = pl.program_id(0) == 0
    for axis in range(1, grid_rank):
        first = first & (pl.program_id(axis) == 0)
    return first


def _kv_kernel(mem_ref, g_ref, w_hbm, kv_ref, w_ref, stage, sem):
    @pl.when(_first_step(1))
    def _():
        _load_weights([(w_hbm, w_ref)], stage, sem)

    mn = _rms(mem_ref[0], g_ref[...]).astype(BF16)
    kv_ref[0] = jnp.dot(mn, w_ref[...], preferred_element_type=F32).astype(BF16)


def _kv_call(mem, g_mem, w_xkv):
    b, m, d = mem.shape
    n = w_xkv.shape[1]
    return pl.pallas_call(
        _kv_kernel,
        grid=(b,),
        in_specs=[pl.BlockSpec((1, m, d), lambda i: (i, 0, 0)),
                  _const_spec((1, d)), _HBM],
        out_specs=pl.BlockSpec((1, m, n), lambda i: (i, 0, 0)),
        out_shape=jax.ShapeDtypeStruct((b, m, n), BF16),
        scratch_shapes=[pltpu.VMEM((d, n), BF16), *_stage_scratch(d, n)],
        compiler_params=pltpu.CompilerParams(
            dimension_semantics=("arbitrary",), vmem_limit_bytes=VMEM_LIMIT),
        name="kv_proj",
    )(mem, g_mem, w_xkv)


def _inproj_kernel(x_ref, g_ref, w_hbm, pos_ref, freq_ref, wp_ref, ps_ref,
                   q_ref, k_ref, v_ref, po_ref, halo_ref, w_ref, stage, sem, *, tm):
    @pl.when(_first_step(2))
    def _():
        _load_weights([(w_hbm, w_ref)], stage, sem)

    i = pl.program_id(1)
    h = _rms(x_ref[0], g_ref[...]).astype(BF16)

    groups = LANES // ROPE_DIM
    rows = tm // groups
    lane = lax.broadcasted_iota(jnp.int32, (rows, LANES), 1)
    lane_group = lax.shift_right_logical(lane, int(math.log2(ROPE_DIM)))
    pos = jnp.broadcast_to(pos_ref[0].astype(F32), (LANES, tm)).T
    packed = jnp.zeros((rows, LANES), F32)
    for g in range(groups):
        packed = jnp.where(lane_group == g, pos[g * rows:(g + 1) * rows], packed)
    ang = packed * freq_ref[0:1, :]
    cos8, sin8 = jnp.cos(ang), jnp.sin(ang)
    first_map = lane < DA_QK_DIM
    rotary = freq_ref[1:2, :] != 0.0

    def spread(packed_table, g):
        shift = (LANES - ROPE_DIM * g) % LANES
        lo = pltpu.roll(packed_table, shift, 1) if shift else packed_table
        hi_shift = (shift + DA_QK_DIM) % LANES
        hi = pltpu.roll(packed_table, hi_shift, 1) if hi_shift else packed_table
        return jnp.where(first_map, lo, hi)

    cos_t = jnp.concatenate([jnp.where(rotary, spread(cos8, g), 1.0) for g in range(groups)], axis=0)
    sin_t = jnp.concatenate([spread(sin8, g) * freq_ref[1:2, :] for g in range(groups)], axis=0)
    first_half = freq_ref[1:2, :] < 0.0

    def rope(t, c, s):
        partner = jnp.where(first_half, pltpu.roll(t, LANES - ROPE_HALF, 1), pltpu.roll(t, ROPE_HALF, 1))
        return t * c + partner * s

    qk_scale = DA_QK_DIM ** -0.5 * LOG2_E
    cos_q, sin_q = cos_t * qk_scale, sin_t * qk_scale
    pq = jnp.dot(h, w_ref[:, 0:QK_WIDTH], preferred_element_type=F32)
    for j in range(DA_HEADS):
        sl = slice(j * LANES, (j + 1) * LANES)
        q_ref[j] = rope(pq[:, sl], cos_q, sin_q).astype(BF16)
    pk = jnp.dot(h, w_ref[:, QK_WIDTH:2 * QK_WIDTH], preferred_element_type=F32)
    for j in range(DA_HEADS):
        sl = slice(j * LANES, (j + 1) * LANES)
        k_ref[j] = rope(pk[:, sl], cos_t, sin_t).astype(BF16)
    pv = jnp.dot(h, w_ref[:, 2 * QK_WIDTH:2 * QK_WIDTH + DA_WIDTH], preferred_element_type=F32)
    for j in range(DA_HEADS):
        v_ref[j] = pv[:, j * DA_V_DIM:(j + 1) * DA_V_DIM].astype(BF16)
    u = jnp.dot(h, w_ref[:, 2 * QK_WIDTH + DA_WIDTH:], preferred_element_type=F32)

    @pl.when(i == 0)
    def _():
        halo_ref[...] = jnp.zeros_like(halo_ref)

    prev = halo_ref[...]
    halo_ref[...] = u[tm - POOL_HALO:, :]
    tpos = i * tm + lax.broadcasted_iota(jnp.int32, (tm, 1), 0)
    for g, w in enumerate(POOL_WINDOWS):
        sl = slice(g * POOL_GROUP_DIM, (g + 1) * POOL_GROUP_DIM)
        ug = u[:, sl]
        s = jnp.concatenate([prev[:, sl], ug], axis=0)
        d = 1
        while d < w:
            s = s + pltpu.roll(s, d, 0)
            d *= 2
        inv_count = 1.0 / jnp.minimum(tpos + 1, w).astype(F32)
        pooled = s[POOL_HALO:, :] * inv_count - ug
        po = jnp.dot(pooled.astype(BF16), wp_ref[g], preferred_element_type=F32) * ps_ref[:, sl]
        po_ref[0, :, sl] = po.astype(BF16)


def _inproj_call(x, g, w_in, pos, freq, w_pool, pool_scale, tm):
    b, s, d = x.shape
    tok = lambda n: pl.BlockSpec((1, tm, n), lambda bi, i: (bi, i, 0))
    heads = pl.BlockSpec((None, DA_HEADS, tm, DA_V_DIM), lambda bi, i: (bi, 0, i, 0))
    head_major = jax.ShapeDtypeStruct((b, DA_HEADS, s, DA_V_DIM), BF16)
    return pl.pallas_call(
        functools.partial(_inproj_kernel, tm=tm),
        grid=(b, s // tm),
        in_specs=[tok(d), _const_spec((1, d)), _HBM,
                  pl.BlockSpec((1, 1, tm), lambda bi, i: (bi, 0, i)), _const_spec(freq.shape),
                  _const_spec(w_pool.shape), _const_spec((1, POOL_WIDTH))],
        out_specs=[heads, heads, heads, tok(POOL_WIDTH)],
        out_shape=[head_major, head_major, head_major,
                   jax.ShapeDtypeStruct((b, s, POOL_WIDTH), BF16)],
        scratch_shapes=[pltpu.VMEM((POOL_HALO, POOL_WIDTH), F32), pltpu.VMEM(w_in.shape, BF16),
                        *_stage_scratch(d, w_in.shape[1])],
        compiler_params=pltpu.CompilerParams(
            dimension_semantics=("arbitrary", "arbitrary"), vmem_limit_bytes=VMEM_LIMIT),
        name="inproj",
    )(x, g, w_in, pos, freq, w_pool, pool_scale)


def _attn_kernel(q_ref, k_ref, v_ref, lq1_ref, lk1_ref, lq2_ref, lk2_ref, gs_ref, o_ref,
                 m_ref, l_ref, acc_ref, qs_ref, s0_ref, s1_ref, p0_ref, p1_ref, *, t, nq):
    lane = lax.broadcasted_iota(jnp.int32, (t, DA_V_DIM), 1)
    key = lax.broadcasted_iota(jnp.int32, (t, t), 0)
    qry = lax.broadcasted_iota(jnp.int32, (t, t), 1)
    causal = jnp.concatenate([key <= qry, key <= qry], axis=1)

    def rows(i):
        return pl.ds(pl.multiple_of(i * t, t), t)

    def stack_maps(i, _):
        q = q_ref[rows(i), :]
        zero = jnp.zeros_like(q)
        qs_ref[i] = jnp.concatenate([jnp.where(lane < DA_QK_DIM, q, zero),
                                     jnp.where(lane >= DA_QK_DIM, q, zero)], axis=0)
        return 0

    lax.fori_loop(0, nq, stack_maps, 0)

    def scores(i, j):
        return lax.dot_general(k_ref[rows(j), :], qs_ref[i], (((1,), (1,)), ((), ())),
                               preferred_element_type=F32)

    def softmax(i, s_ref, p_ref, diagonal):
        s = s_ref[...]
        if diagonal:
            s = jnp.where(causal, s, -jnp.inf)
            m_new = jnp.max(s, axis=0, keepdims=True)
            p = jnp.exp2(s - m_new)
            alpha = None
            l_ref[i] = jnp.sum(p, axis=0, keepdims=True)
        else:
            m = m_ref[i]
            m_new = jnp.maximum(m, jnp.max(s, axis=0, keepdims=True))
            p = jnp.exp2(s - m_new)
            alpha = jnp.exp2(m - m_new)
            l_ref[i] = alpha * l_ref[i] + jnp.sum(p, axis=0, keepdims=True)
        m_ref[i] = m_new
        p_ref[...] = p.astype(BF16)
        return alpha

    def accumulate(i, j, p_ref, alpha):
        pv = lax.dot_general(v_ref[rows(j), :], p_ref[...], (((0,), (0,)), ((), ())),
                             preferred_element_type=F32)
        acc_ref[i] = pv if alpha is None else alpha * acc_ref[i] + pv

    s_bufs, p_bufs = (s0_ref, s1_ref), (p0_ref, p1_ref)

    def pipeline(n_units, unit, diagonal):
        assert n_units % 2 == 0 and n_units >= 2

        def stage(n, slot, alpha_prev):
            s_bufs[1 - slot][...] = scores(*unit(n + 1))
            alpha = softmax(unit(n)[0], s_bufs[slot], p_bufs[slot], diagonal)
            accumulate(*unit(n - 1), p_bufs[1 - slot], alpha_prev)
            return alpha

        s0_ref[...] = scores(*unit(0))
        s1_ref[...] = scores(*unit(1))
        alpha_first = softmax(unit(0)[0], s0_ref, p0_ref, diagonal)

        def body(h, alpha_prev):
            if diagonal:
                stage(2 * h + 2, 0, stage(2 * h + 1, 1, None))
                return alpha_prev
            return stage(2 * h + 2, 0, stage(2 * h + 1, 1, alpha_prev))

        alpha_even = lax.fori_loop(0, n_units // 2 - 1, body, 0 if diagonal else alpha_first)
        alpha_last = softmax(unit(n_units - 1)[0], s1_ref, p1_ref, diagonal)
        accumulate(*unit(n_units - 2), p0_ref, None if diagonal else alpha_even)
        accumulate(*unit(n_units - 1), p1_ref, alpha_last)

    def below_diagonal(n):
        n = jnp.asarray(n, jnp.int32)
        i = 1 + sum((n >= k * (k + 1) // 2).astype(jnp.int32) for k in range(1, nq - 1))
        return i, n - i * (i - 1) // 2

    pipeline(nq, lambda n: (n, n), True)
    pipeline(nq * (nq - 1) // 2, below_diagonal, False)

    lam = (jnp.exp(jnp.sum(lq1_ref[...] * lk1_ref[...], axis=-1, keepdims=True))
           - jnp.exp(jnp.sum(lq2_ref[...] * lk2_ref[...], axis=-1, keepdims=True))
           + LAMBDA_INIT)
    gain = gs_ref[...] * (1.0 - LAMBDA_INIT)
    for i in range(nq):
        o = acc_ref[i] * (1.0 / l_ref[i])
        da = o[:, :t] - lam * o[:, t:]
        inv = lax.rsqrt(jnp.mean(da * da, axis=0, keepdims=True) + EPS)
        o_ref[i * t:(i + 1) * t, :] = (da * inv * gain).T.astype(BF16)


def _attn_call(q, k, v, lq1, lk1, lq2, lk2, g_subln_col, t):
    b, _, s, _ = q.shape
    nq = s // t
    head = pl.BlockSpec((None, None, s, DA_V_DIM), lambda bi, h: (bi, h, 0, 0))
    vec = _const_spec((1, DA_QK_DIM))
    return pl.pallas_call(
        functools.partial(_attn_kernel, t=t, nq=nq),
        grid=(b, DA_HEADS),
        in_specs=[head, head, head, vec, vec, vec, vec, _const_spec((DA_V_DIM, 1))],
        out_specs=head,
        out_shape=jax.ShapeDtypeStruct((b, DA_HEADS, s, DA_V_DIM), BF16),
        scratch_shapes=[pltpu.VMEM((nq, 1, 2 * t), F32), pltpu.VMEM((nq, 1, 2 * t), F32),
                        pltpu.VMEM((nq, DA_V_DIM, 2 * t), F32),
                        pltpu.VMEM((nq, 2 * t, DA_V_DIM), BF16),
                        pltpu.VMEM((t, 2 * t), F32), pltpu.VMEM((t, 2 * t), F32),
                        pltpu.VMEM((t, 2 * t), BF16), pltpu.VMEM((t, 2 * t), BF16)],
        compiler_params=pltpu.CompilerParams(
            dimension_semantics=("arbitrary", "arbitrary"), vmem_limit_bytes=VMEM_LIMIT),
        name="diff_attn",
    )(q, k, v, lq1, lk1, lq2, lk2, g_subln_col)


def _mixout_kernel(da_ref, po_ref, x_ref, wo_hbm, gpost_ref, gxpre_ref, wxq_hbm, kv_ref, wxo_hbm,
                   gxpost_ref, o_ref, wo_ref, wxq_ref, wxo_ref, stage, sem):
    @pl.when(_first_step(2))
    def _():
        _load_weights([(wo_hbm, wo_ref), (wxq_hbm, wxq_ref), (wxo_hbm, wxo_ref)], stage, sem)

    da = jnp.concatenate([da_ref[h] for h in range(DA_HEADS)], axis=-1)
    mix = (jnp.dot(da, wo_ref[0:DA_WIDTH, :], preferred_element_type=F32)
           + jnp.dot(po_ref[0], wo_ref[DA_WIDTH:, :], preferred_element_type=F32))
    x1 = x_ref[0] + _rms(mix, gpost_ref[...])

    hq, inv = _prenorm_operand(x1, gxpre_ref[...])
    x_scale = X_HEAD_DIM ** -0.5
    xq = (jnp.dot(hq, wxq_ref[...], preferred_element_type=F32) * (inv * x_scale)).astype(BF16)
    heads = []
    for h in range(X_HEADS):
        sl = slice(h * X_HEAD_DIM, (h + 1) * X_HEAD_DIM)
        kh = kv_ref[0, :, sl]
        vh = kv_ref[0, :, D_MODEL + h * X_HEAD_DIM:D_MODEL + (h + 1) * X_HEAD_DIM]
        sc = lax.dot_general(xq[:, sl], kh, (((1,), (1,)), ((), ())), preferred_element_type=F32)
        p = jnp.exp(sc - jnp.max(sc, axis=-1, keepdims=True))
        pm = (p * (1.0 / jnp.sum(p, axis=-1, keepdims=True))).astype(BF16)
        heads.append(jnp.dot(pm, vh, preferred_element_type=F32).astype(BF16))
    xo = jnp.concatenate(heads, axis=-1)
    y = jnp.dot(xo, wxo_ref[...], preferred_element_type=F32)
    o_ref[0] = x1 + _rms(y, gxpost_ref[...])


def _mixout_call(da, po, x, w_out, g_mix_post, g_x_pre, w_xq, kv, w_xo, g_x_post, tm):
    b, s, d = x.shape
    tok = lambda n: pl.BlockSpec((1, tm, n), lambda bi, i: (bi, i, 0))
    gvec = _const_spec((1, d))
    return pl.pallas_call(
        _mixout_kernel,
        grid=(b, s // tm),
        in_specs=[pl.BlockSpec((None, DA_HEADS, tm, DA_V_DIM), lambda bi, i: (bi, 0, i, 0)),
                  tok(POOL_WIDTH), tok(d), _HBM, gvec, gvec, _HBM,
                  pl.BlockSpec((1,) + kv.shape[1:], lambda bi, i: (bi, 0, 0)),
                  _HBM, gvec],
        out_specs=tok(d),
        out_shape=jax.ShapeDtypeStruct((b, s, d), F32),
        scratch_shapes=[pltpu.VMEM(w_out.shape, BF16), pltpu.VMEM(w_xq.shape, BF16),
                        pltpu.VMEM(w_xo.shape, BF16), *_stage_scratch(d, d)],
        compiler_params=pltpu.CompilerParams(
            dimension_semantics=("arbitrary", "arbitrary"), vmem_limit_bytes=VMEM_LIMIT),
        name="mixout_xattn",
    )(da, po, x, w_out, g_mix_post, g_x_pre, w_xq, kv, w_xo, g_x_post)


def _ffn_kernel(x_ref, gpre_ref, wg_hbm, wu_hbm, wd_hbm, gpost_ref, o_ref,
                wg_ref, wu_ref, wd_ref, wide_stage, wide_sem, tall_stage, tall_sem):
    @pl.when(_first_step(2))
    def _():
        _load_weights([(wg_hbm, wg_ref), (wu_hbm, wu_ref)], wide_stage, wide_sem)
        _load_weights([(wd_hbm, wd_ref)], tall_stage, tall_sem)

    x = x_ref[0]
    hf, inv = _prenorm_operand(x, gpre_ref[...])
    ff = None
    for lo, hi in FF_CHUNKS:
        gate = jnp.dot(hf, wg_ref[:, lo:hi], preferred_element_type=F32) * inv
        up = jnp.dot(hf, wu_ref[:, lo:hi], preferred_element_type=F32) * inv
        act = (gate * (1.0 / (1.0 + jnp.exp(-gate))) * up).astype(BF16)
        part = jnp.dot(act, wd_ref[lo:hi, :], preferred_element_type=F32)
        ff = part if ff is None else ff + part
    o_ref[0] = x + _rms(ff, gpost_ref[...])


def _ffn_call(x, g_pre, w_gate, w_up, w_down, g_post, tm):
    b, s, d = x.shape
    tok = pl.BlockSpec((1, tm, d), lambda bi, i: (bi, i, 0))
    gvec = _const_spec((1, d))
    return pl.pallas_call(
        _ffn_kernel,
        grid=(b, s // tm),
        in_specs=[tok, gvec, _HBM, _HBM, _HBM, gvec],
        out_specs=tok,
        out_shape=jax.ShapeDtypeStruct((b, s, d), F32),
        scratch_shapes=[pltpu.VMEM(w_gate.shape, BF16), pltpu.VMEM(w_up.shape, BF16),
                        pltpu.VMEM(w_down.shape, BF16),
                        *_stage_scratch(d, w_gate.shape[1]),
                        *_stage_scratch(w_down.shape[0], d)],
        compiler_params=pltpu.CompilerParams(
            dimension_semantics=("arbitrary", "arbitrary"), vmem_limit_bytes=VMEM_LIMIT),
        name="swiglu",
    )(x, g_pre, w_gate, w_up, w_down, g_post)


def _rope_lane_constants():
    inv_freq = ROPE_THETA ** (-jnp.arange(0, ROPE_DIM, 2, dtype=F32) / ROPE_DIM)
    packed_freq = jnp.tile(inv_freq, LANES // ROPE_HALF)
    one_half = jnp.ones((ROPE_HALF,), F32)
    per_map = jnp.concatenate([-one_half, one_half, jnp.zeros((DA_QK_DIM - ROPE_DIM,), F32)])
    sign = jnp.concatenate([per_map, per_map])
    return jnp.concatenate([jnp.stack([packed_freq, sign]), jnp.zeros((6, LANES), F32)], axis=0)


def kernel(x, mem, positions, g_mix_pre, w_in, lambda_q1, lambda_k1, lambda_q2, lambda_k2,
           g_subln, w_pool, pool_scale, w_out, g_mix_post, g_x_pre, g_mem, w_xq, w_xkv, w_xo,
           g_x_post, g_ffn_pre, w_gate, w_up, w_down, g_ffn_post):
    b, s, d = x.shape
    assert d == D_MODEL and s % TOKEN_TILE == 0 and s % ATTN_TILE == 0
    row = lambda a: a.reshape(1, -1).astype(F32)
    bf = lambda a: a.astype(BF16)

    pos = positions.reshape(b, 1, s)
    kv = _kv_call(mem, row(g_mem), w_xkv)
    q, k, v, po = _inproj_call(x, row(g_mix_pre), w_in, pos, _rope_lane_constants(),
                               bf(w_pool), row(pool_scale), TOKEN_TILE)
    da = _attn_call(q, k, v, row(lambda_q1), row(lambda_k1), row(lambda_q2), row(lambda_k2),
                    g_subln.reshape(-1, 1).astype(F32), ATTN_TILE)
    x2 = _mixout_call(da, po, x, w_out, row(g_mix_post), row(g_x_pre), w_xq, kv, w_xo,
                      row(g_x_post), TOKEN_TILE)
    return _ffn_call(x2, row(g_ffn_pre), w_gate, w_up, w_down, row(g_ffn_post), TOKEN_TILE)
```

```python
import functools
import math

import jax
import jax.numpy as jnp
from jax import lax
from jax.experimental import pallas as pl
from jax.experimental.pallas import tpu as pltpu

F32 = jnp.float32
BF16 = jnp.bfloat16

D_MODEL = 1024
MEM_LEN = 256
EPS = 1e-6
DA_HEADS = 4
DA_QK_DIM = 64
DA_V_DIM = 2 * DA_QK_DIM
DA_WIDTH = DA_HEADS * DA_V_DIM
QK_WIDTH = DA_HEADS * 2 * DA_QK_DIM
POOL_WINDOWS = (2, 4, 8, 16)
POOL_WIDTH = D_MODEL - DA_WIDTH
POOL_GROUP_DIM = POOL_WIDTH // len(POOL_WINDOWS)
ROPE_THETA = 500000.0
ROPE_DIM = DA_QK_DIM // 4
ROPE_HALF = ROPE_DIM // 2
X_HEADS = 4
X_HEAD_DIM = D_MODEL // X_HEADS
D_FF = -(-(8 * D_MODEL) // (3 * 256)) * 256
LAMBDA_INIT = 0.8 - 0.6 * math.exp(-0.3 * 0)
LOG2_E = math.log2(math.e)

LANES = 128
POOL_HALO = 16
VMEM_LIMIT = 56 * 1024 * 1024

TOKEN_TILE = 512
ATTN_TILE = 512
FF_CHUNKS = ((0, 1536), (1536, 2816))
STAGE_BYTES = 4 * 1024 * 1024


def _inv_rms(x):
    return lax.rsqrt(jnp.mean(x * x, axis=-1, keepdims=True) + EPS)


def _rms(x, g):
    return x * _inv_rms(x) * g


def _prenorm_operand(x, g):
    return (x * g).astype(BF16), _inv_rms(x)


def _const_spec(shape):
    zeros = (0,) * len(shape)
    return pl.BlockSpec(shape, lambda *_: zeros, pipeline_mode=pl.Buffered(1))


_HBM = pl.BlockSpec(memory_space=pl.ANY)


def _stage_scratch(rows, cols):
    chunks = -(-(rows * cols * 4) // STAGE_BYTES)
    while rows % chunks or (rows // chunks) % 16:
        chunks += 1
    return [pltpu.VMEM((2, rows // chunks, cols), F32), pltpu.SemaphoreType.DMA((2,))]


def _load_weights(weights, stage_ref, sem_ref):
    rows = stage_ref.shape[1]
    jobs = []
    for w_hbm, w_vmem in weights:
        assert w_hbm.shape[0] % rows == 0 and stage_ref.shape[2] == w_hbm.shape[1]
        jobs += [(w_hbm, w_vmem, r) for r in range(0, w_hbm.shape[0], rows)]

    def copy(k):
        w_hbm, _, r = jobs[k]
        return pltpu.make_async_copy(w_hbm.at[pl.ds(r, rows), :], stage_ref.at[k % 2], sem_ref.at[k % 2])

    copy(0).start()
    for k, (_, w_vmem, r) in enumerate(jobs):
        if k + 1 < len(jobs):
            copy(k + 1).start()
        copy(k).wait()
        w_vmem[r:r + rows, :] = stage_ref[k % 2].astype(BF16)


def _first_step(grid_rank):
    first = pl.program_id(0) == 0
    for axis in range(1, grid_rank):
        first = first & (pl.program_id(axis) == 0)
    return first


def _kv_kernel(mem_ref, g_ref, w_hbm, kv_ref, w_ref, stage, sem):
    @pl.when(_first_step(1))
    def _():
        _load_weights([(w_hbm, w_ref)], stage, sem)

    mn = _rms(mem_ref[0], g_ref[...]).astype(BF16)
    kv_ref[0] = jnp.dot(mn, w_ref[...], preferred_element_type=F32).astype(BF16)


def _kv_call(mem, g_mem, w_xkv):
    b, m, d = mem.shape
    n = w_xkv.shape[1]
    return pl.pallas_call(
        _kv_kernel,
        grid=(b,),
        in_specs=[pl.BlockSpec((1, m, d), lambda i: (i, 0, 0)),
                  _const_spec((1, d)), _HBM],
        out_specs=pl.BlockSpec((1, m, n), lambda i: (i, 0, 0)),
        out_shape=jax.ShapeDtypeStruct((b, m, n), BF16),
        scratch_shapes=[pltpu.VMEM((d, n), BF16), *_stage_scratch(d, n)],
        compiler_params=pltpu.CompilerParams(
            dimension_semantics=("arbitrary",), vmem_limit_bytes=VMEM_LIMIT),
        name="kv_proj",
    )(mem, g_mem, w_xkv)


def _inproj_kernel(x_ref, g_ref, w_hbm, pos_ref, freq_ref, wp_ref, ps_ref,
                   q_ref, k_ref, v_ref, po_ref, halo_ref, w_ref, stage, sem, *, tm):
    @pl.when(_first_step(2))
    def _():
        _load_weights([(w_hbm, w_ref)], stage, sem)

    i = pl.program_id(1)
    h = _rms(x_ref[0], g_ref[...]).astype(BF16)

    groups = LANES // ROPE_DIM
    rows = tm // groups
    lane = lax.broadcasted_iota(jnp.int32, (rows, LANES), 1)
    lane_group = lax.shift_right_logical(lane, int(math.log2(ROPE_DIM)))
    pos = jnp.broadcast_to(pos_ref[0].astype(F32), (LANES, tm)).T
    packed = jnp.zeros((rows, LANES), F32)
    for g in range(groups):
        packed = jnp.where(lane_group == g, pos[g * rows:(g + 1) * rows], packed)
    ang = packed * freq_ref[0:1, :]
    cos8, sin8 = jnp.cos(ang), jnp.sin(ang)
    first_map = lane < DA_QK_DIM
    rotary = freq_ref[1:2, :] != 0.0

    def spread(packed_table, g):
        shift = (LANES - ROPE_DIM * g) % LANES
        lo = pltpu.roll(packed_table, shift, 1) if shift else packed_table
        hi_shift = (shift + DA_QK_DIM) % LANES
        hi = pltpu.roll(packed_table, hi_shift, 1) if hi_shift else packed_table
        return jnp.where(first_map, lo, hi)

    cos_t = jnp.concatenate([jnp.where(rotary, spread(cos8, g), 1.0) for g in range(groups)], axis=0)
    sin_t = jnp.concatenate([spread(sin8, g) * freq_ref[1:2, :] for g in range(groups)], axis=0)
    first_half = freq_ref[1:2, :] < 0.0

    def rope(t, c, s):
        partner = jnp.where(first_half, pltpu.roll(t, LANES - ROPE_HALF, 1), pltpu.roll(t, ROPE_HALF, 1))
        return t * c + partner * s

    qk_scale = DA_QK_DIM ** -0.5 * LOG2_E
    cos_q, sin_q = cos_t * qk_scale, sin_t * qk_scale
    pq = jnp.dot(h, w_ref[:, 0:QK_WIDTH], preferred_element_type=F32)
    for j in range(DA_HEADS):
        sl = slice(j * LANES, (j + 1) * LANES)
        q_ref[j] = rope(pq[:, sl], cos_q, sin_q).astype(BF16)
    pk = jnp.dot(h, w_ref[:, QK_WIDTH:2 * QK_WIDTH], preferred_element_type=F32)
    for j in range(DA_HEADS):
        sl = slice(j * LANES, (j + 1) * LANES)
        k_ref[j] = rope(pk[:, sl], cos_t, sin_t).astype(BF16)
    pv = jnp.dot(h, w_ref[:, 2 * QK_WIDTH:2 * QK_WIDTH + DA_WIDTH], preferred_element_type=F32)
    for j in range(DA_HEADS):
        v_ref[j] = pv[:, j * DA_V_DIM:(j + 1) * DA_V_DIM].astype(BF16)
    u = jnp.dot(h, w_ref[:, 2 * QK_WIDTH + DA_WIDTH:], preferred_element_type=F32)

    @pl.when(i == 0)
    def _():
        halo_ref[...] = jnp.zeros_like(halo_ref)

    prev = halo_ref[...]
    halo_ref[...] = u[tm - POOL_HALO:, :]
    tpos = i * tm + lax.broadcasted_iota(jnp.int32, (tm, 1), 0)
    for g, w in enumerate(POOL_WINDOWS):
        sl = slice(g * POOL_GROUP_DIM, (g + 1) * POOL_GROUP_DIM)
        ug = u[:, sl]
        s = jnp.concatenate([prev[:, sl], ug], axis=0)
        d = 1
        while d < w:
            s = s + pltpu.roll(s, d, 0)
            d *= 2
        inv_count = 1.0 / jnp.minimum(tpos + 1, w).astype(F32)
        pooled = s[POOL_HALO:, :] * inv_count - ug
        po = jnp.dot(pooled.astype(BF16), wp_ref[g], preferred_element_type=F32) * ps_ref[:, sl]
        po_ref[0, :, sl] = po.astype(BF16)


def _inproj_call(x, g, w_in, pos, freq, w_pool, pool_scale, tm):
    b, s, d = x.shape
    tok = lambda n: pl.BlockSpec((1, tm, n), lambda bi, i: (bi, i, 0))
    heads = pl.BlockSpec((None, DA_HEADS, tm, DA_V_DIM), lambda bi, i: (bi, 0, i, 0))
    head_major = jax.ShapeDtypeStruct((b, DA_HEADS, s, DA_V_DIM), BF16)
    return pl.pallas_call(
        functools.partial(_inproj_kernel, tm=tm),
        grid=(b, s // tm),
        in_specs=[tok(d), _const_spec((1, d)), _HBM,
                  pl.BlockSpec((1, 1, tm), lambda bi, i: (bi, 0, i)), _const_spec(freq.shape),
                  _const_spec(w_pool.shape), _const_spec((1, POOL_WIDTH))],
        out_specs=[heads, heads, heads, tok(POOL_WIDTH)],
        out_shape=[head_major, head_major, head_major,
                   jax.ShapeDtypeStruct((b, s, POOL_WIDTH), BF16)],
        scratch_shapes=[pltpu.VMEM((POOL_HALO, POOL_WIDTH), F32), pltpu.VMEM(w_in.shape, BF16),
                        *_stage_scratch(d, w_in.shape[1])],
        compiler_params=pltpu.CompilerParams(
            dimension_semantics=("arbitrary", "arbitrary"), vmem_limit_bytes=VMEM_LIMIT),
        name="inproj",
    )(x, g, w_in, pos, freq, w_pool, pool_scale)


def _attn_kernel(q_ref, k_ref, v_ref, lq1_ref, lk1_ref, lq2_ref, lk2_ref, gs_ref, o_ref, *, t, nq):
    lane = lax.broadcasted_iota(jnp.int32, (t, DA_V_DIM), 1)
    key = lax.broadcasted_iota(jnp.int32, (t, t), 0)
    qry = lax.broadcasted_iota(jnp.int32, (t, t), 1)
    causal = jnp.concatenate([key <= qry, key <= qry], axis=1)
    lam = (jnp.exp(jnp.sum(lq1_ref[...] * lk1_ref[...], axis=-1, keepdims=True))
           - jnp.exp(jnp.sum(lq2_ref[...] * lk2_ref[...], axis=-1, keepdims=True))
           + LAMBDA_INIT)
    gain = gs_ref[...] * (1.0 - LAMBDA_INIT)
    nt = (((1,), (1,)), ((), ()))
    tn = (((0,), (0,)), ((), ()))

    for i in range(nq):
        kend = (i + 1) * t
        q = q_ref[i * t:kend, :]
        zero = jnp.zeros_like(q)
        qs = jnp.concatenate([jnp.where(lane < DA_QK_DIM, q, zero),
                              jnp.where(lane >= DA_QK_DIM, q, zero)], axis=0)
        s = lax.dot_general(k_ref[0:kend, :], qs, nt, preferred_element_type=F32)
        diag = jnp.where(causal, s[kend - t:, :], -jnp.inf)
        s = diag if i == 0 else jnp.concatenate([s[:kend - t, :], diag], axis=0)
        p = jnp.exp2(s - jnp.max(s, axis=0, keepdims=True))
        l = jnp.sum(p, axis=0, keepdims=True)
        inv_l1 = 1.0 / l[:, :t]
        ratio = lam * l[:, :t] / l[:, t:]
        a = (p[:, :t] - ratio * p[:, t:]).astype(BF16)
        o = lax.dot_general(v_ref[0:kend, :], a, tn, preferred_element_type=F32) * inv_l1
        inv = lax.rsqrt(jnp.mean(o * o, axis=0, keepdims=True) + EPS)
        o_ref[i * t:kend, :] = (o * inv * gain).T.astype(BF16)


def _attn_call(q, k, v, lq1, lk1, lq2, lk2, g_subln_col, t):
    b, _, s, _ = q.shape
    nq = s // t
    head = pl.BlockSpec((None, None, s, DA_V_DIM), lambda bi, h: (bi, h, 0, 0))
    vec = _const_spec((1, DA_QK_DIM))
    return pl.pallas_call(
        functools.partial(_attn_kernel, t=t, nq=nq),
        grid=(b, DA_HEADS),
        in_specs=[head, head, head, vec, vec, vec, vec, _const_spec((DA_V_DIM, 1))],
        out_specs=head,
        out_shape=jax.ShapeDtypeStruct((b, DA_HEADS, s, DA_V_DIM), BF16),
        compiler_params=pltpu.CompilerParams(
            dimension_semantics=("arbitrary", "arbitrary"), vmem_limit_bytes=VMEM_LIMIT),
        name="diff_attn",
    )(q, k, v, lq1, lk1, lq2, lk2, g_subln_col)


def _mixout_kernel(da_ref, po_ref, x_ref, wo_hbm, gpost_ref, gxpre_ref, wxq_hbm, kv_ref, wxo_hbm,
                   gxpost_ref, o_ref, wo_ref, wxq_ref, wxo_ref, stage, sem):
    @pl.when(_first_step(2))
    def _():
        _load_weights([(wo_hbm, wo_ref), (wxq_hbm, wxq_ref), (wxo_hbm, wxo_ref)], stage, sem)

    da = jnp.concatenate([da_ref[h] for h in range(DA_HEADS)], axis=-1)
    mix = (jnp.dot(da, wo_ref[0:DA_WIDTH, :], preferred_element_type=F32)
           + jnp.dot(po_ref[0], wo_ref[DA_WIDTH:, :], preferred_element_type=F32))
    x1 = x_ref[0] + _rms(mix, gpost_ref[...])

    hq, inv = _prenorm_operand(x1, gxpre_ref[...])
    x_scale = X_HEAD_DIM ** -0.5
    xq = (jnp.dot(hq, wxq_ref[...], preferred_element_type=F32) * (inv * x_scale)).astype(BF16)
    heads = []
    for h in range(X_HEADS):
        sl = slice(h * X_HEAD_DIM, (h + 1) * X_HEAD_DIM)
        kh = kv_ref[0, :, sl]
        vh = kv_ref[0, :, D_MODEL + h * X_HEAD_DIM:D_MODEL + (h + 1) * X_HEAD_DIM]
        sc = lax.dot_general(xq[:, sl], kh, (((1,), (1,)), ((), ())), preferred_element_type=F32)
        p = jnp.exp(sc - jnp.max(sc, axis=-1, keepdims=True))
        pm = (p * (1.0 / jnp.sum(p, axis=-1, keepdims=True))).astype(BF16)
        heads.append(jnp.dot(pm, vh, preferred_element_type=F32).astype(BF16))
    xo = jnp.concatenate(heads, axis=-1)
    y = jnp.dot(xo, wxo_ref[...], preferred_element_type=F32)
    o_ref[0] = x1 + _rms(y, gxpost_ref[...])


def _mixout_call(da, po, x, w_out, g_mix_post, g_x_pre, w_xq, kv, w_xo, g_x_post, tm):
    b, s, d = x.shape
    tok = lambda n: pl.BlockSpec((1, tm, n), lambda bi, i: (bi, i, 0))
    gvec = _const_spec((1, d))
    return pl.pallas_call(
        _mixout_kernel,
        grid=(b, s // tm),
        in_specs=[pl.BlockSpec((None, DA_HEADS, tm, DA_V_DIM), lambda bi, i: (bi, 0, i, 0)),
                  tok(POOL_WIDTH), tok(d), _HBM, gvec, gvec, _HBM,
                  pl.BlockSpec((1,) + kv.shape[1:], lambda bi, i: (bi, 0, 0)),
                  _HBM, gvec],
        out_specs=tok(d),
        out_shape=jax.ShapeDtypeStruct((b, s, d), F32),
        scratch_shapes=[pltpu.VMEM(w_out.shape, BF16), pltpu.VMEM(w_xq.shape, BF16),
                        pltpu.VMEM(w_xo.shape, BF16), *_stage_scratch(d, d)],
        compiler_params=pltpu.CompilerParams(
            dimension_semantics=("arbitrary", "arbitrary"), vmem_limit_bytes=VMEM_LIMIT),
        name="mixout_xattn",
    )(da, po, x, w_out, g_mix_post, g_x_pre, w_xq, kv, w_xo, g_x_post)


def _ffn_kernel(x_ref, gpre_ref, wg_hbm, wu_hbm, wd_hbm, gpost_ref, o_ref,
                wg_ref, wu_ref, wd_ref, wide_stage, wide_sem, tall_stage, tall_sem):
    @pl.when(_first_step(2))
    def _():
        _load_weights([(wg_hbm, wg_ref), (wu_hbm, wu_ref)], wide_stage, wide_sem)
        _load_weights([(wd_hbm, wd_ref)], tall_stage, tall_sem)

    x = x_ref[0]
    hf, inv = _prenorm_operand(x, gpre_ref[...])
    ff = None
    for lo, hi in FF_CHUNKS:
        gate = jnp.dot(hf, wg_ref[:, lo:hi], preferred_element_type=F32) * inv
        up = jnp.dot(hf, wu_ref[:, lo:hi], preferred_element_type=F32) * inv
        act = (gate * (1.0 / (1.0 + jnp.exp(-gate))) * up).astype(BF16)
        part = jnp.dot(act, wd_ref[lo:hi, :], preferred_element_type=F32)
        ff = part if ff is None else ff + part
    o_ref[0] = x + _rms(ff, gpost_ref[...])


def _ffn_call(x, g_pre, w_gate, w_up, w_down, g_post, tm):
    b, s, d = x.shape
    tok = pl.BlockSpec((1, tm, d), lambda bi, i: (bi, i, 0))
    gvec = _const_spec((1, d))
    return pl.pallas_call(
        _ffn_kernel,
        grid=(b, s // tm),
        in_specs=[tok, gvec, _HBM, _HBM, _HBM, gvec],
        out_specs=tok,
        out_shape=jax.ShapeDtypeStruct((b, s, d), F32),
        scratch_shapes=[pltpu.VMEM(w_gate.shape, BF16), pltpu.VMEM(w_up.shape, BF16),
                        pltpu.VMEM(w_down.shape, BF16),
                        *_stage_scratch(d, w_gate.shape[1]),
                        *_stage_scratch(w_down.shape[0], d)],
        compiler_params=pltpu.CompilerParams(
            dimension_semantics=("arbitrary", "arbitrary"), vmem_limit_bytes=VMEM_LIMIT),
        name="swiglu",
    )(x, g_pre, w_gate, w_up, w_down, g_post)


def _rope_lane_constants():
    inv_freq = ROPE_THETA ** (-jnp.arange(0, ROPE_DIM, 2, dtype=F32) / ROPE_DIM)
    packed_freq = jnp.tile(inv_freq, LANES // ROPE_HALF)
    one_half = jnp.ones((ROPE_HALF,), F32)
    per_map = jnp.concatenate([-one_half, one_half, jnp.zeros((DA_QK_DIM - ROPE_DIM,), F32)])
    sign = jnp.concatenate([per_map, per_map])
    return jnp.concatenate([jnp.stack([packed_freq, sign]), jnp.zeros((6, LANES), F32)], axis=0)


def kernel(x, mem, positions, g_mix_pre, w_in, lambda_q1, lambda_k1, lambda_q2, lambda_k2,
           g_subln, w_pool, pool_scale, w_out, g_mix_post, g_x_pre, g_mem, w_xq, w_xkv, w_xo,
           g_x_post, g_ffn_pre, w_gate, w_up, w_down, g_ffn_post):
    b, s, d = x.shape
    assert d == D_MODEL and s % TOKEN_TILE == 0 and s % ATTN_TILE == 0
    row = lambda a: a.reshape(1, -1).astype(F32)
    bf = lambda a: a.astype(BF16)

    pos = positions.reshape(b, 1, s)
    kv = _kv_call(mem, row(g_mem), w_xkv)
    q, k, v, po = _inproj_call(x, row(g_mix_pre), w_in, pos, _rope_lane_constants(),
                               bf(w_pool), row(pool_scale), TOKEN_TILE)
    da = _attn_call(q, k, v, row(lambda_q1), row(lambda_k1), row(lambda_q2), row(lambda_k2),
                    g_subln.reshape(-1, 1).astype(F32), ATTN_TILE)
    x2 = _mixout_call(da, po, x, w_out, row(g_mix_post), row(g_x_pre), w_xq, kv, w_xo,
                      row(g_x_post), TOKEN_TILE)
    return _ffn_call(x2, row(g_ffn_pre), w_gate, w_up, w_down, row(g_ffn_post), TOKEN_TILE)
```

```python
import functools
import itertools
import math

import jax
import jax.numpy as jnp
from jax import lax
from jax.experimental import pallas as pl
from jax.experimental.pallas import tpu as pltpu

F32 = jnp.float32
BF16 = jnp.bfloat16

D_MODEL = 1024
MEM_LEN = 256
EPS = 1e-6
DA_HEADS = 4
DA_QK_DIM = 64
DA_V_DIM = 2 * DA_QK_DIM
DA_WIDTH = DA_HEADS * DA_V_DIM
QK_WIDTH = DA_HEADS * 2 * DA_QK_DIM
POOL_WINDOWS = (2, 4, 8, 16)
POOL_WIDTH = D_MODEL - DA_WIDTH
POOL_GROUP_DIM = POOL_WIDTH // len(POOL_WINDOWS)
ROPE_THETA = 500000.0
ROPE_DIM = DA_QK_DIM // 4
ROPE_HALF = ROPE_DIM // 2
X_HEADS = 4
X_HEAD_DIM = D_MODEL // X_HEADS
D_FF = -(-(8 * D_MODEL) // (3 * 256)) * 256
LAMBDA_INIT = 0.8 - 0.6 * math.exp(-0.3 * 0)
LOG2_E = math.log2(math.e)

LANES = 128
POOL_HALO = 16
VMEM_LIMIT = 56 * 1024 * 1024

TOKEN_TILE = 512
ATTN_TILE = 512
ATTN_CHUNK = 256
FF_CHUNKS = ((0, 1536), (1536, 2816))
STAGE_BYTES = 4 * 1024 * 1024


def _inv_rms(x):
    return lax.rsqrt(jnp.mean(x * x, axis=-1, keepdims=True) + EPS)


def _rms(x, g):
    return x * _inv_rms(x) * g


def _prenorm_operand(x, g):
    return (x * g).astype(BF16), _inv_rms(x)


def _const_spec(shape):
    zeros = (0,) * len(shape)
    return pl.BlockSpec(shape, lambda *_: zeros, pipeline_mode=pl.Buffered(1))


_HBM = pl.BlockSpec(memory_space=pl.ANY)


def _stage_scratch(rows, cols):
    chunks = -(-(rows * cols * 4) // STAGE_BYTES)
    while rows % chunks or (rows // chunks) % 16:
        chunks += 1
    return [pltpu.VMEM((2, rows // chunks, cols), F32), pltpu.SemaphoreType.DMA((2,))]


def _load_weights(weights, stage_ref, sem_ref):
    rows = stage_ref.shape[1]
    jobs = []
    for w_hbm, w_vmem in weights:
        assert w_hbm.shape[0] % rows == 0 and stage_ref.shape[2] == w_hbm.shape[1]
        jobs += [(w_hbm, w_vmem, r) for r in range(0, w_hbm.shape[0], rows)]

    def copy(k):
        w_hbm, _, r = jobs[k]
        return pltpu.make_async_copy(w_hbm.at[pl.ds(r, rows), :], stage_ref.at[k % 2], sem_ref.at[k % 2])

    copy(0).start()
    for k, (_, w_vmem, r) in enumerate(jobs):
        if k + 1 < len(jobs):
            copy(k + 1).start()
        copy(k).wait()
        w_vmem[r:r + rows, :] = stage_ref[k % 2].astype(BF16)


def _first_step(grid_rank):
    first = pl.program_id(0) == 0
    for axis in range(1, grid_rank):
        first = first & (pl.program_id(axis) == 0)
    return first


def _kv_kernel(mem_ref, g_ref, w_hbm, kv_ref, w_ref, stage, sem):
    @pl.when(_first_step(1))
    def _():
        _load_weights([(w_hbm, w_ref)], stage, sem)

    mn = _rms(mem_ref[0], g_ref[...]).astype(BF16)
    kv_ref[0] = jnp.dot(mn, w_ref[...], preferred_element_type=F32).astype(BF16)


def _kv_call(mem, g_mem, w_xkv):
    b, m, d = mem.shape
    n = w_xkv.shape[1]
    return pl.pallas_call(
        _kv_kernel,
        grid=(b,),
        in_specs=[pl.BlockSpec((1, m, d), lambda i: (i, 0, 0)),
                  _const_spec((1, d)), _HBM],
        out_specs=pl.BlockSpec((1, m, n), lambda i: (i, 0, 0)),
        out_shape=jax.ShapeDtypeStruct((b, m, n), BF16),
        scratch_shapes=[pltpu.VMEM((d, n), BF16), *_stage_scratch(d, n)],
        compiler_params=pltpu.CompilerParams(
            dimension_semantics=("arbitrary",), vmem_limit_bytes=VMEM_LIMIT),
        name="kv_proj",
    )(mem, g_mem, w_xkv)


def _inproj_kernel(x_ref, g_ref, w_hbm, pos_ref, freq_ref, wp_ref, ps_ref,
                   q_ref, k_ref, v_ref, po_ref, halo_ref, w_ref, stage, sem, *, tm):
    @pl.when(_first_step(2))
    def _():
        _load_weights([(w_hbm, w_ref)], stage, sem)

    i = pl.program_id(1)
    h = _rms(x_ref[0], g_ref[...]).astype(BF16)

    groups = LANES // ROPE_DIM
    rows = tm // groups
    lane = lax.broadcasted_iota(jnp.int32, (rows, LANES), 1)
    lane_group = lax.shift_right_logical(lane, int(math.log2(ROPE_DIM)))
    pos = jnp.broadcast_to(pos_ref[0].astype(F32), (LANES, tm)).T
    packed = jnp.zeros((rows, LANES), F32)
    for g in range(groups):
        packed = jnp.where(lane_group == g, pos[g * rows:(g + 1) * rows], packed)
    ang = packed * freq_ref[0:1, :]
    cos8, sin8 = jnp.cos(ang), jnp.sin(ang)
    first_map = lane < DA_QK_DIM
    rotary = freq_ref[1:2, :] != 0.0

    def spread(packed_table, g):
        shift = (LANES - ROPE_DIM * g) % LANES
        lo = pltpu.roll(packed_table, shift, 1) if shift else packed_table
        hi_shift = (shift + DA_QK_DIM) % LANES
        hi = pltpu.roll(packed_table, hi_shift, 1) if hi_shift else packed_table
        return jnp.where(first_map, lo, hi)

    cos_t = jnp.concatenate([jnp.where(rotary, spread(cos8, g), 1.0) for g in range(groups)], axis=0)
    sin_t = jnp.concatenate([spread(sin8, g) * freq_ref[1:2, :] for g in range(groups)], axis=0)
    first_half = freq_ref[1:2, :] < 0.0

    def rope(t, c, s):
        partner = jnp.where(first_half, pltpu.roll(t, LANES - ROPE_HALF, 1), pltpu.roll(t, ROPE_HALF, 1))
        return t * c + partner * s

    qk_scale = DA_QK_DIM ** -0.5 * LOG2_E
    cos_q, sin_q = cos_t * qk_scale, sin_t * qk_scale
    pq = jnp.dot(h, w_ref[:, 0:QK_WIDTH], preferred_element_type=F32)
    for j in range(DA_HEADS):
        sl = slice(j * LANES, (j + 1) * LANES)
        q_ref[j] = rope(pq[:, sl], cos_q, sin_q).astype(BF16)
    pk = jnp.dot(h, w_ref[:, QK_WIDTH:2 * QK_WIDTH], preferred_element_type=F32)
    for j in range(DA_HEADS):
        sl = slice(j * LANES, (j + 1) * LANES)
        k_ref[j] = rope(pk[:, sl], cos_t, sin_t).astype(BF16)
    pv = jnp.dot(h, w_ref[:, 2 * QK_WIDTH:2 * QK_WIDTH + DA_WIDTH], preferred_element_type=F32)
    for j in range(DA_HEADS):
        v_ref[j] = pv[:, j * DA_V_DIM:(j + 1) * DA_V_DIM].astype(BF16)
    u = jnp.dot(h, w_ref[:, 2 * QK_WIDTH + DA_WIDTH:], preferred_element_type=F32)

    @pl.when(i == 0)
    def _():
        halo_ref[...] = jnp.zeros_like(halo_ref)

    prev = halo_ref[...]
    halo_ref[...] = u[tm - POOL_HALO:, :]
    tpos = i * tm + lax.broadcasted_iota(jnp.int32, (tm, 1), 0)
    for g, w in enumerate(POOL_WINDOWS):
        sl = slice(g * POOL_GROUP_DIM, (g + 1) * POOL_GROUP_DIM)
        ug = u[:, sl]
        s = jnp.concatenate([prev[:, sl], ug], axis=0)
        d = 1
        while d < w:
            s = s + pltpu.roll(s, d, 0)
            d *= 2
        inv_count = 1.0 / jnp.minimum(tpos + 1, w).astype(F32)
        pooled = s[POOL_HALO:, :] * inv_count - ug
        po = jnp.dot(pooled.astype(BF16), wp_ref[g], preferred_element_type=F32) * ps_ref[:, sl]
        po_ref[0, :, sl] = po.astype(BF16)


def _inproj_call(x, g, w_in, pos, freq, w_pool, pool_scale, tm):
    b, s, d = x.shape
    tok = lambda n: pl.BlockSpec((1, tm, n), lambda bi, i: (bi, i, 0))
    heads = pl.BlockSpec((None, DA_HEADS, tm, DA_V_DIM), lambda bi, i: (bi, 0, i, 0))
    head_major = jax.ShapeDtypeStruct((b, DA_HEADS, s, DA_V_DIM), BF16)
    return pl.pallas_call(
        functools.partial(_inproj_kernel, tm=tm),
        grid=(b, s // tm),
        in_specs=[tok(d), _const_spec((1, d)), _HBM,
                  pl.BlockSpec((1, 1, tm), lambda bi, i: (bi, 0, i)), _const_spec(freq.shape),
                  _const_spec(w_pool.shape), _const_spec((1, POOL_WIDTH))],
        out_specs=[heads, heads, heads, tok(POOL_WIDTH)],
        out_shape=[head_major, head_major, head_major,
                   jax.ShapeDtypeStruct((b, s, POOL_WIDTH), BF16)],
        scratch_shapes=[pltpu.VMEM((POOL_HALO, POOL_WIDTH), F32), pltpu.VMEM(w_in.shape, BF16),
                        *_stage_scratch(d, w_in.shape[1])],
        compiler_params=pltpu.CompilerParams(
            dimension_semantics=("arbitrary", "arbitrary"), vmem_limit_bytes=VMEM_LIMIT),
        name="inproj",
    )(x, g, w_in, pos, freq, w_pool, pool_scale)


def _interleave(*streams):
    tagged = [((k + 0.5) / len(ops), n, op) for n, ops in enumerate(streams) for k, op in enumerate(ops)]
    return [op for _, _, op in sorted(tagged, key=lambda e: e[:2])]


def _tile_order(nq):
    def length(order):
        total = 0.0
        for step in range(nq + 2):
            scores = order[step] + 1 if step < nq else 0
            probs = order[step - 1] + 1 if 0 <= step - 1 < nq else 0
            values = (order[step - 2] + 1) / 2 if 0 <= step - 2 < nq else 0
            total += max(scores + values, probs)
        return total

    return min(itertools.permutations(range(nq)), key=length)


def _attn_kernel(q_ref, k_ref, v_ref, lq1_ref, lk1_ref, lq2_ref, lk2_ref, gs_ref, o_ref, *scratch,
                 t, nq, chunk):
    lane = lax.broadcasted_iota(jnp.int32, (t, DA_V_DIM), 1)
    key = lax.broadcasted_iota(jnp.int32, (chunk, t), 0)
    qry = lax.broadcasted_iota(jnp.int32, (chunk, t), 1)
    lam = (jnp.exp(jnp.sum(lq1_ref[...] * lk1_ref[...], axis=-1, keepdims=True))
           - jnp.exp(jnp.sum(lq2_ref[...] * lk2_ref[...], axis=-1, keepdims=True))
           + LAMBDA_INIT)
    gain = gs_ref[...] * (1.0 - LAMBDA_INIT)
    nt = (((1,), (1,)), ((), ()))
    tn = (((0,), (0,)), ((), ()))
    state = [dict() for _ in range(nq)]

    def score_ops(i):
        s_ref, st = scratch[2 * i], state[i]

        def start():
            q = q_ref[i * t:(i + 1) * t, :]
            zero = jnp.zeros_like(q)
            st["qs"] = jnp.concatenate([jnp.where(lane < DA_QK_DIM, q, zero),
                                        jnp.where(lane >= DA_QK_DIM, q, zero)], axis=0)
            st["m"] = None

        def one(c):
            rows = slice(c * chunk, (c + 1) * chunk)
            s = lax.dot_general(k_ref[rows, :], st["qs"], nt, preferred_element_type=F32)
            offset = c * chunk - i * t
            if offset >= 0:
                keep = key + offset <= qry
                s = jnp.where(jnp.concatenate([keep, keep], axis=1), s, -jnp.inf)
            s_ref[rows, :] = s
            cmax = jnp.max(s, axis=0, keepdims=True)
            st["m"] = cmax if st["m"] is None else jnp.maximum(st["m"], cmax)

        return [start] + [functools.partial(one, c) for c in range((i + 1) * t // chunk)]

    def prob_ops(i):
        s_ref, st = scratch[2 * i], state[i]

        def one(c):
            rows = slice(c * chunk, (c + 1) * chunk)
            p = jnp.exp2(s_ref[rows, :] - st["m"])
            s_ref[rows, :] = p
            csum = jnp.sum(p, axis=0, keepdims=True)
            st["l"] = csum if c == 0 else st["l"] + csum

        return [functools.partial(one, c) for c in range((i + 1) * t // chunk)]

    def value_ops(i):
        s_ref, a_ref, st = scratch[2 * i], scratch[2 * i + 1], state[i]

        def one(c):
            rows = slice(c * chunk, (c + 1) * chunk)
            if c == 0:
                l = st["l"]
                st["ratio"] = lam * l[:, :t] / l[:, t:]
            a = (s_ref[rows, 0:t] - st["ratio"] * s_ref[rows, t:2 * t]).astype(BF16)
            pv = lax.dot_general(v_ref[rows, :], a, tn, preferred_element_type=F32)
            st["acc"] = pv if c == 0 else st["acc"] + pv

        def finish():
            o = st["acc"] * (1.0 / st["l"][:, :t])
            inv = lax.rsqrt(jnp.mean(o * o, axis=0, keepdims=True) + EPS)
            o_ref[i * t:(i + 1) * t, :] = (o * inv * gain).T.astype(BF16)

        return [functools.partial(one, c) for c in range((i + 1) * t // chunk)] + [finish]

    order = _tile_order(nq)
    for step in range(nq + 2):
        streams = []
        if step < nq:
            streams.append(score_ops(order[step]))
        if 0 <= step - 1 < nq:
            streams.append(prob_ops(order[step - 1]))
        if 0 <= step - 2 < nq:
            streams.append(value_ops(order[step - 2]))
        for op in _interleave(*streams):
            op()


def _attn_call(q, k, v, lq1, lk1, lq2, lk2, g_subln_col, t):
    b, _, s, _ = q.shape
    nq = s // t
    head = pl.BlockSpec((None, None, s, DA_V_DIM), lambda bi, h: (bi, h, 0, 0))
    vec = _const_spec((1, DA_QK_DIM))
    return pl.pallas_call(
        functools.partial(_attn_kernel, t=t, nq=nq, chunk=ATTN_CHUNK),
        grid=(b, DA_HEADS),
        in_specs=[head, head, head, vec, vec, vec, vec, _const_spec((DA_V_DIM, 1))],
        out_specs=head,
        out_shape=jax.ShapeDtypeStruct((b, DA_HEADS, s, DA_V_DIM), BF16),
        scratch_shapes=[buf for i in range(nq) for buf in
                        (pltpu.VMEM(((i + 1) * t, 2 * t), F32), pltpu.VMEM(((i + 1) * t, t), BF16))],
        compiler_params=pltpu.CompilerParams(
            dimension_semantics=("arbitrary", "arbitrary"), vmem_limit_bytes=VMEM_LIMIT),
        name="diff_attn",
    )(q, k, v, lq1, lk1, lq2, lk2, g_subln_col)


def _mixout_kernel(da_ref, po_ref, x_ref, wo_hbm, gpost_ref, gxpre_ref, wxq_hbm, kv_ref, wxo_hbm,
                   gxpost_ref, o_ref, wo_ref, wxq_ref, wxo_ref, stage, sem):
    @pl.when(_first_step(2))
    def _():
        _load_weights([(wo_hbm, wo_ref), (wxq_hbm, wxq_ref), (wxo_hbm, wxo_ref)], stage, sem)

    da = jnp.concatenate([da_ref[h] for h in range(DA_HEADS)], axis=-1)
    mix = (jnp.dot(da, wo_ref[0:DA_WIDTH, :], preferred_element_type=F32)
           + jnp.dot(po_ref[0], wo_ref[DA_WIDTH:, :], preferred_element_type=F32))
    x1 = x_ref[0] + _rms(mix, gpost_ref[...])

    hq, inv = _prenorm_operand(x1, gxpre_ref[...])
    x_scale = X_HEAD_DIM ** -0.5
    xq = (jnp.dot(hq, wxq_ref[...], preferred_element_type=F32) * (inv * x_scale)).astype(BF16)
    heads = []
    for h in range(X_HEADS):
        sl = slice(h * X_HEAD_DIM, (h + 1) * X_HEAD_DIM)
        kh = kv_ref[0, :, sl]
        vh = kv_ref[0, :, D_MODEL + h * X_HEAD_DIM:D_MODEL + (h + 1) * X_HEAD_DIM]
        sc = lax.dot_general(xq[:, sl], kh, (((1,), (1,)), ((), ())), preferred_element_type=F32)
        p = jnp.exp(sc - jnp.max(sc, axis=-1, keepdims=True))
        pm = (p * (1.0 / jnp.sum(p, axis=-1, keepdims=True))).astype(BF16)
        heads.append(jnp.dot(pm, vh, preferred_element_type=F32).astype(BF16))
    xo = jnp.concatenate(heads, axis=-1)
    y = jnp.dot(xo, wxo_ref[...], preferred_element_type=F32)
    o_ref[0] = x1 + _rms(y, gxpost_ref[...])


def _mixout_call(da, po, x, w_out, g_mix_post, g_x_pre, w_xq, kv, w_xo, g_x_post, tm):
    b, s, d = x.shape
    tok = lambda n: pl.BlockSpec((1, tm, n), lambda bi, i: (bi, i, 0))
    gvec = _const_spec((1, d))
    return pl.pallas_call(
        _mixout_kernel,
        grid=(b, s // tm),
        in_specs=[pl.BlockSpec((None, DA_HEADS, tm, DA_V_DIM), lambda bi, i: (bi, 0, i, 0)),
                  tok(POOL_WIDTH), tok(d), _HBM, gvec, gvec, _HBM,
                  pl.BlockSpec((1,) + kv.shape[1:], lambda bi, i: (bi, 0, 0)),
                  _HBM, gvec],
        out_specs=tok(d),
        out_shape=jax.ShapeDtypeStruct((b, s, d), F32),
        scratch_shapes=[pltpu.VMEM(w_out.shape, BF16), pltpu.VMEM(w_xq.shape, BF16),
                        pltpu.VMEM(w_xo.shape, BF16), *_stage_scratch(d, d)],
        compiler_params=pltpu.CompilerParams(
            dimension_semantics=("arbitrary", "arbitrary"), vmem_limit_bytes=VMEM_LIMIT),
        name="mixout_xattn",
    )(da, po, x, w_out, g_mix_post, g_x_pre, w_xq, kv, w_xo, g_x_post)


def _ffn_kernel(x_ref, gpre_ref, wg_hbm, wu_hbm, wd_hbm, gpost_ref, o_ref,
                wg_ref, wu_ref, wd_ref, wide_stage, wide_sem, tall_stage, tall_sem):
    @pl.when(_first_step(2))
    def _():
        _load_weights([(wg_hbm, wg_ref), (wu_hbm, wu_ref)], wide_stage, wide_sem)
        _load_weights([(wd_hbm, wd_ref)], tall_stage, tall_sem)

    x = x_ref[0]
    hf, inv = _prenorm_operand(x, gpre_ref[...])
    ff = None
    for lo, hi in FF_CHUNKS:
        gate = jnp.dot(hf, wg_ref[:, lo:hi], preferred_element_type=F32) * inv
        up = jnp.dot(hf, wu_ref[:, lo:hi], preferred_element_type=F32) * inv
        act = (gate * (1.0 / (1.0 + jnp.exp(-gate))) * up).astype(BF16)
        part = jnp.dot(act, wd_ref[lo:hi, :], preferred_element_type=F32)
        ff = part if ff is None else ff + part
    o_ref[0] = x + _rms(ff, gpost_ref[...])


def _ffn_call(x, g_pre, w_gate, w_up, w_down, g_post, tm):
    b, s, d = x.shape
    tok = pl.BlockSpec((1, tm, d), lambda bi, i: (bi, i, 0))
    gvec = _const_spec((1, d))
    return pl.pallas_call(
        _ffn_kernel,
        grid=(b, s // tm),
        in_specs=[tok, gvec, _HBM, _HBM, _HBM, gvec],
        out_specs=tok,
        out_shape=jax.ShapeDtypeStruct((b, s, d), F32),
        scratch_shapes=[pltpu.VMEM(w_gate.shape, BF16), pltpu.VMEM(w_up.shape, BF16),
                        pltpu.VMEM(w_down.shape, BF16),
                        *_stage_scratch(d, w_gate.shape[1]),
                        *_stage_scratch(w_down.shape[0], d)],
        compiler_params=pltpu.CompilerParams(
            dimension_semantics=("arbitrary", "arbitrary"), vmem_limit_bytes=VMEM_LIMIT),
        name="swiglu",
    )(x, g_pre, w_gate, w_up, w_down, g_post)


def _rope_lane_constants():
    inv_freq = ROPE_THETA ** (-jnp.arange(0, ROPE_DIM, 2, dtype=F32) / ROPE_DIM)
    packed_freq = jnp.tile(inv_freq, LANES // ROPE_HALF)
    one_half = jnp.ones((ROPE_HALF,), F32)
    per_map = jnp.concatenate([-one_half, one_half, jnp.zeros((DA_QK_DIM - ROPE_DIM,), F32)])
    sign = jnp.concatenate([per_map, per_map])
    return jnp.concatenate([jnp.stack([packed_freq, sign]), jnp.zeros((6, LANES), F32)], axis=0)


def kernel(x, mem, positions, g_mix_pre, w_in, lambda_q1, lambda_k1, lambda_q2, lambda_k2,
           g_subln, w_pool, pool_scale, w_out, g_mix_post, g_x_pre, g_mem, w_xq, w_xkv, w_xo,
           g_x_post, g_ffn_pre, w_gate, w_up, w_down, g_ffn_post):
    b, s, d = x.shape
    assert d == D_MODEL and s % TOKEN_TILE == 0 and s % ATTN_TILE == 0
    row = lambda a: a.reshape(1, -1).astype(F32)
    bf = lambda a: a.astype(BF16)

    pos = positions.reshape(b, 1, s)
    kv = _kv_call(mem, row(g_mem), w_xkv)
    q, k, v, po = _inproj_call(x, row(g_mix_pre), w_in, pos, _rope_lane_constants(),
                               bf(w_pool), row(pool_scale), TOKEN_TILE)
    da = _attn_call(q, k, v, row(lambda_q1), row(lambda_k1), row(lambda_q2), row(lambda_k2),
                    g_subln.reshape(-1, 1).astype(F32), ATTN_TILE)
    x2 = _mixout_call(da, po, x, w_out, row(g_mix_post), row(g_x_pre), w_xq, kv, w_xo,
                      row(g_x_post), TOKEN_TILE)
    return _ffn_call(x2, row(g_ffn_pre), w_gate, w_up, w_down, row(g_ffn_post), TOKEN_TILE)
```

```python
import functools
import itertools
import math

import jax
import jax.numpy as jnp
from jax import lax
from jax.experimental import pallas as pl
from jax.experimental.pallas import tpu as pltpu

F32 = jnp.float32
BF16 = jnp.bfloat16

D_MODEL = 1024
MEM_LEN = 256
EPS = 1e-6
DA_HEADS = 4
DA_QK_DIM = 64
DA_V_DIM = 2 * DA_QK_DIM
DA_WIDTH = DA_HEADS * DA_V_DIM
QK_WIDTH = DA_HEADS * 2 * DA_QK_DIM
POOL_WINDOWS = (2, 4, 8, 16)
POOL_WIDTH = D_MODEL - DA_WIDTH
POOL_GROUP_DIM = POOL_WIDTH // len(POOL_WINDOWS)
ROPE_THETA = 500000.0
ROPE_DIM = DA_QK_DIM // 4
ROPE_HALF = ROPE_DIM // 2
X_HEADS = 4
X_HEAD_DIM = D_MODEL // X_HEADS
D_FF = -(-(8 * D_MODEL) // (3 * 256)) * 256
LAMBDA_INIT = 0.8 - 0.6 * math.exp(-0.3 * 0)
LOG2_E = math.log2(math.e)

LANES = 128
POOL_HALO = 16
VMEM_LIMIT = 56 * 1024 * 1024

TOKEN_TILE = 512
ATTN_TILE = 512
FF_CHUNKS = ((0, 1536), (1536, 2816))
STAGE_BYTES = 4 * 1024 * 1024


def _inv_rms(x):
    return lax.rsqrt(jnp.mean(x * x, axis=-1, keepdims=True) + EPS)


def _rms(x, g):
    return x * _inv_rms(x) * g


def _prenorm_operand(x, g):
    return (x * g).astype(BF16), _inv_rms(x)


def _const_spec(shape):
    zeros = (0,) * len(shape)
    return pl.BlockSpec(shape, lambda *_: zeros, pipeline_mode=pl.Buffered(1))


_HBM = pl.BlockSpec(memory_space=pl.ANY)


def _stage_scratch(rows, cols):
    chunks = -(-(rows * cols * 4) // STAGE_BYTES)
    while rows % chunks or (rows // chunks) % 16:
        chunks += 1
    return [pltpu.VMEM((2, rows // chunks, cols), F32), pltpu.SemaphoreType.DMA((2,))]


def _load_weights(weights, stage_ref, sem_ref):
    rows = stage_ref.shape[1]
    jobs = []
    for w_hbm, w_vmem in weights:
        assert w_hbm.shape[0] % rows == 0 and stage_ref.shape[2] == w_hbm.shape[1]
        jobs += [(w_hbm, w_vmem, r) for r in range(0, w_hbm.shape[0], rows)]

    def copy(k):
        w_hbm, _, r = jobs[k]
        return pltpu.make_async_copy(w_hbm.at[pl.ds(r, rows), :], stage_ref.at[k % 2], sem_ref.at[k % 2])

    copy(0).start()
    for k, (_, w_vmem, r) in enumerate(jobs):
        if k + 1 < len(jobs):
            copy(k + 1).start()
        copy(k).wait()
        w_vmem[r:r + rows, :] = stage_ref[k % 2].astype(BF16)


def _first_step(grid_rank):
    first = pl.program_id(0) == 0
    for axis in range(1, grid_rank):
        first = first & (pl.program_id(axis) == 0)
    return first


def _kv_kernel(mem_ref, g_ref, w_hbm, kv_ref, w_ref, stage, sem):
    @pl.when(_first_step(1))
    def _():
        _load_weights([(w_hbm, w_ref)], stage, sem)

    mn = _rms(mem_ref[0], g_ref[...]).astype(BF16)
    kv_ref[0] = jnp.dot(mn, w_ref[...], preferred_element_type=F32).astype(BF16)


def _kv_call(mem, g_mem, w_xkv):
    b, m, d = mem.shape
    n = w_xkv.shape[1]
    return pl.pallas_call(
        _kv_kernel,
        grid=(b,),
        in_specs=[pl.BlockSpec((1, m, d), lambda i: (i, 0, 0)),
                  _const_spec((1, d)), _HBM],
        out_specs=pl.BlockSpec((1, m, n), lambda i: (i, 0, 0)),
        out_shape=jax.ShapeDtypeStruct((b, m, n), BF16),
        scratch_shapes=[pltpu.VMEM((d, n), BF16), *_stage_scratch(d, n)],
        compiler_params=pltpu.CompilerParams(
            dimension_semantics=("arbitrary",), vmem_limit_bytes=VMEM_LIMIT),
        name="kv_proj",
    )(mem, g_mem, w_xkv)


def _inproj_kernel(x_ref, g_ref, w_hbm, pos_ref, freq_ref, wp_ref, ps_ref,
                   q_ref, k_ref, v_ref, po_ref, halo_ref, w_ref, stage, sem, *, tm):
    @pl.when(_first_step(2))
    def _():
        _load_weights([(w_hbm, w_ref)], stage, sem)

    i = pl.program_id(1)
    h = _rms(x_ref[0], g_ref[...]).astype(BF16)

    groups = LANES // ROPE_DIM
    rows = tm // groups
    lane = lax.broadcasted_iota(jnp.int32, (rows, LANES), 1)
    lane_group = lax.shift_right_logical(lane, int(math.log2(ROPE_DIM)))
    pos = jnp.broadcast_to(pos_ref[0].astype(F32), (LANES, tm)).T
    packed = jnp.zeros((rows, LANES), F32)
    for g in range(groups):
        packed = jnp.where(lane_group == g, pos[g * rows:(g + 1) * rows], packed)
    ang = packed * freq_ref[0:1, :]
    cos8, sin8 = jnp.cos(ang), jnp.sin(ang)
    first_map = lane < DA_QK_DIM
    rotary = freq_ref[1:2, :] != 0.0

    def spread(packed_table, g):
        shift = (LANES - ROPE_DIM * g) % LANES
        lo = pltpu.roll(packed_table, shift, 1) if shift else packed_table
        hi_shift = (shift + DA_QK_DIM) % LANES
        hi = pltpu.roll(packed_table, hi_shift, 1) if hi_shift else packed_table
        return jnp.where(first_map, lo, hi)

    cos_t = jnp.concatenate([jnp.where(rotary, spread(cos8, g), 1.0) for g in range(groups)], axis=0)
    sin_t = jnp.concatenate([spread(sin8, g) * freq_ref[1:2, :] for g in range(groups)], axis=0)
    first_half = freq_ref[1:2, :] < 0.0

    def rope(t, c, s):
        partner = jnp.where(first_half, pltpu.roll(t, LANES - ROPE_HALF, 1), pltpu.roll(t, ROPE_HALF, 1))
        return t * c + partner * s

    qk_scale = DA_QK_DIM ** -0.5 * LOG2_E
    cos_q, sin_q = cos_t * qk_scale, sin_t * qk_scale
    pq = jnp.dot(h, w_ref[:, 0:QK_WIDTH], preferred_element_type=F32)
    for j in range(DA_HEADS):
        sl = slice(j * LANES, (j + 1) * LANES)
        q_ref[j] = rope(pq[:, sl], cos_q, sin_q).astype(BF16)
    pk = jnp.dot(h, w_ref[:, QK_WIDTH:2 * QK_WIDTH], preferred_element_type=F32)
    for j in range(DA_HEADS):
        sl = slice(j * LANES, (j + 1) * LANES)
        k_ref[j] = rope(pk[:, sl], cos_t, sin_t).astype(BF16)
    pv = jnp.dot(h, w_ref[:, 2 * QK_WIDTH:2 * QK_WIDTH + DA_WIDTH], preferred_element_type=F32)
    for j in range(DA_HEADS):
        v_ref[j] = pv[:, j * DA_V_DIM:(j + 1) * DA_V_DIM].astype(BF16)
    u = jnp.dot(h, w_ref[:, 2 * QK_WIDTH + DA_WIDTH:], preferred_element_type=F32)

    @pl.when(i == 0)
    def _():
        halo_ref[...] = jnp.zeros_like(halo_ref)

    prev = halo_ref[...]
    halo_ref[...] = u[tm - POOL_HALO:, :]
    tpos = i * tm + lax.broadcasted_iota(jnp.int32, (tm, 1), 0)
    for g, w in enumerate(POOL_WINDOWS):
        sl = slice(g * POOL_GROUP_DIM, (g + 1) * POOL_GROUP_DIM)
        ug = u[:, sl]
        s = jnp.concatenate([prev[:, sl], ug], axis=0)
        d = 1
        while d < w:
            s = s + pltpu.roll(s, d, 0)
            d *= 2
        inv_count = 1.0 / jnp.minimum(tpos + 1, w).astype(F32)
        pooled = s[POOL_HALO:, :] * inv_count - ug
        po = jnp.dot(pooled.astype(BF16), wp_ref[g], preferred_element_type=F32) * ps_ref[:, sl]
        po_ref[0, :, sl] = po.astype(BF16)


def _inproj_call(x, g, w_in, pos, freq, w_pool, pool_scale, tm):
    b, s, d = x.shape
    tok = lambda n: pl.BlockSpec((1, tm, n), lambda bi, i: (bi, i, 0))
    heads = pl.BlockSpec((None, DA_HEADS, tm, DA_V_DIM), lambda bi, i: (bi, 0, i, 0))
    head_major = jax.ShapeDtypeStruct((b, DA_HEADS, s, DA_V_DIM), BF16)
    return pl.pallas_call(
        functools.partial(_inproj_kernel, tm=tm),
        grid=(b, s // tm),
        in_specs=[tok(d), _const_spec((1, d)), _HBM,
                  pl.BlockSpec((1, 1, tm), lambda bi, i: (bi, 0, i)), _const_spec(freq.shape),
                  _const_spec(w_pool.shape), _const_spec((1, POOL_WIDTH))],
        out_specs=[heads, heads, heads, tok(POOL_WIDTH)],
        out_shape=[head_major, head_major, head_major,
                   jax.ShapeDtypeStruct((b, s, POOL_WIDTH), BF16)],
        scratch_shapes=[pltpu.VMEM((POOL_HALO, POOL_WIDTH), F32), pltpu.VMEM(w_in.shape, BF16),
                        *_stage_scratch(d, w_in.shape[1])],
        compiler_params=pltpu.CompilerParams(
            dimension_semantics=("arbitrary", "arbitrary"), vmem_limit_bytes=VMEM_LIMIT),
        name="inproj",
    )(x, g, w_in, pos, freq, w_pool, pool_scale)


def _interleave(*streams):
    tagged = [((k + 0.5) / len(ops), n, op) for n, ops in enumerate(streams) for k, op in enumerate(ops)]
    return [op for _, _, op in sorted(tagged, key=lambda e: e[:2])]


def _tile_order(nq):
    def length(order):
        total = 0.0
        for step in range(nq + 2):
            scores = order[step] + 1 if step < nq else 0
            probs = order[step - 1] + 1 if 0 <= step - 1 < nq else 0
            values = (order[step - 2] + 1) / 2 if 0 <= step - 2 < nq else 0
            total += max(scores + values, probs)
        return total

    return min(itertools.permutations(range(nq)), key=length)


def _attn_kernel(q_ref, k_ref, v_ref, lq1_ref, lk1_ref, lq2_ref, lk2_ref, gs_ref, o_ref, *scratch, t, nq):
    h = t // 2
    lane = lax.broadcasted_iota(jnp.int32, (h, DA_V_DIM), 1)
    causal = (lax.broadcasted_iota(jnp.int32, (h, h), 0) <= lax.broadcasted_iota(jnp.int32, (h, h), 1))
    causal = jnp.concatenate([causal, causal], axis=1)
    lam = (jnp.exp(jnp.sum(lq1_ref[...] * lk1_ref[...], axis=-1, keepdims=True))
           - jnp.exp(jnp.sum(lq2_ref[...] * lk2_ref[...], axis=-1, keepdims=True))
           + LAMBDA_INIT)
    gain = gs_ref[...] * (1.0 - LAMBDA_INIT)
    nt = (((1,), (1,)), ((), ()))
    tn = (((0,), (0,)), ((), ()))
    state = [dict() for _ in range(nq)]

    def chunks(i):
        n = 2 * (i + 1)
        return [(slice(c * h, (c + 1) * h), t if c == n - 1 else 0) for c in range(n)]

    def both(st, name, lo, value, combine):
        for half, part in ((0, value[:, :t]), (1, value[:, t:])) if lo == 0 else ((1, value),):
            key = (name, half)
            st[key] = part if key not in st else combine(st[key], part)

    def score_ops(i):
        s_ref, st = scratch[i], state[i]

        def start():
            stacked = []
            for half in range(2):
                q = q_ref[i * t + half * h:i * t + (half + 1) * h, :]
                zero = jnp.zeros_like(q)
                stacked += [jnp.where(lane < DA_QK_DIM, q, zero), jnp.where(lane >= DA_QK_DIM, q, zero)]
            st["qs"] = jnp.concatenate(stacked, axis=0)

        def one(c, rows, lo):
            s = lax.dot_general(k_ref[rows, :], st["qs"][lo:, :], nt, preferred_element_type=F32)
            first_diag = 2 * i
            if c == first_diag:
                s = jnp.concatenate([jnp.where(causal, s[:, :t], -jnp.inf), s[:, t:]], axis=1)
            elif c == first_diag + 1:
                s = jnp.where(causal, s, -jnp.inf)
            s_ref[rows, lo:] = s
            both(st, "m", lo, jnp.max(s, axis=0, keepdims=True), jnp.maximum)

        return [start] + [functools.partial(one, c, rows, lo) for c, (rows, lo) in enumerate(chunks(i))]

    def prob_ops(i):
        s_ref, st = scratch[i], state[i]

        def one(rows, lo):
            m = st[("m", 1)] if lo else jnp.concatenate([st[("m", 0)], st[("m", 1)]], axis=1)
            p = jnp.exp2(s_ref[rows, lo:] - m)
            s_ref[rows, lo:] = p
            both(st, "l", lo, jnp.sum(p, axis=0, keepdims=True), jnp.add)

        return [functools.partial(one, rows, lo) for rows, lo in chunks(i)]

    def value_ops(i):
        s_ref, st = scratch[i], state[i]

        def combine(rows, half):
            l = st[("l", half)]
            ratio = lam * l[:, :h] / l[:, h:]
            base = half * t
            return s_ref[rows, base:base + h] - ratio * s_ref[rows, base + h:base + t]

        def one(rows, lo):
            halves = (1,) if lo else (0, 1)
            a = jnp.concatenate([combine(rows, half) for half in halves], axis=1).astype(BF16)
            pv = lax.dot_general(v_ref[rows, :], a, tn, preferred_element_type=F32)
            for n, half in enumerate(halves):
                part = pv[:, n * h:(n + 1) * h]
                st["acc", half] = part if ("acc", half) not in st else st["acc", half] + part

        def finish():
            o = jnp.concatenate([st[("acc", half)] * (1.0 / st[("l", half)][:, :h]) for half in range(2)], axis=1)
            inv = lax.rsqrt(jnp.mean(o * o, axis=0, keepdims=True) + EPS)
            o_ref[i * t:(i + 1) * t, :] = (o * inv * gain).T.astype(BF16)

        return [functools.partial(one, rows, lo) for rows, lo in chunks(i)] + [finish]

    order = _tile_order(nq)
    for step in range(nq + 2):
        streams = []
        if step < nq:
            streams.append(score_ops(order[step]))
        if 0 <= step - 1 < nq:
            streams.append(prob_ops(order[step - 1]))
        if 0 <= step - 2 < nq:
            streams.append(value_ops(order[step - 2]))
        for op in _interleave(*streams):
            op()


def _attn_call(q, k, v, lq1, lk1, lq2, lk2, g_subln_col, t):
    b, _, s, _ = q.shape
    nq = s // t
    head = pl.BlockSpec((None, None, s, DA_V_DIM), lambda bi, h: (bi, h, 0, 0))
    vec = _const_spec((1, DA_QK_DIM))
    return pl.pallas_call(
        functools.partial(_attn_kernel, t=t, nq=nq),
        grid=(b, DA_HEADS),
        in_specs=[head, head, head, vec, vec, vec, vec, _const_spec((DA_V_DIM, 1))],
        out_specs=head,
        out_shape=jax.ShapeDtypeStruct((b, DA_HEADS, s, DA_V_DIM), BF16),
        scratch_shapes=[pltpu.VMEM(((i + 1) * t, 2 * t), F32) for i in range(nq)],
        compiler_params=pltpu.CompilerParams(
            dimension_semantics=("arbitrary", "arbitrary"), vmem_limit_bytes=VMEM_LIMIT),
        name="diff_attn",
    )(q, k, v, lq1, lk1, lq2, lk2, g_subln_col)


def _mixout_kernel(da_ref, po_ref, x_ref, wo_hbm, gpost_ref, gxpre_ref, wxq_hbm, kv_ref, wxo_hbm,
                   gxpost_ref, o_ref, wo_ref, wxq_ref, wxo_ref, stage, sem):
    @pl.when(_first_step(2))
    def _():
        _load_weights([(wo_hbm, wo_ref), (wxq_hbm, wxq_ref), (wxo_hbm, wxo_ref)], stage, sem)

    da = jnp.concatenate([da_ref[h] for h in range(DA_HEADS)], axis=-1)
    mix = (jnp.dot(da, wo_ref[0:DA_WIDTH, :], preferred_element_type=F32)
           + jnp.dot(po_ref[0], wo_ref[DA_WIDTH:, :], preferred_element_type=F32))
    x1 = x_ref[0] + _rms(mix, gpost_ref[...])

    hq, inv = _prenorm_operand(x1, gxpre_ref[...])
    x_scale = X_HEAD_DIM ** -0.5
    xq = (jnp.dot(hq, wxq_ref[...], preferred_element_type=F32) * (inv * x_scale)).astype(BF16)
    heads = []
    for h in range(X_HEADS):
        sl = slice(h * X_HEAD_DIM, (h + 1) * X_HEAD_DIM)
        kh = kv_ref[0, :, sl]
        vh = kv_ref[0, :, D_MODEL + h * X_HEAD_DIM:D_MODEL + (h + 1) * X_HEAD_DIM]
        sc = lax.dot_general(xq[:, sl], kh, (((1,), (1,)), ((), ())), preferred_element_type=F32)
        p = jnp.exp(sc - jnp.max(sc, axis=-1, keepdims=True))
        pm = (p * (1.0 / jnp.sum(p, axis=-1, keepdims=True))).astype(BF16)
        heads.append(jnp.dot(pm, vh, preferred_element_type=F32).astype(BF16))
    xo = jnp.concatenate(heads, axis=-1)
    y = jnp.dot(xo, wxo_ref[...], preferred_element_type=F32)
    o_ref[0] = x1 + _rms(y, gxpost_ref[...])


def _mixout_call(da, po, x, w_out, g_mix_post, g_x_pre, w_xq, kv, w_xo, g_x_post, tm):
    b, s, d = x.shape
    tok = lambda n: pl.BlockSpec((1, tm, n), lambda bi, i: (bi, i, 0))
    gvec = _const_spec((1, d))
    return pl.pallas_call(
        _mixout_kernel,
        grid=(b, s // tm),
        in_specs=[pl.BlockSpec((None, DA_HEADS, tm, DA_V_DIM), lambda bi, i: (bi, 0, i, 0)),
                  tok(POOL_WIDTH), tok(d), _HBM, gvec, gvec, _HBM,
                  pl.BlockSpec((1,) + kv.shape[1:], lambda bi, i: (bi, 0, 0)),
                  _HBM, gvec],
        out_specs=tok(d),
        out_shape=jax.ShapeDtypeStruct((b, s, d), F32),
        scratch_shapes=[pltpu.VMEM(w_out.shape, BF16), pltpu.VMEM(w_xq.shape, BF16),
                        pltpu.VMEM(w_xo.shape, BF16), *_stage_scratch(d, d)],
        compiler_params=pltpu.CompilerParams(
            dimension_semantics=("arbitrary", "arbitrary"), vmem_limit_bytes=VMEM_LIMIT),
        name="mixout_xattn",
    )(da, po, x, w_out, g_mix_post, g_x_pre, w_xq, kv, w_xo, g_x_post)


def _ffn_kernel(x_ref, gpre_ref, wg_hbm, wu_hbm, wd_hbm, gpost_ref, o_ref,
                wg_ref, wu_ref, wd_ref, wide_stage, wide_sem, tall_stage, tall_sem):
    @pl.when(_first_step(2))
    def _():
        _load_weights([(wg_hbm, wg_ref), (wu_hbm, wu_ref)], wide_stage, wide_sem)
        _load_weights([(wd_hbm, wd_ref)], tall_stage, tall_sem)

    x = x_ref[0]
    hf, inv = _prenorm_operand(x, gpre_ref[...])
    ff = None
    for lo, hi in FF_CHUNKS:
        gate = jnp.dot(hf, wg_ref[:, lo:hi], preferred_element_type=F32) * inv
        up = jnp.dot(hf, wu_ref[:, lo:hi], preferred_element_type=F32) * inv
        act = (gate * (1.0 / (1.0 + jnp.exp(-gate))) * up).astype(BF16)
        part = jnp.dot(act, wd_ref[lo:hi, :], preferred_element_type=F32)
        ff = part if ff is None else ff + part
    o_ref[0] = x + _rms(ff, gpost_ref[...])


def _ffn_call(x, g_pre, w_gate, w_up, w_down, g_post, tm):
    b, s, d = x.shape
    tok = pl.BlockSpec((1, tm, d), lambda bi, i: (bi, i, 0))
    gvec = _const_spec((1, d))
    return pl.pallas_call(
        _ffn_kernel,
        grid=(b, s // tm),
        in_specs=[tok, gvec, _HBM, _HBM, _HBM, gvec],
        out_specs=tok,
        out_shape=jax.ShapeDtypeStruct((b, s, d), F32),
        scratch_shapes=[pltpu.VMEM(w_gate.shape, BF16), pltpu.VMEM(w_up.shape, BF16),
                        pltpu.VMEM(w_down.shape, BF16),
                        *_stage_scratch(d, w_gate.shape[1]),
                        *_stage_scratch(w_down.shape[0], d)],
        compiler_params=pltpu.CompilerParams(
            dimension_semantics=("arbitrary", "arbitrary"), vmem_limit_bytes=VMEM_LIMIT),
        name="swiglu",
    )(x, g_pre, w_gate, w_up, w_down, g_post)


def _rope_lane_constants():
    inv_freq = ROPE_THETA ** (-jnp.arange(0, ROPE_DIM, 2, dtype=F32) / ROPE_DIM)
    packed_freq = jnp.tile(inv_freq, LANES // ROPE_HALF)
    one_half = jnp.ones((ROPE_HALF,), F32)
    per_map = jnp.concatenate([-one_half, one_half, jnp.zeros((DA_QK_DIM - ROPE_DIM,), F32)])
    sign = jnp.concatenate([per_map, per_map])
    return jnp.concatenate([jnp.stack([packed_freq, sign]), jnp.zeros((6, LANES), F32)], axis=0)


def kernel(x, mem, positions, g_mix_pre, w_in, lambda_q1, lambda_k1, lambda_q2, lambda_k2,
           g_subln, w_pool, pool_scale, w_out, g_mix_post, g_x_pre, g_mem, w_xq, w_xkv, w_xo,
           g_x_post, g_ffn_pre, w_gate, w_up, w_down, g_ffn_post):
    b, s, d = x.shape
    assert d == D_MODEL and s % TOKEN_TILE == 0 and s % ATTN_TILE == 0
    row = lambda a: a.reshape(1, -1).astype(F32)
    bf = lambda a: a.astype(BF16)

    pos = positions.reshape(b, 1, s)
    kv = _kv_call(mem, row(g_mem), w_xkv)
    q, k, v, po = _inproj_call(x, row(g_mix_pre), w_in, pos, _rope_lane_constants(),
                               bf(w_pool), row(pool_scale), TOKEN_TILE)
    da = _attn_call(q, k, v, row(lambda_q1), row(lambda_k1), row(lambda_q2), row(lambda_k2),
                    g_subln.reshape(-1, 1).astype(F32), ATTN_TILE)
    x2 = _mixout_call(da, po, x, w_out, row(g_mix_post), row(g_x_pre), w_xq, kv, w_xo,
                      row(g_x_post), TOKEN_TILE)
    return _ffn_call(x2, row(g_ffn_pre), w_gate, w_up, w_down, row(g_ffn_post), TOKEN_TILE)
```

```python
import functools
import itertools
import math

import jax
import jax.numpy as jnp
from jax import lax
from jax.experimental import pallas as pl
from jax.experimental.pallas import tpu as pltpu

F32 = jnp.float32
BF16 = jnp.bfloat16

D_MODEL = 1024
MEM_LEN = 256
EPS = 1e-6
DA_HEADS = 4
DA_QK_DIM = 64
DA_V_DIM = 2 * DA_QK_DIM
DA_WIDTH = DA_HEADS * DA_V_DIM
QK_WIDTH = DA_HEADS * 2 * DA_QK_DIM
POOL_WINDOWS = (2, 4, 8, 16)
POOL_WIDTH = D_MODEL - DA_WIDTH
POOL_GROUP_DIM = POOL_WIDTH // len(POOL_WINDOWS)
ROPE_THETA = 500000.0
ROPE_DIM = DA_QK_DIM // 4
ROPE_HALF = ROPE_DIM // 2
X_HEADS = 4
X_HEAD_DIM = D_MODEL // X_HEADS
D_FF = -(-(8 * D_MODEL) // (3 * 256)) * 256
LAMBDA_INIT = 0.8 - 0.6 * math.exp(-0.3 * 0)
LOG2_E = math.log2(math.e)

LANES = 128
POOL_HALO = 16
VMEM_LIMIT = 56 * 1024 * 1024

TOKEN_TILE = 512
ATTN_TILE = 512
FF_CHUNKS = ((0, 1536), (1536, 2816))
STAGE_BYTES = 4 * 1024 * 1024


def _inv_rms(x):
    return lax.rsqrt(jnp.mean(x * x, axis=-1, keepdims=True) + EPS)


def _rms(x, g):
    return x * _inv_rms(x) * g


def _prenorm_operand(x, g):
    return (x * g).astype(BF16), _inv_rms(x)


def _const_spec(shape):
    zeros = (0,) * len(shape)
    return pl.BlockSpec(shape, lambda *_: zeros, pipeline_mode=pl.Buffered(1))


_HBM = pl.BlockSpec(memory_space=pl.ANY)


def _stage_scratch(rows, cols):
    chunks = -(-(rows * cols * 4) // STAGE_BYTES)
    while rows % chunks or (rows // chunks) % 16:
        chunks += 1
    return [pltpu.VMEM((2, rows // chunks, cols), F32), pltpu.SemaphoreType.DMA((2,))]


def _load_weights(weights, stage_ref, sem_ref):
    rows = stage_ref.shape[1]
    jobs = []
    for w_hbm, w_vmem in weights:
        assert w_hbm.shape[0] % rows == 0 and stage_ref.shape[2] == w_hbm.shape[1]
        jobs += [(w_hbm, w_vmem, r) for r in range(0, w_hbm.shape[0], rows)]

    def copy(k):
        w_hbm, _, r = jobs[k]
        return pltpu.make_async_copy(w_hbm.at[pl.ds(r, rows), :], stage_ref.at[k % 2], sem_ref.at[k % 2])

    copy(0).start()
    for k, (_, w_vmem, r) in enumerate(jobs):
        if k + 1 < len(jobs):
            copy(k + 1).start()
        copy(k).wait()
        w_vmem[r:r + rows, :] = stage_ref[k % 2].astype(BF16)


def _first_step(grid_rank):
    first = pl.program_id(0) == 0
    for axis in range(1, grid_rank):
        first = first & (pl.program_id(axis) == 0)
    return first


def _kv_kernel(mem_ref, g_ref, w_hbm, kv_ref, w_ref, stage, sem):
    @pl.when(_first_step(1))
    def _():
        _load_weights([(w_hbm, w_ref)], stage, sem)

    mn = _rms(mem_ref[0], g_ref[...]).astype(BF16)
    kv_ref[0] = jnp.dot(mn, w_ref[...], preferred_element_type=F32).astype(BF16)


def _kv_call(mem, g_mem, w_xkv):
    b, m, d = mem.shape
    n = w_xkv.shape[1]
    return pl.pallas_call(
        _kv_kernel,
        grid=(b,),
        in_specs=[pl.BlockSpec((1, m, d), lambda i: (i, 0, 0)),
                  _const_spec((1, d)), _HBM],
        out_specs=pl.BlockSpec((1, m, n), lambda i: (i, 0, 0)),
        out_shape=jax.ShapeDtypeStruct((b, m, n), BF16),
        scratch_shapes=[pltpu.VMEM((d, n), BF16), *_stage_scratch(d, n)],
        compiler_params=pltpu.CompilerParams(
            dimension_semantics=("arbitrary",), vmem_limit_bytes=VMEM_LIMIT),
        name="kv_proj",
    )(mem, g_mem, w_xkv)


def _inproj_kernel(x_ref, g_ref, w_hbm, pos_ref, freq_ref, wp_ref, ps_ref,
                   q_ref, k_ref, v_ref, po_ref, halo_ref, w_ref, stage, sem, *, tm):
    @pl.when(_first_step(2))
    def _():
        _load_weights([(w_hbm, w_ref)], stage, sem)

    i = pl.program_id(1)
    h = _rms(x_ref[0], g_ref[...]).astype(BF16)

    groups = LANES // ROPE_DIM
    rows = tm // groups
    lane = lax.broadcasted_iota(jnp.int32, (rows, LANES), 1)
    lane_group = lax.shift_right_logical(lane, int(math.log2(ROPE_DIM)))
    pos = jnp.broadcast_to(pos_ref[0].astype(F32), (LANES, tm)).T
    packed = jnp.zeros((rows, LANES), F32)
    for g in range(groups):
        packed = jnp.where(lane_group == g, pos[g * rows:(g + 1) * rows], packed)
    ang = packed * freq_ref[0:1, :]
    cos8, sin8 = jnp.cos(ang), jnp.sin(ang)
    first_map = lane < DA_QK_DIM
    rotary = freq_ref[1:2, :] != 0.0

    def spread(packed_table, g):
        shift = (LANES - ROPE_DIM * g) % LANES
        lo = pltpu.roll(packed_table, shift, 1) if shift else packed_table
        hi_shift = (shift + DA_QK_DIM) % LANES
        hi = pltpu.roll(packed_table, hi_shift, 1) if hi_shift else packed_table
        return jnp.where(first_map, lo, hi)

    cos_t = jnp.concatenate([jnp.where(rotary, spread(cos8, g), 1.0) for g in range(groups)], axis=0)
    sin_t = jnp.concatenate([spread(sin8, g) * freq_ref[1:2, :] for g in range(groups)], axis=0)
    first_half = freq_ref[1:2, :] < 0.0

    def rope(t, c, s):
        partner = jnp.where(first_half, pltpu.roll(t, LANES - ROPE_HALF, 1), pltpu.roll(t, ROPE_HALF, 1))
        return t * c + partner * s

    qk_scale = DA_QK_DIM ** -0.5 * LOG2_E
    cos_q, sin_q = cos_t * qk_scale, sin_t * qk_scale
    pq = jnp.dot(h, w_ref[:, 0:QK_WIDTH], preferred_element_type=F32)
    for j in range(DA_HEADS):
        sl = slice(j * LANES, (j + 1) * LANES)
        q_ref[j] = rope(pq[:, sl], cos_q, sin_q).astype(BF16)
    pk = jnp.dot(h, w_ref[:, QK_WIDTH:2 * QK_WIDTH], preferred_element_type=F32)
    for j in range(DA_HEADS):
        sl = slice(j * LANES, (j + 1) * LANES)
        k_ref[j] = rope(pk[:, sl], cos_t, sin_t).astype(BF16)
    pv = jnp.dot(h, w_ref[:, 2 * QK_WIDTH:2 * QK_WIDTH + DA_WIDTH], preferred_element_type=F32)
    for j in range(DA_HEADS):
        v_ref[j] = pv[:, j * DA_V_DIM:(j + 1) * DA_V_DIM].astype(BF16)
    u = jnp.dot(h, w_ref[:, 2 * QK_WIDTH + DA_WIDTH:], preferred_element_type=F32)

    @pl.when(i == 0)
    def _():
        halo_ref[...] = jnp.zeros_like(halo_ref)

    prev = halo_ref[...]
    halo_ref[...] = u[tm - POOL_HALO:, :]
    tpos = i * tm + lax.broadcasted_iota(jnp.int32, (tm, 1), 0)
    for g, w in enumerate(POOL_WINDOWS):
        sl = slice(g * POOL_GROUP_DIM, (g + 1) * POOL_GROUP_DIM)
        ug = u[:, sl]
        s = jnp.concatenate([prev[:, sl], ug], axis=0)
        d = 1
        while d < w:
            s = s + pltpu.roll(s, d, 0)
            d *= 2
        inv_count = 1.0 / jnp.minimum(tpos + 1, w).astype(F32)
        pooled = s[POOL_HALO:, :] * inv_count - ug
        po = jnp.dot(pooled.astype(BF16), wp_ref[g], preferred_element_type=F32) * ps_ref[:, sl]
        po_ref[0, :, sl] = po.astype(BF16)


def _inproj_call(x, g, w_in, pos, freq, w_pool, pool_scale, tm):
    b, s, d = x.shape
    tok = lambda n: pl.BlockSpec((1, tm, n), lambda bi, i: (bi, i, 0))
    heads = pl.BlockSpec((None, DA_HEADS, tm, DA_V_DIM), lambda bi, i: (bi, 0, i, 0))
    head_major = jax.ShapeDtypeStruct((b, DA_HEADS, s, DA_V_DIM), BF16)
    return pl.pallas_call(
        functools.partial(_inproj_kernel, tm=tm),
        grid=(b, s // tm),
        in_specs=[tok(d), _const_spec((1, d)), _HBM,
                  pl.BlockSpec((1, 1, tm), lambda bi, i: (bi, 0, i)), _const_spec(freq.shape),
                  _const_spec(w_pool.shape), _const_spec((1, POOL_WIDTH))],
        out_specs=[heads, heads, heads, tok(POOL_WIDTH)],
        out_shape=[head_major, head_major, head_major,
                   jax.ShapeDtypeStruct((b, s, POOL_WIDTH), BF16)],
        scratch_shapes=[pltpu.VMEM((POOL_HALO, POOL_WIDTH), F32), pltpu.VMEM(w_in.shape, BF16),
                        *_stage_scratch(d, w_in.shape[1])],
        compiler_params=pltpu.CompilerParams(
            dimension_semantics=("arbitrary", "arbitrary"), vmem_limit_bytes=VMEM_LIMIT),
        name="inproj",
    )(x, g, w_in, pos, freq, w_pool, pool_scale)


def _interleave(*streams):
    tagged = [((k + 0.5) / len(ops), n, op) for n, ops in enumerate(streams) for k, op in enumerate(ops)]
    return [op for _, _, op in sorted(tagged, key=lambda e: e[:2])]


def _tile_order(nq):
    def length(order):
        total = 0.0
        for step in range(nq + 2):
            scores = order[step] + 1 if step < nq else 0
            probs = order[step - 1] + 1 if 0 <= step - 1 < nq else 0
            values = (order[step - 2] + 1) / 2 if 0 <= step - 2 < nq else 0
            total += max(scores + values, probs)
        return total

    return min(itertools.permutations(range(nq)), key=length)


def _attn_kernel(q_ref, k_ref, v_ref, lq1_ref, lk1_ref, lq2_ref, lk2_ref, gs_ref, o_ref, *scratch, t, nq):
    h = t // 2
    lane = lax.broadcasted_iota(jnp.int32, (h, DA_V_DIM), 1)
    causal = (lax.broadcasted_iota(jnp.int32, (h, h), 0) <= lax.broadcasted_iota(jnp.int32, (h, h), 1))
    causal = jnp.concatenate([causal, causal], axis=1)
    lam = (jnp.exp(jnp.sum(lq1_ref[...] * lk1_ref[...], axis=-1, keepdims=True))
           - jnp.exp(jnp.sum(lq2_ref[...] * lk2_ref[...], axis=-1, keepdims=True))
           + LAMBDA_INIT)
    gain = gs_ref[...] * (1.0 - LAMBDA_INIT)
    nt = (((1,), (1,)), ((), ()))
    tn = (((0,), (0,)), ((), ()))
    state = [dict() for _ in range(nq)]

    def chunks(i):
        n = 2 * (i + 1)
        return [(slice(c * h, (c + 1) * h), t if c == n - 1 else 0) for c in range(n)]

    def both(st, name, lo, value, combine):
        for half, part in ((0, value[:, :t]), (1, value[:, t:])) if lo == 0 else ((1, value),):
            key = (name, half)
            st[key] = part if key not in st else combine(st[key], part)

    def score_ops(i):
        s_ref, st = scratch[i], state[i]

        def start():
            stacked = []
            for half in range(2):
                q = q_ref[i * t + half * h:i * t + (half + 1) * h, :]
                zero = jnp.zeros_like(q)
                stacked += [jnp.where(lane < DA_QK_DIM, q, zero), jnp.where(lane >= DA_QK_DIM, q, zero)]
            st["qs"] = jnp.concatenate(stacked, axis=0)

        def one(c, rows, lo):
            s = lax.dot_general(k_ref[rows, :], st["qs"][lo:, :], nt, preferred_element_type=F32)
            first_diag = 2 * i
            if c == first_diag:
                s = jnp.concatenate([jnp.where(causal, s[:, :t], -jnp.inf), s[:, t:]], axis=1)
            elif c == first_diag + 1:
                s = jnp.where(causal, s, -jnp.inf)
            s_ref[rows, lo:] = s
            both(st, "m", lo, jnp.max(s, axis=0, keepdims=True), jnp.maximum)

        return [start] + [functools.partial(one, c, rows, lo) for c, (rows, lo) in enumerate(chunks(i))]

    def prob_ops(i):
        s_ref, st = scratch[i], state[i]

        def one(rows, lo):
            m = st[("m", 1)] if lo else jnp.concatenate([st[("m", 0)], st[("m", 1)]], axis=1)
            p = jnp.exp2(s_ref[rows, lo:] - m)
            s_ref[rows, lo:] = p
            both(st, "l", lo, jnp.sum(p, axis=0, keepdims=True), jnp.add)

        return [functools.partial(one, rows, lo) for rows, lo in chunks(i)]

    def value_ops(i):
        s_ref, st = scratch[i], state[i]

        def combine(rows, half):
            l = st[("l", half)]
            ratio = lam * l[:, :h] / l[:, h:]
            base = half * t
            return s_ref[rows, base:base + h] - ratio * s_ref[rows, base + h:base + t]

        def one(rows, lo):
            halves = (1,) if lo else (0, 1)
            a = jnp.concatenate([combine(rows, half) for half in halves], axis=1).astype(BF16)
            pv = lax.dot_general(v_ref[rows, :], a, tn, preferred_element_type=F32)
            for n, half in enumerate(halves):
                part = pv[:, n * h:(n + 1) * h]
                st["acc", half] = part if ("acc", half) not in st else st["acc", half] + part

        def finish():
            o = jnp.concatenate([st[("acc", half)] * (1.0 / st[("l", half)][:, :h]) for half in range(2)], axis=1)
            inv = lax.rsqrt(jnp.mean(o * o, axis=0, keepdims=True) + EPS)
            o_ref[i * t:(i + 1) * t, :] = (o * inv * gain).T.astype(BF16)

        return [functools.partial(one, rows, lo) for rows, lo in chunks(i)] + [finish]

    order = _tile_order(nq)
    for step in range(nq + 2):
        streams = []
        if step < nq:
            streams.append(score_ops(order[step]))
        if 0 <= step - 1 < nq:
            streams.append(prob_ops(order[step - 1]))
        if 0 <= step - 2 < nq:
            streams.append(value_ops(order[step - 2]))
        for op in _interleave(*streams):
            op()


def _attn_call(q, k, v, lq1, lk1, lq2, lk2, g_subln_col, t):
    b, _, s, _ = q.shape
    nq = s // t
    head = pl.BlockSpec((None, None, s, DA_V_DIM), lambda bi, h: (bi, h, 0, 0))
    vec = _const_spec((1, DA_QK_DIM))
    return pl.pallas_call(
        functools.partial(_attn_kernel, t=t, nq=nq),
        grid=(b, DA_HEADS),
        in_specs=[head, head, head, vec, vec, vec, vec, _const_spec((DA_V_DIM, 1))],
        out_specs=head,
        out_shape=jax.ShapeDtypeStruct((b, DA_HEADS, s, DA_V_DIM), BF16),
        scratch_shapes=[pltpu.VMEM(((i + 1) * t, 2 * t), F32) for i in range(nq)],
        compiler_params=pltpu.CompilerParams(
            dimension_semantics=("arbitrary", "arbitrary"), vmem_limit_bytes=VMEM_LIMIT),
        name="diff_attn",
    )(q, k, v, lq1, lk1, lq2, lk2, g_subln_col)


def _mixout_kernel(da_ref, po_ref, x_ref, wo_hbm, gpost_ref, gxpre_ref, wxq_hbm, kv_ref, wxo_hbm,
                   gxpost_ref, o_ref, wo_ref, wxq_ref, wxo_ref, stage, sem):
    @pl.when(_first_step(2))
    def _():
        _load_weights([(wo_hbm, wo_ref), (wxq_hbm, wxq_ref), (wxo_hbm, wxo_ref)], stage, sem)

    tm = x_ref.shape[1]
    groups = [slice(r, r + tm // 2) for r in (0, tm // 2)]
    x_scale = X_HEAD_DIM ** -0.5

    def phases(rows):
        st = {}

        def mix():
            da = jnp.concatenate([da_ref[h, rows, :] for h in range(DA_HEADS)], axis=-1)
            st["mix"] = (jnp.dot(da, wo_ref[0:DA_WIDTH, :], preferred_element_type=F32)
                         + jnp.dot(po_ref[0, rows, :], wo_ref[DA_WIDTH:, :], preferred_element_type=F32))

        def norms():
            st["x1"] = x_ref[0, rows, :] + _rms(st.pop("mix"), gpost_ref[...])
            st["hq"], st["inv"] = _prenorm_operand(st["x1"], gxpre_ref[...])

        def queries():
            st["xq"] = (jnp.dot(st.pop("hq"), wxq_ref[...], preferred_element_type=F32)
                        * (st.pop("inv") * x_scale)).astype(BF16)

        def attend():
            heads = []
            for h in range(X_HEADS):
                sl = slice(h * X_HEAD_DIM, (h + 1) * X_HEAD_DIM)
                kh = kv_ref[0, :, sl]
                vh = kv_ref[0, :, D_MODEL + h * X_HEAD_DIM:D_MODEL + (h + 1) * X_HEAD_DIM]
                sc = lax.dot_general(st["xq"][:, sl], kh, (((1,), (1,)), ((), ())), preferred_element_type=F32)
                p = jnp.exp(sc - jnp.max(sc, axis=-1, keepdims=True))
                pm = (p * (1.0 / jnp.sum(p, axis=-1, keepdims=True))).astype(BF16)
                heads.append(jnp.dot(pm, vh, preferred_element_type=F32).astype(BF16))
            st["xo"] = jnp.concatenate(heads, axis=-1)

        def project():
            st["y"] = jnp.dot(st.pop("xo"), wxo_ref[...], preferred_element_type=F32)

        def finish():
            o_ref[0, rows, :] = st["x1"] + _rms(st["y"], gxpost_ref[...])

        return [mix, norms, queries, attend, project, finish]

    first, second = (phases(rows) for rows in groups)
    program = [first[0]]
    for k in range(1, len(first)):
        program += [first[k], second[k - 1]]
    program.append(second[-1])
    for phase in program:
        phase()


def _mixout_call(da, po, x, w_out, g_mix_post, g_x_pre, w_xq, kv, w_xo, g_x_post, tm):
    b, s, d = x.shape
    tok = lambda n: pl.BlockSpec((1, tm, n), lambda bi, i: (bi, i, 0))
    gvec = _const_spec((1, d))
    return pl.pallas_call(
        _mixout_kernel,
        grid=(b, s // tm),
        in_specs=[pl.BlockSpec((None, DA_HEADS, tm, DA_V_DIM), lambda bi, i: (bi, 0, i, 0)),
                  tok(POOL_WIDTH), tok(d), _HBM, gvec, gvec, _HBM,
                  pl.BlockSpec((1,) + kv.shape[1:], lambda bi, i: (bi, 0, 0)),
                  _HBM, gvec],
        out_specs=tok(d),
        out_shape=jax.ShapeDtypeStruct((b, s, d), F32),
        scratch_shapes=[pltpu.VMEM(w_out.shape, BF16), pltpu.VMEM(w_xq.shape, BF16),
                        pltpu.VMEM(w_xo.shape, BF16), *_stage_scratch(d, d)],
        compiler_params=pltpu.CompilerParams(
            dimension_semantics=("arbitrary", "arbitrary"), vmem_limit_bytes=VMEM_LIMIT),
        name="mixout_xattn",
    )(da, po, x, w_out, g_mix_post, g_x_pre, w_xq, kv, w_xo, g_x_post)


def _ffn_kernel(x_ref, gpre_ref, wg_hbm, wu_hbm, wd_hbm, gpost_ref, o_ref,
                wg_ref, wu_ref, wd_ref, wide_stage, wide_sem, tall_stage, tall_sem):
    @pl.when(_first_step(2))
    def _():
        _load_weights([(wg_hbm, wg_ref), (wu_hbm, wu_ref)], wide_stage, wide_sem)
        _load_weights([(wd_hbm, wd_ref)], tall_stage, tall_sem)

    x = x_ref[0]
    hf, inv = _prenorm_operand(x, gpre_ref[...])
    ff = None
    for lo, hi in FF_CHUNKS:
        gate = jnp.dot(hf, wg_ref[:, lo:hi], preferred_element_type=F32) * inv
        up = jnp.dot(hf, wu_ref[:, lo:hi], preferred_element_type=F32) * inv
        act = (gate * (1.0 / (1.0 + jnp.exp(-gate))) * up).astype(BF16)
        part = jnp.dot(act, wd_ref[lo:hi, :], preferred_element_type=F32)
        ff = part if ff is None else ff + part
    o_ref[0] = x + _rms(ff, gpost_ref[...])


def _ffn_call(x, g_pre, w_gate, w_up, w_down, g_post, tm):
    b, s, d = x.shape
    tok = pl.BlockSpec((1, tm, d), lambda bi, i: (bi, i, 0))
    gvec = _const_spec((1, d))
    return pl.pallas_call(
        _ffn_kernel,
        grid=(b, s // tm),
        in_specs=[tok, gvec, _HBM, _HBM, _HBM, gvec],
        out_specs=tok,
        out_shape=jax.ShapeDtypeStruct((b, s, d), F32),
        scratch_shapes=[pltpu.VMEM(w_gate.shape, BF16), pltpu.VMEM(w_up.shape, BF16),
                        pltpu.VMEM(w_down.shape, BF16),
                        *_stage_scratch(d, w_gate.shape[1]),
                        *_stage_scratch(w_down.shape[0], d)],
        compiler_params=pltpu.CompilerParams(
            dimension_semantics=("arbitrary", "arbitrary"), vmem_limit_bytes=VMEM_LIMIT),
        name="swiglu",
    )(x, g_pre, w_gate, w_up, w_down, g_post)


def _rope_lane_constants():
    inv_freq = ROPE_THETA ** (-jnp.arange(0, ROPE_DIM, 2, dtype=F32) / ROPE_DIM)
    packed_freq = jnp.tile(inv_freq, LANES // ROPE_HALF)
    one_half = jnp.ones((ROPE_HALF,), F32)
    per_map = jnp.concatenate([-one_half, one_half, jnp.zeros((DA_QK_DIM - ROPE_DIM,), F32)])
    sign = jnp.concatenate([per_map, per_map])
    return jnp.concatenate([jnp.stack([packed_freq, sign]), jnp.zeros((6, LANES), F32)], axis=0)


def kernel(x, mem, positions, g_mix_pre, w_in, lambda_q1, lambda_k1, lambda_q2, lambda_k2,
           g_subln, w_pool, pool_scale, w_out, g_mix_post, g_x_pre, g_mem, w_xq, w_xkv, w_xo,
           g_x_post, g_ffn_pre, w_gate, w_up, w_down, g_ffn_post):
    b, s, d = x.shape
    assert d == D_MODEL and s % TOKEN_TILE == 0 and s % ATTN_TILE == 0
    row = lambda a: a.reshape(1, -1).astype(F32)
    bf = lambda a: a.astype(BF16)

    pos = positions.reshape(b, 1, s)
    kv = _kv_call(mem, row(g_mem), w_xkv)
    q, k, v, po = _inproj_call(x, row(g_mix_pre), w_in, pos, _rope_lane_constants(),
                               bf(w_pool), row(pool_scale), TOKEN_TILE)
    da = _attn_call(q, k, v, row(lambda_q1), row(lambda_k1), row(lambda_q2), row(lambda_k2),
                    g_subln.reshape(-1, 1).astype(F32), ATTN_TILE)
    x2 = _mixout_call(da, po, x, w_out, row(g_mix_post), row(g_x_pre), w_xq, kv, w_xo,
                      row(g_x_post), TOKEN_TILE)
    return _ffn_call(x2, row(g_ffn_pre), w_gate, w_up, w_down, row(g_ffn_post), TOKEN_TILE)
```

```python
import functools
import itertools
import math

import jax
import jax.numpy as jnp
from jax import lax
from jax.experimental import pallas as pl
from jax.experimental.pallas import tpu as pltpu

F32 = jnp.float32
BF16 = jnp.bfloat16

D_MODEL = 1024
MEM_LEN = 256
EPS = 1e-6
DA_HEADS = 4
DA_QK_DIM = 64
DA_V_DIM = 2 * DA_QK_DIM
DA_WIDTH = DA_HEADS * DA_V_DIM
QK_WIDTH = DA_HEADS * 2 * DA_QK_DIM
POOL_WINDOWS = (2, 4, 8, 16)
POOL_WIDTH = D_MODEL - DA_WIDTH
POOL_GROUP_DIM = POOL_WIDTH // len(POOL_WINDOWS)
ROPE_THETA = 500000.0
ROPE_DIM = DA_QK_DIM // 4
ROPE_HALF = ROPE_DIM // 2
X_HEADS = 4
X_HEAD_DIM = D_MODEL // X_HEADS
D_FF = -(-(8 * D_MODEL) // (3 * 256)) * 256
LAMBDA_INIT = 0.8 - 0.6 * math.exp(-0.3 * 0)
LOG2_E = math.log2(math.e)

LANES = 128
POOL_HALO = 16
VMEM_LIMIT = 56 * 1024 * 1024

TOKEN_TILE = 512
ATTN_TILE = 512
FF_CHUNKS = ((0, 1536), (1536, 2816))
STAGE_BYTES = 4 * 1024 * 1024


def _inv_rms(x):
    return lax.rsqrt(jnp.mean(x * x, axis=-1, keepdims=True) + EPS)


def _rms(x, g):
    return x * _inv_rms(x) * g


def _prenorm_operand(x, g):
    return (x * g).astype(BF16), _inv_rms(x)


def _const_spec(shape):
    zeros = (0,) * len(shape)
    return pl.BlockSpec(shape, lambda *_: zeros, pipeline_mode=pl.Buffered(1))


_HBM = pl.BlockSpec(memory_space=pl.ANY)


def _stage_scratch(rows, cols):
    chunks = -(-(rows * cols * 4) // STAGE_BYTES)
    while rows % chunks or (rows // chunks) % 16:
        chunks += 1
    return [pltpu.VMEM((2, rows // chunks, cols), F32), pltpu.SemaphoreType.DMA((2,))]


def _load_weights(weights, stage_ref, sem_ref):
    rows = stage_ref.shape[1]
    jobs = []
    for w_hbm, w_vmem in weights:
        assert w_hbm.shape[0] % rows == 0 and stage_ref.shape[2] == w_hbm.shape[1]
        jobs += [(w_hbm, w_vmem, r) for r in range(0, w_hbm.shape[0], rows)]

    def copy(k):
        w_hbm, _, r = jobs[k]
        return pltpu.make_async_copy(w_hbm.at[pl.ds(r, rows), :], stage_ref.at[k % 2], sem_ref.at[k % 2])

    copy(0).start()
    for k, (_, w_vmem, r) in enumerate(jobs):
        if k + 1 < len(jobs):
            copy(k + 1).start()
        copy(k).wait()
        w_vmem[r:r + rows, :] = stage_ref[k % 2].astype(BF16)


def _first_step(grid_rank):
    first = pl.program_id(0) == 0
    for axis in range(1, grid_rank):
        first = first & (pl.program_id(axis) == 0)
    return first


def _kv_kernel(mem_ref, g_ref, w_hbm, kv_ref, w_ref, stage, sem):
    @pl.when(_first_step(1))
    def _():
        _load_weights([(w_hbm, w_ref)], stage, sem)

    mn = _rms(mem_ref[0], g_ref[...]).astype(BF16)
    kv_ref[0] = jnp.dot(mn, w_ref[...], preferred_element_type=F32).astype(BF16)


def _kv_call(mem, g_mem, w_xkv):
    b, m, d = mem.shape
    n = w_xkv.shape[1]
    return pl.pallas_call(
        _kv_kernel,
        grid=(b,),
        in_specs=[pl.BlockSpec((1, m, d), lambda i: (i, 0, 0)),
                  _const_spec((1, d)), _HBM],
        out_specs=pl.BlockSpec((1, m, n), lambda i: (i, 0, 0)),
        out_shape=jax.ShapeDtypeStruct((b, m, n), BF16),
        scratch_shapes=[pltpu.VMEM((d, n), BF16), *_stage_scratch(d, n)],
        compiler_params=pltpu.CompilerParams(
            dimension_semantics=("arbitrary",), vmem_limit_bytes=VMEM_LIMIT),
        name="kv_proj",
    )(mem, g_mem, w_xkv)


def _inproj_kernel(x_ref, g_ref, w_hbm, pos_ref, freq_ref, wp_ref, ps_ref,
                   q_ref, k_ref, v_ref, po_ref, halo_ref, w_ref, stage, sem, *, tm):
    @pl.when(_first_step(2))
    def _():
        _load_weights([(w_hbm, w_ref)], stage, sem)

    i = pl.program_id(1)
    h = _rms(x_ref[0], g_ref[...]).astype(BF16)

    groups = LANES // ROPE_DIM
    rows = tm // groups
    lane = lax.broadcasted_iota(jnp.int32, (rows, LANES), 1)
    lane_group = lax.shift_right_logical(lane, int(math.log2(ROPE_DIM)))
    pos = jnp.broadcast_to(pos_ref[0].astype(F32), (LANES, tm)).T
    packed = jnp.zeros((rows, LANES), F32)
    for g in range(groups):
        packed = jnp.where(lane_group == g, pos[g * rows:(g + 1) * rows], packed)
    ang = packed * freq_ref[0:1, :]
    cos8, sin8 = jnp.cos(ang), jnp.sin(ang)
    first_map = lane < DA_QK_DIM
    rotary = freq_ref[1:2, :] != 0.0

    def spread(packed_table, g):
        shift = (LANES - ROPE_DIM * g) % LANES
        lo = pltpu.roll(packed_table, shift, 1) if shift else packed_table
        hi_shift = (shift + DA_QK_DIM) % LANES
        hi = pltpu.roll(packed_table, hi_shift, 1) if hi_shift else packed_table
        return jnp.where(first_map, lo, hi)

    cos_t = jnp.concatenate([jnp.where(rotary, spread(cos8, g), 1.0) for g in range(groups)], axis=0)
    sin_t = jnp.concatenate([spread(sin8, g) * freq_ref[1:2, :] for g in range(groups)], axis=0)
    first_half = freq_ref[1:2, :] < 0.0

    def rope(t, c, s):
        partner = jnp.where(first_half, pltpu.roll(t, LANES - ROPE_HALF, 1), pltpu.roll(t, ROPE_HALF, 1))
        return t * c + partner * s

    qk_scale = DA_QK_DIM ** -0.5 * LOG2_E
    cos_q, sin_q = cos_t * qk_scale, sin_t * qk_scale
    pq = jnp.dot(h, w_ref[:, 0:QK_WIDTH], preferred_element_type=F32)
    for j in range(DA_HEADS):
        sl = slice(j * LANES, (j + 1) * LANES)
        q_ref[j] = rope(pq[:, sl], cos_q, sin_q).astype(BF16)
    pk = jnp.dot(h, w_ref[:, QK_WIDTH:2 * QK_WIDTH], preferred_element_type=F32)
    for j in range(DA_HEADS):
        sl = slice(j * LANES, (j + 1) * LANES)
        k_ref[j] = rope(pk[:, sl], cos_t, sin_t).astype(BF16)
    pv = jnp.dot(h, w_ref[:, 2 * QK_WIDTH:2 * QK_WIDTH + DA_WIDTH], preferred_element_type=F32)
    for j in range(DA_HEADS):
        v_ref[j] = pv[:, j * DA_V_DIM:(j + 1) * DA_V_DIM].astype(BF16)
    u = jnp.dot(h, w_ref[:, 2 * QK_WIDTH + DA_WIDTH:], preferred_element_type=F32)

    @pl.when(i == 0)
    def _():
        halo_ref[...] = jnp.zeros_like(halo_ref)

    prev = halo_ref[...]
    halo_ref[...] = u[tm - POOL_HALO:, :]
    tpos = i * tm + lax.broadcasted_iota(jnp.int32, (tm, 1), 0)
    for g, w in enumerate(POOL_WINDOWS):
        sl = slice(g * POOL_GROUP_DIM, (g + 1) * POOL_GROUP_DIM)
        ug = u[:, sl]
        s = jnp.concatenate([prev[:, sl], ug], axis=0)
        d = 1
        while d < w:
            s = s + pltpu.roll(s, d, 0)
            d *= 2
        inv_count = 1.0 / jnp.minimum(tpos + 1, w).astype(F32)
        pooled = s[POOL_HALO:, :] * inv_count - ug
        po = jnp.dot(pooled.astype(BF16), wp_ref[g], preferred_element_type=F32) * ps_ref[:, sl]
        po_ref[0, :, sl] = po.astype(BF16)


def _inproj_call(x, g, w_in, pos, freq, w_pool, pool_scale, tm):
    b, s, d = x.shape
    tok = lambda n: pl.BlockSpec((1, tm, n), lambda bi, i: (bi, i, 0))
    heads = pl.BlockSpec((None, DA_HEADS, tm, DA_V_DIM), lambda bi, i: (bi, 0, i, 0))
    head_major = jax.ShapeDtypeStruct((b, DA_HEADS, s, DA_V_DIM), BF16)
    return pl.pallas_call(
        functools.partial(_inproj_kernel, tm=tm),
        grid=(b, s // tm),
        in_specs=[tok(d), _const_spec((1, d)), _HBM,
                  pl.BlockSpec((1, 1, tm), lambda bi, i: (bi, 0, i)), _const_spec(freq.shape),
                  _const_spec(w_pool.shape), _const_spec((1, POOL_WIDTH))],
        out_specs=[heads, heads, heads, tok(POOL_WIDTH)],
        out_shape=[head_major, head_major, head_major,
                   jax.ShapeDtypeStruct((b, s, POOL_WIDTH), BF16)],
        scratch_shapes=[pltpu.VMEM((POOL_HALO, POOL_WIDTH), F32), pltpu.VMEM(w_in.shape, BF16),
                        *_stage_scratch(d, w_in.shape[1])],
        compiler_params=pltpu.CompilerParams(
            dimension_semantics=("arbitrary", "arbitrary"), vmem_limit_bytes=VMEM_LIMIT),
        name="inproj",
    )(x, g, w_in, pos, freq, w_pool, pool_scale)


def _interleave(*streams):
    tagged = [((k + 0.5) / len(ops), n, op) for n, ops in enumerate(streams) for k, op in enumerate(ops)]
    return [op for _, _, op in sorted(tagged, key=lambda e: e[:2])]


def _tile_order(nq):
    def length(order):
        total = 0.0
        for step in range(nq + 2):
            scores = order[step] + 1 if step < nq else 0
            probs = order[step - 1] + 1 if 0 <= step - 1 < nq else 0
            values = (order[step - 2] + 1) / 2 if 0 <= step - 2 < nq else 0
            total += max(scores + values, probs)
        return total

    return min(itertools.permutations(range(nq)), key=length)


def _attn_kernel(q_ref, k_ref, v_ref, lq1_ref, lk1_ref, lq2_ref, lk2_ref, gs_ref, o_ref, *scratch, t, nq):
    h = t // 2
    lane = lax.broadcasted_iota(jnp.int32, (h, DA_V_DIM), 1)
    causal = (lax.broadcasted_iota(jnp.int32, (h, h), 0) <= lax.broadcasted_iota(jnp.int32, (h, h), 1))
    causal = jnp.concatenate([causal, causal], axis=1)
    lam = (jnp.exp(jnp.sum(lq1_ref[...] * lk1_ref[...], axis=-1, keepdims=True))
           - jnp.exp(jnp.sum(lq2_ref[...] * lk2_ref[...], axis=-1, keepdims=True))
           + LAMBDA_INIT)
    gain = gs_ref[...] * (1.0 - LAMBDA_INIT)
    nt = (((1,), (1,)), ((), ()))
    tn = (((0,), (0,)), ((), ()))
    state = [dict() for _ in range(nq)]

    def chunks(i):
        n = 2 * (i + 1)
        return [(slice(c * h, (c + 1) * h), t if c == n - 1 else 0) for c in range(n)]

    def both(st, name, lo, value, combine):
        for half, part in ((0, value[:, :t]), (1, value[:, t:])) if lo == 0 else ((1, value),):
            key = (name, half)
            st[key] = part if key not in st else combine(st[key], part)

    def score_ops(i):
        s_ref, st = scratch[i], state[i]

        def start():
            stacked = []
            for half in range(2):
                q = q_ref[i * t + half * h:i * t + (half + 1) * h, :]
                zero = jnp.zeros_like(q)
                stacked += [jnp.where(lane < DA_QK_DIM, q, zero), jnp.where(lane >= DA_QK_DIM, q, zero)]
            st["qs"] = jnp.concatenate(stacked, axis=0)

        def one(c, rows, lo):
            s = lax.dot_general(k_ref[rows, :], st["qs"][lo:, :], nt, preferred_element_type=F32)
            first_diag = 2 * i
            if c == first_diag:
                s = jnp.concatenate([jnp.where(causal, s[:, :t], -jnp.inf), s[:, t:]], axis=1)
            elif c == first_diag + 1:
                s = jnp.where(causal, s, -jnp.inf)
            s_ref[rows, lo:] = s
            both(st, "m", lo, jnp.max(s, axis=0, keepdims=True), jnp.maximum)

        return [start] + [functools.partial(one, c, rows, lo) for c, (rows, lo) in enumerate(chunks(i))]

    def prob_ops(i):
        s_ref, st = scratch[i], state[i]

        def one(rows, lo):
            m = st[("m", 1)] if lo else jnp.concatenate([st[("m", 0)], st[("m", 1)]], axis=1)
            p = jnp.exp2(s_ref[rows, lo:] - m)
            s_ref[rows, lo:] = p
            both(st, "l", lo, jnp.sum(p, axis=0, keepdims=True), jnp.add)

        return [functools.partial(one, rows, lo) for rows, lo in chunks(i)]

    def value_ops(i):
        s_ref, st = scratch[i], state[i]

        def combine(rows, half):
            l = st[("l", half)]
            ratio = lam * l[:, :h] / l[:, h:]
            base = half * t
            return s_ref[rows, base:base + h] - ratio * s_ref[rows, base + h:base + t]

        def one(rows, lo):
            halves = (1,) if lo else (0, 1)
            a = jnp.concatenate([combine(rows, half) for half in halves], axis=1).astype(BF16)
            pv = lax.dot_general(v_ref[rows, :], a, tn, preferred_element_type=F32)
            for n, half in enumerate(halves):
                part = pv[:, n * h:(n + 1) * h]
                st["acc", half] = part if ("acc", half) not in st else st["acc", half] + part

        def finish():
            o = jnp.concatenate([st[("acc", half)] * (1.0 / st[("l", half)][:, :h]) for half in range(2)], axis=1)
            inv = lax.rsqrt(jnp.mean(o * o, axis=0, keepdims=True) + EPS)
            o_ref[i * t:(i + 1) * t, :] = (o * inv * gain).T.astype(BF16)

        return [functools.partial(one, rows, lo) for rows, lo in chunks(i)] + [finish]

    order = _tile_order(nq)
    for step in range(nq + 2):
        streams = []
        if step < nq:
            streams.append(score_ops(order[step]))
        if 0 <= step - 1 < nq:
            streams.append(prob_ops(order[step - 1]))
        if 0 <= step - 2 < nq:
            streams.append(value_ops(order[step - 2]))
        for op in _interleave(*streams):
            op()


def _attn_call(q, k, v, lq1, lk1, lq2, lk2, g_subln_col, t):
    b, _, s, _ = q.shape
    nq = s // t
    head = pl.BlockSpec((None, None, s, DA_V_DIM), lambda bi, h: (bi, h, 0, 0))
    vec = _const_spec((1, DA_QK_DIM))
    return pl.pallas_call(
        functools.partial(_attn_kernel, t=t, nq=nq),
        grid=(b, DA_HEADS),
        in_specs=[head, head, head, vec, vec, vec, vec, _const_spec((DA_V_DIM, 1))],
        out_specs=head,
        out_shape=jax.ShapeDtypeStruct((b, DA_HEADS, s, DA_V_DIM), BF16),
        scratch_shapes=[pltpu.VMEM(((i + 1) * t, 2 * t), F32) for i in range(nq)],
        compiler_params=pltpu.CompilerParams(
            dimension_semantics=("arbitrary", "arbitrary"), vmem_limit_bytes=VMEM_LIMIT),
        name="diff_attn",
    )(q, k, v, lq1, lk1, lq2, lk2, g_subln_col)


def _mixout_kernel(da_ref, po_ref, x_ref, wo_hbm, gpost_ref, gxpre_ref, wxq_hbm, kv_ref, wxo_hbm,
                   gxpost_ref, o_ref, wo_ref, wxq_ref, wxo_ref, stage, sem):
    @pl.when(_first_step(2))
    def _():
        _load_weights([(wo_hbm, wo_ref), (wxq_hbm, wxq_ref), (wxo_hbm, wxo_ref)], stage, sem)

    tm = x_ref.shape[1]
    groups = [slice(r, r + tm // 2) for r in (0, tm // 2)]
    x_scale = X_HEAD_DIM ** -0.5

    def phases(rows):
        st = {}

        def mix():
            da = jnp.concatenate([da_ref[h, rows, :] for h in range(DA_HEADS)], axis=-1)
            st["mix"] = (jnp.dot(da, wo_ref[0:DA_WIDTH, :], preferred_element_type=F32)
                         + jnp.dot(po_ref[0, rows, :], wo_ref[DA_WIDTH:, :], preferred_element_type=F32))

        def norms():
            st["x1"] = x_ref[0, rows, :] + _rms(st.pop("mix"), gpost_ref[...])
            st["hq"], st["inv"] = _prenorm_operand(st["x1"], gxpre_ref[...])

        def queries():
            st["xq"] = (jnp.dot(st.pop("hq"), wxq_ref[...], preferred_element_type=F32)
                        * (st.pop("inv") * x_scale)).astype(BF16)

        def attend():
            heads = []
            for h in range(X_HEADS):
                sl = slice(h * X_HEAD_DIM, (h + 1) * X_HEAD_DIM)
                kh = kv_ref[0, :, sl]
                vh = kv_ref[0, :, D_MODEL + h * X_HEAD_DIM:D_MODEL + (h + 1) * X_HEAD_DIM]
                sc = lax.dot_general(st["xq"][:, sl], kh, (((1,), (1,)), ((), ())), preferred_element_type=F32)
                p = jnp.exp(sc - jnp.max(sc, axis=-1, keepdims=True))
                pm = (p * (1.0 / jnp.sum(p, axis=-1, keepdims=True))).astype(BF16)
                heads.append(jnp.dot(pm, vh, preferred_element_type=F32).astype(BF16))
            st["xo"] = jnp.concatenate(heads, axis=-1)

        def project():
            st["y"] = jnp.dot(st.pop("xo"), wxo_ref[...], preferred_element_type=F32)

        def finish():
            o_ref[0, rows, :] = st["x1"] + _rms(st["y"], gxpost_ref[...])

        return [mix, norms, queries, attend, project, finish]

    first, second = (phases(rows) for rows in groups)
    program = [first[0]]
    for k in range(1, len(first)):
        program += [first[k], second[k - 1]]
    program.append(second[-1])
    for phase in program:
        phase()


def _mixout_call(da, po, x, w_out, g_mix_post, g_x_pre, w_xq, kv, w_xo, g_x_post, tm):
    b, s, d = x.shape
    tok = lambda n: pl.BlockSpec((1, tm, n), lambda bi, i: (bi, i, 0))
    gvec = _const_spec((1, d))
    return pl.pallas_call(
        _mixout_kernel,
        grid=(b, s // tm),
        in_specs=[pl.BlockSpec((None, DA_HEADS, tm, DA_V_DIM), lambda bi, i: (bi, 0, i, 0)),
                  tok(POOL_WIDTH), tok(d), _HBM, gvec, gvec, _HBM,
                  pl.BlockSpec((1,) + kv.shape[1:], lambda bi, i: (bi, 0, 0)),
                  _HBM, gvec],
        out_specs=tok(d),
        out_shape=jax.ShapeDtypeStruct((b, s, d), F32),
        scratch_shapes=[pltpu.VMEM(w_out.shape, BF16), pltpu.VMEM(w_xq.shape, BF16),
                        pltpu.VMEM(w_xo.shape, BF16), *_stage_scratch(d, d)],
        compiler_params=pltpu.CompilerParams(
            dimension_semantics=("arbitrary", "arbitrary"), vmem_limit_bytes=VMEM_LIMIT),
        name="mixout_xattn",
    )(da, po, x, w_out, g_mix_post, g_x_pre, w_xq, kv, w_xo, g_x_post)


def _ffn_kernel(x_ref, gpre_ref, wg_hbm, wu_hbm, wd_hbm, gpost_ref, o_ref,
                wg_ref, wu_ref, wd_ref, wide_stage, wide_sem, tall_stage, tall_sem):
    @pl.when(_first_step(2))
    def _():
        _load_weights([(wg_hbm, wg_ref), (wu_hbm, wu_ref)], wide_stage, wide_sem)
        _load_weights([(wd_hbm, wd_ref)], tall_stage, tall_sem)

    tm = x_ref.shape[1]

    def phases(rows):
        st = {}

        def prenorm():
            st["x"] = x_ref[0, rows, :]
            st["hf"], st["inv"] = _prenorm_operand(st["x"], gpre_ref[...])

        def chunk(lo, hi):
            gate = jnp.dot(st["hf"], wg_ref[:, lo:hi], preferred_element_type=F32) * st["inv"]
            up = jnp.dot(st["hf"], wu_ref[:, lo:hi], preferred_element_type=F32) * st["inv"]
            act = (gate * (1.0 / (1.0 + jnp.exp(-gate))) * up).astype(BF16)
            part = jnp.dot(act, wd_ref[lo:hi, :], preferred_element_type=F32)
            st["ff"] = part if "ff" not in st else st["ff"] + part

        def finish():
            o_ref[0, rows, :] = st["x"] + _rms(st["ff"], gpost_ref[...])

        return [prenorm] + [functools.partial(chunk, lo, hi) for lo, hi in FF_CHUNKS] + [finish]

    first, second = (phases(slice(r, r + tm // 2)) for r in (0, tm // 2))
    program = [first[0]]
    for k in range(1, len(first)):
        program += [first[k], second[k - 1]]
    program.append(second[-1])
    for phase in program:
        phase()


def _ffn_call(x, g_pre, w_gate, w_up, w_down, g_post, tm):
    b, s, d = x.shape
    tok = pl.BlockSpec((1, tm, d), lambda bi, i: (bi, i, 0))
    gvec = _const_spec((1, d))
    return pl.pallas_call(
        _ffn_kernel,
        grid=(b, s // tm),
        in_specs=[tok, gvec, _HBM, _HBM, _HBM, gvec],
        out_specs=tok,
        out_shape=jax.ShapeDtypeStruct((b, s, d), F32),
        scratch_shapes=[pltpu.VMEM(w_gate.shape, BF16), pltpu.VMEM(w_up.shape, BF16),
                        pltpu.VMEM(w_down.shape, BF16),
                        *_stage_scratch(d, w_gate.shape[1]),
                        *_stage_scratch(w_down.shape[0], d)],
        compiler_params=pltpu.CompilerParams(
            dimension_semantics=("arbitrary", "arbitrary"), vmem_limit_bytes=VMEM_LIMIT),
        name="swiglu",
    )(x, g_pre, w_gate, w_up, w_down, g_post)


def _rope_lane_constants():
    inv_freq = ROPE_THETA ** (-jnp.arange(0, ROPE_DIM, 2, dtype=F32) / ROPE_DIM)
    packed_freq = jnp.tile(inv_freq, LANES // ROPE_HALF)
    one_half = jnp.ones((ROPE_HALF,), F32)
    per_map = jnp.concatenate([-one_half, one_half, jnp.zeros((DA_QK_DIM - ROPE_DIM,), F32)])
    sign = jnp.concatenate([per_map, per_map])
    return jnp.concatenate([jnp.stack([packed_freq, sign]), jnp.zeros((6, LANES), F32)], axis=0)


def kernel(x, mem, positions, g_mix_pre, w_in, lambda_q1, lambda_k1, lambda_q2, lambda_k2,
           g_subln, w_pool, pool_scale, w_out, g_mix_post, g_x_pre, g_mem, w_xq, w_xkv, w_xo,
           g_x_post, g_ffn_pre, w_gate, w_up, w_down, g_ffn_post):
    b, s, d = x.shape
    assert d == D_MODEL and s % TOKEN_TILE == 0 and s % ATTN_TILE == 0
    row = lambda a: a.reshape(1, -1).astype(F32)
    bf = lambda a: a.astype(BF16)

    pos = positions.reshape(b, 1, s)
    kv = _kv_call(mem, row(g_mem), w_xkv)
    q, k, v, po = _inproj_call(x, row(g_mix_pre), w_in, pos, _rope_lane_constants(),
                               bf(w_pool), row(pool_scale), TOKEN_TILE)
    da = _attn_call(q, k, v, row(lambda_q1), row(lambda_k1), row(lambda_q2), row(lambda_k2),
                    g_subln.reshape(-1, 1).astype(F32), ATTN_TILE)
    x2 = _mixout_call(da, po, x, w_out, row(g_mix_post), row(g_x_pre), w_xq, kv, w_xo,
                      row(g_x_post), TOKEN_TILE)
    return _ffn_call(x2, row(g_ffn_pre), w_gate, w_up, w_down, row(g_ffn_post), TOKEN_TILE)
```

```python
import functools
import itertools
import math

import jax
import jax.numpy as jnp
from jax import lax
from jax.experimental import pallas as pl
from jax.experimental.pallas import tpu as pltpu

F32 = jnp.float32
BF16 = jnp.bfloat16

D_MODEL = 1024
MEM_LEN = 256
EPS = 1e-6
DA_HEADS = 4
DA_QK_DIM = 64
DA_V_DIM = 2 * DA_QK_DIM
DA_WIDTH = DA_HEADS * DA_V_DIM
QK_WIDTH = DA_HEADS * 2 * DA_QK_DIM
POOL_WINDOWS = (2, 4, 8, 16)
POOL_WIDTH = D_MODEL - DA_WIDTH
POOL_GROUP_DIM = POOL_WIDTH // len(POOL_WINDOWS)
ROPE_THETA = 500000.0
ROPE_DIM = DA_QK_DIM // 4
ROPE_HALF = ROPE_DIM // 2
X_HEADS = 4
X_HEAD_DIM = D_MODEL // X_HEADS
D_FF = -(-(8 * D_MODEL) // (3 * 256)) * 256
LAMBDA_INIT = 0.8 - 0.6 * math.exp(-0.3 * 0)
LOG2_E = math.log2(math.e)

LANES = 128
POOL_HALO = 16
VMEM_LIMIT = 56 * 1024 * 1024

TOKEN_TILE = 512
ATTN_TILE = 512
FF_CHUNKS = ((0, 1536), (1536, 2816))
STAGE_BYTES = 4 * 1024 * 1024


def _inv_rms(x):
    return lax.rsqrt(jnp.mean(x * x, axis=-1, keepdims=True) + EPS)


def _rms(x, g):
    return x * _inv_rms(x) * g


def _prenorm_operand(x, g):
    return (x * g).astype(BF16), _inv_rms(x)


def _const_spec(shape):
    zeros = (0,) * len(shape)
    return pl.BlockSpec(shape, lambda *_: zeros, pipeline_mode=pl.Buffered(1))


_HBM = pl.BlockSpec(memory_space=pl.ANY)


def _stage_scratch(rows, cols):
    chunks = -(-(rows * cols * 4) // STAGE_BYTES)
    while rows % chunks or (rows // chunks) % 16:
        chunks += 1
    return [pltpu.VMEM((2, rows // chunks, cols), F32), pltpu.SemaphoreType.DMA((2,))]


def _load_weights(weights, stage_ref, sem_ref):
    rows = stage_ref.shape[1]
    jobs = []
    for w_hbm, w_vmem in weights:
        assert w_hbm.shape[0] % rows == 0 and stage_ref.shape[2] == w_hbm.shape[1]
        jobs += [(w_hbm, w_vmem, r) for r in range(0, w_hbm.shape[0], rows)]

    def copy(k):
        w_hbm, _, r = jobs[k]
        return pltpu.make_async_copy(w_hbm.at[pl.ds(r, rows), :], stage_ref.at[k % 2], sem_ref.at[k % 2])

    copy(0).start()
    for k, (_, w_vmem, r) in enumerate(jobs):
        if k + 1 < len(jobs):
            copy(k + 1).start()
        copy(k).wait()
        w_vmem[r:r + rows, :] = stage_ref[k % 2].astype(BF16)


def _first_step(grid_rank):
    first = pl.program_id(0) == 0
    for axis in range(1, grid_rank):
        first = first & (pl.program_id(axis) == 0)
    return first


def _kv_kernel(mem_ref, g_ref, w_hbm, kv_ref, w_ref, stage, sem):
    @pl.when(_first_step(1))
    def _():
        _load_weights([(w_hbm, w_ref)], stage, sem)

    mn = _rms(mem_ref[0], g_ref[...]).astype(BF16)
    kv_ref[0] = jnp.dot(mn, w_ref[...], preferred_element_type=F32).astype(BF16)


def _kv_call(mem, g_mem, w_xkv):
    b, m, d = mem.shape
    n = w_xkv.shape[1]
    return pl.pallas_call(
        _kv_kernel,
        grid=(b,),
        in_specs=[pl.BlockSpec((1, m, d), lambda i: (i, 0, 0)),
                  _const_spec((1, d)), _HBM],
        out_specs=pl.BlockSpec((1, m, n), lambda i: (i, 0, 0)),
        out_shape=jax.ShapeDtypeStruct((b, m, n), BF16),
        scratch_shapes=[pltpu.VMEM((d, n), BF16), *_stage_scratch(d, n)],
        compiler_params=pltpu.CompilerParams(
            dimension_semantics=("arbitrary",), vmem_limit_bytes=VMEM_LIMIT),
        name="kv_proj",
    )(mem, g_mem, w_xkv)


def _inproj_kernel(x_ref, g_ref, w_hbm, pos_ref, freq_ref, wp_ref, ps_ref,
                   q_ref, k_ref, v_ref, po_ref, halo_ref, w_ref, stage, sem, *, tm):
    @pl.when(_first_step(2))
    def _():
        _load_weights([(w_hbm, w_ref)], stage, sem)

    i = pl.program_id(1)
    h = _rms(x_ref[0], g_ref[...]).astype(BF16)

    groups = LANES // ROPE_DIM
    rows = tm // groups
    lane = lax.broadcasted_iota(jnp.int32, (rows, LANES), 1)
    lane_group = lax.shift_right_logical(lane, int(math.log2(ROPE_DIM)))
    pos = jnp.broadcast_to(pos_ref[0].astype(F32), (LANES, tm)).T
    packed = jnp.zeros((rows, LANES), F32)
    for g in range(groups):
        packed = jnp.where(lane_group == g, pos[g * rows:(g + 1) * rows], packed)
    ang = packed * freq_ref[0:1, :]
    cos8, sin8 = jnp.cos(ang), jnp.sin(ang)
    first_map = lane < DA_QK_DIM
    rotary = freq_ref[1:2, :] != 0.0

    def spread(packed_table, g):
        shift = (LANES - ROPE_DIM * g) % LANES
        lo = pltpu.roll(packed_table, shift, 1) if shift else packed_table
        hi_shift = (shift + DA_QK_DIM) % LANES
        hi = pltpu.roll(packed_table, hi_shift, 1) if hi_shift else packed_table
        return jnp.where(first_map, lo, hi)

    cos_t = jnp.concatenate([jnp.where(rotary, spread(cos8, g), 1.0) for g in range(groups)], axis=0)
    sin_t = jnp.concatenate([spread(sin8, g) * freq_ref[1:2, :] for g in range(groups)], axis=0)
    first_half = freq_ref[1:2, :] < 0.0

    def rope(t, c, s):
        partner = jnp.where(first_half, pltpu.roll(t, LANES - ROPE_HALF, 1), pltpu.roll(t, ROPE_HALF, 1))
        return t * c + partner * s

    qk_scale = DA_QK_DIM ** -0.5 * LOG2_E
    cos_q, sin_q = cos_t * qk_scale, sin_t * qk_scale
    pq = jnp.dot(h, w_ref[:, 0:QK_WIDTH], preferred_element_type=F32)
    for j in range(DA_HEADS):
        sl = slice(j * LANES, (j + 1) * LANES)
        q_ref[j] = rope(pq[:, sl], cos_q, sin_q).astype(BF16)
    pk = jnp.dot(h, w_ref[:, QK_WIDTH:2 * QK_WIDTH], preferred_element_type=F32)
    for j in range(DA_HEADS):
        sl = slice(j * LANES, (j + 1) * LANES)
        k_ref[j] = rope(pk[:, sl], cos_t, sin_t).astype(BF16)
    pv = jnp.dot(h, w_ref[:, 2 * QK_WIDTH:2 * QK_WIDTH + DA_WIDTH], preferred_element_type=F32)
    for j in range(DA_HEADS):
        v_ref[j] = pv[:, j * DA_V_DIM:(j + 1) * DA_V_DIM].astype(BF16)
    u = jnp.dot(h, w_ref[:, 2 * QK_WIDTH + DA_WIDTH:], preferred_element_type=F32)

    @pl.when(i == 0)
    def _():
        halo_ref[...] = jnp.zeros_like(halo_ref)

    prev = halo_ref[...]
    halo_ref[...] = u[tm - POOL_HALO:, :]
    tpos = i * tm + lax.broadcasted_iota(jnp.int32, (tm, 1), 0)
    for g, w in enumerate(POOL_WINDOWS):
        sl = slice(g * POOL_GROUP_DIM, (g + 1) * POOL_GROUP_DIM)
        ug = u[:, sl]
        s = jnp.concatenate([prev[:, sl], ug], axis=0)
        d = 1
        while d < w:
            s = s + pltpu.roll(s, d, 0)
            d *= 2
        inv_count = 1.0 / jnp.minimum(tpos + 1, w).astype(F32)
        pooled = s[POOL_HALO:, :] * inv_count - ug
        po = jnp.dot(pooled.astype(BF16), wp_ref[g], preferred_element_type=F32) * ps_ref[:, sl]
        po_ref[0, :, sl] = po.astype(BF16)


def _inproj_call(x, g, w_in, pos, freq, w_pool, pool_scale, tm):
    b, s, d = x.shape
    tok = lambda n: pl.BlockSpec((1, tm, n), lambda bi, i: (bi, i, 0))
    heads = pl.BlockSpec((None, DA_HEADS, tm, DA_V_DIM), lambda bi, i: (bi, 0, i, 0))
    head_major = jax.ShapeDtypeStruct((b, DA_HEADS, s, DA_V_DIM), BF16)
    return pl.pallas_call(
        functools.partial(_inproj_kernel, tm=tm),
        grid=(b, s // tm),
        in_specs=[tok(d), _const_spec((1, d)), _HBM,
                  pl.BlockSpec((1, 1, tm), lambda bi, i: (bi, 0, i)), _const_spec(freq.shape),
                  _const_spec(w_pool.shape), _const_spec((1, POOL_WIDTH))],
        out_specs=[heads, heads, heads, tok(POOL_WIDTH)],
        out_shape=[head_major, head_major, head_major,
                   jax.ShapeDtypeStruct((b, s, POOL_WIDTH), BF16)],
        scratch_shapes=[pltpu.VMEM((POOL_HALO, POOL_WIDTH), F32), pltpu.VMEM(w_in.shape, BF16),
                        *_stage_scratch(d, w_in.shape[1])],
        compiler_params=pltpu.CompilerParams(
            dimension_semantics=("arbitrary", "arbitrary"), vmem_limit_bytes=VMEM_LIMIT),
        name="inproj",
    )(x, g, w_in, pos, freq, w_pool, pool_scale)


def _interleave(*streams):
    tagged = [((k + 0.5) / len(ops), n, op) for n, ops in enumerate(streams) for k, op in enumerate(ops)]
    return [op for _, _, op in sorted(tagged, key=lambda e: e[:2])]


def _tile_order(nq):
    def length(order):
        total = 0.0
        for step in range(nq + 2):
            scores = order[step] + 1 if step < nq else 0
            probs = order[step - 1] + 1 if 0 <= step - 1 < nq else 0
            values = (order[step - 2] + 1) / 2 if 0 <= step - 2 < nq else 0
            total += max(scores + values, probs)
        return total

    return min(itertools.permutations(range(nq)), key=length)


def _attn_kernel(q_ref, k_ref, v_ref, lq1_ref, lk1_ref, lq2_ref, lk2_ref, gs_ref, o_ref, *scratch, t, nq):
    h = t // 2
    lane = lax.broadcasted_iota(jnp.int32, (h, DA_V_DIM), 1)
    causal = (lax.broadcasted_iota(jnp.int32, (h, h), 0) <= lax.broadcasted_iota(jnp.int32, (h, h), 1))
    causal = jnp.concatenate([causal, causal], axis=1)
    lam = (jnp.exp(jnp.sum(lq1_ref[...] * lk1_ref[...], axis=-1, keepdims=True))
           - jnp.exp(jnp.sum(lq2_ref[...] * lk2_ref[...], axis=-1, keepdims=True))
           + LAMBDA_INIT)
    gain = gs_ref[...] * (1.0 - LAMBDA_INIT)
    nt = (((1,), (1,)), ((), ()))
    tn = (((0,), (0,)), ((), ()))
    state = [dict() for _ in range(nq)]

    def chunks(i):
        n = 2 * (i + 1)
        return [(slice(c * h, (c + 1) * h), t if c == n - 1 else 0) for c in range(n)]

    def both(st, name, lo, value, combine):
        for half, part in ((0, value[:, :t]), (1, value[:, t:])) if lo == 0 else ((1, value),):
            key = (name, half)
            st[key] = part if key not in st else combine(st[key], part)

    def score_ops(i):
        s_ref, st = scratch[i], state[i]

        def start():
            stacked = []
            for half in range(2):
                q = q_ref[i * t + half * h:i * t + (half + 1) * h, :]
                zero = jnp.zeros_like(q)
                stacked += [jnp.where(lane < DA_QK_DIM, q, zero), jnp.where(lane >= DA_QK_DIM, q, zero)]
            st["qs"] = jnp.concatenate(stacked, axis=0)

        def one(c, rows, lo):
            s = lax.dot_general(k_ref[rows, :], st["qs"][lo:, :], nt, preferred_element_type=F32)
            first_diag = 2 * i
            if c == first_diag:
                s = jnp.concatenate([jnp.where(causal, s[:, :t], -jnp.inf), s[:, t:]], axis=1)
            elif c == first_diag + 1:
                s = jnp.where(causal, s, -jnp.inf)
            s_ref[rows, lo:] = s
            both(st, "m", lo, jnp.max(s, axis=0, keepdims=True), jnp.maximum)

        return [start] + [functools.partial(one, c, rows, lo) for c, (rows, lo) in enumerate(chunks(i))]

    def prob_ops(i):
        s_ref, st = scratch[i], state[i]

        def one(rows, lo):
            m = st[("m", 1)] if lo else jnp.concatenate([st[("m", 0)], st[("m", 1)]], axis=1)
            p = jnp.exp2(s_ref[rows, lo:] - m)
            s_ref[rows, lo:] = p
            both(st, "l", lo, jnp.sum(p, axis=0, keepdims=True), jnp.add)

        return [functools.partial(one, rows, lo) for rows, lo in chunks(i)]

    def value_ops(i):
        s_ref, st = scratch[i], state[i]

        def combine(rows, half):
            l = st[("l", half)]
            ratio = lam * l[:, :h] / l[:, h:]
            base = half * t
            return s_ref[rows, base:base + h] - ratio * s_ref[rows, base + h:base + t]

        def one(rows, lo):
            halves = (1,) if lo else (0, 1)
            a = jnp.concatenate([combine(rows, half) for half in halves], axis=1).astype(BF16)
            pv = lax.dot_general(v_ref[rows, :], a, tn, preferred_element_type=F32)
            for n, half in enumerate(halves):
                part = pv[:, n * h:(n + 1) * h]
                st["acc", half] = part if ("acc", half) not in st else st["acc", half] + part

        def finish():
            o = jnp.concatenate([st[("acc", half)] * (1.0 / st[("l", half)][:, :h]) for half in range(2)], axis=1)
            inv = lax.rsqrt(jnp.mean(o * o, axis=0, keepdims=True) + EPS)
            o_ref[i * t:(i + 1) * t, :] = (o * inv * gain).T.astype(BF16)

        return [functools.partial(one, rows, lo) for rows, lo in chunks(i)] + [finish]

    order = _tile_order(nq)
    for step in range(nq + 2):
        streams = []
        if step < nq:
            streams.append(score_ops(order[step]))
        if 0 <= step - 1 < nq:
            streams.append(prob_ops(order[step - 1]))
        if 0 <= step - 2 < nq:
            streams.append(value_ops(order[step - 2]))
        for op in _interleave(*streams):
            op()


def _attn_call(q, k, v, lq1, lk1, lq2, lk2, g_subln_col, t):
    b, _, s, _ = q.shape
    nq = s // t
    head = pl.BlockSpec((None, None, s, DA_V_DIM), lambda bi, h: (bi, h, 0, 0))
    vec = _const_spec((1, DA_QK_DIM))
    return pl.pallas_call(
        functools.partial(_attn_kernel, t=t, nq=nq),
        grid=(b, DA_HEADS),
        in_specs=[head, head, head, vec, vec, vec, vec, _const_spec((DA_V_DIM, 1))],
        out_specs=head,
        out_shape=jax.ShapeDtypeStruct((b, DA_HEADS, s, DA_V_DIM), BF16),
        scratch_shapes=[pltpu.VMEM(((i + 1) * t, 2 * t), F32) for i in range(nq)],
        compiler_params=pltpu.CompilerParams(
            dimension_semantics=("arbitrary", "arbitrary"), vmem_limit_bytes=VMEM_LIMIT),
        name="diff_attn",
    )(q, k, v, lq1, lk1, lq2, lk2, g_subln_col)


def _mixout_kernel(da_ref, po_ref, x_ref, wo_hbm, gpost_ref, gxpre_ref, wxq_hbm, kv_ref, wxo_hbm,
                   gxpost_ref, o_ref, wo_ref, wxq_ref, wxo_ref, stage, sem):
    @pl.when(_first_step(2))
    def _():
        _load_weights([(wo_hbm, wo_ref), (wxq_hbm, wxq_ref), (wxo_hbm, wxo_ref)], stage, sem)

    tm = x_ref.shape[1]
    groups = [slice(r, r + tm // 2) for r in (0, tm // 2)]
    x_scale = X_HEAD_DIM ** -0.5

    def phases(rows):
        st = {}

        def mix():
            da = jnp.concatenate([da_ref[h, rows, :] for h in range(DA_HEADS)], axis=-1)
            st["mix"] = (jnp.dot(da, wo_ref[0:DA_WIDTH, :], preferred_element_type=F32)
                         + jnp.dot(po_ref[0, rows, :], wo_ref[DA_WIDTH:, :], preferred_element_type=F32))

        def norms():
            st["x1"] = x_ref[0, rows, :] + _rms(st.pop("mix"), gpost_ref[...])
            st["hq"], st["inv"] = _prenorm_operand(st["x1"], gxpre_ref[...])

        def queries():
            st["xq"] = (jnp.dot(st.pop("hq"), wxq_ref[...], preferred_element_type=F32)
                        * (st.pop("inv") * x_scale)).astype(BF16)

        def attend():
            heads = []
            for h in range(X_HEADS):
                sl = slice(h * X_HEAD_DIM, (h + 1) * X_HEAD_DIM)
                kh = kv_ref[0, :, sl]
                vh = kv_ref[0, :, D_MODEL + h * X_HEAD_DIM:D_MODEL + (h + 1) * X_HEAD_DIM]
                sc = lax.dot_general(st["xq"][:, sl], kh, (((1,), (1,)), ((), ())), preferred_element_type=F32)
                p = jnp.exp(sc - jnp.max(sc, axis=-1, keepdims=True))
                pm = (p * (1.0 / jnp.sum(p, axis=-1, keepdims=True))).astype(BF16)
                heads.append(jnp.dot(pm, vh, preferred_element_type=F32).astype(BF16))
            st["xo"] = jnp.concatenate(heads, axis=-1)

        def project():
            st["y"] = jnp.dot(st.pop("xo"), wxo_ref[...], preferred_element_type=F32)

        def finish():
            o_ref[0, rows, :] = st["x1"] + _rms(st["y"], gxpost_ref[...])

        return [mix, norms, queries, attend, project, finish]

    first, second = (phases(rows) for rows in groups)
    program = [first[0]]
    for k in range(1, len(first)):
        program += [first[k], second[k - 1]]
    program.append(second[-1])
    for phase in program:
        phase()


def _mixout_call(da, po, x, w_out, g_mix_post, g_x_pre, w_xq, kv, w_xo, g_x_post, tm):
    b, s, d = x.shape
    tok = lambda n: pl.BlockSpec((1, tm, n), lambda bi, i: (bi, i, 0))
    gvec = _const_spec((1, d))
    return pl.pallas_call(
        _mixout_kernel,
        grid=(b, s // tm),
        in_specs=[pl.BlockSpec((None, DA_HEADS, tm, DA_V_DIM), lambda bi, i: (bi, 0, i, 0)),
                  tok(POOL_WIDTH), tok(d), _HBM, gvec, gvec, _HBM,
                  pl.BlockSpec((1,) + kv.shape[1:], lambda bi, i: (bi, 0, 0)),
                  _HBM, gvec],
        out_specs=tok(d),
        out_shape=jax.ShapeDtypeStruct((b, s, d), F32),
        scratch_shapes=[pltpu.VMEM(w_out.shape, BF16), pltpu.VMEM(w_xq.shape, BF16),
                        pltpu.VMEM(w_xo.shape, BF16), *_stage_scratch(d, d)],
        compiler_params=pltpu.CompilerParams(
            dimension_semantics=("arbitrary", "arbitrary"), vmem_limit_bytes=VMEM_LIMIT),
        name="mixout_xattn",
    )(da, po, x, w_out, g_mix_post, g_x_pre, w_xq, kv, w_xo, g_x_post)


def _ffn_kernel(x_ref, gpre_ref, wg_hbm, wu_hbm, wd_hbm, gpost_ref, o_ref,
                wg_ref, wu_ref, wd_ref, wide_stage, wide_sem, tall_stage, tall_sem):
    @pl.when(_first_step(2))
    def _():
        _load_weights([(wg_hbm, wg_ref), (wu_hbm, wu_ref)], wide_stage, wide_sem)
        _load_weights([(wd_hbm, wd_ref)], tall_stage, tall_sem)

    tm = x_ref.shape[1]

    def phases(rows):
        st = {}

        def prenorm():
            st["x"] = x_ref[0, rows, :]
            st["hf"], st["inv"] = _prenorm_operand(st["x"], gpre_ref[...])

        def chunk(lo, hi):
            gate = jnp.dot(st["hf"], wg_ref[:, lo:hi], preferred_element_type=F32) * st["inv"]
            up = jnp.dot(st["hf"], wu_ref[:, lo:hi], preferred_element_type=F32) * st["inv"]
            act = (gate * (1.0 / (1.0 + jnp.exp(-gate))) * up).astype(BF16)
            part = jnp.dot(act, wd_ref[lo:hi, :], preferred_element_type=F32)
            st["ff"] = part if "ff" not in st else st["ff"] + part

        def finish():
            o_ref[0, rows, :] = st["x"] + _rms(st["ff"], gpost_ref[...])

        return [prenorm] + [functools.partial(chunk, lo, hi) for lo, hi in FF_CHUNKS] + [finish]

    first, second = (phases(slice(r, r + tm // 2)) for r in (0, tm // 2))
    program = [first[0]]
    for k in range(1, len(first)):
        program += [first[k], second[k - 1]]
    program.append(second[-1])
    for phase in program:
        phase()


def _ffn_call(x, g_pre, w_gate, w_up, w_down, g_post, tm):
    b, s, d = x.shape
    tok = pl.BlockSpec((1, tm, d), lambda bi, i: (bi, i, 0))
    gvec = _const_spec((1, d))
    return pl.pallas_call(
        _ffn_kernel,
        grid=(b, s // tm),
        in_specs=[tok, gvec, _HBM, _HBM, _HBM, gvec],
        out_specs=tok,
        out_shape=jax.ShapeDtypeStruct((b, s, d), F32),
        scratch_shapes=[pltpu.VMEM(w_gate.shape, BF16), pltpu.VMEM(w_up.shape, BF16),
                        pltpu.VMEM(w_down.shape, BF16),
                        *_stage_scratch(d, w_gate.shape[1]),
                        *_stage_scratch(w_down.shape[0], d)],
        compiler_params=pltpu.CompilerParams(
            dimension_semantics=("arbitrary", "arbitrary"), vmem_limit_bytes=VMEM_LIMIT),
        name="swiglu",
    )(x, g_pre, w_gate, w_up, w_down, g_post)


def _rope_lane_constants():
    inv_freq = ROPE_THETA ** (-jnp.arange(0, ROPE_DIM, 2, dtype=F32) / ROPE_DIM)
    packed_freq = jnp.tile(inv_freq, LANES // ROPE_HALF)
    one_half = jnp.ones((ROPE_HALF,), F32)
    per_map = jnp.concatenate([-one_half, one_half, jnp.zeros((DA_QK_DIM - ROPE_DIM,), F32)])
    sign = jnp.concatenate([per_map, per_map])
    return jnp.concatenate([jnp.stack([packed_freq, sign]), jnp.zeros((6, LANES), F32)], axis=0)


def kernel(x, mem, positions, g_mix_pre, w_in, lambda_q1, lambda_k1, lambda_q2, lambda_k2,
           g_subln, w_pool, pool_scale, w_out, g_mix_post, g_x_pre, g_mem, w_xq, w_xkv, w_xo,
           g_x_post, g_ffn_pre, w_gate, w_up, w_down, g_ffn_post):
    b, s, d = x.shape
    assert d == D_MODEL and s % TOKEN_TILE == 0 and s % ATTN_TILE == 0
    row = lambda a: a.reshape(1, -1).astype(F32)
    bf = lambda a: a.astype(BF16)

    pos = positions.reshape(b, 1, s)
    kv = _kv_call(mem, row(g_mem), w_xkv)
    q, k, v, po = _inproj_call(x, row(g_mix_pre), w_in, pos, _rope_lane_constants(),
                               bf(w_pool), row(pool_scale), TOKEN_TILE)
    da = _attn_call(q, k, v, row(lambda_q1), row(lambda_k1), row(lambda_q2), row(lambda_k2),
                    g_subln.reshape(-1, 1).astype(F32), ATTN_TILE)
    x2 = _mixout_call(da, po, x, w_out, row(g_mix_post), row(g_x_pre), w_xq, kv, w_xo,
                      row(g_x_post), 2 * TOKEN_TILE)
    return _ffn_call(x2, row(g_ffn_pre), w_gate, w_up, w_down, row(g_ffn_post), TOKEN_TILE)
```

```python
import functools
import itertools
import math

import jax
import jax.numpy as jnp
from jax import lax
from jax.experimental import pallas as pl
from jax.experimental.pallas import tpu as pltpu

F32 = jnp.float32
BF16 = jnp.bfloat16

D_MODEL = 1024
MEM_LEN = 256
EPS = 1e-6
DA_HEADS = 4
DA_QK_DIM = 64
DA_V_DIM = 2 * DA_QK_DIM
DA_WIDTH = DA_HEADS * DA_V_DIM
QK_WIDTH = DA_HEADS * 2 * DA_QK_DIM
POOL_WINDOWS = (2, 4, 8, 16)
POOL_WIDTH = D_MODEL - DA_WIDTH
POOL_GROUP_DIM = POOL_WIDTH // len(POOL_WINDOWS)
ROPE_THETA = 500000.0
ROPE_DIM = DA_QK_DIM // 4
ROPE_HALF = ROPE_DIM // 2
X_HEADS = 4
X_HEAD_DIM = D_MODEL // X_HEADS
D_FF = -(-(8 * D_MODEL) // (3 * 256)) * 256
LAMBDA_INIT = 0.8 - 0.6 * math.exp(-0.3 * 0)
LOG2_E = math.log2(math.e)

LANES = 128
POOL_HALO = 16
VMEM_LIMIT = 56 * 1024 * 1024

TOKEN_TILE = 512
ATTN_TILE = 512
FF_CHUNKS = ((0, 1536), (1536, 2816))
STAGE_BYTES = 4 * 1024 * 1024


def _inv_rms(x):
    return lax.rsqrt(jnp.mean(x * x, axis=-1, keepdims=True) + EPS)


def _rms(x, g):
    return x * _inv_rms(x) * g


def _prenorm_operand(x, g):
    return (x * g).astype(BF16), _inv_rms(x)


def _const_spec(shape):
    zeros = (0,) * len(shape)
    return pl.BlockSpec(shape, lambda *_: zeros, pipeline_mode=pl.Buffered(1))


_HBM = pl.BlockSpec(memory_space=pl.ANY)


def _stage_scratch(rows, cols):
    chunks = -(-(rows * cols * 4) // STAGE_BYTES)
    while rows % chunks or (rows // chunks) % 16:
        chunks += 1
    return [pltpu.VMEM((2, rows // chunks, cols), F32), pltpu.SemaphoreType.DMA((2,))]


def _load_weights(weights, stage_ref, sem_ref):
    rows = stage_ref.shape[1]
    jobs = []
    for w_hbm, w_vmem in weights:
        assert w_hbm.shape[0] % rows == 0 and stage_ref.shape[2] == w_hbm.shape[1]
        jobs += [(w_hbm, w_vmem, r) for r in range(0, w_hbm.shape[0], rows)]

    def copy(k):
        w_hbm, _, r = jobs[k]
        return pltpu.make_async_copy(w_hbm.at[pl.ds(r, rows), :], stage_ref.at[k % 2], sem_ref.at[k % 2])

    copy(0).start()
    for k, (_, w_vmem, r) in enumerate(jobs):
        if k + 1 < len(jobs):
            copy(k + 1).start()
        copy(k).wait()
        w_vmem[r:r + rows, :] = stage_ref[k % 2].astype(BF16)


def _first_step(grid_rank):
    first = pl.program_id(0) == 0
    for axis in range(1, grid_rank):
        first = first & (pl.program_id(axis) == 0)
    return first


def _kv_kernel(mem_ref, g_ref, w_hbm, kv_ref, w_ref, stage, sem):
    @pl.when(_first_step(1))
    def _():
        _load_weights([(w_hbm, w_ref)], stage, sem)

    mn = _rms(mem_ref[0], g_ref[...]).astype(BF16)
    kv_ref[0] = jnp.dot(mn, w_ref[...], preferred_element_type=F32).astype(BF16)


def _kv_call(mem, g_mem, w_xkv):
    b, m, d = mem.shape
    n = w_xkv.shape[1]
    return pl.pallas_call(
        _kv_kernel,
        grid=(b,),
        in_specs=[pl.BlockSpec((1, m, d), lambda i: (i, 0, 0)),
                  _const_spec((1, d)), _HBM],
        out_specs=pl.BlockSpec((1, m, n), lambda i: (i, 0, 0)),
        out_shape=jax.ShapeDtypeStruct((b, m, n), BF16),
        scratch_shapes=[pltpu.VMEM((d, n), BF16), *_stage_scratch(d, n)],
        compiler_params=pltpu.CompilerParams(
            dimension_semantics=("arbitrary",), vmem_limit_bytes=VMEM_LIMIT),
        name="kv_proj",
    )(mem, g_mem, w_xkv)


def _inproj_kernel(x_ref, g_ref, w_hbm, pos_ref, freq_ref, wp_ref, ps_ref,
                   q_ref, k_ref, v_ref, po_ref, halo_ref, w_ref, stage, sem, *, tm):
    @pl.when(_first_step(2))
    def _():
        _load_weights([(w_hbm, w_ref)], stage, sem)

    i = pl.program_id(1)
    h = _rms(x_ref[0], g_ref[...]).astype(BF16)

    groups = LANES // ROPE_DIM
    rows = tm // groups
    lane = lax.broadcasted_iota(jnp.int32, (rows, LANES), 1)
    lane_group = lax.shift_right_logical(lane, int(math.log2(ROPE_DIM)))
    pos = jnp.broadcast_to(pos_ref[0].astype(F32), (LANES, tm)).T
    packed = jnp.zeros((rows, LANES), F32)
    for g in range(groups):
        packed = jnp.where(lane_group == g, pos[g * rows:(g + 1) * rows], packed)
    ang = packed * freq_ref[0:1, :]
    cos8, sin8 = jnp.cos(ang), jnp.sin(ang)
    first_map = lane < DA_QK_DIM
    rotary = freq_ref[1:2, :] != 0.0

    def spread(packed_table, g):
        shift = (LANES - ROPE_DIM * g) % LANES
        lo = pltpu.roll(packed_table, shift, 1) if shift else packed_table
        hi_shift = (shift + DA_QK_DIM) % LANES
        hi = pltpu.roll(packed_table, hi_shift, 1) if hi_shift else packed_table
        return jnp.where(first_map, lo, hi)

    cos_t = jnp.concatenate([jnp.where(rotary, spread(cos8, g), 1.0) for g in range(groups)], axis=0)
    sin_t = jnp.concatenate([spread(sin8, g) * freq_ref[1:2, :] for g in range(groups)], axis=0)
    first_half = freq_ref[1:2, :] < 0.0

    def rope(t, c, s):
        partner = jnp.where(first_half, pltpu.roll(t, LANES - ROPE_HALF, 1), pltpu.roll(t, ROPE_HALF, 1))
        return t * c + partner * s

    qk_scale = DA_QK_DIM ** -0.5 * LOG2_E
    cos_q, sin_q = cos_t * qk_scale, sin_t * qk_scale
    pq = jnp.dot(h, w_ref[:, 0:QK_WIDTH], preferred_element_type=F32)
    for j in range(DA_HEADS):
        sl = slice(j * LANES, (j + 1) * LANES)
        q_ref[j] = rope(pq[:, sl], cos_q, sin_q).astype(BF16)
    pk = jnp.dot(h, w_ref[:, QK_WIDTH:2 * QK_WIDTH], preferred_element_type=F32)
    for j in range(DA_HEADS):
        sl = slice(j * LANES, (j + 1) * LANES)
        k_ref[j] = rope(pk[:, sl], cos_t, sin_t).astype(BF16)
    pv = jnp.dot(h, w_ref[:, 2 * QK_WIDTH:2 * QK_WIDTH + DA_WIDTH], preferred_element_type=F32)
    for j in range(DA_HEADS):
        v_ref[j] = pv[:, j * DA_V_DIM:(j + 1) * DA_V_DIM].astype(BF16)
    u = jnp.dot(h, w_ref[:, 2 * QK_WIDTH + DA_WIDTH:], preferred_element_type=F32)

    @pl.when(i == 0)
    def _():
        halo_ref[...] = jnp.zeros_like(halo_ref)

    prev = halo_ref[...]
    halo_ref[...] = u[tm - POOL_HALO:, :]
    tpos = i * tm + lax.broadcasted_iota(jnp.int32, (tm, 1), 0)
    for g, w in enumerate(POOL_WINDOWS):
        sl = slice(g * POOL_GROUP_DIM, (g + 1) * POOL_GROUP_DIM)
        ug = u[:, sl]
        s = jnp.concatenate([prev[:, sl], ug], axis=0)
        d = 1
        while d < w:
            s = s + pltpu.roll(s, d, 0)
            d *= 2
        inv_count = 1.0 / jnp.minimum(tpos + 1, w).astype(F32)
        pooled = s[POOL_HALO:, :] * inv_count - ug
        po = jnp.dot(pooled.astype(BF16), wp_ref[g], preferred_element_type=F32) * ps_ref[:, sl]
        po_ref[0, :, sl] = po.astype(BF16)


def _inproj_call(x, g, w_in, pos, freq, w_pool, pool_scale, tm):
    b, s, d = x.shape
    tok = lambda n: pl.BlockSpec((1, tm, n), lambda bi, i: (bi, i, 0))
    heads = pl.BlockSpec((None, DA_HEADS, tm, DA_V_DIM), lambda bi, i: (bi, 0, i, 0))
    head_major = jax.ShapeDtypeStruct((b, DA_HEADS, s, DA_V_DIM), BF16)
    return pl.pallas_call(
        functools.partial(_inproj_kernel, tm=tm),
        grid=(b, s // tm),
        in_specs=[tok(d), _const_spec((1, d)), _HBM,
                  pl.BlockSpec((1, 1, tm), lambda bi, i: (bi, 0, i)), _const_spec(freq.shape),
                  _const_spec(w_pool.shape), _const_spec((1, POOL_WIDTH))],
        out_specs=[heads, heads, heads, tok(POOL_WIDTH)],
        out_shape=[head_major, head_major, head_major,
                   jax.ShapeDtypeStruct((b, s, POOL_WIDTH), BF16)],
        scratch_shapes=[pltpu.VMEM((POOL_HALO, POOL_WIDTH), F32), pltpu.VMEM(w_in.shape, BF16),
                        *_stage_scratch(d, w_in.shape[1])],
        compiler_params=pltpu.CompilerParams(
            dimension_semantics=("arbitrary", "arbitrary"), vmem_limit_bytes=VMEM_LIMIT),
        name="inproj",
    )(x, g, w_in, pos, freq, w_pool, pool_scale)


def _interleave(*streams):
    tagged = [((k + 0.5) / len(ops), n, op) for n, ops in enumerate(streams) for k, op in enumerate(ops)]
    return [op for _, _, op in sorted(tagged, key=lambda e: e[:2])]


def _tile_order(nq):
    def length(order):
        total = 0.0
        for step in range(nq + 2):
            scores = order[step] + 1 if step < nq else 0
            probs = order[step - 1] + 1 if 0 <= step - 1 < nq else 0
            values = (order[step - 2] + 1) / 2 if 0 <= step - 2 < nq else 0
            total += max(scores + values, probs)
        return total

    return min(itertools.permutations(range(nq)), key=length)


def _attn_kernel(q_ref, k_ref, v_ref, lq1_ref, lk1_ref, lq2_ref, lk2_ref, gs_ref, o_ref, *scratch, t, nq):
    h = t // 2
    lane = lax.broadcasted_iota(jnp.int32, (h, DA_V_DIM), 1)
    causal = (lax.broadcasted_iota(jnp.int32, (h, h), 0) <= lax.broadcasted_iota(jnp.int32, (h, h), 1))
    causal = jnp.concatenate([causal, causal], axis=1)
    lam = (jnp.exp(jnp.sum(lq1_ref[...] * lk1_ref[...], axis=-1, keepdims=True))
           - jnp.exp(jnp.sum(lq2_ref[...] * lk2_ref[...], axis=-1, keepdims=True))
           + LAMBDA_INIT)
    gain = gs_ref[...] * (1.0 - LAMBDA_INIT)
    nt = (((1,), (1,)), ((), ()))
    tn = (((0,), (0,)), ((), ()))
    state = [dict() for _ in range(nq)]

    def chunks(i):
        n = 2 * (i + 1)
        return [(slice(c * h, (c + 1) * h), t if c == n - 1 else 0) for c in range(n)]

    def both(st, name, lo, value, combine):
        for half, part in ((0, value[:, :t]), (1, value[:, t:])) if lo == 0 else ((1, value),):
            key = (name, half)
            st[key] = part if key not in st else combine(st[key], part)

    def score_ops(i):
        s_ref, st = scratch[i], state[i]

        def start():
            stacked = []
            for half in range(2):
                q = q_ref[i * t + half * h:i * t + (half + 1) * h, :]
                zero = jnp.zeros_like(q)
                stacked += [jnp.where(lane < DA_QK_DIM, q, zero), jnp.where(lane >= DA_QK_DIM, q, zero)]
            st["qs"] = jnp.concatenate(stacked, axis=0)

        def one(c, rows, lo):
            s = lax.dot_general(k_ref[rows, :], st["qs"][lo:, :], nt, preferred_element_type=F32)
            first_diag = 2 * i
            if c == first_diag:
                s = jnp.concatenate([jnp.where(causal, s[:, :t], -jnp.inf), s[:, t:]], axis=1)
            elif c == first_diag + 1:
                s = jnp.where(causal, s, -jnp.inf)
            s_ref[rows, lo:] = s
            both(st, "m", lo, jnp.max(s, axis=0, keepdims=True), jnp.maximum)

        return [start] + [functools.partial(one, c, rows, lo) for c, (rows, lo) in enumerate(chunks(i))]

    def prob_ops(i):
        s_ref, st = scratch[i], state[i]

        def one(rows, lo):
            m = st[("m", 1)] if lo else jnp.concatenate([st[("m", 0)], st[("m", 1)]], axis=1)
            p = jnp.exp2(s_ref[rows, lo:] - m)
            s_ref[rows, lo:] = p
            both(st, "l", lo, jnp.sum(p, axis=0, keepdims=True), jnp.add)

        return [functools.partial(one, rows, lo) for rows, lo in chunks(i)]

    def value_ops(i):
        s_ref, st = scratch[i], state[i]

        def combine(rows, half):
            l = st[("l", half)]
            ratio = lam * l[:, :h] / l[:, h:]
            base = half * t
            return s_ref[rows, base:base + h] - ratio * s_ref[rows, base + h:base + t]

        def one(rows, lo):
            halves = (1,) if lo else (0, 1)
            a = jnp.concatenate([combine(rows, half) for half in halves], axis=1).astype(BF16)
            pv = lax.dot_general(v_ref[rows, :], a, tn, preferred_element_type=F32)
            for n, half in enumerate(halves):
                part = pv[:, n * h:(n + 1) * h]
                st["acc", half] = part if ("acc", half) not in st else st["acc", half] + part

        def finish():
            o = jnp.concatenate([st[("acc", half)] * (1.0 / st[("l", half)][:, :h]) for half in range(2)], axis=1)
            inv = lax.rsqrt(jnp.mean(o * o, axis=0, keepdims=True) + EPS)
            o_ref[i * t:(i + 1) * t, :] = (o * inv * gain).T.astype(BF16)

        return [functools.partial(one, rows, lo) for rows, lo in chunks(i)] + [finish]

    order = _tile_order(nq)
    for step in range(nq + 2):
        streams = []
        if step < nq:
            streams.append(score_ops(order[step]))
        if 0 <= step - 1 < nq:
            streams.append(prob_ops(order[step - 1]))
        if 0 <= step - 2 < nq:
            streams.append(value_ops(order[step - 2]))
        for op in _interleave(*streams):
            op()


def _attn_call(q, k, v, lq1, lk1, lq2, lk2, g_subln_col, t):
    b, _, s, _ = q.shape
    nq = s // t
    head = pl.BlockSpec((None, None, s, DA_V_DIM), lambda bi, h: (bi, h, 0, 0))
    vec = _const_spec((1, DA_QK_DIM))
    return pl.pallas_call(
        functools.partial(_attn_kernel, t=t, nq=nq),
        grid=(b, DA_HEADS),
        in_specs=[head, head, head, vec, vec, vec, vec, _const_spec((DA_V_DIM, 1))],
        out_specs=head,
        out_shape=jax.ShapeDtypeStruct((b, DA_HEADS, s, DA_V_DIM), BF16),
        scratch_shapes=[pltpu.VMEM(((i + 1) * t, 2 * t), F32) for i in range(nq)],
        compiler_params=pltpu.CompilerParams(
            dimension_semantics=("arbitrary", "arbitrary"), vmem_limit_bytes=VMEM_LIMIT),
        name="diff_attn",
    )(q, k, v, lq1, lk1, lq2, lk2, g_subln_col)


def _mixout_kernel(da_ref, po_ref, x_ref, wo_hbm, gpost_ref, gxpre_ref, wxq_hbm, kv_ref, wxo_hbm,
                   gxpost_ref, o_ref, wo_ref, wxq_ref, wxo_ref, stage, sem):
    @pl.when(_first_step(2))
    def _():
        _load_weights([(wo_hbm, wo_ref), (wxq_hbm, wxq_ref), (wxo_hbm, wxo_ref)], stage, sem)

    tm = x_ref.shape[1]
    groups = [slice(r, r + tm // 2) for r in (0, tm // 2)]
    x_scale = X_HEAD_DIM ** -0.5

    def phases(rows):
        st = {}

        def mix():
            da = jnp.concatenate([da_ref[h, rows, :] for h in range(DA_HEADS)], axis=-1)
            st["mix"] = (jnp.dot(da, wo_ref[0:DA_WIDTH, :], preferred_element_type=F32)
                         + jnp.dot(po_ref[0, rows, :], wo_ref[DA_WIDTH:, :], preferred_element_type=F32))

        def norms():
            st["x1"] = x_ref[0, rows, :] + _rms(st.pop("mix"), gpost_ref[...])
            st["hq"], st["inv"] = _prenorm_operand(st["x1"], gxpre_ref[...])

        def queries():
            st["xq"] = (jnp.dot(st.pop("hq"), wxq_ref[...], preferred_element_type=F32)
                        * (st.pop("inv") * x_scale)).astype(BF16)

        def attend():
            heads = []
            for h in range(X_HEADS):
                sl = slice(h * X_HEAD_DIM, (h + 1) * X_HEAD_DIM)
                kh = kv_ref[0, :, sl]
                vh = kv_ref[0, :, D_MODEL + h * X_HEAD_DIM:D_MODEL + (h + 1) * X_HEAD_DIM]
                sc = lax.dot_general(st["xq"][:, sl], kh, (((1,), (1,)), ((), ())), preferred_element_type=F32)
                p = jnp.exp(sc - jnp.max(sc, axis=-1, keepdims=True))
                pm = (p * (1.0 / jnp.sum(p, axis=-1, keepdims=True))).astype(BF16)
                heads.append(jnp.dot(pm, vh, preferred_element_type=F32).astype(BF16))
            st["xo"] = jnp.concatenate(heads, axis=-1)

        def project():
            st["y"] = jnp.dot(st.pop("xo"), wxo_ref[...], preferred_element_type=F32)

        def finish():
            o_ref[0, rows, :] = st["x1"] + _rms(st["y"], gxpost_ref[...])

        return [mix, norms, queries, attend, project, finish]

    first, second = (phases(rows) for rows in groups)
    program = [first[0]]
    for k in range(1, len(first)):
        program += [first[k], second[k - 1]]
    program.append(second[-1])
    for phase in program:
        phase()


def _mixout_call(da, po, x, w_out, g_mix_post, g_x_pre, w_xq, kv, w_xo, g_x_post, tm):
    b, s, d = x.shape
    tok = lambda n: pl.BlockSpec((1, tm, n), lambda bi, i: (bi, i, 0))
    gvec = _const_spec((1, d))
    return pl.pallas_call(
        _mixout_kernel,
        grid=(b, s // tm),
        in_specs=[pl.BlockSpec((None, DA_HEADS, tm, DA_V_DIM), lambda bi, i: (bi, 0, i, 0)),
                  tok(POOL_WIDTH), tok(d), _HBM, gvec, gvec, _HBM,
                  pl.BlockSpec((1,) + kv.shape[1:], lambda bi, i: (bi, 0, 0)),
                  _HBM, gvec],
        out_specs=tok(d),
        out_shape=jax.ShapeDtypeStruct((b, s, d), F32),
        scratch_shapes=[pltpu.VMEM(w_out.shape, BF16), pltpu.VMEM(w_xq.shape, BF16),
                        pltpu.VMEM(w_xo.shape, BF16), *_stage_scratch(d, d)],
        compiler_params=pltpu.CompilerParams(
            dimension_semantics=("arbitrary", "arbitrary"), vmem_limit_bytes=VMEM_LIMIT),
        name="mixout_xattn",
    )(da, po, x, w_out, g_mix_post, g_x_pre, w_xq, kv, w_xo, g_x_post)


def _ffn_kernel(x_ref, gpre_ref, wg_hbm, wu_hbm, wd_hbm, gpost_ref, o_ref,
                wg_ref, wu_ref, wd_ref, wide_stage, wide_sem, tall_stage, tall_sem):
    @pl.when(_first_step(2))
    def _():
        _load_weights([(wg_hbm, wg_ref), (wu_hbm, wu_ref)], wide_stage, wide_sem)
        _load_weights([(wd_hbm, wd_ref)], tall_stage, tall_sem)

    tm = x_ref.shape[1]

    def phases(rows):
        st = {}

        def prenorm():
            st["x"] = x_ref[0, rows, :]
            st["hf"], st["inv"] = _prenorm_operand(st["x"], gpre_ref[...])

        def chunk(lo, hi):
            gate = jnp.dot(st["hf"], wg_ref[:, lo:hi], preferred_element_type=F32) * st["inv"]
            up = jnp.dot(st["hf"], wu_ref[:, lo:hi], preferred_element_type=F32) * st["inv"]
            act = (gate * (1.0 / (1.0 + jnp.exp(-gate))) * up).astype(BF16)
            part = jnp.dot(act, wd_ref[lo:hi, :], preferred_element_type=F32)
            st["ff"] = part if "ff" not in st else st["ff"] + part

        def finish():
            o_ref[0, rows, :] = st["x"] + _rms(st["ff"], gpost_ref[...])

        return [prenorm] + [functools.partial(chunk, lo, hi) for lo, hi in FF_CHUNKS] + [finish]

    first, second = (phases(slice(r, r + tm // 2)) for r in (0, tm // 2))
    program = [first[0]]
    for k in range(1, len(first)):
        program += [first[k], second[k - 1]]
    program.append(second[-1])
    for phase in program:
        phase()


def _ffn_call(x, g_pre, w_gate, w_up, w_down, g_post, tm):
    b, s, d = x.shape
    tok = pl.BlockSpec((1, tm, d), lambda bi, i: (bi, i, 0))
    gvec = _const_spec((1, d))
    return pl.pallas_call(
        _ffn_kernel,
        grid=(b, s // tm),
        in_specs=[tok, gvec, _HBM, _HBM, _HBM, gvec],
        out_specs=tok,
        out_shape=jax.ShapeDtypeStruct((b, s, d), F32),
        scratch_shapes=[pltpu.VMEM(w_gate.shape, BF16), pltpu.VMEM(w_up.shape, BF16),
                        pltpu.VMEM(w_down.shape, BF16),
                        *_stage_scratch(d, w_gate.shape[1]),
                        *_stage_scratch(w_down.shape[0], d)],
        compiler_params=pltpu.CompilerParams(
            dimension_semantics=("arbitrary", "arbitrary"), vmem_limit_bytes=VMEM_LIMIT),
        name="swiglu",
    )(x, g_pre, w_gate, w_up, w_down, g_post)


def _rope_lane_constants():
    inv_freq = ROPE_THETA ** (-jnp.arange(0, ROPE_DIM, 2, dtype=F32) / ROPE_DIM)
    packed_freq = jnp.tile(inv_freq, LANES // ROPE_HALF)
    one_half = jnp.ones((ROPE_HALF,), F32)
    per_map = jnp.concatenate([-one_half, one_half, jnp.zeros((DA_QK_DIM - ROPE_DIM,), F32)])
    sign = jnp.concatenate([per_map, per_map])
    return jnp.concatenate([jnp.stack([packed_freq, sign]), jnp.zeros((6, LANES), F32)], axis=0)


def kernel(x, mem, positions, g_mix_pre, w_in, lambda_q1, lambda_k1, lambda_q2, lambda_k2,
           g_subln, w_pool, pool_scale, w_out, g_mix_post, g_x_pre, g_mem, w_xq, w_xkv, w_xo,
           g_x_post, g_ffn_pre, w_gate, w_up, w_down, g_ffn_post):
    b, s, d = x.shape
    assert d == D_MODEL and s % TOKEN_TILE == 0 and s % ATTN_TILE == 0
    row = lambda a: a.reshape(1, -1).astype(F32)
    bf = lambda a: a.astype(BF16)

    pos = positions.reshape(b, 1, s)
    kv = _kv_call(mem, row(g_mem), w_xkv)
    q, k, v, po = _inproj_call(x, row(g_mix_pre), w_in, pos, _rope_lane_constants(),
                               bf(w_pool), row(pool_scale), 2 * TOKEN_TILE)
    da = _attn_call(q, k, v, row(lambda_q1), row(lambda_k1), row(lambda_q2), row(lambda_k2),
                    g_subln.reshape(-1, 1).astype(F32), ATTN_TILE)
    x2 = _mixout_call(da, po, x, w_out, row(g_mix_post), row(g_x_pre), w_xq, kv, w_xo,
                      row(g_x_post), 2 * TOKEN_TILE)
    return _ffn_call(x2, row(g_ffn_pre), w_gate, w_up, w_down, row(g_ffn_post), 2 * TOKEN_TILE)
```

```python
import functools
import itertools
import math

import jax
import jax.numpy as jnp
from jax import lax
from jax.experimental import pallas as pl
from jax.experimental.pallas import tpu as pltpu

F32 = jnp.float32
BF16 = jnp.bfloat16

D_MODEL = 1024
MEM_LEN = 256
EPS = 1e-6
DA_HEADS = 4
DA_QK_DIM = 64
DA_V_DIM = 2 * DA_QK_DIM
DA_WIDTH = DA_HEADS * DA_V_DIM
QK_WIDTH = DA_HEADS * 2 * DA_QK_DIM
POOL_WINDOWS = (2, 4, 8, 16)
POOL_WIDTH = D_MODEL - DA_WIDTH
POOL_GROUP_DIM = POOL_WIDTH // len(POOL_WINDOWS)
ROPE_THETA = 500000.0
ROPE_DIM = DA_QK_DIM // 4
ROPE_HALF = ROPE_DIM // 2
X_HEADS = 4
X_HEAD_DIM = D_MODEL // X_HEADS
D_FF = -(-(8 * D_MODEL) // (3 * 256)) * 256
LAMBDA_INIT = 0.8 - 0.6 * math.exp(-0.3 * 0)
LOG2_E = math.log2(math.e)

LANES = 128
POOL_HALO = 16
VMEM_LIMIT = 56 * 1024 * 1024

TOKEN_TILE = 512
ATTN_TILE = 512
FF_CHUNKS = ((0, 1536), (1536, 2816))
STAGE_BYTES = 4 * 1024 * 1024


def _inv_rms(x):
    return lax.rsqrt(jnp.mean(x * x, axis=-1, keepdims=True) + EPS)


def _rms(x, g):
    return x * _inv_rms(x) * g


def _prenorm_operand(x, g):
    return (x * g).astype(BF16), _inv_rms(x)


def _const_spec(shape):
    zeros = (0,) * len(shape)
    return pl.BlockSpec(shape, lambda *_: zeros, pipeline_mode=pl.Buffered(1))


_HBM = pl.BlockSpec(memory_space=pl.ANY)


def _stage_scratch(rows, cols):
    chunks = -(-(rows * cols * 4) // STAGE_BYTES)
    while rows % chunks or (rows // chunks) % 16:
        chunks += 1
    return [pltpu.VMEM((2, rows // chunks, cols), F32), pltpu.SemaphoreType.DMA((2,))]


def _load_weights(weights, stage_ref, sem_ref):
    rows = stage_ref.shape[1]
    jobs = []
    for w_hbm, w_vmem in weights:
        assert w_hbm.shape[0] % rows == 0 and stage_ref.shape[2] == w_hbm.shape[1]
        jobs += [(w_hbm, w_vmem, r) for r in range(0, w_hbm.shape[0], rows)]

    def copy(k):
        w_hbm, _, r = jobs[k]
        return pltpu.make_async_copy(w_hbm.at[pl.ds(r, rows), :], stage_ref.at[k % 2], sem_ref.at[k % 2])

    copy(0).start()
    for k, (_, w_vmem, r) in enumerate(jobs):
        if k + 1 < len(jobs):
            copy(k + 1).start()
        copy(k).wait()
        w_vmem[r:r + rows, :] = stage_ref[k % 2].astype(BF16)


def _first_step(grid_rank):
    first = pl.program_id(0) == 0
    for axis in range(1, grid_rank):
        first = first & (pl.program_id(axis) == 0)
    return first


def _kv_kernel(mem_ref, g_ref, w_hbm, kv_ref, w_ref, stage, sem):
    @pl.when(_first_step(1))
    def _():
        _load_weights([(w_hbm, w_ref)], stage, sem)

    mn = _rms(mem_ref[...], g_ref[...]).astype(BF16)
    kv_ref[...] = jnp.dot(mn, w_ref[...], preferred_element_type=F32).astype(BF16)


def _kv_call(mem, g_mem, w_xkv, tm):
    b, m, d = mem.shape
    n = w_xkv.shape[1]
    rows = b * m
    tm = min(tm, rows)
    assert rows % tm == 0
    kv = pl.pallas_call(
        _kv_kernel,
        grid=(rows // tm,),
        in_specs=[pl.BlockSpec((tm, d), lambda i: (i, 0)), _const_spec((1, d)), _HBM],
        out_specs=pl.BlockSpec((tm, n), lambda i: (i, 0)),
        out_shape=jax.ShapeDtypeStruct((rows, n), BF16),
        scratch_shapes=[pltpu.VMEM((d, n), BF16), *_stage_scratch(d, n)],
        compiler_params=pltpu.CompilerParams(
            dimension_semantics=("arbitrary",), vmem_limit_bytes=VMEM_LIMIT),
        name="kv_proj",
    )(mem.reshape(rows, d), g_mem, w_xkv)
    return kv.reshape(b, m, n)


def _inproj_kernel(x_ref, g_ref, w_hbm, pos_ref, freq_ref, wp_ref, ps_ref,
                   q_ref, k_ref, v_ref, po_ref, halo_ref, w_ref, stage, sem, *, tm):
    @pl.when(_first_step(2))
    def _():
        _load_weights([(w_hbm, w_ref)], stage, sem)

    i = pl.program_id(1)
    h = _rms(x_ref[0], g_ref[...]).astype(BF16)

    groups = LANES // ROPE_DIM
    rows = tm // groups
    lane = lax.broadcasted_iota(jnp.int32, (rows, LANES), 1)
    lane_group = lax.shift_right_logical(lane, int(math.log2(ROPE_DIM)))
    pos = jnp.broadcast_to(pos_ref[0].astype(F32), (LANES, tm)).T
    packed = jnp.zeros((rows, LANES), F32)
    for g in range(groups):
        packed = jnp.where(lane_group == g, pos[g * rows:(g + 1) * rows], packed)
    ang = packed * freq_ref[0:1, :]
    cos8, sin8 = jnp.cos(ang), jnp.sin(ang)
    first_map = lane < DA_QK_DIM
    rotary = freq_ref[1:2, :] != 0.0

    def spread(packed_table, g):
        shift = (LANES - ROPE_DIM * g) % LANES
        lo = pltpu.roll(packed_table, shift, 1) if shift else packed_table
        hi_shift = (shift + DA_QK_DIM) % LANES
        hi = pltpu.roll(packed_table, hi_shift, 1) if hi_shift else packed_table
        return jnp.where(first_map, lo, hi)

    cos_t = jnp.concatenate([jnp.where(rotary, spread(cos8, g), 1.0) for g in range(groups)], axis=0)
    sin_t = jnp.concatenate([spread(sin8, g) * freq_ref[1:2, :] for g in range(groups)], axis=0)
    first_half = freq_ref[1:2, :] < 0.0

    def rope(t, c, s):
        partner = jnp.where(first_half, pltpu.roll(t, LANES - ROPE_HALF, 1), pltpu.roll(t, ROPE_HALF, 1))
        return t * c + partner * s

    qk_scale = DA_QK_DIM ** -0.5 * LOG2_E
    cos_q, sin_q = cos_t * qk_scale, sin_t * qk_scale
    pq = jnp.dot(h, w_ref[:, 0:QK_WIDTH], preferred_element_type=F32)
    for j in range(DA_HEADS):
        sl = slice(j * LANES, (j + 1) * LANES)
        q_ref[j] = rope(pq[:, sl], cos_q, sin_q).astype(BF16)
    pk = jnp.dot(h, w_ref[:, QK_WIDTH:2 * QK_WIDTH], preferred_element_type=F32)
    for j in range(DA_HEADS):
        sl = slice(j * LANES, (j + 1) * LANES)
        k_ref[j] = rope(pk[:, sl], cos_t, sin_t).astype(BF16)
    pv = jnp.dot(h, w_ref[:, 2 * QK_WIDTH:2 * QK_WIDTH + DA_WIDTH], preferred_element_type=F32)
    for j in range(DA_HEADS):
        v_ref[j] = pv[:, j * DA_V_DIM:(j + 1) * DA_V_DIM].astype(BF16)
    u = jnp.dot(h, w_ref[:, 2 * QK_WIDTH + DA_WIDTH:], preferred_element_type=F32)

    @pl.when(i == 0)
    def _():
        halo_ref[...] = jnp.zeros_like(halo_ref)

    prev = halo_ref[...]
    halo_ref[...] = u[tm - POOL_HALO:, :]
    tpos = i * tm + lax.broadcasted_iota(jnp.int32, (tm, 1), 0)
    for g, w in enumerate(POOL_WINDOWS):
        sl = slice(g * POOL_GROUP_DIM, (g + 1) * POOL_GROUP_DIM)
        ug = u[:, sl]
        s = jnp.concatenate([prev[:, sl], ug], axis=0)
        d = 1
        while d < w:
            s = s + pltpu.roll(s, d, 0)
            d *= 2
        inv_count = 1.0 / jnp.minimum(tpos + 1, w).astype(F32)
        pooled = s[POOL_HALO:, :] * inv_count - ug
        po = jnp.dot(pooled.astype(BF16), wp_ref[g], preferred_element_type=F32) * ps_ref[:, sl]
        po_ref[0, :, sl] = po.astype(BF16)


def _inproj_call(x, g, w_in, pos, freq, w_pool, pool_scale, tm):
    b, s, d = x.shape
    tok = lambda n: pl.BlockSpec((1, tm, n), lambda bi, i: (bi, i, 0))
    heads = pl.BlockSpec((None, DA_HEADS, tm, DA_V_DIM), lambda bi, i: (bi, 0, i, 0))
    head_major = jax.ShapeDtypeStruct((b, DA_HEADS, s, DA_V_DIM), BF16)
    return pl.pallas_call(
        functools.partial(_inproj_kernel, tm=tm),
        grid=(b, s // tm),
        in_specs=[tok(d), _const_spec((1, d)), _HBM,
                  pl.BlockSpec((1, 1, tm), lambda bi, i: (bi, 0, i)), _const_spec(freq.shape),
                  _const_spec(w_pool.shape), _const_spec((1, POOL_WIDTH))],
        out_specs=[heads, heads, heads, tok(POOL_WIDTH)],
        out_shape=[head_major, head_major, head_major,
                   jax.ShapeDtypeStruct((b, s, POOL_WIDTH), BF16)],
        scratch_shapes=[pltpu.VMEM((POOL_HALO, POOL_WIDTH), F32), pltpu.VMEM(w_in.shape, BF16),
                        *_stage_scratch(d, w_in.shape[1])],
        compiler_params=pltpu.CompilerParams(
            dimension_semantics=("arbitrary", "arbitrary"), vmem_limit_bytes=VMEM_LIMIT),
        name="inproj",
    )(x, g, w_in, pos, freq, w_pool, pool_scale)


def _interleave(*streams):
    tagged = [((k + 0.5) / len(ops), n, op) for n, ops in enumerate(streams) for k, op in enumerate(ops)]
    return [op for _, _, op in sorted(tagged, key=lambda e: e[:2])]


def _tile_order(nq):
    def length(order):
        total = 0.0
        for step in range(nq + 2):
            scores = order[step] + 1 if step < nq else 0
            probs = order[step - 1] + 1 if 0 <= step - 1 < nq else 0
            values = (order[step - 2] + 1) / 2 if 0 <= step - 2 < nq else 0
            total += max(scores + values, probs)
        return total

    return min(itertools.permutations(range(nq)), key=length)


def _attn_kernel(q_ref, k_ref, v_ref, lq1_ref, lk1_ref, lq2_ref, lk2_ref, gs_ref, o_ref, *scratch, t, nq):
    h = t // 2
    lane = lax.broadcasted_iota(jnp.int32, (h, DA_V_DIM), 1)
    causal = (lax.broadcasted_iota(jnp.int32, (h, h), 0) <= lax.broadcasted_iota(jnp.int32, (h, h), 1))
    causal = jnp.concatenate([causal, causal], axis=1)
    lam = (jnp.exp(jnp.sum(lq1_ref[...] * lk1_ref[...], axis=-1, keepdims=True))
           - jnp.exp(jnp.sum(lq2_ref[...] * lk2_ref[...], axis=-1, keepdims=True))
           + LAMBDA_INIT)
    gain = gs_ref[...] * (1.0 - LAMBDA_INIT)
    nt = (((1,), (1,)), ((), ()))
    tn = (((0,), (0,)), ((), ()))
    state = [dict() for _ in range(nq)]

    def chunks(i):
        n = 2 * (i + 1)
        return [(slice(c * h, (c + 1) * h), t if c == n - 1 else 0) for c in range(n)]

    def both(st, name, lo, value, combine):
        for half, part in ((0, value[:, :t]), (1, value[:, t:])) if lo == 0 else ((1, value),):
            key = (name, half)
            st[key] = part if key not in st else combine(st[key], part)

    def score_ops(i):
        s_ref, st = scratch[i], state[i]

        def start():
            stacked = []
            for half in range(2):
                q = q_ref[i * t + half * h:i * t + (half + 1) * h, :]
                zero = jnp.zeros_like(q)
                stacked += [jnp.where(lane < DA_QK_DIM, q, zero), jnp.where(lane >= DA_QK_DIM, q, zero)]
            st["qs"] = jnp.concatenate(stacked, axis=0)

        def one(c, rows, lo):
            s = lax.dot_general(k_ref[rows, :], st["qs"][lo:, :], nt, preferred_element_type=F32)
            first_diag = 2 * i
            if c == first_diag:
                s = jnp.concatenate([jnp.where(causal, s[:, :t], -jnp.inf), s[:, t:]], axis=1)
            elif c == first_diag + 1:
                s = jnp.where(causal, s, -jnp.inf)
            s_ref[rows, lo:] = s
            both(st, "m", lo, jnp.max(s, axis=0, keepdims=True), jnp.maximum)

        return [start] + [functools.partial(one, c, rows, lo) for c, (rows, lo) in enumerate(chunks(i))]

    def prob_ops(i):
        s_ref, st = scratch[i], state[i]

        def one(rows, lo):
            m = st[("m", 1)] if lo else jnp.concatenate([st[("m", 0)], st[("m", 1)]], axis=1)
            p = jnp.exp2(s_ref[rows, lo:] - m)
            s_ref[rows, lo:] = p
            both(st, "l", lo, jnp.sum(p, axis=0, keepdims=True), jnp.add)

        return [functools.partial(one, rows, lo) for rows, lo in chunks(i)]

    def value_ops(i):
        s_ref, st = scratch[i], state[i]

        def combine(rows, half):
            l = st[("l", half)]
            ratio = lam * l[:, :h] / l[:, h:]
            base = half * t
            return s_ref[rows, base:base + h] - ratio * s_ref[rows, base + h:base + t]

        def one(rows, lo):
            halves = (1,) if lo else (0, 1)
            a = jnp.concatenate([combine(rows, half) for half in halves], axis=1).astype(BF16)
            pv = lax.dot_general(v_ref[rows, :], a, tn, preferred_element_type=F32)
            for n, half in enumerate(halves):
                part = pv[:, n * h:(n + 1) * h]
                st["acc", half] = part if ("acc", half) not in st else st["acc", half] + part

        def finish():
            o = jnp.concatenate([st[("acc", half)] * (1.0 / st[("l", half)][:, :h]) for half in range(2)], axis=1)
            inv = lax.rsqrt(jnp.mean(o * o, axis=0, keepdims=True) + EPS)
            o_ref[i * t:(i + 1) * t, :] = (o * inv * gain).T.astype(BF16)

        return [functools.partial(one, rows, lo) for rows, lo in chunks(i)] + [finish]

    order = _tile_order(nq)
    for step in range(nq + 2):
        streams = []
        if step < nq:
            streams.append(score_ops(order[step]))
        if 0 <= step - 1 < nq:
            streams.append(prob_ops(order[step - 1]))
        if 0 <= step - 2 < nq:
            streams.append(value_ops(order[step - 2]))
        for op in _interleave(*streams):
            op()


def _attn_call(q, k, v, lq1, lk1, lq2, lk2, g_subln_col, t):
    b, _, s, _ = q.shape
    nq = s // t
    head = pl.BlockSpec((None, None, s, DA_V_DIM), lambda bi, h: (bi, h, 0, 0))
    vec = _const_spec((1, DA_QK_DIM))
    return pl.pallas_call(
        functools.partial(_attn_kernel, t=t, nq=nq),
        grid=(b, DA_HEADS),
        in_specs=[head, head, head, vec, vec, vec, vec, _const_spec((DA_V_DIM, 1))],
        out_specs=head,
        out_shape=jax.ShapeDtypeStruct((b, DA_HEADS, s, DA_V_DIM), BF16),
        scratch_shapes=[pltpu.VMEM(((i + 1) * t, 2 * t), F32) for i in range(nq)],
        compiler_params=pltpu.CompilerParams(
            dimension_semantics=("arbitrary", "arbitrary"), vmem_limit_bytes=VMEM_LIMIT),
        name="diff_attn",
    )(q, k, v, lq1, lk1, lq2, lk2, g_subln_col)


def _mixout_kernel(da_ref, po_ref, x_ref, wo_hbm, gpost_ref, gxpre_ref, wxq_hbm, kv_ref, wxo_hbm,
                   gxpost_ref, o_ref, wo_ref, wxq_ref, wxo_ref, stage, sem):
    @pl.when(_first_step(2))
    def _():
        _load_weights([(wo_hbm, wo_ref), (wxq_hbm, wxq_ref), (wxo_hbm, wxo_ref)], stage, sem)

    tm = x_ref.shape[1]
    groups = [slice(r, r + tm // 2) for r in (0, tm // 2)]
    x_scale = X_HEAD_DIM ** -0.5

    def phases(rows):
        st = {}

        def mix():
            da = jnp.concatenate([da_ref[h, rows, :] for h in range(DA_HEADS)], axis=-1)
            st["mix"] = (jnp.dot(da, wo_ref[0:DA_WIDTH, :], preferred_element_type=F32)
                         + jnp.dot(po_ref[0, rows, :], wo_ref[DA_WIDTH:, :], preferred_element_type=F32))

        def norms():
            st["x1"] = x_ref[0, rows, :] + _rms(st.pop("mix"), gpost_ref[...])
            st["hq"], st["inv"] = _prenorm_operand(st["x1"], gxpre_ref[...])

        def queries():
            st["xq"] = (jnp.dot(st.pop("hq"), wxq_ref[...], preferred_element_type=F32)
                        * (st.pop("inv") * x_scale)).astype(BF16)

        def attend():
            heads = []
            for h in range(X_HEADS):
                sl = slice(h * X_HEAD_DIM, (h + 1) * X_HEAD_DIM)
                kh = kv_ref[0, :, sl]
                vh = kv_ref[0, :, D_MODEL + h * X_HEAD_DIM:D_MODEL + (h + 1) * X_HEAD_DIM]
                sc = lax.dot_general(st["xq"][:, sl], kh, (((1,), (1,)), ((), ())), preferred_element_type=F32)
                p = jnp.exp(sc - jnp.max(sc, axis=-1, keepdims=True))
                pm = (p * (1.0 / jnp.sum(p, axis=-1, keepdims=True))).astype(BF16)
                heads.append(jnp.dot(pm, vh, preferred_element_type=F32).astype(BF16))
            st["xo"] = jnp.concatenate(heads, axis=-1)

        def project():
            st["y"] = jnp.dot(st.pop("xo"), wxo_ref[...], preferred_element_type=F32)

        def finish():
            o_ref[0, rows, :] = st["x1"] + _rms(st["y"], gxpost_ref[...])

        return [mix, norms, queries, attend, project, finish]

    first, second = (phases(rows) for rows in groups)
    program = [first[0]]
    for k in range(1, len(first)):
        program += [first[k], second[k - 1]]
    program.append(second[-1])
    for phase in program:
        phase()


def _mixout_call(da, po, x, w_out, g_mix_post, g_x_pre, w_xq, kv, w_xo, g_x_post, tm):
    b, s, d = x.shape
    tok = lambda n: pl.BlockSpec((1, tm, n), lambda bi, i: (bi, i, 0))
    gvec = _const_spec((1, d))
    return pl.pallas_call(
        _mixout_kernel,
        grid=(b, s // tm),
        in_specs=[pl.BlockSpec((None, DA_HEADS, tm, DA_V_DIM), lambda bi, i: (bi, 0, i, 0)),
                  tok(POOL_WIDTH), tok(d), _HBM, gvec, gvec, _HBM,
                  pl.BlockSpec((1,) + kv.shape[1:], lambda bi, i: (bi, 0, 0)),
                  _HBM, gvec],
        out_specs=tok(d),
        out_shape=jax.ShapeDtypeStruct((b, s, d), F32),
        scratch_shapes=[pltpu.VMEM(w_out.shape, BF16), pltpu.VMEM(w_xq.shape, BF16),
                        pltpu.VMEM(w_xo.shape, BF16), *_stage_scratch(d, d)],
        compiler_params=pltpu.CompilerParams(
            dimension_semantics=("arbitrary", "arbitrary"), vmem_limit_bytes=VMEM_LIMIT),
        name="mixout_xattn",
    )(da, po, x, w_out, g_mix_post, g_x_pre, w_xq, kv, w_xo, g_x_post)


def _ffn_kernel(x_ref, gpre_ref, wg_hbm, wu_hbm, wd_hbm, gpost_ref, o_ref,
                wg_ref, wu_ref, wd_ref, wide_stage, wide_sem, tall_stage, tall_sem):
    @pl.when(_first_step(2))
    def _():
        _load_weights([(wg_hbm, wg_ref), (wu_hbm, wu_ref)], wide_stage, wide_sem)
        _load_weights([(wd_hbm, wd_ref)], tall_stage, tall_sem)

    tm = x_ref.shape[1]

    def phases(rows):
        st = {}

        def prenorm():
            st["x"] = x_ref[0, rows, :]
            st["hf"], st["inv"] = _prenorm_operand(st["x"], gpre_ref[...])

        def chunk(lo, hi):
            gate = jnp.dot(st["hf"], wg_ref[:, lo:hi], preferred_element_type=F32) * st["inv"]
            up = jnp.dot(st["hf"], wu_ref[:, lo:hi], preferred_element_type=F32) * st["inv"]
            act = (gate * (1.0 / (1.0 + jnp.exp(-gate))) * up).astype(BF16)
            part = jnp.dot(act, wd_ref[lo:hi, :], preferred_element_type=F32)
            st["ff"] = part if "ff" not in st else st["ff"] + part

        def finish():
            o_ref[0, rows, :] = st["x"] + _rms(st["ff"], gpost_ref[...])

        return [prenorm] + [functools.partial(chunk, lo, hi) for lo, hi in FF_CHUNKS] + [finish]

    first, second = (phases(slice(r, r + tm // 2)) for r in (0, tm // 2))
    program = [first[0]]
    for k in range(1, len(first)):
        program += [first[k], second[k - 1]]
    program.append(second[-1])
    for phase in program:
        phase()


def _ffn_call(x, g_pre, w_gate, w_up, w_down, g_post, tm):
    b, s, d = x.shape
    tok = pl.BlockSpec((1, tm, d), lambda bi, i: (bi, i, 0))
    gvec = _const_spec((1, d))
    return pl.pallas_call(
        _ffn_kernel,
        grid=(b, s // tm),
        in_specs=[tok, gvec, _HBM, _HBM, _HBM, gvec],
        out_specs=tok,
        out_shape=jax.ShapeDtypeStruct((b, s, d), F32),
        scratch_shapes=[pltpu.VMEM(w_gate.shape, BF16), pltpu.VMEM(w_up.shape, BF16),
                        pltpu.VMEM(w_down.shape, BF16),
                        *_stage_scratch(d, w_gate.shape[1]),
                        *_stage_scratch(w_down.shape[0], d)],
        compiler_params=pltpu.CompilerParams(
            dimension_semantics=("arbitrary", "arbitrary"), vmem_limit_bytes=VMEM_LIMIT),
        name="swiglu",
    )(x, g_pre, w_gate, w_up, w_down, g_post)


def _rope_lane_constants():
    inv_freq = ROPE_THETA ** (-jnp.arange(0, ROPE_DIM, 2, dtype=F32) / ROPE_DIM)
    packed_freq = jnp.tile(inv_freq, LANES // ROPE_HALF)
    one_half = jnp.ones((ROPE_HALF,), F32)
    per_map = jnp.concatenate([-one_half, one_half, jnp.zeros((DA_QK_DIM - ROPE_DIM,), F32)])
    sign = jnp.concatenate([per_map, per_map])
    return jnp.concatenate([jnp.stack([packed_freq, sign]), jnp.zeros((6, LANES), F32)], axis=0)


def kernel(x, mem, positions, g_mix_pre, w_in, lambda_q1, lambda_k1, lambda_q2, lambda_k2,
           g_subln, w_pool, pool_scale, w_out, g_mix_post, g_x_pre, g_mem, w_xq, w_xkv, w_xo,
           g_x_post, g_ffn_pre, w_gate, w_up, w_down, g_ffn_post):
    b, s, d = x.shape
    assert d == D_MODEL and s % TOKEN_TILE == 0 and s % ATTN_TILE == 0
    row = lambda a: a.reshape(1, -1).astype(F32)
    bf = lambda a: a.astype(BF16)

    pos = positions.reshape(b, 1, s)
    kv = _kv_call(mem, row(g_mem), w_xkv, 2 * TOKEN_TILE)
    q, k, v, po = _inproj_call(x, row(g_mix_pre), w_in, pos, _rope_lane_constants(),
                               bf(w_pool), row(pool_scale), 2 * TOKEN_TILE)
    da = _attn_call(q, k, v, row(lambda_q1), row(lambda_k1), row(lambda_q2), row(lambda_k2),
                    g_subln.reshape(-1, 1).astype(F32), ATTN_TILE)
    x2 = _mixout_call(da, po, x, w_out, row(g_mix_post), row(g_x_pre), w_xq, kv, w_xo,
                      row(g_x_post), 2 * TOKEN_TILE)
    return _ffn_call(x2, row(g_ffn_pre), w_gate, w_up, w_down, row(g_ffn_post), TOKEN_TILE)
```

```python
import functools
import itertools
import math

import jax
import jax.numpy as jnp
from jax import lax
from jax.experimental import pallas as pl
from jax.experimental.pallas import tpu as pltpu

F32 = jnp.float32
BF16 = jnp.bfloat16

D_MODEL = 1024
MEM_LEN = 256
EPS = 1e-6
DA_HEADS = 4
DA_QK_DIM = 64
DA_V_DIM = 2 * DA_QK_DIM
DA_WIDTH = DA_HEADS * DA_V_DIM
QK_WIDTH = DA_HEADS * 2 * DA_QK_DIM
POOL_WINDOWS = (2, 4, 8, 16)
POOL_WIDTH = D_MODEL - DA_WIDTH
POOL_GROUP_DIM = POOL_WIDTH // len(POOL_WINDOWS)
ROPE_THETA = 500000.0
ROPE_DIM = DA_QK_DIM // 4
ROPE_HALF = ROPE_DIM // 2
X_HEADS = 4
X_HEAD_DIM = D_MODEL // X_HEADS
D_FF = -(-(8 * D_MODEL) // (3 * 256)) * 256
LAMBDA_INIT = 0.8 - 0.6 * math.exp(-0.3 * 0)
LOG2_E = math.log2(math.e)

LANES = 128
POOL_HALO = 16
VMEM_LIMIT = 56 * 1024 * 1024

TOKEN_TILE = 512
ATTN_TILE = 512
ATTN_HEADS_PER_STEP = 2
FF_CHUNKS = ((0, 1536), (1536, 2816))
STAGE_BYTES = 4 * 1024 * 1024


def _inv_rms(x):
    return lax.rsqrt(jnp.mean(x * x, axis=-1, keepdims=True) + EPS)


def _rms(x, g):
    return x * _inv_rms(x) * g


def _prenorm_operand(x, g):
    return (x * g).astype(BF16), _inv_rms(x)


def _const_spec(shape):
    zeros = (0,) * len(shape)
    return pl.BlockSpec(shape, lambda *_: zeros, pipeline_mode=pl.Buffered(1))


_HBM = pl.BlockSpec(memory_space=pl.ANY)


def _stage_scratch(rows, cols):
    chunks = -(-(rows * cols * 4) // STAGE_BYTES)
    while rows % chunks or (rows // chunks) % 16:
        chunks += 1
    return [pltpu.VMEM((2, rows // chunks, cols), F32), pltpu.SemaphoreType.DMA((2,))]


def _load_weights(weights, stage_ref, sem_ref):
    rows = stage_ref.shape[1]
    jobs = []
    for w_hbm, w_vmem in weights:
        assert w_hbm.shape[0] % rows == 0 and stage_ref.shape[2] == w_hbm.shape[1]
        jobs += [(w_hbm, w_vmem, r) for r in range(0, w_hbm.shape[0], rows)]

    def copy(k):
        w_hbm, _, r = jobs[k]
        return pltpu.make_async_copy(w_hbm.at[pl.ds(r, rows), :], stage_ref.at[k % 2], sem_ref.at[k % 2])

    copy(0).start()
    for k, (_, w_vmem, r) in enumerate(jobs):
        if k + 1 < len(jobs):
            copy(k + 1).start()
        copy(k).wait()
        w_vmem[r:r + rows, :] = stage_ref[k % 2].astype(BF16)


def _first_step(grid_rank):
    first = pl.program_id(0) == 0
    for axis in range(1, grid_rank):
        first = first & (pl.program_id(axis) == 0)
    return first


def _kv_kernel(mem_ref, g_ref, w_hbm, kv_ref, w_ref, stage, sem):
    @pl.when(_first_step(1))
    def _():
        _load_weights([(w_hbm, w_ref)], stage, sem)

    mn = _rms(mem_ref[...], g_ref[...]).astype(BF16)
    kv_ref[...] = jnp.dot(mn, w_ref[...], preferred_element_type=F32).astype(BF16)


def _kv_call(mem, g_mem, w_xkv, tm):
    b, m, d = mem.shape
    n = w_xkv.shape[1]
    rows = b * m
    tm = min(tm, rows)
    assert rows % tm == 0
    kv = pl.pallas_call(
        _kv_kernel,
        grid=(rows // tm,),
        in_specs=[pl.BlockSpec((tm, d), lambda i: (i, 0)), _const_spec((1, d)), _HBM],
        out_specs=pl.BlockSpec((tm, n), lambda i: (i, 0)),
        out_shape=jax.ShapeDtypeStruct((rows, n), BF16),
        scratch_shapes=[pltpu.VMEM((d, n), BF16), *_stage_scratch(d, n)],
        compiler_params=pltpu.CompilerParams(
            dimension_semantics=("arbitrary",), vmem_limit_bytes=VMEM_LIMIT),
        name="kv_proj",
    )(mem.reshape(rows, d), g_mem, w_xkv)
    return kv.reshape(b, m, n)


def _inproj_kernel(x_ref, g_ref, w_hbm, pos_ref, freq_ref, wp_ref, ps_ref,
                   q_ref, k_ref, v_ref, po_ref, halo_ref, w_ref, stage, sem, *, tm):
    @pl.when(_first_step(2))
    def _():
        _load_weights([(w_hbm, w_ref)], stage, sem)

    i = pl.program_id(1)
    h = _rms(x_ref[0], g_ref[...]).astype(BF16)

    groups = LANES // ROPE_DIM
    rows = tm // groups
    lane = lax.broadcasted_iota(jnp.int32, (rows, LANES), 1)
    lane_group = lax.shift_right_logical(lane, int(math.log2(ROPE_DIM)))
    pos = jnp.broadcast_to(pos_ref[0].astype(F32), (LANES, tm)).T
    packed = jnp.zeros((rows, LANES), F32)
    for g in range(groups):
        packed = jnp.where(lane_group == g, pos[g * rows:(g + 1) * rows], packed)
    ang = packed * freq_ref[0:1, :]
    cos8, sin8 = jnp.cos(ang), jnp.sin(ang)
    first_map = lane < DA_QK_DIM
    rotary = freq_ref[1:2, :] != 0.0

    def spread(packed_table, g):
        shift = (LANES - ROPE_DIM * g) % LANES
        lo = pltpu.roll(packed_table, shift, 1) if shift else packed_table
        hi_shift = (shift + DA_QK_DIM) % LANES
        hi = pltpu.roll(packed_table, hi_shift, 1) if hi_shift else packed_table
        return jnp.where(first_map, lo, hi)

    cos_t = jnp.concatenate([jnp.where(rotary, spread(cos8, g), 1.0) for g in range(groups)], axis=0)
    sin_t = jnp.concatenate([spread(sin8, g) * freq_ref[1:2, :] for g in range(groups)], axis=0)
    first_half = freq_ref[1:2, :] < 0.0

    def rope(t, c, s):
        partner = jnp.where(first_half, pltpu.roll(t, LANES - ROPE_HALF, 1), pltpu.roll(t, ROPE_HALF, 1))
        return t * c + partner * s

    qk_scale = DA_QK_DIM ** -0.5 * LOG2_E
    cos_q, sin_q = cos_t * qk_scale, sin_t * qk_scale
    pq = jnp.dot(h, w_ref[:, 0:QK_WIDTH], preferred_element_type=F32)
    for j in range(DA_HEADS):
        sl = slice(j * LANES, (j + 1) * LANES)
        q_ref[j] = rope(pq[:, sl], cos_q, sin_q).astype(BF16)
    pk = jnp.dot(h, w_ref[:, QK_WIDTH:2 * QK_WIDTH], preferred_element_type=F32)
    for j in range(DA_HEADS):
        sl = slice(j * LANES, (j + 1) * LANES)
        k_ref[j] = rope(pk[:, sl], cos_t, sin_t).astype(BF16)
    pv = jnp.dot(h, w_ref[:, 2 * QK_WIDTH:2 * QK_WIDTH + DA_WIDTH], preferred_element_type=F32)
    for j in range(DA_HEADS):
        v_ref[j] = pv[:, j * DA_V_DIM:(j + 1) * DA_V_DIM].astype(BF16)
    u = jnp.dot(h, w_ref[:, 2 * QK_WIDTH + DA_WIDTH:], preferred_element_type=F32)

    @pl.when(i == 0)
    def _():
        halo_ref[...] = jnp.zeros_like(halo_ref)

    prev = halo_ref[...]
    halo_ref[...] = u[tm - POOL_HALO:, :]
    tpos = i * tm + lax.broadcasted_iota(jnp.int32, (tm, 1), 0)
    for g, w in enumerate(POOL_WINDOWS):
        sl = slice(g * POOL_GROUP_DIM, (g + 1) * POOL_GROUP_DIM)
        ug = u[:, sl]
        s = jnp.concatenate([prev[:, sl], ug], axis=0)
        d = 1
        while d < w:
            s = s + pltpu.roll(s, d, 0)
            d *= 2
        inv_count = 1.0 / jnp.minimum(tpos + 1, w).astype(F32)
        pooled = s[POOL_HALO:, :] * inv_count - ug
        po = jnp.dot(pooled.astype(BF16), wp_ref[g], preferred_element_type=F32) * ps_ref[:, sl]
        po_ref[0, :, sl] = po.astype(BF16)


def _inproj_call(x, g, w_in, pos, freq, w_pool, pool_scale, tm):
    b, s, d = x.shape
    tok = lambda n: pl.BlockSpec((1, tm, n), lambda bi, i: (bi, i, 0))
    heads = pl.BlockSpec((None, DA_HEADS, tm, DA_V_DIM), lambda bi, i: (bi, 0, i, 0))
    head_major = jax.ShapeDtypeStruct((b, DA_HEADS, s, DA_V_DIM), BF16)
    return pl.pallas_call(
        functools.partial(_inproj_kernel, tm=tm),
        grid=(b, s // tm),
        in_specs=[tok(d), _const_spec((1, d)), _HBM,
                  pl.BlockSpec((1, 1, tm), lambda bi, i: (bi, 0, i)), _const_spec(freq.shape),
                  _const_spec(w_pool.shape), _const_spec((1, POOL_WIDTH))],
        out_specs=[heads, heads, heads, tok(POOL_WIDTH)],
        out_shape=[head_major, head_major, head_major,
                   jax.ShapeDtypeStruct((b, s, POOL_WIDTH), BF16)],
        scratch_shapes=[pltpu.VMEM((POOL_HALO, POOL_WIDTH), F32), pltpu.VMEM(w_in.shape, BF16),
                        *_stage_scratch(d, w_in.shape[1])],
        compiler_params=pltpu.CompilerParams(
            dimension_semantics=("arbitrary", "arbitrary"), vmem_limit_bytes=VMEM_LIMIT),
        name="inproj",
    )(x, g, w_in, pos, freq, w_pool, pool_scale)


def _interleave(*streams):
    tagged = [((k + 0.5) / len(ops), n, op) for n, ops in enumerate(streams) for k, op in enumerate(ops)]
    return [op for _, _, op in sorted(tagged, key=lambda e: e[:2])]


def _tile_order(nq):
    def length(order):
        total = 0.0
        for step in range(nq + 2):
            scores = order[step] + 1 if step < nq else 0
            probs = order[step - 1] + 1 if 0 <= step - 1 < nq else 0
            values = (order[step - 2] + 1) / 2 if 0 <= step - 2 < nq else 0
            total += max(scores + values, probs)
        return total

    return min(itertools.permutations(range(nq)), key=length)


def _attn_kernel(q_ref, k_ref, v_ref, lq1_ref, lk1_ref, lq2_ref, lk2_ref, gs_ref, o_ref, *scratch, t, nq, heads):
    h = t // 2
    lane = lax.broadcasted_iota(jnp.int32, (h, DA_V_DIM), 1)
    causal = (lax.broadcasted_iota(jnp.int32, (h, h), 0) <= lax.broadcasted_iota(jnp.int32, (h, h), 1))
    causal = jnp.concatenate([causal, causal], axis=1)
    lam = (jnp.exp(jnp.sum(lq1_ref[...] * lk1_ref[...], axis=-1, keepdims=True))
           - jnp.exp(jnp.sum(lq2_ref[...] * lk2_ref[...], axis=-1, keepdims=True))
           + LAMBDA_INIT)
    gain = gs_ref[...] * (1.0 - LAMBDA_INIT)
    nt = (((1,), (1,)), ((), ()))
    tn = (((0,), (0,)), ((), ()))
    state = {}

    def chunks(i):
        n = 2 * (i + 1)
        return [(slice(c * h, (c + 1) * h), t if c == n - 1 else 0) for c in range(n)]

    def both(st, name, lo, value, combine):
        for half, part in ((0, value[:, :t]), (1, value[:, t:])) if lo == 0 else ((1, value),):
            key = (name, half)
            st[key] = part if key not in st else combine(st[key], part)

    def score_ops(hd, i):
        s_ref, st = scratch[i], state.setdefault((hd, i), {})

        def start():
            stacked = []
            for half in range(2):
                q = q_ref[hd, i * t + half * h:i * t + (half + 1) * h, :]
                zero = jnp.zeros_like(q)
                stacked += [jnp.where(lane < DA_QK_DIM, q, zero), jnp.where(lane >= DA_QK_DIM, q, zero)]
            st["qs"] = jnp.concatenate(stacked, axis=0)

        def one(c, rows, lo):
            s = lax.dot_general(k_ref[hd, rows, :], st["qs"][lo:, :], nt, preferred_element_type=F32)
            first_diag = 2 * i
            if c == first_diag:
                s = jnp.concatenate([jnp.where(causal, s[:, :t], -jnp.inf), s[:, t:]], axis=1)
            elif c == first_diag + 1:
                s = jnp.where(causal, s, -jnp.inf)
            s_ref[rows, lo:] = s
            both(st, "m", lo, jnp.max(s, axis=0, keepdims=True), jnp.maximum)

        return [start] + [functools.partial(one, c, rows, lo) for c, (rows, lo) in enumerate(chunks(i))]

    def prob_ops(hd, i):
        s_ref, st = scratch[i], state[hd, i]

        def one(rows, lo):
            m = st[("m", 1)] if lo else jnp.concatenate([st[("m", 0)], st[("m", 1)]], axis=1)
            p = jnp.exp2(s_ref[rows, lo:] - m)
            s_ref[rows, lo:] = p
            both(st, "l", lo, jnp.sum(p, axis=0, keepdims=True), jnp.add)

        return [functools.partial(one, rows, lo) for rows, lo in chunks(i)]

    def value_ops(hd, i):
        s_ref, st = scratch[i], state[hd, i]

        def combine(rows, half):
            l = st[("l", half)]
            ratio = lam * l[:, :h] / l[:, h:]
            base = half * t
            return s_ref[rows, base:base + h] - ratio * s_ref[rows, base + h:base + t]

        def one(rows, lo):
            halves = (1,) if lo else (0, 1)
            a = jnp.concatenate([combine(rows, half) for half in halves], axis=1).astype(BF16)
            pv = lax.dot_general(v_ref[hd, rows, :], a, tn, preferred_element_type=F32)
            for n, half in enumerate(halves):
                part = pv[:, n * h:(n + 1) * h]
                st["acc", half] = part if ("acc", half) not in st else st["acc", half] + part

        def finish():
            o = jnp.concatenate([st[("acc", half)] * (1.0 / st[("l", half)][:, :h]) for half in range(2)], axis=1)
            inv = lax.rsqrt(jnp.mean(o * o, axis=0, keepdims=True) + EPS)
            o_ref[hd, i * t:(i + 1) * t, :] = (o * inv * gain).T.astype(BF16)

        return [functools.partial(one, rows, lo) for rows, lo in chunks(i)] + [finish]

    order = [(hd, i) for hd in range(heads) for i in _tile_order(nq)]
    for step in range(len(order) + 2):
        streams = []
        if step < len(order):
            streams.append(score_ops(*order[step]))
        if 0 <= step - 1 < len(order):
            streams.append(prob_ops(*order[step - 1]))
        if 0 <= step - 2 < len(order):
            streams.append(value_ops(*order[step - 2]))
        for op in _interleave(*streams):
            op()


def _attn_call(q, k, v, lq1, lk1, lq2, lk2, g_subln_col, t):
    b, _, s, _ = q.shape
    nq = s // t
    heads = ATTN_HEADS_PER_STEP
    head = pl.BlockSpec((None, heads, s, DA_V_DIM), lambda bi, h: (bi, h, 0, 0))
    vec = _const_spec((1, DA_QK_DIM))
    return pl.pallas_call(
        functools.partial(_attn_kernel, t=t, nq=nq, heads=heads),
        grid=(b, DA_HEADS // heads),
        in_specs=[head, head, head, vec, vec, vec, vec, _const_spec((DA_V_DIM, 1))],
        out_specs=head,
        out_shape=jax.ShapeDtypeStruct((b, DA_HEADS, s, DA_V_DIM), BF16),
        scratch_shapes=[pltpu.VMEM(((i + 1) * t, 2 * t), F32) for i in range(nq)],
        compiler_params=pltpu.CompilerParams(
            dimension_semantics=("arbitrary", "arbitrary"), vmem_limit_bytes=VMEM_LIMIT),
        name="diff_attn",
    )(q, k, v, lq1, lk1, lq2, lk2, g_subln_col)


def _mixout_kernel(da_ref, po_ref, x_ref, wo_hbm, gpost_ref, gxpre_ref, wxq_hbm, kv_ref, wxo_hbm,
                   gxpost_ref, o_ref, wo_ref, wxq_ref, wxo_ref, stage, sem):
    @pl.when(_first_step(2))
    def _():
        _load_weights([(wo_hbm, wo_ref), (wxq_hbm, wxq_ref), (wxo_hbm, wxo_ref)], stage, sem)

    tm = x_ref.shape[1]
    groups = [slice(r, r + tm // 2) for r in (0, tm // 2)]
    x_scale = X_HEAD_DIM ** -0.5

    def phases(rows):
        st = {}

        def mix():
            da = jnp.concatenate([da_ref[h, rows, :] for h in range(DA_HEADS)], axis=-1)
            st["mix"] = (jnp.dot(da, wo_ref[0:DA_WIDTH, :], preferred_element_type=F32)
                         + jnp.dot(po_ref[0, rows, :], wo_ref[DA_WIDTH:, :], preferred_element_type=F32))

        def norms():
            st["x1"] = x_ref[0, rows, :] + _rms(st.pop("mix"), gpost_ref[...])
            st["hq"], st["inv"] = _prenorm_operand(st["x1"], gxpre_ref[...])

        def queries():
            st["xq"] = (jnp.dot(st.pop("hq"), wxq_ref[...], preferred_element_type=F32)
                        * (st.pop("inv") * x_scale)).astype(BF16)

        def attend():
            heads = []
            for h in range(X_HEADS):
                sl = slice(h * X_HEAD_DIM, (h + 1) * X_HEAD_DIM)
                kh = kv_ref[0, :, sl]
                vh = kv_ref[0, :, D_MODEL + h * X_HEAD_DIM:D_MODEL + (h + 1) * X_HEAD_DIM]
                sc = lax.dot_general(st["xq"][:, sl], kh, (((1,), (1,)), ((), ())), preferred_element_type=F32)
                p = jnp.exp(sc - jnp.max(sc, axis=-1, keepdims=True))
                pm = (p * (1.0 / jnp.sum(p, axis=-1, keepdims=True))).astype(BF16)
                heads.append(jnp.dot(pm, vh, preferred_element_type=F32).astype(BF16))
            st["xo"] = jnp.concatenate(heads, axis=-1)

        def project():
            st["y"] = jnp.dot(st.pop("xo"), wxo_ref[...], preferred_element_type=F32)

        def finish():
            o_ref[0, rows, :] = st["x1"] + _rms(st["y"], gxpost_ref[...])

        return [mix, norms, queries, attend, project, finish]

    first, second = (phases(rows) for rows in groups)
    program = [first[0]]
    for k in range(1, len(first)):
        program += [first[k], second[k - 1]]
    program.append(second[-1])
    for phase in program:
        phase()


def _mixout_call(da, po, x, w_out, g_mix_post, g_x_pre, w_xq, kv, w_xo, g_x_post, tm):
    b, s, d = x.shape
    tok = lambda n: pl.BlockSpec((1, tm, n), lambda bi, i: (bi, i, 0))
    gvec = _const_spec((1, d))
    return pl.pallas_call(
        _mixout_kernel,
        grid=(b, s // tm),
        in_specs=[pl.BlockSpec((None, DA_HEADS, tm, DA_V_DIM), lambda bi, i: (bi, 0, i, 0)),
                  tok(POOL_WIDTH), tok(d), _HBM, gvec, gvec, _HBM,
                  pl.BlockSpec((1,) + kv.shape[1:], lambda bi, i: (bi, 0, 0)),
                  _HBM, gvec],
        out_specs=tok(d),
        out_shape=jax.ShapeDtypeStruct((b, s, d), F32),
        scratch_shapes=[pltpu.VMEM(w_out.shape, BF16), pltpu.VMEM(w_xq.shape, BF16),
                        pltpu.VMEM(w_xo.shape, BF16), *_stage_scratch(d, d)],
        compiler_params=pltpu.CompilerParams(
            dimension_semantics=("arbitrary", "arbitrary"), vmem_limit_bytes=VMEM_LIMIT),
        name="mixout_xattn",
    )(da, po, x, w_out, g_mix_post, g_x_pre, w_xq, kv, w_xo, g_x_post)


def _ffn_kernel(x_ref, gpre_ref, wg_hbm, wu_hbm, wd_hbm, gpost_ref, o_ref,
                wg_ref, wu_ref, wd_ref, wide_stage, wide_sem, tall_stage, tall_sem):
    @pl.when(_first_step(2))
    def _():
        _load_weights([(wg_hbm, wg_ref), (wu_hbm, wu_ref)], wide_stage, wide_sem)
        _load_weights([(wd_hbm, wd_ref)], tall_stage, tall_sem)

    tm = x_ref.shape[1]

    def phases(rows):
        st = {}

        def prenorm():
            st["x"] = x_ref[0, rows, :]
            st["hf"], st["inv"] = _prenorm_operand(st["x"], gpre_ref[...])

        def chunk(lo, hi):
            gate = jnp.dot(st["hf"], wg_ref[:, lo:hi], preferred_element_type=F32) * st["inv"]
            up = jnp.dot(st["hf"], wu_ref[:, lo:hi], preferred_element_type=F32) * st["inv"]
            act = (gate * (1.0 / (1.0 + jnp.exp(-gate))) * up).astype(BF16)
            part = jnp.dot(act, wd_ref[lo:hi, :], preferred_element_type=F32)
            st["ff"] = part if "ff" not in st else st["ff"] + part

        def finish():
            o_ref[0, rows, :] = st["x"] + _rms(st["ff"], gpost_ref[...])

        return [prenorm] + [functools.partial(chunk, lo, hi) for lo, hi in FF_CHUNKS] + [finish]

    first, second = (phases(slice(r, r + tm // 2)) for r in (0, tm // 2))
    program = [first[0]]
    for k in range(1, len(first)):
        program += [first[k], second[k - 1]]
    program.append(second[-1])
    for phase in program:
        phase()


def _ffn_call(x, g_pre, w_gate, w_up, w_down, g_post, tm):
    b, s, d = x.shape
    tok = pl.BlockSpec((1, tm, d), lambda bi, i: (bi, i, 0))
    gvec = _const_spec((1, d))
    return pl.pallas_call(
        _ffn_kernel,
        grid=(b, s // tm),
        in_specs=[tok, gvec, _HBM, _HBM, _HBM, gvec],
        out_specs=tok,
        out_shape=jax.ShapeDtypeStruct((b, s, d), F32),
        scratch_shapes=[pltpu.VMEM(w_gate.shape, BF16), pltpu.VMEM(w_up.shape, BF16),
                        pltpu.VMEM(w_down.shape, BF16),
                        *_stage_scratch(d, w_gate.shape[1]),
                        *_stage_scratch(w_down.shape[0], d)],
        compiler_params=pltpu.CompilerParams(
            dimension_semantics=("arbitrary", "arbitrary"), vmem_limit_bytes=VMEM_LIMIT),
        name="swiglu",
    )(x, g_pre, w_gate, w_up, w_down, g_post)


def _rope_lane_constants():
    inv_freq = ROPE_THETA ** (-jnp.arange(0, ROPE_DIM, 2, dtype=F32) / ROPE_DIM)
    packed_freq = jnp.tile(inv_freq, LANES // ROPE_HALF)
    one_half = jnp.ones((ROPE_HALF,), F32)
    per_map = jnp.concatenate([-one_half, one_half, jnp.zeros((DA_QK_DIM - ROPE_DIM,), F32)])
    sign = jnp.concatenate([per_map, per_map])
    return jnp.concatenate([jnp.stack([packed_freq, sign]), jnp.zeros((6, LANES), F32)], axis=0)


def kernel(x, mem, positions, g_mix_pre, w_in, lambda_q1, lambda_k1, lambda_q2, lambda_k2,
           g_subln, w_pool, pool_scale, w_out, g_mix_post, g_x_pre, g_mem, w_xq, w_xkv, w_xo,
           g_x_post, g_ffn_pre, w_gate, w_up, w_down, g_ffn_post):
    b, s, d = x.shape
    assert d == D_MODEL and s % TOKEN_TILE == 0 and s % ATTN_TILE == 0
    row = lambda a: a.reshape(1, -1).astype(F32)
    bf = lambda a: a.astype(BF16)

    pos = positions.reshape(b, 1, s)
    kv = _kv_call(mem, row(g_mem), w_xkv, 2 * TOKEN_TILE)
    q, k, v, po = _inproj_call(x, row(g_mix_pre), w_in, pos, _rope_lane_constants(),
                               bf(w_pool), row(pool_scale), 2 * TOKEN_TILE)
    da = _attn_call(q, k, v, row(lambda_q1), row(lambda_k1), row(lambda_q2), row(lambda_k2),
                    g_subln.reshape(-1, 1).astype(F32), ATTN_TILE)
    x2 = _mixout_call(da, po, x, w_out, row(g_mix_post), row(g_x_pre), w_xq, kv, w_xo,
                      row(g_x_post), 2 * TOKEN_TILE)
    return _ffn_call(x2, row(g_ffn_pre), w_gate, w_up, w_down, row(g_ffn_post), TOKEN_TILE)
```

```python
import functools
import itertools
import math

import jax
import jax.numpy as jnp
from jax import lax
from jax.experimental import pallas as pl
from jax.experimental.pallas import tpu as pltpu

F32 = jnp.float32
BF16 = jnp.bfloat16

D_MODEL = 1024
MEM_LEN = 256
EPS = 1e-6
DA_HEADS = 4
DA_QK_DIM = 64
DA_V_DIM = 2 * DA_QK_DIM
DA_WIDTH = DA_HEADS * DA_V_DIM
QK_WIDTH = DA_HEADS * 2 * DA_QK_DIM
POOL_WINDOWS = (2, 4, 8, 16)
POOL_WIDTH = D_MODEL - DA_WIDTH
POOL_GROUP_DIM = POOL_WIDTH // len(POOL_WINDOWS)
ROPE_THETA = 500000.0
ROPE_DIM = DA_QK_DIM // 4
ROPE_HALF = ROPE_DIM // 2
X_HEADS = 4
X_HEAD_DIM = D_MODEL // X_HEADS
D_FF = -(-(8 * D_MODEL) // (3 * 256)) * 256
LAMBDA_INIT = 0.8 - 0.6 * math.exp(-0.3 * 0)
LOG2_E = math.log2(math.e)

LANES = 128
POOL_HALO = 16
VMEM_LIMIT = 56 * 1024 * 1024

TOKEN_TILE = 512
ATTN_TILE = 512
ATTN_HEADS_PER_STEP = 2
FF_CHUNKS = ((0, 1536), (1536, 2816))
STAGE_BYTES = 4 * 1024 * 1024


def _inv_rms(x):
    return lax.rsqrt(jnp.mean(x * x, axis=-1, keepdims=True) + EPS)


def _rms(x, g):
    return x * _inv_rms(x) * g


def _prenorm_operand(x, g):
    return (x * g).astype(BF16), _inv_rms(x)


def _const_spec(shape):
    zeros = (0,) * len(shape)
    return pl.BlockSpec(shape, lambda *_: zeros, pipeline_mode=pl.Buffered(1))


_HBM = pl.BlockSpec(memory_space=pl.ANY)


def _stage_scratch(rows, cols):
    chunks = -(-(rows * cols * 4) // STAGE_BYTES)
    while rows % chunks or (rows // chunks) % 16:
        chunks += 1
    return [pltpu.VMEM((2, rows // chunks, cols), F32), pltpu.SemaphoreType.DMA((2,))]


def _load_weights(weights, stage_ref, sem_ref):
    rows = stage_ref.shape[1]
    jobs = []
    for w_hbm, w_vmem in weights:
        assert w_hbm.shape[0] % rows == 0 and stage_ref.shape[2] == w_hbm.shape[1]
        jobs += [(w_hbm, w_vmem, r) for r in range(0, w_hbm.shape[0], rows)]

    def copy(k):
        w_hbm, _, r = jobs[k]
        return pltpu.make_async_copy(w_hbm.at[pl.ds(r, rows), :], stage_ref.at[k % 2], sem_ref.at[k % 2])

    copy(0).start()
    for k, (_, w_vmem, r) in enumerate(jobs):
        if k + 1 < len(jobs):
            copy(k + 1).start()
        copy(k).wait()
        w_vmem[r:r + rows, :] = stage_ref[k % 2].astype(BF16)


def _first_step(grid_rank):
    first = pl.program_id(0) == 0
    for axis in range(1, grid_rank):
        first = first & (pl.program_id(axis) == 0)
    return first


def _kv_kernel(mem_ref, g_ref, w_hbm, kv_ref, w_ref, stage, sem):
    @pl.when(_first_step(1))
    def _():
        _load_weights([(w_hbm, w_ref)], stage, sem)

    mn = _rms(mem_ref[...], g_ref[...]).astype(BF16)
    kv_ref[...] = jnp.dot(mn, w_ref[...], preferred_element_type=F32).astype(BF16)


def _kv_call(mem, g_mem, w_xkv, tm):
    b, m, d = mem.shape
    n = w_xkv.shape[1]
    rows = b * m
    tm = min(tm, rows)
    assert rows % tm == 0
    kv = pl.pallas_call(
        _kv_kernel,
        grid=(rows // tm,),
        in_specs=[pl.BlockSpec((tm, d), lambda i: (i, 0)), _const_spec((1, d)), _HBM],
        out_specs=pl.BlockSpec((tm, n), lambda i: (i, 0)),
        out_shape=jax.ShapeDtypeStruct((rows, n), BF16),
        scratch_shapes=[pltpu.VMEM((d, n), BF16), *_stage_scratch(d, n)],
        compiler_params=pltpu.CompilerParams(
            dimension_semantics=("arbitrary",), vmem_limit_bytes=VMEM_LIMIT),
        name="kv_proj",
    )(mem.reshape(rows, d), g_mem, w_xkv)
    return kv.reshape(b, m, n)


def _inproj_kernel(x_ref, g_ref, w_hbm, pos_ref, freq_ref, wp_ref, ps_ref,
                   q_ref, k_ref, v_ref, po_ref, halo_ref, w_ref, stage, sem, *, tm):
    @pl.when(_first_step(2))
    def _():
        _load_weights([(w_hbm, w_ref)], stage, sem)

    i = pl.program_id(1)

    @pl.when(i == 0)
    def _():
        halo_ref[...] = jnp.zeros_like(halo_ref)

    tg = tm // 2
    blocks = LANES // ROPE_DIM
    rows = tg // blocks
    lane = lax.broadcasted_iota(jnp.int32, (rows, LANES), 1)
    lane_block = lax.shift_right_logical(lane, int(math.log2(ROPE_DIM)))
    first_map = lane < DA_QK_DIM
    sign = freq_ref[1:2, :]
    qk_scale = DA_QK_DIM ** -0.5 * LOG2_E

    def spread(packed_table, blk):
        shift = (LANES - ROPE_DIM * blk) % LANES
        lo = pltpu.roll(packed_table, shift, 1) if shift else packed_table
        hi_shift = (shift + DA_QK_DIM) % LANES
        hi = pltpu.roll(packed_table, hi_shift, 1) if hi_shift else packed_table
        return jnp.where(first_map, lo, hi)

    def rope(t, c, s):
        partner = jnp.where(sign < 0.0, pltpu.roll(t, LANES - ROPE_HALF, 1), pltpu.roll(t, ROPE_HALF, 1))
        return t * c + partner * s

    tails = {}

    def phases(gi):
        r0 = gi * tg
        tok = slice(r0, r0 + tg)
        st = {}

        def prepare():
            st["h"] = _rms(x_ref[0, tok, :], g_ref[...]).astype(BF16)
            pos = jnp.broadcast_to(pos_ref[0][:, tok].astype(F32), (LANES, tg)).T
            packed = jnp.zeros((rows, LANES), F32)
            for blk in range(blocks):
                packed = jnp.where(lane_block == blk, pos[blk * rows:(blk + 1) * rows], packed)
            ang = packed * freq_ref[0:1, :]
            cos8, sin8 = jnp.cos(ang), jnp.sin(ang)
            st["cos"] = jnp.concatenate([jnp.where(sign != 0.0, spread(cos8, blk), 1.0) for blk in range(blocks)],
                                        axis=0)
            st["sin"] = jnp.concatenate([spread(sin8, blk) * sign for blk in range(blocks)], axis=0)

        def queries():
            cos_q, sin_q = st["cos"] * qk_scale, st["sin"] * qk_scale
            pq = jnp.dot(st["h"], w_ref[:, 0:QK_WIDTH], preferred_element_type=F32)
            for j in range(DA_HEADS):
                q_ref[j, tok, :] = rope(pq[:, j * LANES:(j + 1) * LANES], cos_q, sin_q).astype(BF16)

        def keys():
            pk = jnp.dot(st["h"], w_ref[:, QK_WIDTH:2 * QK_WIDTH], preferred_element_type=F32)
            for j in range(DA_HEADS):
                k_ref[j, tok, :] = rope(pk[:, j * LANES:(j + 1) * LANES], st["cos"], st["sin"]).astype(BF16)

        def values():
            pv = jnp.dot(st["h"], w_ref[:, 2 * QK_WIDTH:2 * QK_WIDTH + DA_WIDTH], preferred_element_type=F32)
            for j in range(DA_HEADS):
                v_ref[j, tok, :] = pv[:, j * DA_V_DIM:(j + 1) * DA_V_DIM].astype(BF16)

        def pool():
            u = jnp.dot(st["h"], w_ref[:, 2 * QK_WIDTH + DA_WIDTH:], preferred_element_type=F32)
            prev = tails[gi - 1] if gi else halo_ref[...]
            tails[gi] = u[tg - POOL_HALO:, :]
            if r0 + tg == tm:
                halo_ref[...] = tails[gi]
            tpos = i * tm + r0 + lax.broadcasted_iota(jnp.int32, (tg, 1), 0)
            for g, w in enumerate(POOL_WINDOWS):
                sl = slice(g * POOL_GROUP_DIM, (g + 1) * POOL_GROUP_DIM)
                ug = u[:, sl]
                s = jnp.concatenate([prev[:, sl], ug], axis=0)
                d = 1
                while d < w:
                    s = s + pltpu.roll(s, d, 0)
                    d *= 2
                inv_count = 1.0 / jnp.minimum(tpos + 1, w).astype(F32)
                pooled = s[POOL_HALO:, :] * inv_count - ug
                po = jnp.dot(pooled.astype(BF16), wp_ref[g], preferred_element_type=F32) * ps_ref[:, sl]
                po_ref[0, tok, sl] = po.astype(BF16)

        return [prepare, queries, keys, values, pool]

    first, second = phases(0), phases(1)
    program = [first[0]]
    for k in range(1, len(first)):
        program += [first[k], second[k - 1]]
    program.append(second[-1])
    for phase in program:
        phase()


def _inproj_call(x, g, w_in, pos, freq, w_pool, pool_scale, tm):
    b, s, d = x.shape
    tok = lambda n: pl.BlockSpec((1, tm, n), lambda bi, i: (bi, i, 0))
    heads = pl.BlockSpec((None, DA_HEADS, tm, DA_V_DIM), lambda bi, i: (bi, 0, i, 0))
    head_major = jax.ShapeDtypeStruct((b, DA_HEADS, s, DA_V_DIM), BF16)
    return pl.pallas_call(
        functools.partial(_inproj_kernel, tm=tm),
        grid=(b, s // tm),
        in_specs=[tok(d), _const_spec((1, d)), _HBM,
                  pl.BlockSpec((1, 1, tm), lambda bi, i: (bi, 0, i)), _const_spec(freq.shape),
                  _const_spec(w_pool.shape), _const_spec((1, POOL_WIDTH))],
        out_specs=[heads, heads, heads, tok(POOL_WIDTH)],
        out_shape=[head_major, head_major, head_major,
                   jax.ShapeDtypeStruct((b, s, POOL_WIDTH), BF16)],
        scratch_shapes=[pltpu.VMEM((POOL_HALO, POOL_WIDTH), F32), pltpu.VMEM(w_in.shape, BF16),
                        *_stage_scratch(d, w_in.shape[1])],
        compiler_params=pltpu.CompilerParams(
            dimension_semantics=("arbitrary", "arbitrary"), vmem_limit_bytes=VMEM_LIMIT),
        name="inproj",
    )(x, g, w_in, pos, freq, w_pool, pool_scale)


def _interleave(*streams):
    tagged = [((k + 0.5) / len(ops), n, op) for n, ops in enumerate(streams) for k, op in enumerate(ops)]
    return [op for _, _, op in sorted(tagged, key=lambda e: e[:2])]


def _tile_order(nq):
    def length(order):
        total = 0.0
        for step in range(nq + 2):
            scores = order[step] + 1 if step < nq else 0
            probs = order[step - 1] + 1 if 0 <= step - 1 < nq else 0
            values = (order[step - 2] + 1) / 2 if 0 <= step - 2 < nq else 0
            total += max(scores + values, probs)
        return total

    return min(itertools.permutations(range(nq)), key=length)


def _attn_kernel(q_ref, k_ref, v_ref, lq1_ref, lk1_ref, lq2_ref, lk2_ref, gs_ref, o_ref, *scratch, t, nq, heads):
    h = t // 2
    lane = lax.broadcasted_iota(jnp.int32, (h, DA_V_DIM), 1)
    causal = (lax.broadcasted_iota(jnp.int32, (h, h), 0) <= lax.broadcasted_iota(jnp.int32, (h, h), 1))
    causal = jnp.concatenate([causal, causal], axis=1)
    lam = (jnp.exp(jnp.sum(lq1_ref[...] * lk1_ref[...], axis=-1, keepdims=True))
           - jnp.exp(jnp.sum(lq2_ref[...] * lk2_ref[...], axis=-1, keepdims=True))
           + LAMBDA_INIT)
    gain = gs_ref[...] * (1.0 - LAMBDA_INIT)
    nt = (((1,), (1,)), ((), ()))
    tn = (((0,), (0,)), ((), ()))
    state = {}

    def chunks(i):
        n = 2 * (i + 1)
        return [(slice(c * h, (c + 1) * h), t if c == n - 1 else 0) for c in range(n)]

    def both(st, name, lo, value, combine):
        for half, part in ((0, value[:, :t]), (1, value[:, t:])) if lo == 0 else ((1, value),):
            key = (name, half)
            st[key] = part if key not in st else combine(st[key], part)

    def score_ops(hd, i):
        s_ref, st = scratch[i], state.setdefault((hd, i), {})

        def start():
            stacked = []
            for half in range(2):
                q = q_ref[hd, i * t + half * h:i * t + (half + 1) * h, :]
                zero = jnp.zeros_like(q)
                stacked += [jnp.where(lane < DA_QK_DIM, q, zero), jnp.where(lane >= DA_QK_DIM, q, zero)]
            st["qs"] = jnp.concatenate(stacked, axis=0)

        def one(c, rows, lo):
            s = lax.dot_general(k_ref[hd, rows, :], st["qs"][lo:, :], nt, preferred_element_type=F32)
            first_diag = 2 * i
            if c == first_diag:
                s = jnp.concatenate([jnp.where(causal, s[:, :t], -jnp.inf), s[:, t:]], axis=1)
            elif c == first_diag + 1:
                s = jnp.where(causal, s, -jnp.inf)
            s_ref[rows, lo:] = s
            both(st, "m", lo, jnp.max(s, axis=0, keepdims=True), jnp.maximum)

        return [start] + [functools.partial(one, c, rows, lo) for c, (rows, lo) in enumerate(chunks(i))]

    def prob_ops(hd, i):
        s_ref, st = scratch[i], state[hd, i]

        def one(rows, lo):
            m = st[("m", 1)] if lo else jnp.concatenate([st[("m", 0)], st[("m", 1)]], axis=1)
            p = jnp.exp2(s_ref[rows, lo:] - m)
            s_ref[rows, lo:] = p
            both(st, "l", lo, jnp.sum(p, axis=0, keepdims=True), jnp.add)

        return [functools.partial(one, rows, lo) for rows, lo in chunks(i)]

    def value_ops(hd, i):
        s_ref, st = scratch[i], state[hd, i]

        def combine(rows, half):
            l = st[("l", half)]
            ratio = lam * l[:, :h] / l[:, h:]
            base = half * t
            return s_ref[rows, base:base + h] - ratio * s_ref[rows, base + h:base + t]

        def one(rows, lo):
            halves = (1,) if lo else (0, 1)
            a = jnp.concatenate([combine(rows, half) for half in halves], axis=1).astype(BF16)
            pv = lax.dot_general(v_ref[hd, rows, :], a, tn, preferred_element_type=F32)
            for n, half in enumerate(halves):
                part = pv[:, n * h:(n + 1) * h]
                st["acc", half] = part if ("acc", half) not in st else st["acc", half] + part

        def finish():
            o = jnp.concatenate([st[("acc", half)] * (1.0 / st[("l", half)][:, :h]) for half in range(2)], axis=1)
            inv = lax.rsqrt(jnp.mean(o * o, axis=0, keepdims=True) + EPS)
            o_ref[hd, i * t:(i + 1) * t, :] = (o * inv * gain).T.astype(BF16)

        return [functools.partial(one, rows, lo) for rows, lo in chunks(i)] + [finish]

    order = [(hd, i) for hd in range(heads) for i in _tile_order(nq)]
    for step in range(len(order) + 2):
        streams = []
        if step < len(order):
            streams.append(score_ops(*order[step]))
        if 0 <= step - 1 < len(order):
            streams.append(prob_ops(*order[step - 1]))
        if 0 <= step - 2 < len(order):
            streams.append(value_ops(*order[step - 2]))
        for op in _interleave(*streams):
            op()


def _attn_call(q, k, v, lq1, lk1, lq2, lk2, g_subln_col, t):
    b, _, s, _ = q.shape
    nq = s // t
    heads = ATTN_HEADS_PER_STEP
    head = pl.BlockSpec((None, heads, s, DA_V_DIM), lambda bi, h: (bi, h, 0, 0))
    vec = _const_spec((1, DA_QK_DIM))
    return pl.pallas_call(
        functools.partial(_attn_kernel, t=t, nq=nq, heads=heads),
        grid=(b, DA_HEADS // heads),
        in_specs=[head, head, head, vec, vec, vec, vec, _const_spec((DA_V_DIM, 1))],
        out_specs=head,
        out_shape=jax.ShapeDtypeStruct((b, DA_HEADS, s, DA_V_DIM), BF16),
        scratch_shapes=[pltpu.VMEM(((i + 1) * t, 2 * t), F32) for i in range(nq)],
        compiler_params=pltpu.CompilerParams(
            dimension_semantics=("arbitrary", "arbitrary"), vmem_limit_bytes=VMEM_LIMIT),
        name="diff_attn",
    )(q, k, v, lq1, lk1, lq2, lk2, g_subln_col)


def _mixout_kernel(da_ref, po_ref, x_ref, wo_hbm, gpost_ref, gxpre_ref, wxq_hbm, kv_ref, wxo_hbm,
                   gxpost_ref, o_ref, wo_ref, wxq_ref, wxo_ref, stage, sem):
    @pl.when(_first_step(2))
    def _():
        _load_weights([(wo_hbm, wo_ref), (wxq_hbm, wxq_ref), (wxo_hbm, wxo_ref)], stage, sem)

    tm = x_ref.shape[1]
    groups = [slice(r, r + tm // 2) for r in (0, tm // 2)]
    x_scale = X_HEAD_DIM ** -0.5

    def phases(rows):
        st = {}

        def mix():
            da = jnp.concatenate([da_ref[h, rows, :] for h in range(DA_HEADS)], axis=-1)
            st["mix"] = (jnp.dot(da, wo_ref[0:DA_WIDTH, :], preferred_element_type=F32)
                         + jnp.dot(po_ref[0, rows, :], wo_ref[DA_WIDTH:, :], preferred_element_type=F32))

        def norms():
            st["x1"] = x_ref[0, rows, :] + _rms(st.pop("mix"), gpost_ref[...])
            st["hq"], st["inv"] = _prenorm_operand(st["x1"], gxpre_ref[...])

        def queries():
            st["xq"] = (jnp.dot(st.pop("hq"), wxq_ref[...], preferred_element_type=F32)
                        * (st.pop("inv") * x_scale)).astype(BF16)

        def attend():
            heads = []
            for h in range(X_HEADS):
                sl = slice(h * X_HEAD_DIM, (h + 1) * X_HEAD_DIM)
                kh = kv_ref[0, :, sl]
                vh = kv_ref[0, :, D_MODEL + h * X_HEAD_DIM:D_MODEL + (h + 1) * X_HEAD_DIM]
                sc = lax.dot_general(st["xq"][:, sl], kh, (((1,), (1,)), ((), ())), preferred_element_type=F32)
                p = jnp.exp(sc - jnp.max(sc, axis=-1, keepdims=True))
                pm = (p * (1.0 / jnp.sum(p, axis=-1, keepdims=True))).astype(BF16)
                heads.append(jnp.dot(pm, vh, preferred_element_type=F32).astype(BF16))
            st["xo"] = jnp.concatenate(heads, axis=-1)

        def project():
            st["y"] = jnp.dot(st.pop("xo"), wxo_ref[...], preferred_element_type=F32)

        def finish():
            o_ref[0, rows, :] = st["x1"] + _rms(st["y"], gxpost_ref[...])

        return [mix, norms, queries, attend, project, finish]

    first, second = (phases(rows) for rows in groups)
    program = [first[0]]
    for k in range(1, len(first)):
        program += [first[k], second[k - 1]]
    program.append(second[-1])
    for phase in program:
        phase()


def _mixout_call(da, po, x, w_out, g_mix_post, g_x_pre, w_xq, kv, w_xo, g_x_post, tm):
    b, s, d = x.shape
    tok = lambda n: pl.BlockSpec((1, tm, n), lambda bi, i: (bi, i, 0))
    gvec = _const_spec((1, d))
    return pl.pallas_call(
        _mixout_kernel,
        grid=(b, s // tm),
        in_specs=[pl.BlockSpec((None, DA_HEADS, tm, DA_V_DIM), lambda bi, i: (bi, 0, i, 0)),
                  tok(POOL_WIDTH), tok(d), _HBM, gvec, gvec, _HBM,
                  pl.BlockSpec((1,) + kv.shape[1:], lambda bi, i: (bi, 0, 0)),
                  _HBM, gvec],
        out_specs=tok(d),
        out_shape=jax.ShapeDtypeStruct((b, s, d), F32),
        scratch_shapes=[pltpu.VMEM(w_out.shape, BF16), pltpu.VMEM(w_xq.shape, BF16),
                        pltpu.VMEM(w_xo.shape, BF16), *_stage_scratch(d, d)],
        compiler_params=pltpu.CompilerParams(
            dimension_semantics=("arbitrary", "arbitrary"), vmem_limit_bytes=VMEM_LIMIT),
        name="mixout_xattn",
    )(da, po, x, w_out, g_mix_post, g_x_pre, w_xq, kv, w_xo, g_x_post)


def _ffn_kernel(x_ref, gpre_ref, wg_hbm, wu_hbm, wd_hbm, gpost_ref, o_ref,
                wg_ref, wu_ref, wd_ref, wide_stage, wide_sem, tall_stage, tall_sem):
    @pl.when(_first_step(2))
    def _():
        _load_weights([(wg_hbm, wg_ref), (wu_hbm, wu_ref)], wide_stage, wide_sem)
        _load_weights([(wd_hbm, wd_ref)], tall_stage, tall_sem)

    tm = x_ref.shape[1]

    def phases(rows):
        st = {}

        def prenorm():
            st["x"] = x_ref[0, rows, :]
            st["hf"], st["inv"] = _prenorm_operand(st["x"], gpre_ref[...])

        def chunk(lo, hi):
            gate = jnp.dot(st["hf"], wg_ref[:, lo:hi], preferred_element_type=F32) * st["inv"]
            up = jnp.dot(st["hf"], wu_ref[:, lo:hi], preferred_element_type=F32) * st["inv"]
            act = (gate * (1.0 / (1.0 + jnp.exp(-gate))) * up).astype(BF16)
            part = jnp.dot(act, wd_ref[lo:hi, :], preferred_element_type=F32)
            st["ff"] = part if "ff" not in st else st["ff"] + part

        def finish():
            o_ref[0, rows, :] = st["x"] + _rms(st["ff"], gpost_ref[...])

        return [prenorm] + [functools.partial(chunk, lo, hi) for lo, hi in FF_CHUNKS] + [finish]

    first, second = (phases(slice(r, r + tm // 2)) for r in (0, tm // 2))
    program = [first[0]]
    for k in range(1, len(first)):
        program += [first[k], second[k - 1]]
    program.append(second[-1])
    for phase in program:
        phase()


def _ffn_call(x, g_pre, w_gate, w_up, w_down, g_post, tm):
    b, s, d = x.shape
    tok = pl.BlockSpec((1, tm, d), lambda bi, i: (bi, i, 0))
    gvec = _const_spec((1, d))
    return pl.pallas_call(
        _ffn_kernel,
        grid=(b, s // tm),
        in_specs=[tok, gvec, _HBM, _HBM, _HBM, gvec],
        out_specs=tok,
        out_shape=jax.ShapeDtypeStruct((b, s, d), F32),
        scratch_shapes=[pltpu.VMEM(w_gate.shape, BF16), pltpu.VMEM(w_up.shape, BF16),
                        pltpu.VMEM(w_down.shape, BF16),
                        *_stage_scratch(d, w_gate.shape[1]),
                        *_stage_scratch(w_down.shape[0], d)],
        compiler_params=pltpu.CompilerParams(
            dimension_semantics=("arbitrary", "arbitrary"), vmem_limit_bytes=VMEM_LIMIT),
        name="swiglu",
    )(x, g_pre, w_gate, w_up, w_down, g_post)


def _rope_lane_constants():
    inv_freq = ROPE_THETA ** (-jnp.arange(0, ROPE_DIM, 2, dtype=F32) / ROPE_DIM)
    packed_freq = jnp.tile(inv_freq, LANES // ROPE_HALF)
    one_half = jnp.ones((ROPE_HALF,), F32)
    per_map = jnp.concatenate([-one_half, one_half, jnp.zeros((DA_QK_DIM - ROPE_DIM,), F32)])
    sign = jnp.concatenate([per_map, per_map])
    return jnp.concatenate([jnp.stack([packed_freq, sign]), jnp.zeros((6, LANES), F32)], axis=0)


def kernel(x, mem, positions, g_mix_pre, w_in, lambda_q1, lambda_k1, lambda_q2, lambda_k2,
           g_subln, w_pool, pool_scale, w_out, g_mix_post, g_x_pre, g_mem, w_xq, w_xkv, w_xo,
           g_x_post, g_ffn_pre, w_gate, w_up, w_down, g_ffn_post):
    b, s, d = x.shape
    assert d == D_MODEL and s % TOKEN_TILE == 0 and s % ATTN_TILE == 0
    row = lambda a: a.reshape(1, -1).astype(F32)
    bf = lambda a: a.astype(BF16)

    pos = positions.reshape(b, 1, s)
    kv = _kv_call(mem, row(g_mem), w_xkv, 2 * TOKEN_TILE)
    q, k, v, po = _inproj_call(x, row(g_mix_pre), w_in, pos, _rope_lane_constants(),
                               bf(w_pool), row(pool_scale), 2 * TOKEN_TILE)
    da = _attn_call(q, k, v, row(lambda_q1), row(lambda_k1), row(lambda_q2), row(lambda_k2),
                    g_subln.reshape(-1, 1).astype(F32), ATTN_TILE)
    x2 = _mixout_call(da, po, x, w_out, row(g_mix_post), row(g_x_pre), w_xq, kv, w_xo,
                      row(g_x_post), 2 * TOKEN_TILE)
    return _ffn_call(x2, row(g_ffn_pre), w_gate, w_up, w_down, row(g_ffn_post), TOKEN_TILE)
```

```python
import functools
import itertools
import math

import jax
import jax.numpy as jnp
from jax import lax
from jax.experimental import pallas as pl
from jax.experimental.pallas import tpu as pltpu

F32 = jnp.float32
BF16 = jnp.bfloat16

D_MODEL = 1024
MEM_LEN = 256
EPS = 1e-6
DA_HEADS = 4
DA_QK_DIM = 64
DA_V_DIM = 2 * DA_QK_DIM
DA_WIDTH = DA_HEADS * DA_V_DIM
QK_WIDTH = DA_HEADS * 2 * DA_QK_DIM
POOL_WINDOWS = (2, 4, 8, 16)
POOL_WIDTH = D_MODEL - DA_WIDTH
POOL_GROUP_DIM = POOL_WIDTH // len(POOL_WINDOWS)
ROPE_THETA = 500000.0
ROPE_DIM = DA_QK_DIM // 4
ROPE_HALF = ROPE_DIM // 2
X_HEADS = 4
X_HEAD_DIM = D_MODEL // X_HEADS
D_FF = -(-(8 * D_MODEL) // (3 * 256)) * 256
LAMBDA_INIT = 0.8 - 0.6 * math.exp(-0.3 * 0)
LOG2_E = math.log2(math.e)

LANES = 128
POOL_HALO = 16
VMEM_LIMIT = 56 * 1024 * 1024

TOKEN_TILE = 512
MIXOUT_GROUP = 256
ATTN_TILE = 512
ATTN_HEADS_PER_STEP = 2
FF_CHUNKS = ((0, 1536), (1536, 2816))
STAGE_BYTES = 4 * 1024 * 1024


def _inv_rms(x):
    return lax.rsqrt(jnp.mean(x * x, axis=-1, keepdims=True) + EPS)


def _rms(x, g):
    return x * _inv_rms(x) * g


def _prenorm_operand(x, g):
    return (x * g).astype(BF16), _inv_rms(x)


def _const_spec(shape):
    zeros = (0,) * len(shape)
    return pl.BlockSpec(shape, lambda *_: zeros, pipeline_mode=pl.Buffered(1))


_HBM = pl.BlockSpec(memory_space=pl.ANY)


def _stage_scratch(rows, cols):
    chunks = -(-(rows * cols * 4) // STAGE_BYTES)
    while rows % chunks or (rows // chunks) % 16:
        chunks += 1
    return [pltpu.VMEM((2, rows // chunks, cols), F32), pltpu.SemaphoreType.DMA((2,))]


def _load_weights(weights, stage_ref, sem_ref):
    rows = stage_ref.shape[1]
    jobs = []
    for w_hbm, w_vmem in weights:
        assert w_hbm.shape[0] % rows == 0 and stage_ref.shape[2] == w_hbm.shape[1]
        jobs += [(w_hbm, w_vmem, r) for r in range(0, w_hbm.shape[0], rows)]

    def copy(k):
        w_hbm, _, r = jobs[k]
        return pltpu.make_async_copy(w_hbm.at[pl.ds(r, rows), :], stage_ref.at[k % 2], sem_ref.at[k % 2])

    copy(0).start()
    for k, (_, w_vmem, r) in enumerate(jobs):
        if k + 1 < len(jobs):
            copy(k + 1).start()
        copy(k).wait()
        w_vmem[r:r + rows, :] = stage_ref[k % 2].astype(BF16)


def _run_staggered(chains):
    depth = len(chains[0])
    for time in range(depth + len(chains) - 1):
        for g, chain in enumerate(chains):
            if 0 <= time - g < depth:
                chain[time - g]()


def _first_step(grid_rank):
    first = pl.program_id(0) == 0
    for axis in range(1, grid_rank):
        first = first & (pl.program_id(axis) == 0)
    return first


def _kv_kernel(mem_ref, g_ref, w_hbm, kv_ref, w_ref, stage, sem):
    @pl.when(_first_step(1))
    def _():
        _load_weights([(w_hbm, w_ref)], stage, sem)

    mn = _rms(mem_ref[...], g_ref[...]).astype(BF16)
    kv_ref[...] = jnp.dot(mn, w_ref[...], preferred_element_type=F32).astype(BF16)


def _kv_call(mem, g_mem, w_xkv, tm):
    b, m, d = mem.shape
    n = w_xkv.shape[1]
    rows = b * m
    tm = min(tm, rows)
    assert rows % tm == 0
    kv = pl.pallas_call(
        _kv_kernel,
        grid=(rows // tm,),
        in_specs=[pl.BlockSpec((tm, d), lambda i: (i, 0)), _const_spec((1, d)), _HBM],
        out_specs=pl.BlockSpec((tm, n), lambda i: (i, 0)),
        out_shape=jax.ShapeDtypeStruct((rows, n), BF16),
        scratch_shapes=[pltpu.VMEM((d, n), BF16), *_stage_scratch(d, n)],
        compiler_params=pltpu.CompilerParams(
            dimension_semantics=("arbitrary",), vmem_limit_bytes=VMEM_LIMIT),
        name="kv_proj",
    )(mem.reshape(rows, d), g_mem, w_xkv)
    return kv.reshape(b, m, n)


def _inproj_kernel(x_ref, g_ref, w_hbm, pos_ref, freq_ref, wp_ref, ps_ref,
                   q_ref, k_ref, v_ref, po_ref, halo_ref, w_ref, stage, sem, *, tm):
    @pl.when(_first_step(2))
    def _():
        _load_weights([(w_hbm, w_ref)], stage, sem)

    i = pl.program_id(1)
    h = _rms(x_ref[0], g_ref[...]).astype(BF16)

    groups = LANES // ROPE_DIM
    rows = tm // groups
    lane = lax.broadcasted_iota(jnp.int32, (rows, LANES), 1)
    lane_group = lax.shift_right_logical(lane, int(math.log2(ROPE_DIM)))
    pos = jnp.broadcast_to(pos_ref[0].astype(F32), (LANES, tm)).T
    packed = jnp.zeros((rows, LANES), F32)
    for g in range(groups):
        packed = jnp.where(lane_group == g, pos[g * rows:(g + 1) * rows], packed)
    ang = packed * freq_ref[0:1, :]
    cos8, sin8 = jnp.cos(ang), jnp.sin(ang)
    first_map = lane < DA_QK_DIM
    rotary = freq_ref[1:2, :] != 0.0

    def spread(packed_table, g):
        shift = (LANES - ROPE_DIM * g) % LANES
        lo = pltpu.roll(packed_table, shift, 1) if shift else packed_table
        hi_shift = (shift + DA_QK_DIM) % LANES
        hi = pltpu.roll(packed_table, hi_shift, 1) if hi_shift else packed_table
        return jnp.where(first_map, lo, hi)

    cos_t = jnp.concatenate([jnp.where(rotary, spread(cos8, g), 1.0) for g in range(groups)], axis=0)
    sin_t = jnp.concatenate([spread(sin8, g) * freq_ref[1:2, :] for g in range(groups)], axis=0)
    first_half = freq_ref[1:2, :] < 0.0

    def rope(t, c, s):
        partner = jnp.where(first_half, pltpu.roll(t, LANES - ROPE_HALF, 1), pltpu.roll(t, ROPE_HALF, 1))
        return t * c + partner * s

    qk_scale = DA_QK_DIM ** -0.5 * LOG2_E
    cos_q, sin_q = cos_t * qk_scale, sin_t * qk_scale
    pq = jnp.dot(h, w_ref[:, 0:QK_WIDTH], preferred_element_type=F32)
    for j in range(DA_HEADS):
        sl = slice(j * LANES, (j + 1) * LANES)
        q_ref[j] = rope(pq[:, sl], cos_q, sin_q).astype(BF16)
    pk = jnp.dot(h, w_ref[:, QK_WIDTH:2 * QK_WIDTH], preferred_element_type=F32)
    for j in range(DA_HEADS):
        sl = slice(j * LANES, (j + 1) * LANES)
        k_ref[j] = rope(pk[:, sl], cos_t, sin_t).astype(BF16)
    pv = jnp.dot(h, w_ref[:, 2 * QK_WIDTH:2 * QK_WIDTH + DA_WIDTH], preferred_element_type=F32)
    for j in range(DA_HEADS):
        v_ref[j] = pv[:, j * DA_V_DIM:(j + 1) * DA_V_DIM].astype(BF16)
    u = jnp.dot(h, w_ref[:, 2 * QK_WIDTH + DA_WIDTH:], preferred_element_type=F32)

    @pl.when(i == 0)
    def _():
        halo_ref[...] = jnp.zeros_like(halo_ref)

    prev = halo_ref[...]
    halo_ref[...] = u[tm - POOL_HALO:, :]
    tpos = i * tm + lax.broadcasted_iota(jnp.int32, (tm, 1), 0)
    for g, w in enumerate(POOL_WINDOWS):
        sl = slice(g * POOL_GROUP_DIM, (g + 1) * POOL_GROUP_DIM)
        ug = u[:, sl]
        s = jnp.concatenate([prev[:, sl], ug], axis=0)
        d = 1
        while d < w:
            s = s + pltpu.roll(s, d, 0)
            d *= 2
        inv_count = 1.0 / jnp.minimum(tpos + 1, w).astype(F32)
        pooled = s[POOL_HALO:, :] * inv_count - ug
        po = jnp.dot(pooled.astype(BF16), wp_ref[g], preferred_element_type=F32) * ps_ref[:, sl]
        po_ref[0, :, sl] = po.astype(BF16)


def _inproj_call(x, g, w_in, pos, freq, w_pool, pool_scale, tm):
    b, s, d = x.shape
    tok = lambda n: pl.BlockSpec((1, tm, n), lambda bi, i: (bi, i, 0))
    heads = pl.BlockSpec((None, DA_HEADS, tm, DA_V_DIM), lambda bi, i: (bi, 0, i, 0))
    head_major = jax.ShapeDtypeStruct((b, DA_HEADS, s, DA_V_DIM), BF16)
    return pl.pallas_call(
        functools.partial(_inproj_kernel, tm=tm),
        grid=(b, s // tm),
        in_specs=[tok(d), _const_spec((1, d)), _HBM,
                  pl.BlockSpec((1, 1, tm), lambda bi, i: (bi, 0, i)), _const_spec(freq.shape),
                  _const_spec(w_pool.shape), _const_spec((1, POOL_WIDTH))],
        out_specs=[heads, heads, heads, tok(POOL_WIDTH)],
        out_shape=[head_major, head_major, head_major,
                   jax.ShapeDtypeStruct((b, s, POOL_WIDTH), BF16)],
        scratch_shapes=[pltpu.VMEM((POOL_HALO, POOL_WIDTH), F32), pltpu.VMEM(w_in.shape, BF16),
                        *_stage_scratch(d, w_in.shape[1])],
        compiler_params=pltpu.CompilerParams(
            dimension_semantics=("arbitrary", "arbitrary"), vmem_limit_bytes=VMEM_LIMIT),
        name="inproj",
    )(x, g, w_in, pos, freq, w_pool, pool_scale)


def _interleave(*streams):
    tagged = [((k + 0.5) / len(ops), n, op) for n, ops in enumerate(streams) for k, op in enumerate(ops)]
    return [op for _, _, op in sorted(tagged, key=lambda e: e[:2])]


def _tile_order(nq):
    def length(order):
        total = 0.0
        for step in range(nq + 2):
            scores = order[step] + 1 if step < nq else 0
            probs = order[step - 1] + 1 if 0 <= step - 1 < nq else 0
            values = (order[step - 2] + 1) / 2 if 0 <= step - 2 < nq else 0
            total += max(scores + values, probs)
        return total

    return min(itertools.permutations(range(nq)), key=length)


def _attn_kernel(q_ref, k_ref, v_ref, lq1_ref, lk1_ref, lq2_ref, lk2_ref, gs_ref, o_ref, *scratch, t, nq, heads):
    h = t // 2
    lane = lax.broadcasted_iota(jnp.int32, (h, DA_V_DIM), 1)
    causal = (lax.broadcasted_iota(jnp.int32, (h, h), 0) <= lax.broadcasted_iota(jnp.int32, (h, h), 1))
    causal = jnp.concatenate([causal, causal], axis=1)
    lam = (jnp.exp(jnp.sum(lq1_ref[...] * lk1_ref[...], axis=-1, keepdims=True))
           - jnp.exp(jnp.sum(lq2_ref[...] * lk2_ref[...], axis=-1, keepdims=True))
           + LAMBDA_INIT)
    gain = gs_ref[...] * (1.0 - LAMBDA_INIT)
    nt = (((1,), (1,)), ((), ()))
    tn = (((0,), (0,)), ((), ()))
    state = {}

    def chunks(i):
        n = 2 * (i + 1)
        return [(slice(c * h, (c + 1) * h), t if c == n - 1 else 0) for c in range(n)]

    def both(st, name, lo, value, combine):
        for half, part in ((0, value[:, :t]), (1, value[:, t:])) if lo == 0 else ((1, value),):
            key = (name, half)
            st[key] = part if key not in st else combine(st[key], part)

    def score_ops(hd, i):
        s_ref, st = scratch[i], state.setdefault((hd, i), {})

        def start():
            stacked = []
            for half in range(2):
                q = q_ref[hd, i * t + half * h:i * t + (half + 1) * h, :]
                zero = jnp.zeros_like(q)
                stacked += [jnp.where(lane < DA_QK_DIM, q, zero), jnp.where(lane >= DA_QK_DIM, q, zero)]
            st["qs"] = jnp.concatenate(stacked, axis=0)

        def one(c, rows, lo):
            s = lax.dot_general(k_ref[hd, rows, :], st["qs"][lo:, :], nt, preferred_element_type=F32)
            first_diag = 2 * i
            if c == first_diag:
                s = jnp.concatenate([jnp.where(causal, s[:, :t], -jnp.inf), s[:, t:]], axis=1)
            elif c == first_diag + 1:
                s = jnp.where(causal, s, -jnp.inf)
            s_ref[rows, lo:] = s
            both(st, "m", lo, jnp.max(s, axis=0, keepdims=True), jnp.maximum)

        return [start] + [functools.partial(one, c, rows, lo) for c, (rows, lo) in enumerate(chunks(i))]

    def prob_ops(hd, i):
        s_ref, st = scratch[i], state[hd, i]

        def one(rows, lo):
            m = st[("m", 1)] if lo else jnp.concatenate([st[("m", 0)], st[("m", 1)]], axis=1)
            p = jnp.exp2(s_ref[rows, lo:] - m)
            s_ref[rows, lo:] = p
            both(st, "l", lo, jnp.sum(p, axis=0, keepdims=True), jnp.add)

        return [functools.partial(one, rows, lo) for rows, lo in chunks(i)]

    def value_ops(hd, i):
        s_ref, st = scratch[i], state[hd, i]

        def combine(rows, half):
            l = st[("l", half)]
            ratio = lam * l[:, :h] / l[:, h:]
            base = half * t
            return s_ref[rows, base:base + h] - ratio * s_ref[rows, base + h:base + t]

        def one(rows, lo):
            halves = (1,) if lo else (0, 1)
            a = jnp.concatenate([combine(rows, half) for half in halves], axis=1).astype(BF16)
            pv = lax.dot_general(v_ref[hd, rows, :], a, tn, preferred_element_type=F32)
            for n, half in enumerate(halves):
                part = pv[:, n * h:(n + 1) * h]
                st["acc", half] = part if ("acc", half) not in st else st["acc", half] + part

        def finish():
            o = jnp.concatenate([st[("acc", half)] * (1.0 / st[("l", half)][:, :h]) for half in range(2)], axis=1)
            inv = lax.rsqrt(jnp.mean(o * o, axis=0, keepdims=True) + EPS)
            o_ref[hd, i * t:(i + 1) * t, :] = (o * inv * gain).T.astype(BF16)

        return [functools.partial(one, rows, lo) for rows, lo in chunks(i)] + [finish]

    order = [(hd, i) for hd in range(heads) for i in _tile_order(nq)]
    for step in range(len(order) + 2):
        streams = []
        if step < len(order):
            streams.append(score_ops(*order[step]))
        if 0 <= step - 1 < len(order):
            streams.append(prob_ops(*order[step - 1]))
        if 0 <= step - 2 < len(order):
            streams.append(value_ops(*order[step - 2]))
        for op in _interleave(*streams):
            op()


def _attn_call(q, k, v, lq1, lk1, lq2, lk2, g_subln_col, t):
    b, _, s, _ = q.shape
    nq = s // t
    heads = ATTN_HEADS_PER_STEP
    head = pl.BlockSpec((None, heads, s, DA_V_DIM), lambda bi, h: (bi, h, 0, 0))
    vec = _const_spec((1, DA_QK_DIM))
    return pl.pallas_call(
        functools.partial(_attn_kernel, t=t, nq=nq, heads=heads),
        grid=(b, DA_HEADS // heads),
        in_specs=[head, head, head, vec, vec, vec, vec, _const_spec((DA_V_DIM, 1))],
        out_specs=head,
        out_shape=jax.ShapeDtypeStruct((b, DA_HEADS, s, DA_V_DIM), BF16),
        scratch_shapes=[pltpu.VMEM(((i + 1) * t, 2 * t), F32) for i in range(nq)],
        compiler_params=pltpu.CompilerParams(
            dimension_semantics=("arbitrary", "arbitrary"), vmem_limit_bytes=VMEM_LIMIT),
        name="diff_attn",
    )(q, k, v, lq1, lk1, lq2, lk2, g_subln_col)


def _mixout_kernel(da_ref, po_ref, x_ref, wo_hbm, gpost_ref, gxpre_ref, wxq_hbm, kv_ref, wxo_hbm,
                   gxpost_ref, o_ref, wo_ref, wxq_ref, wxo_ref, stage, sem):
    @pl.when(_first_step(2))
    def _():
        _load_weights([(wo_hbm, wo_ref), (wxq_hbm, wxq_ref), (wxo_hbm, wxo_ref)], stage, sem)

    tm = x_ref.shape[1]
    groups = [slice(r, r + MIXOUT_GROUP) for r in range(0, tm, MIXOUT_GROUP)]
    x_scale = X_HEAD_DIM ** -0.5

    def phases(rows):
        st = {}

        def mix():
            da = jnp.concatenate([da_ref[h, rows, :] for h in range(DA_HEADS)], axis=-1)
            st["mix"] = (jnp.dot(da, wo_ref[0:DA_WIDTH, :], preferred_element_type=F32)
                         + jnp.dot(po_ref[0, rows, :], wo_ref[DA_WIDTH:, :], preferred_element_type=F32))

        def norms():
            st["x1"] = x_ref[0, rows, :] + _rms(st.pop("mix"), gpost_ref[...])
            st["hq"], st["inv"] = _prenorm_operand(st["x1"], gxpre_ref[...])

        def queries():
            st["xq"] = (jnp.dot(st.pop("hq"), wxq_ref[...], preferred_element_type=F32)
                        * (st.pop("inv") * x_scale)).astype(BF16)

        def attend():
            heads = []
            for h in range(X_HEADS):
                sl = slice(h * X_HEAD_DIM, (h + 1) * X_HEAD_DIM)
                kh = kv_ref[0, :, sl]
                vh = kv_ref[0, :, D_MODEL + h * X_HEAD_DIM:D_MODEL + (h + 1) * X_HEAD_DIM]
                sc = lax.dot_general(st["xq"][:, sl], kh, (((1,), (1,)), ((), ())), preferred_element_type=F32)
                p = jnp.exp(sc - jnp.max(sc, axis=-1, keepdims=True))
                pm = (p * (1.0 / jnp.sum(p, axis=-1, keepdims=True))).astype(BF16)
                heads.append(jnp.dot(pm, vh, preferred_element_type=F32).astype(BF16))
            st["xo"] = jnp.concatenate(heads, axis=-1)

        def project():
            st["y"] = jnp.dot(st.pop("xo"), wxo_ref[...], preferred_element_type=F32)

        def finish():
            o_ref[0, rows, :] = st["x1"] + _rms(st["y"], gxpost_ref[...])

        return [mix, norms, queries, attend, project, finish]

    _run_staggered([phases(rows) for rows in groups])


def _mixout_call(da, po, x, w_out, g_mix_post, g_x_pre, w_xq, kv, w_xo, g_x_post, tm):
    b, s, d = x.shape
    tok = lambda n: pl.BlockSpec((1, tm, n), lambda bi, i: (bi, i, 0))
    gvec = _const_spec((1, d))
    return pl.pallas_call(
        _mixout_kernel,
        grid=(b, s // tm),
        in_specs=[pl.BlockSpec((None, DA_HEADS, tm, DA_V_DIM), lambda bi, i: (bi, 0, i, 0)),
                  tok(POOL_WIDTH), tok(d), _HBM, gvec, gvec, _HBM,
                  pl.BlockSpec((1,) + kv.shape[1:], lambda bi, i: (bi, 0, 0)),
                  _HBM, gvec],
        out_specs=tok(d),
        out_shape=jax.ShapeDtypeStruct((b, s, d), F32),
        scratch_shapes=[pltpu.VMEM(w_out.shape, BF16), pltpu.VMEM(w_xq.shape, BF16),
                        pltpu.VMEM(w_xo.shape, BF16), *_stage_scratch(d, d)],
        compiler_params=pltpu.CompilerParams(
            dimension_semantics=("arbitrary", "arbitrary"), vmem_limit_bytes=VMEM_LIMIT),
        name="mixout_xattn",
    )(da, po, x, w_out, g_mix_post, g_x_pre, w_xq, kv, w_xo, g_x_post)


def _ffn_kernel(x_ref, gpre_ref, wg_hbm, wu_hbm, wd_hbm, gpost_ref, o_ref,
                wg_ref, wu_ref, wd_ref, wide_stage, wide_sem, tall_stage, tall_sem):
    @pl.when(_first_step(2))
    def _():
        _load_weights([(wg_hbm, wg_ref), (wu_hbm, wu_ref)], wide_stage, wide_sem)
        _load_weights([(wd_hbm, wd_ref)], tall_stage, tall_sem)

    tm = x_ref.shape[1]

    def phases(rows):
        st = {}

        def prenorm():
            st["x"] = x_ref[0, rows, :]
            st["hf"], st["inv"] = _prenorm_operand(st["x"], gpre_ref[...])

        def chunk(lo, hi):
            gate = jnp.dot(st["hf"], wg_ref[:, lo:hi], preferred_element_type=F32) * st["inv"]
            up = jnp.dot(st["hf"], wu_ref[:, lo:hi], preferred_element_type=F32) * st["inv"]
            act = (gate * (1.0 / (1.0 + jnp.exp(-gate))) * up).astype(BF16)
            part = jnp.dot(act, wd_ref[lo:hi, :], preferred_element_type=F32)
            st["ff"] = part if "ff" not in st else st["ff"] + part

        def finish():
            o_ref[0, rows, :] = st["x"] + _rms(st["ff"], gpost_ref[...])

        return [prenorm] + [functools.partial(chunk, lo, hi) for lo, hi in FF_CHUNKS] + [finish]

    _run_staggered([phases(slice(r, r + tm // 2)) for r in (0, tm // 2)])


def _ffn_call(x, g_pre, w_gate, w_up, w_down, g_post, tm):
    b, s, d = x.shape
    tok = pl.BlockSpec((1, tm, d), lambda bi, i: (bi, i, 0))
    gvec = _const_spec((1, d))
    return pl.pallas_call(
        _ffn_kernel,
        grid=(b, s // tm),
        in_specs=[tok, gvec, _HBM, _HBM, _HBM, gvec],
        out_specs=tok,
        out_shape=jax.ShapeDtypeStruct((b, s, d), F32),
        scratch_shapes=[pltpu.VMEM(w_gate.shape, BF16), pltpu.VMEM(w_up.shape, BF16),
                        pltpu.VMEM(w_down.shape, BF16),
                        *_stage_scratch(d, w_gate.shape[1]),
                        *_stage_scratch(w_down.shape[0], d)],
        compiler_params=pltpu.CompilerParams(
            dimension_semantics=("arbitrary", "arbitrary"), vmem_limit_bytes=VMEM_LIMIT),
        name="swiglu",
    )(x, g_pre, w_gate, w_up, w_down, g_post)


def _rope_lane_constants():
    inv_freq = ROPE_THETA ** (-jnp.arange(0, ROPE_DIM, 2, dtype=F32) / ROPE_DIM)
    packed_freq = jnp.tile(inv_freq, LANES // ROPE_HALF)
    one_half = jnp.ones((ROPE_HALF,), F32)
    per_map = jnp.concatenate([-one_half, one_half, jnp.zeros((DA_QK_DIM - ROPE_DIM,), F32)])
    sign = jnp.concatenate([per_map, per_map])
    return jnp.concatenate([jnp.stack([packed_freq, sign]), jnp.zeros((6, LANES), F32)], axis=0)


def kernel(x, mem, positions, g_mix_pre, w_in, lambda_q1, lambda_k1, lambda_q2, lambda_k2,
           g_subln, w_pool, pool_scale, w_out, g_mix_post, g_x_pre, g_mem, w_xq, w_xkv, w_xo,
           g_x_post, g_ffn_pre, w_gate, w_up, w_down, g_ffn_post):
    b, s, d = x.shape
    assert d == D_MODEL and s % TOKEN_TILE == 0 and s % ATTN_TILE == 0
    row = lambda a: a.reshape(1, -1).astype(F32)
    bf = lambda a: a.astype(BF16)

    pos = positions.reshape(b, 1, s)
    kv = _kv_call(mem, row(g_mem), w_xkv, 2 * TOKEN_TILE)
    q, k, v, po = _inproj_call(x, row(g_mix_pre), w_in, pos, _rope_lane_constants(),
                               bf(w_pool), row(pool_scale), 2 * TOKEN_TILE)
    da = _attn_call(q, k, v, row(lambda_q1), row(lambda_k1), row(lambda_q2), row(lambda_k2),
                    g_subln.reshape(-1, 1).astype(F32), ATTN_TILE)
    x2 = _mixout_call(da, po, x, w_out, row(g_mix_post), row(g_x_pre), w_xq, kv, w_xo,
                      row(g_x_post), 2 * TOKEN_TILE)
    return _ffn_call(x2, row(g_ffn_pre), w_gate, w_up, w_down, row(g_ffn_post), TOKEN_TILE)
```

```python
import functools
import itertools
import math

import jax
import jax.numpy as jnp
from jax import lax
from jax.experimental import pallas as pl
from jax.experimental.pallas import tpu as pltpu

F32 = jnp.float32
BF16 = jnp.bfloat16

D_MODEL = 1024
MEM_LEN = 256
EPS = 1e-6
DA_HEADS = 4
DA_QK_DIM = 64
DA_V_DIM = 2 * DA_QK_DIM
DA_WIDTH = DA_HEADS * DA_V_DIM
QK_WIDTH = DA_HEADS * 2 * DA_QK_DIM
POOL_WINDOWS = (2, 4, 8, 16)
POOL_WIDTH = D_MODEL - DA_WIDTH
POOL_GROUP_DIM = POOL_WIDTH // len(POOL_WINDOWS)
ROPE_THETA = 500000.0
ROPE_DIM = DA_QK_DIM // 4
ROPE_HALF = ROPE_DIM // 2
X_HEADS = 4
X_HEAD_DIM = D_MODEL // X_HEADS
D_FF = -(-(8 * D_MODEL) // (3 * 256)) * 256
LAMBDA_INIT = 0.8 - 0.6 * math.exp(-0.3 * 0)
LOG2_E = math.log2(math.e)

LANES = 128
POOL_HALO = 16
VMEM_LIMIT = 56 * 1024 * 1024

TOKEN_TILE = 512
MIXOUT_GROUP = 256
ATTN_TILE = 512
ATTN_HEADS_PER_STEP = 2
FF_CHUNKS = ((0, 1536), (1536, 2816))
STAGE_BYTES = 4 * 1024 * 1024


def _inv_rms(x):
    return lax.rsqrt(jnp.mean(x * x, axis=-1, keepdims=True) + EPS)


def _rms(x, g):
    return x * _inv_rms(x) * g


def _prenorm_operand(x, g):
    return (x * g).astype(BF16), _inv_rms(x)


def _const_spec(shape):
    zeros = (0,) * len(shape)
    return pl.BlockSpec(shape, lambda *_: zeros, pipeline_mode=pl.Buffered(1))


_HBM = pl.BlockSpec(memory_space=pl.ANY)


def _stage_scratch(rows, cols):
    chunks = -(-(rows * cols * 4) // STAGE_BYTES)
    while rows % chunks or (rows // chunks) % 16:
        chunks += 1
    return [pltpu.VMEM((2, rows // chunks, cols), F32), pltpu.SemaphoreType.DMA((2,))]


def _load_weights(weights, stage_ref, sem_ref):
    rows = stage_ref.shape[1]
    jobs = []
    for w_hbm, w_vmem in weights:
        assert w_hbm.shape[0] % rows == 0 and stage_ref.shape[2] == w_hbm.shape[1]
        jobs += [(w_hbm, w_vmem, r) for r in range(0, w_hbm.shape[0], rows)]

    def copy(k):
        w_hbm, _, r = jobs[k]
        return pltpu.make_async_copy(w_hbm.at[pl.ds(r, rows), :], stage_ref.at[k % 2], sem_ref.at[k % 2])

    copy(0).start()
    for k, (_, w_vmem, r) in enumerate(jobs):
        if k + 1 < len(jobs):
            copy(k + 1).start()
        copy(k).wait()
        chunk = stage_ref[k % 2].astype(BF16)
        if len(w_vmem.shape) == 2:
            w_vmem[r:r + rows, :] = chunk
        else:
            width = w_vmem.shape[2]
            for part in range(w_vmem.shape[0]):
                w_vmem[part, r:r + rows, :] = chunk[:, part * width:(part + 1) * width]


def _run_staggered(chains):
    depth = len(chains[0])
    for time in range(depth + len(chains) - 1):
        for g, chain in enumerate(chains):
            if 0 <= time - g < depth:
                chain[time - g]()


def _first_step(grid_rank):
    first = pl.program_id(0) == 0
    for axis in range(1, grid_rank):
        first = first & (pl.program_id(axis) == 0)
    return first


def _inproj_kernel(x_ref, g_ref, w_hbm, pos_ref, freq_ref, wp_ref, ps_ref,
                   q_ref, k_ref, v_ref, po_ref, halo_ref, w_ref, stage, sem, *, tm):
    @pl.when(_first_step(2))
    def _():
        _load_weights([(w_hbm, w_ref)], stage, sem)

    i = pl.program_id(1)
    h = _rms(x_ref[0], g_ref[...]).astype(BF16)

    groups = LANES // ROPE_DIM
    rows = tm // groups
    lane = lax.broadcasted_iota(jnp.int32, (rows, LANES), 1)
    lane_group = lax.shift_right_logical(lane, int(math.log2(ROPE_DIM)))
    pos = jnp.broadcast_to(pos_ref[0].astype(F32), (LANES, tm)).T
    packed = jnp.zeros((rows, LANES), F32)
    for g in range(groups):
        packed = jnp.where(lane_group == g, pos[g * rows:(g + 1) * rows], packed)
    ang = packed * freq_ref[0:1, :]
    cos8, sin8 = jnp.cos(ang), jnp.sin(ang)
    first_map = lane < DA_QK_DIM
    rotary = freq_ref[1:2, :] != 0.0

    def spread(packed_table, g):
        shift = (LANES - ROPE_DIM * g) % LANES
        lo = pltpu.roll(packed_table, shift, 1) if shift else packed_table
        hi_shift = (shift + DA_QK_DIM) % LANES
        hi = pltpu.roll(packed_table, hi_shift, 1) if hi_shift else packed_table
        return jnp.where(first_map, lo, hi)

    cos_t = jnp.concatenate([jnp.where(rotary, spread(cos8, g), 1.0) for g in range(groups)], axis=0)
    sin_t = jnp.concatenate([spread(sin8, g) * freq_ref[1:2, :] for g in range(groups)], axis=0)
    first_half = freq_ref[1:2, :] < 0.0

    def rope(t, c, s):
        partner = jnp.where(first_half, pltpu.roll(t, LANES - ROPE_HALF, 1), pltpu.roll(t, ROPE_HALF, 1))
        return t * c + partner * s

    qk_scale = DA_QK_DIM ** -0.5 * LOG2_E
    cos_q, sin_q = cos_t * qk_scale, sin_t * qk_scale
    pq = jnp.dot(h, w_ref[:, 0:QK_WIDTH], preferred_element_type=F32)
    for j in range(DA_HEADS):
        sl = slice(j * LANES, (j + 1) * LANES)
        q_ref[j] = rope(pq[:, sl], cos_q, sin_q).astype(BF16)
    pk = jnp.dot(h, w_ref[:, QK_WIDTH:2 * QK_WIDTH], preferred_element_type=F32)
    for j in range(DA_HEADS):
        sl = slice(j * LANES, (j + 1) * LANES)
        k_ref[j] = rope(pk[:, sl], cos_t, sin_t).astype(BF16)
    pv = jnp.dot(h, w_ref[:, 2 * QK_WIDTH:2 * QK_WIDTH + DA_WIDTH], preferred_element_type=F32)
    for j in range(DA_HEADS):
        v_ref[j] = pv[:, j * DA_V_DIM:(j + 1) * DA_V_DIM].astype(BF16)
    u = jnp.dot(h, w_ref[:, 2 * QK_WIDTH + DA_WIDTH:], preferred_element_type=F32)

    @pl.when(i == 0)
    def _():
        halo_ref[...] = jnp.zeros_like(halo_ref)

    prev = halo_ref[...]
    halo_ref[...] = u[tm - POOL_HALO:, :]
    tpos = i * tm + lax.broadcasted_iota(jnp.int32, (tm, 1), 0)
    for g, w in enumerate(POOL_WINDOWS):
        sl = slice(g * POOL_GROUP_DIM, (g + 1) * POOL_GROUP_DIM)
        ug = u[:, sl]
        s = jnp.concatenate([prev[:, sl], ug], axis=0)
        d = 1
        while d < w:
            s = s + pltpu.roll(s, d, 0)
            d *= 2
        inv_count = 1.0 / jnp.minimum(tpos + 1, w).astype(F32)
        pooled = s[POOL_HALO:, :] * inv_count - ug
        po = jnp.dot(pooled.astype(BF16), wp_ref[g], preferred_element_type=F32) * ps_ref[:, sl]
        po_ref[0, :, sl] = po.astype(BF16)


def _inproj_call(x, g, w_in, pos, freq, w_pool, pool_scale, tm):
    b, s, d = x.shape
    tok = lambda n: pl.BlockSpec((1, tm, n), lambda bi, i: (bi, i, 0))
    heads = pl.BlockSpec((None, DA_HEADS, tm, DA_V_DIM), lambda bi, i: (bi, 0, i, 0))
    head_major = jax.ShapeDtypeStruct((b, DA_HEADS, s, DA_V_DIM), BF16)
    return pl.pallas_call(
        functools.partial(_inproj_kernel, tm=tm),
        grid=(b, s // tm),
        in_specs=[tok(d), _const_spec((1, d)), _HBM,
                  pl.BlockSpec((1, 1, tm), lambda bi, i: (bi, 0, i)), _const_spec(freq.shape),
                  _const_spec(w_pool.shape), _const_spec((1, POOL_WIDTH))],
        out_specs=[heads, heads, heads, tok(POOL_WIDTH)],
        out_shape=[head_major, head_major, head_major,
                   jax.ShapeDtypeStruct((b, s, POOL_WIDTH), BF16)],
        scratch_shapes=[pltpu.VMEM((POOL_HALO, POOL_WIDTH), F32), pltpu.VMEM(w_in.shape, BF16),
                        *_stage_scratch(d, w_in.shape[1])],
        compiler_params=pltpu.CompilerParams(
            dimension_semantics=("arbitrary", "arbitrary"), vmem_limit_bytes=VMEM_LIMIT),
        name="inproj",
    )(x, g, w_in, pos, freq, w_pool, pool_scale)


def _interleave(*streams):
    tagged = [((k + 0.5) / len(ops), n, op) for n, ops in enumerate(streams) for k, op in enumerate(ops)]
    return [op for _, _, op in sorted(tagged, key=lambda e: e[:2])]


def _tile_order(nq):
    def length(order):
        total = 0.0
        for step in range(nq + 2):
            scores = order[step] + 1 if step < nq else 0
            probs = order[step - 1] + 1 if 0 <= step - 1 < nq else 0
            values = (order[step - 2] + 1) / 2 if 0 <= step - 2 < nq else 0
            total += max(scores + values, probs)
        return total

    return min(itertools.permutations(range(nq)), key=length)


def _attn_kernel(q_ref, k_ref, v_ref, lq1_ref, lk1_ref, lq2_ref, lk2_ref, gs_ref, mem_ref, gmem_ref, wkv_hbm,
                 o_ref, kv_ref, *scratch, t, nq, heads):
    *scratch, wkv_ref, stage, sem = scratch

    @pl.when(_first_step(2))
    def _():
        _load_weights([(wkv_hbm, wkv_ref)], stage, sem)

    h = t // 2
    lane = lax.broadcasted_iota(jnp.int32, (h, DA_V_DIM), 1)
    causal = (lax.broadcasted_iota(jnp.int32, (h, h), 0) <= lax.broadcasted_iota(jnp.int32, (h, h), 1))
    causal = jnp.concatenate([causal, causal], axis=1)
    lam = (jnp.exp(jnp.sum(lq1_ref[...] * lk1_ref[...], axis=-1, keepdims=True))
           - jnp.exp(jnp.sum(lq2_ref[...] * lk2_ref[...], axis=-1, keepdims=True))
           + LAMBDA_INIT)
    gain = gs_ref[...] * (1.0 - LAMBDA_INIT)
    nt = (((1,), (1,)), ((), ()))
    tn = (((0,), (0,)), ((), ()))
    state = {}

    def chunks(i):
        n = 2 * (i + 1)
        return [(slice(c * h, (c + 1) * h), t if c == n - 1 else 0) for c in range(n)]

    def both(st, name, lo, value, combine):
        for half, part in ((0, value[:, :t]), (1, value[:, t:])) if lo == 0 else ((1, value),):
            key = (name, half)
            st[key] = part if key not in st else combine(st[key], part)

    def score_ops(hd, i):
        s_ref, st = scratch[i], state.setdefault((hd, i), {})

        def start():
            stacked = []
            for half in range(2):
                q = q_ref[hd, i * t + half * h:i * t + (half + 1) * h, :]
                zero = jnp.zeros_like(q)
                stacked += [jnp.where(lane < DA_QK_DIM, q, zero), jnp.where(lane >= DA_QK_DIM, q, zero)]
            st["qs"] = jnp.concatenate(stacked, axis=0)

        def one(c, rows, lo):
            s = lax.dot_general(k_ref[hd, rows, :], st["qs"][lo:, :], nt, preferred_element_type=F32)
            first_diag = 2 * i
            if c == first_diag:
                s = jnp.concatenate([jnp.where(causal, s[:, :t], -jnp.inf), s[:, t:]], axis=1)
            elif c == first_diag + 1:
                s = jnp.where(causal, s, -jnp.inf)
            s_ref[rows, lo:] = s
            both(st, "m", lo, jnp.max(s, axis=0, keepdims=True), jnp.maximum)

        return [start] + [functools.partial(one, c, rows, lo) for c, (rows, lo) in enumerate(chunks(i))]

    def prob_ops(hd, i):
        s_ref, st = scratch[i], state[hd, i]

        def one(rows, lo):
            m = st[("m", 1)] if lo else jnp.concatenate([st[("m", 0)], st[("m", 1)]], axis=1)
            p = jnp.exp2(s_ref[rows, lo:] - m)
            s_ref[rows, lo:] = p
            both(st, "l", lo, jnp.sum(p, axis=0, keepdims=True), jnp.add)

        return [functools.partial(one, rows, lo) for rows, lo in chunks(i)]

    def value_ops(hd, i):
        s_ref, st = scratch[i], state[hd, i]

        def combine(rows, half):
            l = st[("l", half)]
            ratio = lam * l[:, :h] / l[:, h:]
            base = half * t
            return s_ref[rows, base:base + h] - ratio * s_ref[rows, base + h:base + t]

        def one(rows, lo):
            halves = (1,) if lo else (0, 1)
            a = jnp.concatenate([combine(rows, half) for half in halves], axis=1).astype(BF16)
            pv = lax.dot_general(v_ref[hd, rows, :], a, tn, preferred_element_type=F32)
            for n, half in enumerate(halves):
                part = pv[:, n * h:(n + 1) * h]
                st["acc", half] = part if ("acc", half) not in st else st["acc", half] + part

        def finish():
            o = jnp.concatenate([st[("acc", half)] * (1.0 / st[("l", half)][:, :h]) for half in range(2)], axis=1)
            inv = lax.rsqrt(jnp.mean(o * o, axis=0, keepdims=True) + EPS)
            o_ref[hd, i * t:(i + 1) * t, :] = (o * inv * gain).T.astype(BF16)

        return [functools.partial(one, rows, lo) for rows, lo in chunks(i)] + [finish]

    def memory_ops():
        st = {}
        w_part = wkv_ref.at[pl.program_id(1)]

        def start():
            st["mem"] = _rms(mem_ref[...], gmem_ref[...]).astype(BF16)

        def one(cols):
            kv_ref[:, cols] = jnp.dot(st["mem"], w_part[:, cols], preferred_element_type=F32).astype(BF16)

        width = kv_ref.shape[1]
        return [start] + [functools.partial(one, slice(c, c + MEM_LEN)) for c in range(0, width, MEM_LEN)]

    order = [(hd, i) for hd in range(heads) for i in _tile_order(nq)]
    extra = memory_ops()
    for step in range(len(order) + 2):
        streams = [[extra.pop(0)]] if extra and step >= 1 else []
        if step < len(order):
            streams.append(score_ops(*order[step]))
        if 0 <= step - 1 < len(order):
            streams.append(prob_ops(*order[step - 1]))
        if 0 <= step - 2 < len(order):
            streams.append(value_ops(*order[step - 2]))
        for op in _interleave(*streams):
            op()


def _attn_call(q, k, v, lq1, lk1, lq2, lk2, g_subln_col, mem, g_mem, w_xkv, t):
    b, _, s, _ = q.shape
    _, m, d = mem.shape
    n = w_xkv.shape[1]
    nq = s // t
    heads = ATTN_HEADS_PER_STEP
    steps = DA_HEADS // heads
    width = n // steps
    assert n % steps == 0 and width % LANES == 0
    head = pl.BlockSpec((None, heads, s, DA_V_DIM), lambda bi, h: (bi, h, 0, 0))
    vec = _const_spec((1, DA_QK_DIM))
    return pl.pallas_call(
        functools.partial(_attn_kernel, t=t, nq=nq, heads=heads),
        grid=(b, steps),
        in_specs=[head, head, head, vec, vec, vec, vec, _const_spec((DA_V_DIM, 1)),
                  pl.BlockSpec((None, m, d), lambda bi, h: (bi, 0, 0)), _const_spec((1, d)), _HBM],
        out_specs=[head, pl.BlockSpec((None, m, width), lambda bi, h: (bi, 0, h))],
        out_shape=[jax.ShapeDtypeStruct((b, DA_HEADS, s, DA_V_DIM), BF16),
                   jax.ShapeDtypeStruct((b, m, n), BF16)],
        scratch_shapes=[pltpu.VMEM(((i + 1) * t, 2 * t), F32) for i in range(nq)]
        + [pltpu.VMEM((steps, d, width), BF16), *_stage_scratch(d, n)],
        compiler_params=pltpu.CompilerParams(
            dimension_semantics=("arbitrary", "arbitrary"), vmem_limit_bytes=VMEM_LIMIT),
        name="diff_attn",
    )(q, k, v, lq1, lk1, lq2, lk2, g_subln_col, mem, g_mem, w_xkv)


def _mixout_kernel(da_ref, po_ref, x_ref, wo_hbm, gpost_ref, gxpre_ref, wxq_hbm, kv_ref, wxo_hbm,
                   gxpost_ref, o_ref, wo_ref, wxq_ref, wxo_ref, stage, sem):
    @pl.when(_first_step(2))
    def _():
        _load_weights([(wo_hbm, wo_ref), (wxq_hbm, wxq_ref), (wxo_hbm, wxo_ref)], stage, sem)

    tm = x_ref.shape[1]
    groups = [slice(r, r + MIXOUT_GROUP) for r in range(0, tm, MIXOUT_GROUP)]
    x_scale = X_HEAD_DIM ** -0.5

    def phases(rows):
        st = {}

        def mix():
            da = jnp.concatenate([da_ref[h, rows, :] for h in range(DA_HEADS)], axis=-1)
            st["mix"] = (jnp.dot(da, wo_ref[0:DA_WIDTH, :], preferred_element_type=F32)
                         + jnp.dot(po_ref[0, rows, :], wo_ref[DA_WIDTH:, :], preferred_element_type=F32))

        def norms():
            st["x1"] = x_ref[0, rows, :] + _rms(st.pop("mix"), gpost_ref[...])
            st["hq"], st["inv"] = _prenorm_operand(st["x1"], gxpre_ref[...])

        def queries():
            st["xq"] = (jnp.dot(st.pop("hq"), wxq_ref[...], preferred_element_type=F32)
                        * (st.pop("inv") * x_scale)).astype(BF16)

        def attend():
            heads = []
            for h in range(X_HEADS):
                sl = slice(h * X_HEAD_DIM, (h + 1) * X_HEAD_DIM)
                kh = kv_ref[0, :, sl]
                vh = kv_ref[0, :, D_MODEL + h * X_HEAD_DIM:D_MODEL + (h + 1) * X_HEAD_DIM]
                sc = lax.dot_general(st["xq"][:, sl], kh, (((1,), (1,)), ((), ())), preferred_element_type=F32)
                p = jnp.exp(sc - jnp.max(sc, axis=-1, keepdims=True))
                pm = (p * (1.0 / jnp.sum(p, axis=-1, keepdims=True))).astype(BF16)
                heads.append(jnp.dot(pm, vh, preferred_element_type=F32).astype(BF16))
            st["xo"] = jnp.concatenate(heads, axis=-1)

        def project():
            st["y"] = jnp.dot(st.pop("xo"), wxo_ref[...], preferred_element_type=F32)

        def finish():
            o_ref[0, rows, :] = st["x1"] + _rms(st["y"], gxpost_ref[...])

        return [mix, norms, queries, attend, project, finish]

    _run_staggered([phases(rows) for rows in groups])


def _mixout_call(da, po, x, w_out, g_mix_post, g_x_pre, w_xq, kv, w_xo, g_x_post, tm):
    b, s, d = x.shape
    tok = lambda n: pl.BlockSpec((1, tm, n), lambda bi, i: (bi, i, 0))
    gvec = _const_spec((1, d))
    return pl.pallas_call(
        _mixout_kernel,
        grid=(b, s // tm),
        in_specs=[pl.BlockSpec((None, DA_HEADS, tm, DA_V_DIM), lambda bi, i: (bi, 0, i, 0)),
                  tok(POOL_WIDTH), tok(d), _HBM, gvec, gvec, _HBM,
                  pl.BlockSpec((1,) + kv.shape[1:], lambda bi, i: (bi, 0, 0)),
                  _HBM, gvec],
        out_specs=tok(d),
        out_shape=jax.ShapeDtypeStruct((b, s, d), F32),
        scratch_shapes=[pltpu.VMEM(w_out.shape, BF16), pltpu.VMEM(w_xq.shape, BF16),
                        pltpu.VMEM(w_xo.shape, BF16), *_stage_scratch(d, d)],
        compiler_params=pltpu.CompilerParams(
            dimension_semantics=("arbitrary", "arbitrary"), vmem_limit_bytes=VMEM_LIMIT),
        name="mixout_xattn",
    )(da, po, x, w_out, g_mix_post, g_x_pre, w_xq, kv, w_xo, g_x_post)


def _ffn_kernel(x_ref, gpre_ref, wg_hbm, wu_hbm, wd_hbm, gpost_ref, o_ref,
                wg_ref, wu_ref, wd_ref, wide_stage, wide_sem, tall_stage, tall_sem):
    @pl.when(_first_step(2))
    def _():
        _load_weights([(wg_hbm, wg_ref), (wu_hbm, wu_ref)], wide_stage, wide_sem)
        _load_weights([(wd_hbm, wd_ref)], tall_stage, tall_sem)

    tm = x_ref.shape[1]

    def phases(rows):
        st = {}

        def prenorm():
            st["x"] = x_ref[0, rows, :]
            st["hf"], st["inv"] = _prenorm_operand(st["x"], gpre_ref[...])

        def chunk(lo, hi):
            gate = jnp.dot(st["hf"], wg_ref[:, lo:hi], preferred_element_type=F32) * st["inv"]
            up = jnp.dot(st["hf"], wu_ref[:, lo:hi], preferred_element_type=F32) * st["inv"]
            act = (gate * (1.0 / (1.0 + jnp.exp(-gate))) * up).astype(BF16)
            part = jnp.dot(act, wd_ref[lo:hi, :], preferred_element_type=F32)
            st["ff"] = part if "ff" not in st else st["ff"] + part

        def finish():
            o_ref[0, rows, :] = st["x"] + _rms(st["ff"], gpost_ref[...])

        return [prenorm] + [functools.partial(chunk, lo, hi) for lo, hi in FF_CHUNKS] + [finish]

    _run_staggered([phases(slice(r, r + tm // 2)) for r in (0, tm // 2)])


def _ffn_call(x, g_pre, w_gate, w_up, w_down, g_post, tm):
    b, s, d = x.shape
    tok = pl.BlockSpec((1, tm, d), lambda bi, i: (bi, i, 0))
    gvec = _const_spec((1, d))
    return pl.pallas_call(
        _ffn_kernel,
        grid=(b, s // tm),
        in_specs=[tok, gvec, _HBM, _HBM, _HBM, gvec],
        out_specs=tok,
        out_shape=jax.ShapeDtypeStruct((b, s, d), F32),
        scratch_shapes=[pltpu.VMEM(w_gate.shape, BF16), pltpu.VMEM(w_up.shape, BF16),
                        pltpu.VMEM(w_down.shape, BF16),
                        *_stage_scratch(d, w_gate.shape[1]),
                        *_stage_scratch(w_down.shape[0], d)],
        compiler_params=pltpu.CompilerParams(
            dimension_semantics=("arbitrary", "arbitrary"), vmem_limit_bytes=VMEM_LIMIT),
        name="swiglu",
    )(x, g_pre, w_gate, w_up, w_down, g_post)


def _rope_lane_constants():
    inv_freq = ROPE_THETA ** (-jnp.arange(0, ROPE_DIM, 2, dtype=F32) / ROPE_DIM)
    packed_freq = jnp.tile(inv_freq, LANES // ROPE_HALF)
    one_half = jnp.ones((ROPE_HALF,), F32)
    per_map = jnp.concatenate([-one_half, one_half, jnp.zeros((DA_QK_DIM - ROPE_DIM,), F32)])
    sign = jnp.concatenate([per_map, per_map])
    return jnp.concatenate([jnp.stack([packed_freq, sign]), jnp.zeros((6, LANES), F32)], axis=0)


def kernel(x, mem, positions, g_mix_pre, w_in, lambda_q1, lambda_k1, lambda_q2, lambda_k2,
           g_subln, w_pool, pool_scale, w_out, g_mix_post, g_x_pre, g_mem, w_xq, w_xkv, w_xo,
           g_x_post, g_ffn_pre, w_gate, w_up, w_down, g_ffn_post):
    b, s, d = x.shape
    assert d == D_MODEL and s % TOKEN_TILE == 0 and s % ATTN_TILE == 0
    row = lambda a: a.reshape(1, -1).astype(F32)
    bf = lambda a: a.astype(BF16)

    pos = positions.reshape(b, 1, s)
    q, k, v, po = _inproj_call(x, row(g_mix_pre), w_in, pos, _rope_lane_constants(),
                               bf(w_pool), row(pool_scale), 2 * TOKEN_TILE)
    da, kv = _attn_call(q, k, v, row(lambda_q1), row(lambda_k1), row(lambda_q2), row(lambda_k2),
                        g_subln.reshape(-1, 1).astype(F32), mem, row(g_mem), w_xkv, ATTN_TILE)
    x2 = _mixout_call(da, po, x, w_out, row(g_mix_post), row(g_x_pre), w_xq, kv, w_xo,
                      row(g_x_post), 2 * TOKEN_TILE)
    return _ffn_call(x2, row(g_ffn_pre), w_gate, w_up, w_down, row(g_ffn_post), TOKEN_TILE)
```

```python
import functools
import itertools
import math

import jax
import jax.numpy as jnp
from jax import lax
from jax.experimental import pallas as pl
from jax.experimental.pallas import tpu as pltpu

F32 = jnp.float32
BF16 = jnp.bfloat16

D_MODEL = 1024
MEM_LEN = 256
EPS = 1e-6
DA_HEADS = 4
DA_QK_DIM = 64
DA_V_DIM = 2 * DA_QK_DIM
DA_WIDTH = DA_HEADS * DA_V_DIM
QK_WIDTH = DA_HEADS * 2 * DA_QK_DIM
POOL_WINDOWS = (2, 4, 8, 16)
POOL_WIDTH = D_MODEL - DA_WIDTH
POOL_GROUP_DIM = POOL_WIDTH // len(POOL_WINDOWS)
ROPE_THETA = 500000.0
ROPE_DIM = DA_QK_DIM // 4
ROPE_HALF = ROPE_DIM // 2
X_HEADS = 4
X_HEAD_DIM = D_MODEL // X_HEADS
D_FF = -(-(8 * D_MODEL) // (3 * 256)) * 256
LAMBDA_INIT = 0.8 - 0.6 * math.exp(-0.3 * 0)
LOG2_E = math.log2(math.e)

LANES = 128
POOL_HALO = 16
VMEM_LIMIT = 56 * 1024 * 1024

TOKEN_TILE = 512
MIXOUT_GROUP = 256
ATTN_TILE = 512
ATTN_HEADS_PER_STEP = 2
FF_CHUNKS = ((0, 1536), (1536, 2816))
STAGE_BYTES = 4 * 1024 * 1024


def _inv_rms(x):
    return lax.rsqrt(jnp.mean(x * x, axis=-1, keepdims=True) + EPS)


def _rms(x, g):
    return x * _inv_rms(x) * g


def _prenorm_operand(x, g):
    return (x * g).astype(BF16), _inv_rms(x)


def _const_spec(shape):
    zeros = (0,) * len(shape)
    return pl.BlockSpec(shape, lambda *_: zeros, pipeline_mode=pl.Buffered(1))


_HBM = pl.BlockSpec(memory_space=pl.ANY)


def _stage_scratch(rows, cols):
    chunks = -(-(rows * cols * 4) // STAGE_BYTES)
    while rows % chunks or (rows // chunks) % 16:
        chunks += 1
    return [pltpu.VMEM((2, rows // chunks, cols), F32), pltpu.SemaphoreType.DMA((2,))]


def _load_weights(weights, stage_ref, sem_ref):
    rows = stage_ref.shape[1]
    jobs = []
    for w_hbm, w_vmem in weights:
        assert w_hbm.shape[0] % rows == 0 and stage_ref.shape[2] == w_hbm.shape[1]
        jobs += [(w_hbm, w_vmem, r) for r in range(0, w_hbm.shape[0], rows)]

    def copy(k):
        w_hbm, _, r = jobs[k]
        return pltpu.make_async_copy(w_hbm.at[pl.ds(r, rows), :], stage_ref.at[k % 2], sem_ref.at[k % 2])

    copy(0).start()
    for k, (_, w_vmem, r) in enumerate(jobs):
        if k + 1 < len(jobs):
            copy(k + 1).start()
        copy(k).wait()
        chunk = stage_ref[k % 2].astype(BF16)
        if len(w_vmem.shape) == 2:
            w_vmem[r:r + rows, :] = chunk
        else:
            width = w_vmem.shape[2]
            for part in range(w_vmem.shape[0]):
                w_vmem[part, r:r + rows, :] = chunk[:, part * width:(part + 1) * width]


def _run_staggered(chains):
    depth = len(chains[0])
    for time in range(depth + len(chains) - 1):
        for g, chain in enumerate(chains):
            if 0 <= time - g < depth:
                chain[time - g]()


def _first_step(grid_rank):
    first = pl.program_id(0) == 0
    for axis in range(1, grid_rank):
        first = first & (pl.program_id(axis) == 0)
    return first


def _inproj_kernel(x_ref, g_ref, w_hbm, pos_ref, freq_ref, wp_ref, ps_ref,
                   q_ref, k_ref, v_ref, po_ref, halo_ref, w_ref, stage, sem, *, tm):
    @pl.when(_first_step(2))
    def _():
        _load_weights([(w_hbm, w_ref)], stage, sem)

    i = pl.program_id(1)
    h = _rms(x_ref[0], g_ref[...]).astype(BF16)

    groups = LANES // ROPE_DIM
    rows = tm // groups
    lane = lax.broadcasted_iota(jnp.int32, (rows, LANES), 1)
    lane_group = lax.shift_right_logical(lane, int(math.log2(ROPE_DIM)))
    pos = jnp.broadcast_to(pos_ref[0].astype(F32), (LANES, tm)).T
    packed = jnp.zeros((rows, LANES), F32)
    for g in range(groups):
        packed = jnp.where(lane_group == g, pos[g * rows:(g + 1) * rows], packed)
    ang = packed * freq_ref[0:1, :]
    cos8, sin8 = jnp.cos(ang), jnp.sin(ang)
    first_map = lane < DA_QK_DIM
    rotary = freq_ref[1:2, :] != 0.0

    def spread(packed_table, g):
        shift = (LANES - ROPE_DIM * g) % LANES
        lo = pltpu.roll(packed_table, shift, 1) if shift else packed_table
        hi_shift = (shift + DA_QK_DIM) % LANES
        hi = pltpu.roll(packed_table, hi_shift, 1) if hi_shift else packed_table
        return jnp.where(first_map, lo, hi)

    cos_t = jnp.concatenate([jnp.where(rotary, spread(cos8, g), 1.0) for g in range(groups)], axis=0)
    sin_t = jnp.concatenate([spread(sin8, g) * freq_ref[1:2, :] for g in range(groups)], axis=0)
    first_half = freq_ref[1:2, :] < 0.0

    def rope(t, c, s):
        partner = jnp.where(first_half, pltpu.roll(t, LANES - ROPE_HALF, 1), pltpu.roll(t, ROPE_HALF, 1))
        return t * c + partner * s

    qk_scale = DA_QK_DIM ** -0.5 * LOG2_E
    cos_q, sin_q = cos_t * qk_scale, sin_t * qk_scale
    pq = jnp.dot(h, w_ref[:, 0:QK_WIDTH], preferred_element_type=F32)
    for j in range(DA_HEADS):
        sl = slice(j * LANES, (j + 1) * LANES)
        q_ref[j] = rope(pq[:, sl], cos_q, sin_q).astype(BF16)
    pk = jnp.dot(h, w_ref[:, QK_WIDTH:2 * QK_WIDTH], preferred_element_type=F32)
    for j in range(DA_HEADS):
        sl = slice(j * LANES, (j + 1) * LANES)
        k_ref[j] = rope(pk[:, sl], cos_t, sin_t).astype(BF16)
    pv = jnp.dot(h, w_ref[:, 2 * QK_WIDTH:2 * QK_WIDTH + DA_WIDTH], preferred_element_type=F32)
    for j in range(DA_HEADS):
        v_ref[j] = pv[:, j * DA_V_DIM:(j + 1) * DA_V_DIM].astype(BF16)
    u = jnp.dot(h, w_ref[:, 2 * QK_WIDTH + DA_WIDTH:], preferred_element_type=F32)

    @pl.when(i == 0)
    def _():
        halo_ref[...] = jnp.zeros_like(halo_ref)

    prev = halo_ref[...]
    halo_ref[...] = u[tm - POOL_HALO:, :]
    tpos = i * tm + lax.broadcasted_iota(jnp.int32, (tm, 1), 0)
    for g, w in enumerate(POOL_WINDOWS):
        sl = slice(g * POOL_GROUP_DIM, (g + 1) * POOL_GROUP_DIM)
        ug = u[:, sl]
        s = jnp.concatenate([prev[:, sl], ug], axis=0)
        d = 1
        while d < w:
            s = s + pltpu.roll(s, d, 0)
            d *= 2
        inv_count = 1.0 / jnp.minimum(tpos + 1, w).astype(F32)
        pooled = s[POOL_HALO:, :] * inv_count - ug
        po = jnp.dot(pooled.astype(BF16), wp_ref[g], preferred_element_type=F32) * ps_ref[:, sl]
        po_ref[0, :, sl] = po.astype(BF16)


def _inproj_call(x, g, w_in, pos, freq, w_pool, pool_scale, tm):
    b, s, d = x.shape
    tok = lambda n: pl.BlockSpec((1, tm, n), lambda bi, i: (bi, i, 0))
    heads = pl.BlockSpec((None, DA_HEADS, tm, DA_V_DIM), lambda bi, i: (bi, 0, i, 0))
    head_major = jax.ShapeDtypeStruct((b, DA_HEADS, s, DA_V_DIM), BF16)
    return pl.pallas_call(
        functools.partial(_inproj_kernel, tm=tm),
        grid=(b, s // tm),
        in_specs=[tok(d), _const_spec((1, d)), _HBM,
                  pl.BlockSpec((1, 1, tm), lambda bi, i: (bi, 0, i)), _const_spec(freq.shape),
                  _const_spec(w_pool.shape), _const_spec((1, POOL_WIDTH))],
        out_specs=[heads, heads, heads, tok(POOL_WIDTH)],
        out_shape=[head_major, head_major, head_major,
                   jax.ShapeDtypeStruct((b, s, POOL_WIDTH), BF16)],
        scratch_shapes=[pltpu.VMEM((POOL_HALO, POOL_WIDTH), F32), pltpu.VMEM(w_in.shape, BF16),
                        *_stage_scratch(d, w_in.shape[1])],
        compiler_params=pltpu.CompilerParams(
            dimension_semantics=("arbitrary", "arbitrary"), vmem_limit_bytes=VMEM_LIMIT),
        name="inproj",
    )(x, g, w_in, pos, freq, w_pool, pool_scale)


def _interleave(*streams):
    tagged = [((k + 0.5) / len(ops), n, op) for n, ops in enumerate(streams) for k, op in enumerate(ops)]
    return [op for _, _, op in sorted(tagged, key=lambda e: e[:2])]


def _tile_order(nq):
    def length(order):
        total = 0.0
        for step in range(nq + 2):
            scores = order[step] + 1 if step < nq else 0
            probs = order[step - 1] + 1 if 0 <= step - 1 < nq else 0
            values = (order[step - 2] + 1) / 2 if 0 <= step - 2 < nq else 0
            total += max(scores + values, probs)
        return total

    return min(itertools.permutations(range(nq)), key=length)


def _attn_kernel(q_ref, k_ref, v_ref, lq1_ref, lk1_ref, lq2_ref, lk2_ref, gs_ref, mem_ref, gmem_ref, wkv_hbm,
                 o_ref, kv_ref, *scratch, t, nq, heads):
    *scratch, wkv_ref, stage, sem = scratch

    @pl.when(_first_step(2))
    def _():
        _load_weights([(wkv_hbm, wkv_ref)], stage, sem)

    h = t // 2
    lane = lax.broadcasted_iota(jnp.int32, (h, DA_V_DIM), 1)
    causal = (lax.broadcasted_iota(jnp.int32, (h, h), 0) <= lax.broadcasted_iota(jnp.int32, (h, h), 1))
    causal = jnp.concatenate([causal, causal], axis=1)
    lam = (jnp.exp(jnp.sum(lq1_ref[...] * lk1_ref[...], axis=-1, keepdims=True))
           - jnp.exp(jnp.sum(lq2_ref[...] * lk2_ref[...], axis=-1, keepdims=True))
           + LAMBDA_INIT)
    gain = gs_ref[...] * (1.0 - LAMBDA_INIT)
    nt = (((1,), (1,)), ((), ()))
    tn = (((0,), (0,)), ((), ()))
    state = {}

    def chunks(i):
        n = 2 * (i + 1)
        return [(slice(c * h, (c + 1) * h), t if c == n - 1 else 0) for c in range(n)]

    def both(st, name, lo, value, combine):
        for half, part in ((0, value[:, :t]), (1, value[:, t:])) if lo == 0 else ((1, value),):
            key = (name, half)
            st[key] = part if key not in st else combine(st[key], part)

    def score_ops(hd, i):
        s_ref, st = scratch[i], state.setdefault((hd, i), {})

        def start():
            stacked = []
            for half in range(2):
                q = q_ref[hd, i * t + half * h:i * t + (half + 1) * h, :]
                zero = jnp.zeros_like(q)
                stacked += [jnp.where(lane < DA_QK_DIM, q, zero), jnp.where(lane >= DA_QK_DIM, q, zero)]
            st["qs"] = jnp.concatenate(stacked, axis=0)

        def one(c, rows, lo):
            s = lax.dot_general(k_ref[hd, rows, :], st["qs"][lo:, :], nt, preferred_element_type=F32)
            first_diag = 2 * i
            if c == first_diag:
                s = jnp.concatenate([jnp.where(causal, s[:, :t], -jnp.inf), s[:, t:]], axis=1)
            elif c == first_diag + 1:
                s = jnp.where(causal, s, -jnp.inf)
            s_ref[rows, lo:] = s
            both(st, "m", lo, jnp.max(s, axis=0, keepdims=True), jnp.maximum)

        return [start] + [functools.partial(one, c, rows, lo) for c, (rows, lo) in enumerate(chunks(i))]

    def prob_ops(hd, i):
        s_ref, st = scratch[i], state[hd, i]

        def one(rows, lo):
            m = st[("m", 1)] if lo else jnp.concatenate([st[("m", 0)], st[("m", 1)]], axis=1)
            p = jnp.exp2(s_ref[rows, lo:] - m)
            s_ref[rows, lo:] = p
            both(st, "l", lo, jnp.sum(p, axis=0, keepdims=True), jnp.add)

        return [functools.partial(one, rows, lo) for rows, lo in chunks(i)]

    def value_ops(hd, i):
        s_ref, st = scratch[i], state[hd, i]

        def combine(rows, half):
            l = st[("l", half)]
            ratio = lam * l[:, :h] / l[:, h:]
            base = half * t
            return s_ref[rows, base:base + h] - ratio * s_ref[rows, base + h:base + t]

        def one(rows, lo):
            halves = (1,) if lo else (0, 1)
            a = jnp.concatenate([combine(rows, half) for half in halves], axis=1).astype(BF16)
            pv = lax.dot_general(v_ref[hd, rows, :], a, tn, preferred_element_type=F32)
            for n, half in enumerate(halves):
                part = pv[:, n * h:(n + 1) * h]
                st["acc", half] = part if ("acc", half) not in st else st["acc", half] + part

        def finish():
            o = jnp.concatenate([st[("acc", half)] * (1.0 / st[("l", half)][:, :h]) for half in range(2)], axis=1)
            inv = lax.rsqrt(jnp.mean(o * o, axis=0, keepdims=True) + EPS)
            o_ref[hd, i * t:(i + 1) * t, :] = (o * inv * gain).T.astype(BF16)

        return [functools.partial(one, rows, lo) for rows, lo in chunks(i)] + [finish]

    def memory_ops():
        st = {}
        w_part = wkv_ref.at[pl.program_id(1)]

        def start():
            st["mem"] = _rms(mem_ref[...], gmem_ref[...]).astype(BF16)

        def one(cols):
            kv_ref[:, cols] = jnp.dot(st["mem"], w_part[:, cols], preferred_element_type=F32).astype(BF16)

        width = kv_ref.shape[1]
        return [start] + [functools.partial(one, slice(c, c + MEM_LEN)) for c in range(0, width, MEM_LEN)]

    order = [(hd, i) for hd in range(heads) for i in _tile_order(nq)]
    extra = memory_ops()
    for step in range(len(order) + 2):
        streams = [[extra.pop(0)]] if extra and step >= 1 else []
        if step < len(order):
            streams.append(score_ops(*order[step]))
        if 0 <= step - 1 < len(order):
            streams.append(prob_ops(*order[step - 1]))
        if 0 <= step - 2 < len(order):
            streams.append(value_ops(*order[step - 2]))
        for op in _interleave(*streams):
            op()


def _attn_call(q, k, v, lq1, lk1, lq2, lk2, g_subln_col, mem, g_mem, w_xkv, t):
    b, _, s, _ = q.shape
    _, m, d = mem.shape
    n = w_xkv.shape[1]
    nq = s // t
    heads = ATTN_HEADS_PER_STEP
    steps = DA_HEADS // heads
    width = n // steps
    assert n % steps == 0 and width % LANES == 0
    head = pl.BlockSpec((None, heads, s, DA_V_DIM), lambda bi, h: (bi, h, 0, 0))
    vec = _const_spec((1, DA_QK_DIM))
    return pl.pallas_call(
        functools.partial(_attn_kernel, t=t, nq=nq, heads=heads),
        grid=(b, steps),
        in_specs=[head, head, head, vec, vec, vec, vec, _const_spec((DA_V_DIM, 1)),
                  pl.BlockSpec((None, m, d), lambda bi, h: (bi, 0, 0)), _const_spec((1, d)), _HBM],
        out_specs=[head, pl.BlockSpec((None, m, width), lambda bi, h: (bi, 0, h))],
        out_shape=[jax.ShapeDtypeStruct((b, DA_HEADS, s, DA_V_DIM), BF16),
                   jax.ShapeDtypeStruct((b, m, n), BF16)],
        scratch_shapes=[pltpu.VMEM(((i + 1) * t, 2 * t), F32) for i in range(nq)]
        + [pltpu.VMEM((steps, d, width), BF16), *_stage_scratch(d, n)],
        compiler_params=pltpu.CompilerParams(
            dimension_semantics=("arbitrary", "arbitrary"), vmem_limit_bytes=VMEM_LIMIT),
        name="diff_attn",
    )(q, k, v, lq1, lk1, lq2, lk2, g_subln_col, mem, g_mem, w_xkv)


def _mixout_kernel(da_ref, po_ref, x_ref, wo_hbm, gpost_ref, gxpre_ref, wxq_hbm, kv_ref, wxo_hbm,
                   gxpost_ref, wg_in, wu_in, wd_in, o_ref, wg_out, wu_out, wd_out,
                   wo_ref, wxq_ref, wxo_ref, stage, sem):
    @pl.when(_first_step(2))
    def _():
        _load_weights([(wo_hbm, wo_ref), (wxq_hbm, wxq_ref), (wxo_hbm, wxo_ref)], stage, sem)

    for w_in, w_out in ((wg_in, wg_out), (wu_in, wu_out), (wd_in, wd_out)):
        w_out[...] = w_in[...].astype(BF16)

    tm = x_ref.shape[1]
    groups = [slice(r, r + MIXOUT_GROUP) for r in range(0, tm, MIXOUT_GROUP)]
    x_scale = X_HEAD_DIM ** -0.5

    def phases(rows):
        st = {}

        def mix():
            da = jnp.concatenate([da_ref[h, rows, :] for h in range(DA_HEADS)], axis=-1)
            st["mix"] = (jnp.dot(da, wo_ref[0:DA_WIDTH, :], preferred_element_type=F32)
                         + jnp.dot(po_ref[0, rows, :], wo_ref[DA_WIDTH:, :], preferred_element_type=F32))

        def norms():
            st["x1"] = x_ref[0, rows, :] + _rms(st.pop("mix"), gpost_ref[...])
            st["hq"], st["inv"] = _prenorm_operand(st["x1"], gxpre_ref[...])

        def queries():
            st["xq"] = (jnp.dot(st.pop("hq"), wxq_ref[...], preferred_element_type=F32)
                        * (st.pop("inv") * x_scale)).astype(BF16)

        def attend():
            heads = []
            for h in range(X_HEADS):
                sl = slice(h * X_HEAD_DIM, (h + 1) * X_HEAD_DIM)
                kh = kv_ref[0, :, sl]
                vh = kv_ref[0, :, D_MODEL + h * X_HEAD_DIM:D_MODEL + (h + 1) * X_HEAD_DIM]
                sc = lax.dot_general(st["xq"][:, sl], kh, (((1,), (1,)), ((), ())), preferred_element_type=F32)
                p = jnp.exp(sc - jnp.max(sc, axis=-1, keepdims=True))
                pm = (p * (1.0 / jnp.sum(p, axis=-1, keepdims=True))).astype(BF16)
                heads.append(jnp.dot(pm, vh, preferred_element_type=F32).astype(BF16))
            st["xo"] = jnp.concatenate(heads, axis=-1)

        def project():
            st["y"] = jnp.dot(st.pop("xo"), wxo_ref[...], preferred_element_type=F32)

        def finish():
            o_ref[0, rows, :] = st["x1"] + _rms(st["y"], gxpost_ref[...])

        return [mix, norms, queries, attend, project, finish]

    _run_staggered([phases(rows) for rows in groups])


def _mixout_call(da, po, x, w_out, g_mix_post, g_x_pre, w_xq, kv, w_xo, g_x_post, ffn_weights, tm):
    b, s, d = x.shape
    steps = s // tm
    tok = lambda n: pl.BlockSpec((1, tm, n), lambda bi, i: (bi, i, 0))
    gvec = _const_spec((1, d))

    def slab(w):
        rows = w.shape[0] // (b * steps)
        assert rows * b * steps == w.shape[0] and rows % 16 == 0
        return pl.BlockSpec((rows, w.shape[1]), lambda bi, i: (bi * steps + i, 0))

    return pl.pallas_call(
        _mixout_kernel,
        grid=(b, steps),
        in_specs=[pl.BlockSpec((None, DA_HEADS, tm, DA_V_DIM), lambda bi, i: (bi, 0, i, 0)),
                  tok(POOL_WIDTH), tok(d), _HBM, gvec, gvec, _HBM,
                  pl.BlockSpec((1,) + kv.shape[1:], lambda bi, i: (bi, 0, 0)),
                  _HBM, gvec, *[slab(w) for w in ffn_weights]],
        out_specs=[tok(d), *[slab(w) for w in ffn_weights]],
        out_shape=[jax.ShapeDtypeStruct((b, s, d), F32),
                   *[jax.ShapeDtypeStruct(w.shape, BF16) for w in ffn_weights]],
        scratch_shapes=[pltpu.VMEM(w_out.shape, BF16), pltpu.VMEM(w_xq.shape, BF16),
                        pltpu.VMEM(w_xo.shape, BF16), *_stage_scratch(d, d)],
        compiler_params=pltpu.CompilerParams(
            dimension_semantics=("arbitrary", "arbitrary"), vmem_limit_bytes=VMEM_LIMIT),
        name="mixout_xattn",
    )(da, po, x, w_out, g_mix_post, g_x_pre, w_xq, kv, w_xo, g_x_post, *ffn_weights)


def _ffn_kernel(x_ref, gpre_ref, wg_ref, wu_ref, wd_ref, gpost_ref, o_ref):
    tm = x_ref.shape[1]

    def phases(rows):
        st = {}

        def prenorm():
            st["x"] = x_ref[0, rows, :]
            st["hf"], st["inv"] = _prenorm_operand(st["x"], gpre_ref[...])

        def chunk(lo, hi):
            gate = jnp.dot(st["hf"], wg_ref[:, lo:hi], preferred_element_type=F32) * st["inv"]
            up = jnp.dot(st["hf"], wu_ref[:, lo:hi], preferred_element_type=F32) * st["inv"]
            act = (gate * (1.0 / (1.0 + jnp.exp(-gate))) * up).astype(BF16)
            part = jnp.dot(act, wd_ref[lo:hi, :], preferred_element_type=F32)
            st["ff"] = part if "ff" not in st else st["ff"] + part

        def finish():
            o_ref[0, rows, :] = st["x"] + _rms(st["ff"], gpost_ref[...])

        return [prenorm] + [functools.partial(chunk, lo, hi) for lo, hi in FF_CHUNKS] + [finish]

    _run_staggered([phases(slice(r, r + tm // 2)) for r in (0, tm // 2)])


def _ffn_call(x, g_pre, w_gate, w_up, w_down, g_post, tm):
    b, s, d = x.shape
    tok = pl.BlockSpec((1, tm, d), lambda bi, i: (bi, i, 0))
    gvec = _const_spec((1, d))
    return pl.pallas_call(
        _ffn_kernel,
        grid=(b, s // tm),
        in_specs=[tok, gvec, _const_spec(w_gate.shape), _const_spec(w_up.shape),
                  _const_spec(w_down.shape), gvec],
        out_specs=tok,
        out_shape=jax.ShapeDtypeStruct((b, s, d), F32),
        compiler_params=pltpu.CompilerParams(
            dimension_semantics=("arbitrary", "arbitrary"), vmem_limit_bytes=VMEM_LIMIT),
        name="swiglu",
    )(x, g_pre, w_gate, w_up, w_down, g_post)


def _rope_lane_constants():
    inv_freq = ROPE_THETA ** (-jnp.arange(0, ROPE_DIM, 2, dtype=F32) / ROPE_DIM)
    packed_freq = jnp.tile(inv_freq, LANES // ROPE_HALF)
    one_half = jnp.ones((ROPE_HALF,), F32)
    per_map = jnp.concatenate([-one_half, one_half, jnp.zeros((DA_QK_DIM - ROPE_DIM,), F32)])
    sign = jnp.concatenate([per_map, per_map])
    return jnp.concatenate([jnp.stack([packed_freq, sign]), jnp.zeros((6, LANES), F32)], axis=0)


def kernel(x, mem, positions, g_mix_pre, w_in, lambda_q1, lambda_k1, lambda_q2, lambda_k2,
           g_subln, w_pool, pool_scale, w_out, g_mix_post, g_x_pre, g_mem, w_xq, w_xkv, w_xo,
           g_x_post, g_ffn_pre, w_gate, w_up, w_down, g_ffn_post):
    b, s, d = x.shape
    assert d == D_MODEL and s % TOKEN_TILE == 0 and s % ATTN_TILE == 0
    row = lambda a: a.reshape(1, -1).astype(F32)
    bf = lambda a: a.astype(BF16)

    pos = positions.reshape(b, 1, s)
    q, k, v, po = _inproj_call(x, row(g_mix_pre), w_in, pos, _rope_lane_constants(),
                               bf(w_pool), row(pool_scale), 2 * TOKEN_TILE)
    da, kv = _attn_call(q, k, v, row(lambda_q1), row(lambda_k1), row(lambda_q2), row(lambda_k2),
                        g_subln.reshape(-1, 1).astype(F32), mem, row(g_mem), w_xkv, ATTN_TILE)
    x2, *ffn_weights = _mixout_call(da, po, x, w_out, row(g_mix_post), row(g_x_pre), w_xq, kv, w_xo,
                                    row(g_x_post), (w_gate, w_up, w_down), 2 * TOKEN_TILE)
    return _ffn_call(x2, row(g_ffn_pre), *ffn_weights, row(g_ffn_post), TOKEN_TILE)
```

```python
import functools
import itertools
import math

import jax
import jax.numpy as jnp
from jax import lax
from jax.experimental import pallas as pl
from jax.experimental.pallas import tpu as pltpu

F32 = jnp.float32
BF16 = jnp.bfloat16

D_MODEL = 1024
MEM_LEN = 256
EPS = 1e-6
DA_HEADS = 4
DA_QK_DIM = 64
DA_V_DIM = 2 * DA_QK_DIM
DA_WIDTH = DA_HEADS * DA_V_DIM
QK_WIDTH = DA_HEADS * 2 * DA_QK_DIM
POOL_WINDOWS = (2, 4, 8, 16)
POOL_WIDTH = D_MODEL - DA_WIDTH
POOL_GROUP_DIM = POOL_WIDTH // len(POOL_WINDOWS)
ROPE_THETA = 500000.0
ROPE_DIM = DA_QK_DIM // 4
ROPE_HALF = ROPE_DIM // 2
X_HEADS = 4
X_HEAD_DIM = D_MODEL // X_HEADS
D_FF = -(-(8 * D_MODEL) // (3 * 256)) * 256
LAMBDA_INIT = 0.8 - 0.6 * math.exp(-0.3 * 0)
LOG2_E = math.log2(math.e)

LANES = 128
POOL_HALO = 16
VMEM_LIMIT = 56 * 1024 * 1024

TOKEN_TILE = 512
MIXOUT_GROUP = 256
ATTN_TILE = 512
ATTN_HEADS_PER_STEP = 2
FF_CHUNKS = ((0, 1536), (1536, 2816))
STAGE_BYTES = 4 * 1024 * 1024


def _inv_rms(x):
    return lax.rsqrt(jnp.mean(x * x, axis=-1, keepdims=True) + EPS)


def _rms(x, g):
    return x * _inv_rms(x) * g


def _prenorm_operand(x, g):
    return (x * g).astype(BF16), _inv_rms(x)


def _const_spec(shape):
    zeros = (0,) * len(shape)
    return pl.BlockSpec(shape, lambda *_: zeros, pipeline_mode=pl.Buffered(1))


_HBM = pl.BlockSpec(memory_space=pl.ANY)


def _stage_scratch(rows, cols):
    chunks = -(-(rows * cols * 4) // STAGE_BYTES)
    while rows % chunks or (rows // chunks) % 16:
        chunks += 1
    return [pltpu.VMEM((2, rows // chunks, cols), F32), pltpu.SemaphoreType.DMA((2,))]


def _load_weights(weights, stage_ref, sem_ref):
    rows = stage_ref.shape[1]
    jobs = []
    for w_hbm, w_vmem in weights:
        assert w_hbm.shape[0] % rows == 0 and stage_ref.shape[2] == w_hbm.shape[1]
        jobs += [(w_hbm, w_vmem, r) for r in range(0, w_hbm.shape[0], rows)]

    def copy(k):
        w_hbm, _, r = jobs[k]
        return pltpu.make_async_copy(w_hbm.at[pl.ds(r, rows), :], stage_ref.at[k % 2], sem_ref.at[k % 2])

    copy(0).start()
    for k, (_, w_vmem, r) in enumerate(jobs):
        if k + 1 < len(jobs):
            copy(k + 1).start()
        copy(k).wait()
        w_vmem[r:r + rows, :] = stage_ref[k % 2].astype(BF16)


def _cast_slabs(pairs):
    for w_in, w_out in pairs:
        slab = w_in[...].astype(BF16)
        if len(w_out.shape) == 2:
            w_out[...] = slab
        else:
            width = w_out.shape[2]
            for part in range(w_out.shape[0]):
                w_out[part] = slab[:, part * width:(part + 1) * width]


def _slab_specs(weights, grid, parts=1):
    n_steps = grid[0] * grid[1]
    step = lambda bi, i: bi * grid[1] + i
    ins, outs, shapes = [], [], []
    for w in weights:
        rows, cols = w.shape[0] // n_steps, w.shape[1]
        assert rows * n_steps == w.shape[0] and rows % 16 == 0 and cols % (parts * LANES) == 0
        ins.append(pl.BlockSpec((rows, cols), lambda bi, i: (step(bi, i), 0)))
        if parts == 1:
            outs.append(pl.BlockSpec((rows, cols), lambda bi, i: (step(bi, i), 0)))
            shapes.append(jax.ShapeDtypeStruct(w.shape, BF16))
        else:
            outs.append(pl.BlockSpec((parts, rows, cols // parts), lambda bi, i: (0, step(bi, i), 0)))
            shapes.append(jax.ShapeDtypeStruct((parts, w.shape[0], cols // parts), BF16))
    return ins, outs, shapes


def _run_staggered(chains):
    depth = len(chains[0])
    for time in range(depth + len(chains) - 1):
        for g, chain in enumerate(chains):
            if 0 <= time - g < depth:
                chain[time - g]()


def _first_step(grid_rank):
    first = pl.program_id(0) == 0
    for axis in range(1, grid_rank):
        first = first & (pl.program_id(axis) == 0)
    return first


def _inproj_kernel(x_ref, g_ref, w_hbm, pos_ref, freq_ref, wp_ref, ps_ref, wkv_in,
                   q_ref, k_ref, v_ref, po_ref, wkv_out, halo_ref, w_ref, stage, sem, *, tm):
    @pl.when(_first_step(2))
    def _():
        _load_weights([(w_hbm, w_ref)], stage, sem)

    _cast_slabs([(wkv_in, wkv_out)])

    i = pl.program_id(1)
    h = _rms(x_ref[0], g_ref[...]).astype(BF16)

    groups = LANES // ROPE_DIM
    rows = tm // groups
    lane = lax.broadcasted_iota(jnp.int32, (rows, LANES), 1)
    lane_group = lax.shift_right_logical(lane, int(math.log2(ROPE_DIM)))
    pos = jnp.broadcast_to(pos_ref[0].astype(F32), (LANES, tm)).T
    packed = jnp.zeros((rows, LANES), F32)
    for g in range(groups):
        packed = jnp.where(lane_group == g, pos[g * rows:(g + 1) * rows], packed)
    ang = packed * freq_ref[0:1, :]
    cos8, sin8 = jnp.cos(ang), jnp.sin(ang)
    first_map = lane < DA_QK_DIM
    rotary = freq_ref[1:2, :] != 0.0

    def spread(packed_table, g):
        shift = (LANES - ROPE_DIM * g) % LANES
        lo = pltpu.roll(packed_table, shift, 1) if shift else packed_table
        hi_shift = (shift + DA_QK_DIM) % LANES
        hi = pltpu.roll(packed_table, hi_shift, 1) if hi_shift else packed_table
        return jnp.where(first_map, lo, hi)

    cos_t = jnp.concatenate([jnp.where(rotary, spread(cos8, g), 1.0) for g in range(groups)], axis=0)
    sin_t = jnp.concatenate([spread(sin8, g) * freq_ref[1:2, :] for g in range(groups)], axis=0)
    first_half = freq_ref[1:2, :] < 0.0

    def rope(t, c, s):
        partner = jnp.where(first_half, pltpu.roll(t, LANES - ROPE_HALF, 1), pltpu.roll(t, ROPE_HALF, 1))
        return t * c + partner * s

    qk_scale = DA_QK_DIM ** -0.5 * LOG2_E
    cos_q, sin_q = cos_t * qk_scale, sin_t * qk_scale
    pq = jnp.dot(h, w_ref[:, 0:QK_WIDTH], preferred_element_type=F32)
    for j in range(DA_HEADS):
        sl = slice(j * LANES, (j + 1) * LANES)
        q_ref[j] = rope(pq[:, sl], cos_q, sin_q).astype(BF16)
    pk = jnp.dot(h, w_ref[:, QK_WIDTH:2 * QK_WIDTH], preferred_element_type=F32)
    for j in range(DA_HEADS):
        sl = slice(j * LANES, (j + 1) * LANES)
        k_ref[j] = rope(pk[:, sl], cos_t, sin_t).astype(BF16)
    pv = jnp.dot(h, w_ref[:, 2 * QK_WIDTH:2 * QK_WIDTH + DA_WIDTH], preferred_element_type=F32)
    for j in range(DA_HEADS):
        v_ref[j] = pv[:, j * DA_V_DIM:(j + 1) * DA_V_DIM].astype(BF16)
    u = jnp.dot(h, w_ref[:, 2 * QK_WIDTH + DA_WIDTH:], preferred_element_type=F32)

    @pl.when(i == 0)
    def _():
        halo_ref[...] = jnp.zeros_like(halo_ref)

    prev = halo_ref[...]
    halo_ref[...] = u[tm - POOL_HALO:, :]
    tpos = i * tm + lax.broadcasted_iota(jnp.int32, (tm, 1), 0)
    for g, w in enumerate(POOL_WINDOWS):
        sl = slice(g * POOL_GROUP_DIM, (g + 1) * POOL_GROUP_DIM)
        ug = u[:, sl]
        s = jnp.concatenate([prev[:, sl], ug], axis=0)
        d = 1
        while d < w:
            s = s + pltpu.roll(s, d, 0)
            d *= 2
        inv_count = 1.0 / jnp.minimum(tpos + 1, w).astype(F32)
        pooled = s[POOL_HALO:, :] * inv_count - ug
        po = jnp.dot(pooled.astype(BF16), wp_ref[g], preferred_element_type=F32) * ps_ref[:, sl]
        po_ref[0, :, sl] = po.astype(BF16)


def _inproj_call(x, g, w_in, pos, freq, w_pool, pool_scale, w_xkv, kv_parts, tm):
    b, s, d = x.shape
    grid = (b, s // tm)
    cast_in, cast_out, cast_shape = _slab_specs([w_xkv], grid, parts=kv_parts)
    tok = lambda n: pl.BlockSpec((1, tm, n), lambda bi, i: (bi, i, 0))
    heads = pl.BlockSpec((None, DA_HEADS, tm, DA_V_DIM), lambda bi, i: (bi, 0, i, 0))
    head_major = jax.ShapeDtypeStruct((b, DA_HEADS, s, DA_V_DIM), BF16)
    return pl.pallas_call(
        functools.partial(_inproj_kernel, tm=tm),
        grid=grid,
        in_specs=[tok(d), _const_spec((1, d)), _HBM,
                  pl.BlockSpec((1, 1, tm), lambda bi, i: (bi, 0, i)), _const_spec(freq.shape),
                  _const_spec(w_pool.shape), _const_spec((1, POOL_WIDTH)), *cast_in],
        out_specs=[heads, heads, heads, tok(POOL_WIDTH), *cast_out],
        out_shape=[head_major, head_major, head_major,
                   jax.ShapeDtypeStruct((b, s, POOL_WIDTH), BF16), *cast_shape],
        scratch_shapes=[pltpu.VMEM((POOL_HALO, POOL_WIDTH), F32), pltpu.VMEM(w_in.shape, BF16),
                        *_stage_scratch(d, w_in.shape[1])],
        compiler_params=pltpu.CompilerParams(
            dimension_semantics=("arbitrary", "arbitrary"), vmem_limit_bytes=VMEM_LIMIT),
        name="inproj",
    )(x, g, w_in, pos, freq, w_pool, pool_scale, w_xkv)


def _interleave(*streams):
    tagged = [((k + 0.5) / len(ops), n, op) for n, ops in enumerate(streams) for k, op in enumerate(ops)]
    return [op for _, _, op in sorted(tagged, key=lambda e: e[:2])]


def _tile_order(nq):
    def length(order):
        total = 0.0
        for step in range(nq + 2):
            scores = order[step] + 1 if step < nq else 0
            probs = order[step - 1] + 1 if 0 <= step - 1 < nq else 0
            values = (order[step - 2] + 1) / 2 if 0 <= step - 2 < nq else 0
            total += max(scores + values, probs)
        return total

    return min(itertools.permutations(range(nq)), key=length)


def _attn_kernel(q_ref, k_ref, v_ref, lq1_ref, lk1_ref, lq2_ref, lk2_ref, gs_ref, mem_ref, gmem_ref, wkv_ref,
                 wo_in, wxq_in, wxo_in, o_ref, kv_ref, wo_out, wxq_out, wxo_out, *scratch, t, nq, heads):
    _cast_slabs([(wo_in, wo_out), (wxq_in, wxq_out), (wxo_in, wxo_out)])

    h = t // 2
    lane = lax.broadcasted_iota(jnp.int32, (h, DA_V_DIM), 1)
    causal = (lax.broadcasted_iota(jnp.int32, (h, h), 0) <= lax.broadcasted_iota(jnp.int32, (h, h), 1))
    causal = jnp.concatenate([causal, causal], axis=1)
    lam = (jnp.exp(jnp.sum(lq1_ref[...] * lk1_ref[...], axis=-1, keepdims=True))
           - jnp.exp(jnp.sum(lq2_ref[...] * lk2_ref[...], axis=-1, keepdims=True))
           + LAMBDA_INIT)
    gain = gs_ref[...] * (1.0 - LAMBDA_INIT)
    nt = (((1,), (1,)), ((), ()))
    tn = (((0,), (0,)), ((), ()))
    state = {}

    def chunks(i):
        n = 2 * (i + 1)
        return [(slice(c * h, (c + 1) * h), t if c == n - 1 else 0) for c in range(n)]

    def both(st, name, lo, value, combine):
        for half, part in ((0, value[:, :t]), (1, value[:, t:])) if lo == 0 else ((1, value),):
            key = (name, half)
            st[key] = part if key not in st else combine(st[key], part)

    def score_ops(hd, i):
        s_ref, st = scratch[i], state.setdefault((hd, i), {})

        def start():
            stacked = []
            for half in range(2):
                q = q_ref[hd, i * t + half * h:i * t + (half + 1) * h, :]
                zero = jnp.zeros_like(q)
                stacked += [jnp.where(lane < DA_QK_DIM, q, zero), jnp.where(lane >= DA_QK_DIM, q, zero)]
            st["qs"] = jnp.concatenate(stacked, axis=0)

        def one(c, rows, lo):
            s = lax.dot_general(k_ref[hd, rows, :], st["qs"][lo:, :], nt, preferred_element_type=F32)
            first_diag = 2 * i
            if c == first_diag:
                s = jnp.concatenate([jnp.where(causal, s[:, :t], -jnp.inf), s[:, t:]], axis=1)
            elif c == first_diag + 1:
                s = jnp.where(causal, s, -jnp.inf)
            s_ref[rows, lo:] = s
            both(st, "m", lo, jnp.max(s, axis=0, keepdims=True), jnp.maximum)

        return [start] + [functools.partial(one, c, rows, lo) for c, (rows, lo) in enumerate(chunks(i))]

    def prob_ops(hd, i):
        s_ref, st = scratch[i], state[hd, i]

        def one(rows, lo):
            m = st[("m", 1)] if lo else jnp.concatenate([st[("m", 0)], st[("m", 1)]], axis=1)
            p = jnp.exp2(s_ref[rows, lo:] - m)
            s_ref[rows, lo:] = p
            both(st, "l", lo, jnp.sum(p, axis=0, keepdims=True), jnp.add)

        return [functools.partial(one, rows, lo) for rows, lo in chunks(i)]

    def value_ops(hd, i):
        s_ref, st = scratch[i], state[hd, i]

        def combine(rows, half):
            l = st[("l", half)]
            ratio = lam * l[:, :h] / l[:, h:]
            base = half * t
            return s_ref[rows, base:base + h] - ratio * s_ref[rows, base + h:base + t]

        def one(rows, lo):
            halves = (1,) if lo else (0, 1)
            a = jnp.concatenate([combine(rows, half) for half in halves], axis=1).astype(BF16)
            pv = lax.dot_general(v_ref[hd, rows, :], a, tn, preferred_element_type=F32)
            for n, half in enumerate(halves):
                part = pv[:, n * h:(n + 1) * h]
                st["acc", half] = part if ("acc", half) not in st else st["acc", half] + part

        def finish():
            o = jnp.concatenate([st[("acc", half)] * (1.0 / st[("l", half)][:, :h]) for half in range(2)], axis=1)
            inv = lax.rsqrt(jnp.mean(o * o, axis=0, keepdims=True) + EPS)
            o_ref[hd, i * t:(i + 1) * t, :] = (o * inv * gain).T.astype(BF16)

        return [functools.partial(one, rows, lo) for rows, lo in chunks(i)] + [finish]

    def memory_ops():
        st = {}
        w_part = wkv_ref.at[pl.program_id(1)]

        def start():
            st["mem"] = _rms(mem_ref[...], gmem_ref[...]).astype(BF16)

        def one(cols):
            kv_ref[:, cols] = jnp.dot(st["mem"], w_part[:, cols], preferred_element_type=F32).astype(BF16)

        width = kv_ref.shape[1]
        return [start] + [functools.partial(one, slice(c, c + MEM_LEN)) for c in range(0, width, MEM_LEN)]

    order = [(hd, i) for hd in range(heads) for i in _tile_order(nq)]
    extra = memory_ops()
    for step in range(len(order) + 2):
        streams = [[extra.pop(0)]] if extra and step >= 1 else []
        if step < len(order):
            streams.append(score_ops(*order[step]))
        if 0 <= step - 1 < len(order):
            streams.append(prob_ops(*order[step - 1]))
        if 0 <= step - 2 < len(order):
            streams.append(value_ops(*order[step - 2]))
        for op in _interleave(*streams):
            op()


def _attn_call(q, k, v, lq1, lk1, lq2, lk2, g_subln_col, mem, g_mem, w_xkv_parts, later_weights, t):
    b, _, s, _ = q.shape
    _, m, d = mem.shape
    nq = s // t
    heads = ATTN_HEADS_PER_STEP
    steps, _, width = w_xkv_parts.shape
    assert steps == DA_HEADS // heads
    cast_in, cast_out, cast_shape = _slab_specs(later_weights, (b, steps))
    head = pl.BlockSpec((None, heads, s, DA_V_DIM), lambda bi, h: (bi, h, 0, 0))
    vec = _const_spec((1, DA_QK_DIM))
    return pl.pallas_call(
        functools.partial(_attn_kernel, t=t, nq=nq, heads=heads),
        grid=(b, steps),
        in_specs=[head, head, head, vec, vec, vec, vec, _const_spec((DA_V_DIM, 1)),
                  pl.BlockSpec((None, m, d), lambda bi, h: (bi, 0, 0)), _const_spec((1, d)),
                  _const_spec(w_xkv_parts.shape), *cast_in],
        out_specs=[head, pl.BlockSpec((None, m, width), lambda bi, h: (bi, 0, h)), *cast_out],
        out_shape=[jax.ShapeDtypeStruct((b, DA_HEADS, s, DA_V_DIM), BF16),
                   jax.ShapeDtypeStruct((b, m, steps * width), BF16), *cast_shape],
        scratch_shapes=[pltpu.VMEM(((i + 1) * t, 2 * t), F32) for i in range(nq)],
        compiler_params=pltpu.CompilerParams(
            dimension_semantics=("arbitrary", "arbitrary"), vmem_limit_bytes=VMEM_LIMIT),
        name="diff_attn",
    )(q, k, v, lq1, lk1, lq2, lk2, g_subln_col, mem, g_mem, w_xkv_parts, *later_weights)


def _mixout_kernel(da_ref, po_ref, x_ref, wo_ref, gpost_ref, gxpre_ref, wxq_ref, kv_ref, wxo_ref,
                   gxpost_ref, wg_in, wu_in, wd_in, o_ref, wg_out, wu_out, wd_out):
    _cast_slabs([(wg_in, wg_out), (wu_in, wu_out), (wd_in, wd_out)])

    tm = x_ref.shape[1]
    groups = [slice(r, r + MIXOUT_GROUP) for r in range(0, tm, MIXOUT_GROUP)]
    x_scale = X_HEAD_DIM ** -0.5

    def phases(rows):
        st = {}

        def mix():
            da = jnp.concatenate([da_ref[h, rows, :] for h in range(DA_HEADS)], axis=-1)
            st["mix"] = (jnp.dot(da, wo_ref[0:DA_WIDTH, :], preferred_element_type=F32)
                         + jnp.dot(po_ref[0, rows, :], wo_ref[DA_WIDTH:, :], preferred_element_type=F32))

        def norms():
            st["x1"] = x_ref[0, rows, :] + _rms(st.pop("mix"), gpost_ref[...])
            st["hq"], st["inv"] = _prenorm_operand(st["x1"], gxpre_ref[...])

        def queries():
            st["xq"] = (jnp.dot(st.pop("hq"), wxq_ref[...], preferred_element_type=F32)
                        * (st.pop("inv") * x_scale)).astype(BF16)

        def attend():
            heads = []
            for h in range(X_HEADS):
                sl = slice(h * X_HEAD_DIM, (h + 1) * X_HEAD_DIM)
                kh = kv_ref[0, :, sl]
                vh = kv_ref[0, :, D_MODEL + h * X_HEAD_DIM:D_MODEL + (h + 1) * X_HEAD_DIM]
                sc = lax.dot_general(st["xq"][:, sl], kh, (((1,), (1,)), ((), ())), preferred_element_type=F32)
                p = jnp.exp(sc - jnp.max(sc, axis=-1, keepdims=True))
                pm = (p * (1.0 / jnp.sum(p, axis=-1, keepdims=True))).astype(BF16)
                heads.append(jnp.dot(pm, vh, preferred_element_type=F32).astype(BF16))
            st["xo"] = jnp.concatenate(heads, axis=-1)

        def project():
            st["y"] = jnp.dot(st.pop("xo"), wxo_ref[...], preferred_element_type=F32)

        def finish():
            o_ref[0, rows, :] = st["x1"] + _rms(st["y"], gxpost_ref[...])

        return [mix, norms, queries, attend, project, finish]

    _run_staggered([phases(rows) for rows in groups])


def _mixout_call(da, po, x, w_out, g_mix_post, g_x_pre, w_xq, kv, w_xo, g_x_post, ffn_weights, tm):
    b, s, d = x.shape
    grid = (b, s // tm)
    cast_in, cast_out, cast_shape = _slab_specs(ffn_weights, grid)
    tok = lambda n: pl.BlockSpec((1, tm, n), lambda bi, i: (bi, i, 0))
    gvec = _const_spec((1, d))
    return pl.pallas_call(
        _mixout_kernel,
        grid=grid,
        in_specs=[pl.BlockSpec((None, DA_HEADS, tm, DA_V_DIM), lambda bi, i: (bi, 0, i, 0)),
                  tok(POOL_WIDTH), tok(d), _const_spec(w_out.shape), gvec, gvec, _const_spec(w_xq.shape),
                  pl.BlockSpec((1,) + kv.shape[1:], lambda bi, i: (bi, 0, 0)),
                  _const_spec(w_xo.shape), gvec, *cast_in],
        out_specs=[tok(d), *cast_out],
        out_shape=[jax.ShapeDtypeStruct((b, s, d), F32), *cast_shape],
        compiler_params=pltpu.CompilerParams(
            dimension_semantics=("arbitrary", "arbitrary"), vmem_limit_bytes=VMEM_LIMIT),
        name="mixout_xattn",
    )(da, po, x, w_out, g_mix_post, g_x_pre, w_xq, kv, w_xo, g_x_post, *ffn_weights)


def _ffn_kernel(x_ref, gpre_ref, wg_ref, wu_ref, wd_ref, gpost_ref, o_ref):
    tm = x_ref.shape[1]

    def phases(rows):
        st = {}

        def prenorm():
            st["x"] = x_ref[0, rows, :]
            st["hf"], st["inv"] = _prenorm_operand(st["x"], gpre_ref[...])

        def chunk(lo, hi):
            gate = jnp.dot(st["hf"], wg_ref[:, lo:hi], preferred_element_type=F32) * st["inv"]
            up = jnp.dot(st["hf"], wu_ref[:, lo:hi], preferred_element_type=F32) * st["inv"]
            act = (gate * (1.0 / (1.0 + jnp.exp(-gate))) * up).astype(BF16)
            part = jnp.dot(act, wd_ref[lo:hi, :], preferred_element_type=F32)
            st["ff"] = part if "ff" not in st else st["ff"] + part

        def finish():
            o_ref[0, rows, :] = st["x"] + _rms(st["ff"], gpost_ref[...])

        return [prenorm] + [functools.partial(chunk, lo, hi) for lo, hi in FF_CHUNKS] + [finish]

    _run_staggered([phases(slice(r, r + tm // 2)) for r in (0, tm // 2)])


def _ffn_call(x, g_pre, w_gate, w_up, w_down, g_post, tm):
    b, s, d = x.shape
    tok = pl.BlockSpec((1, tm, d), lambda bi, i: (bi, i, 0))
    gvec = _const_spec((1, d))
    return pl.pallas_call(
        _ffn_kernel,
        grid=(b, s // tm),
        in_specs=[tok, gvec, _const_spec(w_gate.shape), _const_spec(w_up.shape),
                  _const_spec(w_down.shape), gvec],
        out_specs=tok,
        out_shape=jax.ShapeDtypeStruct((b, s, d), F32),
        compiler_params=pltpu.CompilerParams(
            dimension_semantics=("arbitrary", "arbitrary"), vmem_limit_bytes=VMEM_LIMIT),
        name="swiglu",
    )(x, g_pre, w_gate, w_up, w_down, g_post)


def _rope_lane_constants():
    inv_freq = ROPE_THETA ** (-jnp.arange(0, ROPE_DIM, 2, dtype=F32) / ROPE_DIM)
    packed_freq = jnp.tile(inv_freq, LANES // ROPE_HALF)
    one_half = jnp.ones((ROPE_HALF,), F32)
    per_map = jnp.concatenate([-one_half, one_half, jnp.zeros((DA_QK_DIM - ROPE_DIM,), F32)])
    sign = jnp.concatenate([per_map, per_map])
    return jnp.concatenate([jnp.stack([packed_freq, sign]), jnp.zeros((6, LANES), F32)], axis=0)


def kernel(x, mem, positions, g_mix_pre, w_in, lambda_q1, lambda_k1, lambda_q2, lambda_k2,
           g_subln, w_pool, pool_scale, w_out, g_mix_post, g_x_pre, g_mem, w_xq, w_xkv, w_xo,
           g_x_post, g_ffn_pre, w_gate, w_up, w_down, g_ffn_post):
    b, s, d = x.shape
    assert d == D_MODEL and s % TOKEN_TILE == 0 and s % ATTN_TILE == 0
    row = lambda a: a.reshape(1, -1).astype(F32)
    bf = lambda a: a.astype(BF16)

    pos = positions.reshape(b, 1, s)
    q, k, v, po, w_xkv_parts = _inproj_call(x, row(g_mix_pre), w_in, pos, _rope_lane_constants(), bf(w_pool),
                                            row(pool_scale), w_xkv, DA_HEADS // ATTN_HEADS_PER_STEP,
                                            2 * TOKEN_TILE)
    da, kv, *mix_weights = _attn_call(q, k, v, row(lambda_q1), row(lambda_k1), row(lambda_q2),
                                      row(lambda_k2), g_subln.reshape(-1, 1).astype(F32), mem, row(g_mem),
                                      w_xkv_parts, (w_out, w_xq, w_xo), ATTN_TILE)
    w_out, w_xq, w_xo = mix_weights
    x2, *ffn_weights = _mixout_call(da, po, x, w_out, row(g_mix_post), row(g_x_pre), w_xq, kv, w_xo,
                                    row(g_x_post), (w_gate, w_up, w_down), 2 * TOKEN_TILE)
    return _ffn_call(x2, row(g_ffn_pre), *ffn_weights, row(g_ffn_post), TOKEN_TILE)
```

```python
import functools
import itertools
import math

import jax
import jax.numpy as jnp
from jax import lax
from jax.experimental import pallas as pl
from jax.experimental.pallas import tpu as pltpu

F32 = jnp.float32
BF16 = jnp.bfloat16

D_MODEL = 1024
MEM_LEN = 256
EPS = 1e-6
DA_HEADS = 4
DA_QK_DIM = 64
DA_V_DIM = 2 * DA_QK_DIM
DA_WIDTH = DA_HEADS * DA_V_DIM
QK_WIDTH = DA_HEADS * 2 * DA_QK_DIM
POOL_WINDOWS = (2, 4, 8, 16)
POOL_WIDTH = D_MODEL - DA_WIDTH
POOL_GROUP_DIM = POOL_WIDTH // len(POOL_WINDOWS)
ROPE_THETA = 500000.0
ROPE_DIM = DA_QK_DIM // 4
ROPE_HALF = ROPE_DIM // 2
X_HEADS = 4
X_HEAD_DIM = D_MODEL // X_HEADS
D_FF = -(-(8 * D_MODEL) // (3 * 256)) * 256
LAMBDA_INIT = 0.8 - 0.6 * math.exp(-0.3 * 0)
LOG2_E = math.log2(math.e)

LANES = 128
POOL_HALO = 16
VMEM_LIMIT = 56 * 1024 * 1024

TOKEN_TILE = 512
MIXOUT_GROUP = 256
ATTN_TILE = 512
ATTN_HEADS_PER_STEP = 2
FF_CHUNKS = ((0, 1536), (1536, 2816))
STAGE_BYTES = 4 * 1024 * 1024


def _inv_rms(x):
    return lax.rsqrt(jnp.mean(x * x, axis=-1, keepdims=True) + EPS)


def _rms(x, g):
    return x * _inv_rms(x) * g


def _prenorm_operand(x, g):
    return (x * g).astype(BF16), _inv_rms(x)


def _const_spec(shape):
    zeros = (0,) * len(shape)
    return pl.BlockSpec(shape, lambda *_: zeros, pipeline_mode=pl.Buffered(1))


_HBM = pl.BlockSpec(memory_space=pl.ANY)


def _stage_scratch(rows, cols):
    chunks = -(-(rows * cols * 4) // STAGE_BYTES)
    while rows % chunks or (rows // chunks) % 16:
        chunks += 1
    return [pltpu.VMEM((2, rows // chunks, cols), F32), pltpu.SemaphoreType.DMA((2,))]


def _load_weights(weights, stage_ref, sem_ref):
    rows = stage_ref.shape[1]
    jobs = []
    for w_hbm, w_vmem in weights:
        assert w_hbm.shape[0] % rows == 0 and stage_ref.shape[2] == w_hbm.shape[1]
        jobs += [(w_hbm, w_vmem, r) for r in range(0, w_hbm.shape[0], rows)]

    def copy(k):
        w_hbm, _, r = jobs[k]
        return pltpu.make_async_copy(w_hbm.at[pl.ds(r, rows), :], stage_ref.at[k % 2], sem_ref.at[k % 2])

    copy(0).start()
    for k, (_, w_vmem, r) in enumerate(jobs):
        if k + 1 < len(jobs):
            copy(k + 1).start()
        copy(k).wait()
        w_vmem[r:r + rows, :] = stage_ref[k % 2].astype(BF16)


def _cast_slabs(pairs):
    for w_in, w_out in pairs:
        w_out[...] = w_in[...].astype(BF16)


def _slab_specs(weights, grid):
    n_steps = grid[0] * grid[1]
    specs, shapes = [], []
    for w in weights:
        rows = w.shape[0] // n_steps
        assert rows * n_steps == w.shape[0] and rows % 16 == 0
        specs.append(pl.BlockSpec((rows, w.shape[1]), lambda bi, i: (bi * grid[1] + i, 0)))
        shapes.append(jax.ShapeDtypeStruct(w.shape, BF16))
    return specs, specs, shapes


def _run_staggered(chains):
    depth = len(chains[0])
    for time in range(depth + len(chains) - 1):
        for g, chain in enumerate(chains):
            if 0 <= time - g < depth:
                chain[time - g]()


def _first_step(grid_rank):
    first = pl.program_id(0) == 0
    for axis in range(1, grid_rank):
        first = first & (pl.program_id(axis) == 0)
    return first


def _kv_kernel(mem_ref, g_ref, w_ref, kv_ref):
    mn = _rms(mem_ref[...], g_ref[...]).astype(BF16)
    kv_ref[...] = jnp.dot(mn, w_ref[...], preferred_element_type=F32).astype(BF16)


def _kv_call(mem, g_mem, w_xkv, tm):
    b, m, d = mem.shape
    n = w_xkv.shape[1]
    rows = b * m
    tm = min(tm, rows)
    assert rows % tm == 0
    kv = pl.pallas_call(
        _kv_kernel,
        grid=(rows // tm,),
        in_specs=[pl.BlockSpec((tm, d), lambda i: (i, 0)), _const_spec((1, d)), _const_spec(w_xkv.shape)],
        out_specs=pl.BlockSpec((tm, n), lambda i: (i, 0)),
        out_shape=jax.ShapeDtypeStruct((rows, n), BF16),
        compiler_params=pltpu.CompilerParams(
            dimension_semantics=("arbitrary",), vmem_limit_bytes=VMEM_LIMIT),
        name="kv_proj",
    )(mem.reshape(rows, d), g_mem, w_xkv)
    return kv.reshape(b, m, n)


def _inproj_kernel(x_ref, g_ref, w_hbm, pos_ref, freq_ref, wp_ref, ps_ref, wkv_in,
                   q_ref, k_ref, v_ref, po_ref, wkv_out, halo_ref, w_ref, stage, sem, *, tm):
    @pl.when(_first_step(2))
    def _():
        _load_weights([(w_hbm, w_ref)], stage, sem)

    _cast_slabs([(wkv_in, wkv_out)])

    i = pl.program_id(1)
    h = _rms(x_ref[0], g_ref[...]).astype(BF16)

    groups = LANES // ROPE_DIM
    rows = tm // groups
    lane = lax.broadcasted_iota(jnp.int32, (rows, LANES), 1)
    lane_group = lax.shift_right_logical(lane, int(math.log2(ROPE_DIM)))
    pos = jnp.broadcast_to(pos_ref[0].astype(F32), (LANES, tm)).T
    packed = jnp.zeros((rows, LANES), F32)
    for g in range(groups):
        packed = jnp.where(lane_group == g, pos[g * rows:(g + 1) * rows], packed)
    ang = packed * freq_ref[0:1, :]
    cos8, sin8 = jnp.cos(ang), jnp.sin(ang)
    first_map = lane < DA_QK_DIM
    rotary = freq_ref[1:2, :] != 0.0

    def spread(packed_table, g):
        shift = (LANES - ROPE_DIM * g) % LANES
        lo = pltpu.roll(packed_table, shift, 1) if shift else packed_table
        hi_shift = (shift + DA_QK_DIM) % LANES
        hi = pltpu.roll(packed_table, hi_shift, 1) if hi_shift else packed_table
        return jnp.where(first_map, lo, hi)

    cos_t = jnp.concatenate([jnp.where(rotary, spread(cos8, g), 1.0) for g in range(groups)], axis=0)
    sin_t = jnp.concatenate([spread(sin8, g) * freq_ref[1:2, :] for g in range(groups)], axis=0)
    first_half = freq_ref[1:2, :] < 0.0

    def rope(t, c, s):
        partner = jnp.where(first_half, pltpu.roll(t, LANES - ROPE_HALF, 1), pltpu.roll(t, ROPE_HALF, 1))
        return t * c + partner * s

    qk_scale = DA_QK_DIM ** -0.5 * LOG2_E
    cos_q, sin_q = cos_t * qk_scale, sin_t * qk_scale
    pq = jnp.dot(h, w_ref[:, 0:QK_WIDTH], preferred_element_type=F32)
    for j in range(DA_HEADS):
        sl = slice(j * LANES, (j + 1) * LANES)
        q_ref[j] = rope(pq[:, sl], cos_q, sin_q).astype(BF16)
    pk = jnp.dot(h, w_ref[:, QK_WIDTH:2 * QK_WIDTH], preferred_element_type=F32)
    for j in range(DA_HEADS):
        sl = slice(j * LANES, (j + 1) * LANES)
        k_ref[j] = rope(pk[:, sl], cos_t, sin_t).astype(BF16)
    pv = jnp.dot(h, w_ref[:, 2 * QK_WIDTH:2 * QK_WIDTH + DA_WIDTH], preferred_element_type=F32)
    for j in range(DA_HEADS):
        v_ref[j] = pv[:, j * DA_V_DIM:(j + 1) * DA_V_DIM].astype(BF16)
    u = jnp.dot(h, w_ref[:, 2 * QK_WIDTH + DA_WIDTH:], preferred_element_type=F32)

    @pl.when(i == 0)
    def _():
        halo_ref[...] = jnp.zeros_like(halo_ref)

    prev = halo_ref[...]
    halo_ref[...] = u[tm - POOL_HALO:, :]
    tpos = i * tm + lax.broadcasted_iota(jnp.int32, (tm, 1), 0)
    for g, w in enumerate(POOL_WINDOWS):
        sl = slice(g * POOL_GROUP_DIM, (g + 1) * POOL_GROUP_DIM)
        ug = u[:, sl]
        s = jnp.concatenate([prev[:, sl], ug], axis=0)
        d = 1
        while d < w:
            s = s + pltpu.roll(s, d, 0)
            d *= 2
        inv_count = 1.0 / jnp.minimum(tpos + 1, w).astype(F32)
        pooled = s[POOL_HALO:, :] * inv_count - ug
        po = jnp.dot(pooled.astype(BF16), wp_ref[g], preferred_element_type=F32) * ps_ref[:, sl]
        po_ref[0, :, sl] = po.astype(BF16)


def _inproj_call(x, g, w_in, pos, freq, w_pool, pool_scale, w_xkv, tm):
    b, s, d = x.shape
    grid = (b, s // tm)
    cast_in, cast_out, cast_shape = _slab_specs([w_xkv], grid)
    tok = lambda n: pl.BlockSpec((1, tm, n), lambda bi, i: (bi, i, 0))
    heads = pl.BlockSpec((None, DA_HEADS, tm, DA_V_DIM), lambda bi, i: (bi, 0, i, 0))
    head_major = jax.ShapeDtypeStruct((b, DA_HEADS, s, DA_V_DIM), BF16)
    return pl.pallas_call(
        functools.partial(_inproj_kernel, tm=tm),
        grid=grid,
        in_specs=[tok(d), _const_spec((1, d)), _HBM,
                  pl.BlockSpec((1, 1, tm), lambda bi, i: (bi, 0, i)), _const_spec(freq.shape),
                  _const_spec(w_pool.shape), _const_spec((1, POOL_WIDTH)), *cast_in],
        out_specs=[heads, heads, heads, tok(POOL_WIDTH), *cast_out],
        out_shape=[head_major, head_major, head_major,
                   jax.ShapeDtypeStruct((b, s, POOL_WIDTH), BF16), *cast_shape],
        scratch_shapes=[pltpu.VMEM((POOL_HALO, POOL_WIDTH), F32), pltpu.VMEM(w_in.shape, BF16),
                        *_stage_scratch(d, w_in.shape[1])],
        compiler_params=pltpu.CompilerParams(
            dimension_semantics=("arbitrary", "arbitrary"), vmem_limit_bytes=VMEM_LIMIT),
        name="inproj",
    )(x, g, w_in, pos, freq, w_pool, pool_scale, w_xkv)


def _interleave(*streams):
    tagged = [((k + 0.5) / len(ops), n, op) for n, ops in enumerate(streams) for k, op in enumerate(ops)]
    return [op for _, _, op in sorted(tagged, key=lambda e: e[:2])]


def _tile_order(nq):
    def length(order):
        total = 0.0
        for step in range(nq + 2):
            scores = order[step] + 1 if step < nq else 0
            probs = order[step - 1] + 1 if 0 <= step - 1 < nq else 0
            values = (order[step - 2] + 1) / 2 if 0 <= step - 2 < nq else 0
            total += max(scores + values, probs)
        return total

    return min(itertools.permutations(range(nq)), key=length)


def _attn_kernel(q_ref, k_ref, v_ref, lq1_ref, lk1_ref, lq2_ref, lk2_ref, gs_ref, wo_in, wxq_in, wxo_in,
                 o_ref, wo_out, wxq_out, wxo_out, *scratch, t, nq, heads):
    _cast_slabs([(wo_in, wo_out), (wxq_in, wxq_out), (wxo_in, wxo_out)])

    h = t // 2
    lane = lax.broadcasted_iota(jnp.int32, (h, DA_V_DIM), 1)
    causal = (lax.broadcasted_iota(jnp.int32, (h, h), 0) <= lax.broadcasted_iota(jnp.int32, (h, h), 1))
    causal = jnp.concatenate([causal, causal], axis=1)
    lam = (jnp.exp(jnp.sum(lq1_ref[...] * lk1_ref[...], axis=-1, keepdims=True))
           - jnp.exp(jnp.sum(lq2_ref[...] * lk2_ref[...], axis=-1, keepdims=True))
           + LAMBDA_INIT)
    gain = gs_ref[...] * (1.0 - LAMBDA_INIT)
    nt = (((1,), (1,)), ((), ()))
    tn = (((0,), (0,)), ((), ()))
    state = {}

    def chunks(i):
        n = 2 * (i + 1)
        return [(slice(c * h, (c + 1) * h), t if c == n - 1 else 0) for c in range(n)]

    def both(st, name, lo, value, combine):
        for half, part in ((0, value[:, :t]), (1, value[:, t:])) if lo == 0 else ((1, value),):
            key = (name, half)
            st[key] = part if key not in st else combine(st[key], part)

    def score_ops(hd, i):
        s_ref, st = scratch[i], state.setdefault((hd, i), {})

        def start():
            stacked = []
            for half in range(2):
                q = q_ref[hd, i * t + half * h:i * t + (half + 1) * h, :]
                zero = jnp.zeros_like(q)
                stacked += [jnp.where(lane < DA_QK_DIM, q, zero), jnp.where(lane >= DA_QK_DIM, q, zero)]
            st["qs"] = jnp.concatenate(stacked, axis=0)

        def one(c, rows, lo):
            s = lax.dot_general(k_ref[hd, rows, :], st["qs"][lo:, :], nt, preferred_element_type=F32)
            first_diag = 2 * i
            if c == first_diag:
                s = jnp.concatenate([jnp.where(causal, s[:, :t], -jnp.inf), s[:, t:]], axis=1)
            elif c == first_diag + 1:
                s = jnp.where(causal, s, -jnp.inf)
            s_ref[rows, lo:] = s
            both(st, "m", lo, jnp.max(s, axis=0, keepdims=True), jnp.maximum)

        return [start] + [functools.partial(one, c, rows, lo) for c, (rows, lo) in enumerate(chunks(i))]

    def prob_ops(hd, i):
        s_ref, st = scratch[i], state[hd, i]

        def one(rows, lo):
            m = st[("m", 1)] if lo else jnp.concatenate([st[("m", 0)], st[("m", 1)]], axis=1)
            p = jnp.exp2(s_ref[rows, lo:] - m)
            s_ref[rows, lo:] = p
            both(st, "l", lo, jnp.sum(p, axis=0, keepdims=True), jnp.add)

        return [functools.partial(one, rows, lo) for rows, lo in chunks(i)]

    def value_ops(hd, i):
        s_ref, st = scratch[i], state[hd, i]

        def combine(rows, half):
            l = st[("l", half)]
            ratio = lam * l[:, :h] / l[:, h:]
            base = half * t
            return s_ref[rows, base:base + h] - ratio * s_ref[rows, base + h:base + t]

        def one(rows, lo):
            halves = (1,) if lo else (0, 1)
            a = jnp.concatenate([combine(rows, half) for half in halves], axis=1).astype(BF16)
            pv = lax.dot_general(v_ref[hd, rows, :], a, tn, preferred_element_type=F32)
            for n, half in enumerate(halves):
                part = pv[:, n * h:(n + 1) * h]
                st["acc", half] = part if ("acc", half) not in st else st["acc", half] + part

        def finish():
            o = jnp.concatenate([st[("acc", half)] * (1.0 / st[("l", half)][:, :h]) for half in range(2)], axis=1)
            inv = lax.rsqrt(jnp.mean(o * o, axis=0, keepdims=True) + EPS)
            o_ref[hd, i * t:(i + 1) * t, :] = (o * inv * gain).T.astype(BF16)

        return [functools.partial(one, rows, lo) for rows, lo in chunks(i)] + [finish]

    order = [(hd, i) for hd in range(heads) for i in _tile_order(nq)]
    for step in range(len(order) + 2):
        streams = []
        if step < len(order):
            streams.append(score_ops(*order[step]))
        if 0 <= step - 1 < len(order):
            streams.append(prob_ops(*order[step - 1]))
        if 0 <= step - 2 < len(order):
            streams.append(value_ops(*order[step - 2]))
        for op in _interleave(*streams):
            op()


def _attn_call(q, k, v, lq1, lk1, lq2, lk2, g_subln_col, later_weights, t):
    b, _, s, _ = q.shape
    nq = s // t
    heads = ATTN_HEADS_PER_STEP
    steps = DA_HEADS // heads
    cast_in, cast_out, cast_shape = _slab_specs(later_weights, (b, steps))
    head = pl.BlockSpec((None, heads, s, DA_V_DIM), lambda bi, h: (bi, h, 0, 0))
    vec = _const_spec((1, DA_QK_DIM))
    return pl.pallas_call(
        functools.partial(_attn_kernel, t=t, nq=nq, heads=heads),
        grid=(b, steps),
        in_specs=[head, head, head, vec, vec, vec, vec, _const_spec((DA_V_DIM, 1)), *cast_in],
        out_specs=[head, *cast_out],
        out_shape=[jax.ShapeDtypeStruct((b, DA_HEADS, s, DA_V_DIM), BF16), *cast_shape],
        scratch_shapes=[pltpu.VMEM(((i + 1) * t, 2 * t), F32) for i in range(nq)],
        compiler_params=pltpu.CompilerParams(
            dimension_semantics=("arbitrary", "arbitrary"), vmem_limit_bytes=VMEM_LIMIT),
        name="diff_attn",
    )(q, k, v, lq1, lk1, lq2, lk2, g_subln_col, *later_weights)


def _mixout_kernel(da_ref, po_ref, x_ref, wo_ref, gpost_ref, gxpre_ref, wxq_ref, kv_ref, wxo_ref,
                   gxpost_ref, wg_in, wu_in, wd_in, o_ref, wg_out, wu_out, wd_out):
    _cast_slabs([(wg_in, wg_out), (wu_in, wu_out), (wd_in, wd_out)])

    tm = x_ref.shape[1]
    groups = [slice(r, r + MIXOUT_GROUP) for r in range(0, tm, MIXOUT_GROUP)]
    x_scale = X_HEAD_DIM ** -0.5

    def phases(rows):
        st = {}

        def mix():
            da = jnp.concatenate([da_ref[h, rows, :] for h in range(DA_HEADS)], axis=-1)
            st["mix"] = (jnp.dot(da, wo_ref[0:DA_WIDTH, :], preferred_element_type=F32)
                         + jnp.dot(po_ref[0, rows, :], wo_ref[DA_WIDTH:, :], preferred_element_type=F32))

        def norms():
            st["x1"] = x_ref[0, rows, :] + _rms(st.pop("mix"), gpost_ref[...])
            st["hq"], st["inv"] = _prenorm_operand(st["x1"], gxpre_ref[...])

        def queries():
            st["xq"] = (jnp.dot(st.pop("hq"), wxq_ref[...], preferred_element_type=F32)
                        * (st.pop("inv") * x_scale)).astype(BF16)

        def attend():
            heads = []
            for h in range(X_HEADS):
                sl = slice(h * X_HEAD_DIM, (h + 1) * X_HEAD_DIM)
                kh = kv_ref[0, :, sl]
                vh = kv_ref[0, :, D_MODEL + h * X_HEAD_DIM:D_MODEL + (h + 1) * X_HEAD_DIM]
                sc = lax.dot_general(st["xq"][:, sl], kh, (((1,), (1,)), ((), ())), preferred_element_type=F32)
                p = jnp.exp(sc - jnp.max(sc, axis=-1, keepdims=True))
                pm = (p * (1.0 / jnp.sum(p, axis=-1, keepdims=True))).astype(BF16)
                heads.append(jnp.dot(pm, vh, preferred_element_type=F32).astype(BF16))
            st["xo"] = jnp.concatenate(heads, axis=-1)

        def project():
            st["y"] = jnp.dot(st.pop("xo"), wxo_ref[...], preferred_element_type=F32)

        def finish():
            o_ref[0, rows, :] = st["x1"] + _rms(st["y"], gxpost_ref[...])

        return [mix, norms, queries, attend, project, finish]

    _run_staggered([phases(rows) for rows in groups])


def _mixout_call(da, po, x, w_out, g_mix_post, g_x_pre, w_xq, kv, w_xo, g_x_post, ffn_weights, tm):
    b, s, d = x.shape
    grid = (b, s // tm)
    cast_in, cast_out, cast_shape = _slab_specs(ffn_weights, grid)
    tok = lambda n: pl.BlockSpec((1, tm, n), lambda bi, i: (bi, i, 0))
    gvec = _const_spec((1, d))
    return pl.pallas_call(
        _mixout_kernel,
        grid=grid,
        in_specs=[pl.BlockSpec((None, DA_HEADS, tm, DA_V_DIM), lambda bi, i: (bi, 0, i, 0)),
                  tok(POOL_WIDTH), tok(d), _const_spec(w_out.shape), gvec, gvec, _const_spec(w_xq.shape),
                  pl.BlockSpec((1,) + kv.shape[1:], lambda bi, i: (bi, 0, 0)),
                  _const_spec(w_xo.shape), gvec, *cast_in],
        out_specs=[tok(d), *cast_out],
        out_shape=[jax.ShapeDtypeStruct((b, s, d), F32), *cast_shape],
        compiler_params=pltpu.CompilerParams(
            dimension_semantics=("arbitrary", "arbitrary"), vmem_limit_bytes=VMEM_LIMIT),
        name="mixout_xattn",
    )(da, po, x, w_out, g_mix_post, g_x_pre, w_xq, kv, w_xo, g_x_post, *ffn_weights)


def _ffn_kernel(x_ref, gpre_ref, wg_ref, wu_ref, wd_ref, gpost_ref, o_ref):
    tm = x_ref.shape[1]

    def phases(rows):
        st = {}

        def prenorm():
            st["x"] = x_ref[0, rows, :]
            st["hf"], st["inv"] = _prenorm_operand(st["x"], gpre_ref[...])

        def chunk(lo, hi):
            gate = jnp.dot(st["hf"], wg_ref[:, lo:hi], preferred_element_type=F32) * st["inv"]
            up = jnp.dot(st["hf"], wu_ref[:, lo:hi], preferred_element_type=F32) * st["inv"]
            act = (gate * (1.0 / (1.0 + jnp.exp(-gate))) * up).astype(BF16)
            part = jnp.dot(act, wd_ref[lo:hi, :], preferred_element_type=F32)
            st["ff"] = part if "ff" not in st else st["ff"] + part

        def finish():
            o_ref[0, rows, :] = st["x"] + _rms(st["ff"], gpost_ref[...])

        return [prenorm] + [functools.partial(chunk, lo, hi) for lo, hi in FF_CHUNKS] + [finish]

    _run_staggered([phases(slice(r, r + tm // 2)) for r in (0, tm // 2)])


def _ffn_call(x, g_pre, w_gate, w_up, w_down, g_post, tm):
    b, s, d = x.shape
    tok = pl.BlockSpec((1, tm, d), lambda bi, i: (bi, i, 0))
    gvec = _const_spec((1, d))
    return pl.pallas_call(
        _ffn_kernel,
        grid=(b, s // tm),
        in_specs=[tok, gvec, _const_spec(w_gate.shape), _const_spec(w_up.shape),
                  _const_spec(w_down.shape), gvec],
        out_specs=tok,
        out_shape=jax.ShapeDtypeStruct((b, s, d), F32),
        compiler_params=pltpu.CompilerParams(
            dimension_semantics=("arbitrary", "arbitrary"), vmem_limit_bytes=VMEM_LIMIT),
        name="swiglu",
    )(x, g_pre, w_gate, w_up, w_down, g_post)


def _rope_lane_constants():
    inv_freq = ROPE_THETA ** (-jnp.arange(0, ROPE_DIM, 2, dtype=F32) / ROPE_DIM)
    packed_freq = jnp.tile(inv_freq, LANES // ROPE_HALF)
    one_half = jnp.ones((ROPE_HALF,), F32)
    per_map = jnp.concatenate([-one_half, one_half, jnp.zeros((DA_QK_DIM - ROPE_DIM,), F32)])
    sign = jnp.concatenate([per_map, per_map])
    return jnp.concatenate([jnp.stack([packed_freq, sign]), jnp.zeros((6, LANES), F32)], axis=0)


def kernel(x, mem, positions, g_mix_pre, w_in, lambda_q1, lambda_k1, lambda_q2, lambda_k2,
           g_subln, w_pool, pool_scale, w_out, g_mix_post, g_x_pre, g_mem, w_xq, w_xkv, w_xo,
           g_x_post, g_ffn_pre, w_gate, w_up, w_down, g_ffn_post):
    b, s, d = x.shape
    assert d == D_MODEL and s % TOKEN_TILE == 0 and s % ATTN_TILE == 0
    row = lambda a: a.reshape(1, -1).astype(F32)
    bf = lambda a: a.astype(BF16)

    pos = positions.reshape(b, 1, s)
    q, k, v, po, w_xkv = _inproj_call(x, row(g_mix_pre), w_in, pos, _rope_lane_constants(), bf(w_pool),
                                      row(pool_scale), w_xkv, 2 * TOKEN_TILE)
    kv = _kv_call(mem, row(g_mem), w_xkv, 2 * TOKEN_TILE)
    da, w_out, w_xq, w_xo = _attn_call(q, k, v, row(lambda_q1), row(lambda_k1), row(lambda_q2),
                                       row(lambda_k2), g_subln.reshape(-1, 1).astype(F32),
                                       (w_out, w_xq, w_xo), ATTN_TILE)
    x2, *ffn_weights = _mixout_call(da, po, x, w_out, row(g_mix_post), row(g_x_pre), w_xq, kv, w_xo,
                                    row(g_x_post), (w_gate, w_up, w_down), 2 * TOKEN_TILE)
    return _ffn_call(x2, row(g_ffn_pre), *ffn_weights, row(g_ffn_post), TOKEN_TILE)
```

```python
import functools
import itertools
import math

import jax
import jax.numpy as jnp
from jax import lax
from jax.experimental import pallas as pl
from jax.experimental.pallas import tpu as pltpu

F32 = jnp.float32
BF16 = jnp.bfloat16

D_MODEL = 1024
MEM_LEN = 256
EPS = 1e-6
DA_HEADS = 4
DA_QK_DIM = 64
DA_V_DIM = 2 * DA_QK_DIM
DA_WIDTH = DA_HEADS * DA_V_DIM
QK_WIDTH = DA_HEADS * 2 * DA_QK_DIM
POOL_WINDOWS = (2, 4, 8, 16)
POOL_WIDTH = D_MODEL - DA_WIDTH
POOL_GROUP_DIM = POOL_WIDTH // len(POOL_WINDOWS)
ROPE_THETA = 500000.0
ROPE_DIM = DA_QK_DIM // 4
ROPE_HALF = ROPE_DIM // 2
X_HEADS = 4
X_HEAD_DIM = D_MODEL // X_HEADS
D_FF = -(-(8 * D_MODEL) // (3 * 256)) * 256
LAMBDA_INIT = 0.8 - 0.6 * math.exp(-0.3 * 0)
LOG2_E = math.log2(math.e)

LANES = 128
POOL_HALO = 16
VMEM_LIMIT = 56 * 1024 * 1024

TOKEN_TILE = 512
MIXOUT_GROUP = 256
ATTN_TILE = 512
ATTN_HEADS_PER_STEP = 2
FF_CHUNKS = ((0, 1536), (1536, 2816))
STAGE_BYTES = 4 * 1024 * 1024


def _inv_rms(x):
    return lax.rsqrt(jnp.mean(x * x, axis=-1, keepdims=True) + EPS)


def _rms(x, g):
    return x * _inv_rms(x) * g


def _prenorm_operand(x, g):
    return (x * g).astype(BF16), _inv_rms(x)


def _const_spec(shape):
    zeros = (0,) * len(shape)
    return pl.BlockSpec(shape, lambda *_: zeros, pipeline_mode=pl.Buffered(1))


_HBM = pl.BlockSpec(memory_space=pl.ANY)


def _stage_scratch(rows, cols):
    chunks = -(-(rows * cols * 4) // STAGE_BYTES)
    while rows % chunks or (rows // chunks) % 16:
        chunks += 1
    return [pltpu.VMEM((2, rows // chunks, cols), F32), pltpu.SemaphoreType.DMA((2,))]


def _load_weights(weights, stage_ref, sem_ref):
    rows = stage_ref.shape[1]
    jobs = []
    for w_hbm, w_vmem in weights:
        assert w_hbm.shape[0] % rows == 0 and stage_ref.shape[2] == w_hbm.shape[1]
        jobs += [(w_hbm, w_vmem, r) for r in range(0, w_hbm.shape[0], rows)]

    def copy(k):
        w_hbm, _, r = jobs[k]
        return pltpu.make_async_copy(w_hbm.at[pl.ds(r, rows), :], stage_ref.at[k % 2], sem_ref.at[k % 2])

    copy(0).start()
    for k, (_, w_vmem, r) in enumerate(jobs):
        if k + 1 < len(jobs):
            copy(k + 1).start()
        copy(k).wait()
        w_vmem[r:r + rows, :] = stage_ref[k % 2].astype(BF16)


def _cast_slabs(pairs):
    for w_in, w_out in pairs:
        w_out[...] = w_in[...].astype(BF16)


def _slab_specs(weights, grid):
    n_steps = grid[0] * grid[1]
    specs, shapes = [], []
    for w in weights:
        rows = w.shape[0] // n_steps
        assert rows * n_steps == w.shape[0] and rows % 16 == 0
        specs.append(pl.BlockSpec((rows, w.shape[1]), lambda bi, i: (bi * grid[1] + i, 0)))
        shapes.append(jax.ShapeDtypeStruct(w.shape, BF16))
    return specs, specs, shapes


def _run_staggered(chains):
    depth = len(chains[0])
    for time in range(depth + len(chains) - 1):
        for g, chain in enumerate(chains):
            if 0 <= time - g < depth:
                chain[time - g]()


def _first_step(grid_rank):
    first = pl.program_id(0) == 0
    for axis in range(1, grid_rank):
        first = first & (pl.program_id(axis) == 0)
    return first


def _kv_kernel(mem_ref, g_ref, w_ref, kv_ref):
    mn = _rms(mem_ref[...], g_ref[...]).astype(BF16)
    kv_ref[...] = jnp.dot(mn, w_ref[...], preferred_element_type=F32).astype(BF16)


def _kv_call(mem, g_mem, w_xkv, tm):
    b, m, d = mem.shape
    n = w_xkv.shape[1]
    rows = b * m
    tm = min(tm, rows)
    assert rows % tm == 0
    kv = pl.pallas_call(
        _kv_kernel,
        grid=(rows // tm,),
        in_specs=[pl.BlockSpec((tm, d), lambda i: (i, 0)), _const_spec((1, d)), _const_spec(w_xkv.shape)],
        out_specs=pl.BlockSpec((tm, n), lambda i: (i, 0)),
        out_shape=jax.ShapeDtypeStruct((rows, n), BF16),
        compiler_params=pltpu.CompilerParams(
            dimension_semantics=("arbitrary",), vmem_limit_bytes=VMEM_LIMIT),
        name="kv_proj",
    )(mem.reshape(rows, d), g_mem, w_xkv)
    return kv.reshape(b, m, n)


def _inproj_kernel(x_ref, g_ref, w_hbm, pos_ref, freq_ref, wp_ref, ps_ref, wkv_in,
                   q_ref, k_ref, v_ref, po_ref, wkv_out, halo_ref, w_ref, stage, sem, *, tm):
    @pl.when(_first_step(2))
    def _():
        _load_weights([(w_hbm, w_ref)], stage, sem)

    _cast_slabs([(wkv_in, wkv_out)])

    i = pl.program_id(1)
    h = _rms(x_ref[0], g_ref[...]).astype(BF16)

    groups = LANES // ROPE_DIM
    rows = tm // groups
    lane = lax.broadcasted_iota(jnp.int32, (rows, LANES), 1)
    lane_group = lax.shift_right_logical(lane, int(math.log2(ROPE_DIM)))
    pos = jnp.broadcast_to(pos_ref[0].astype(F32), (LANES, tm)).T
    packed = jnp.zeros((rows, LANES), F32)
    for g in range(groups):
        packed = jnp.where(lane_group == g, pos[g * rows:(g + 1) * rows], packed)
    ang = packed * freq_ref[0:1, :]
    cos8, sin8 = jnp.cos(ang), jnp.sin(ang)
    first_map = lane < DA_QK_DIM
    rotary = freq_ref[1:2, :] != 0.0

    def spread(packed_table, g):
        shift = (LANES - ROPE_DIM * g) % LANES
        lo = pltpu.roll(packed_table, shift, 1) if shift else packed_table
        hi_shift = (shift + DA_QK_DIM) % LANES
        hi = pltpu.roll(packed_table, hi_shift, 1) if hi_shift else packed_table
        return jnp.where(first_map, lo, hi)

    cos_t = jnp.concatenate([jnp.where(rotary, spread(cos8, g), 1.0) for g in range(groups)], axis=0)
    sin_t = jnp.concatenate([spread(sin8, g) * freq_ref[1:2, :] for g in range(groups)], axis=0)
    first_half = freq_ref[1:2, :] < 0.0

    def rope(t, c, s):
        partner = jnp.where(first_half, pltpu.roll(t, LANES - ROPE_HALF, 1), pltpu.roll(t, ROPE_HALF, 1))
        return t * c + partner * s

    qk_scale = DA_QK_DIM ** -0.5 * LOG2_E
    cos_q, sin_q = cos_t * qk_scale, sin_t * qk_scale
    pq = jnp.dot(h, w_ref[:, 0:QK_WIDTH], preferred_element_type=F32)
    for j in range(DA_HEADS):
        sl = slice(j * LANES, (j + 1) * LANES)
        q_ref[j] = rope(pq[:, sl], cos_q, sin_q).astype(BF16)
    pk = jnp.dot(h, w_ref[:, QK_WIDTH:2 * QK_WIDTH], preferred_element_type=F32)
    for j in range(DA_HEADS):
        sl = slice(j * LANES, (j + 1) * LANES)
        k_ref[j] = rope(pk[:, sl], cos_t, sin_t).astype(BF16)
    pv = jnp.dot(h, w_ref[:, 2 * QK_WIDTH:2 * QK_WIDTH + DA_WIDTH], preferred_element_type=F32)
    for j in range(DA_HEADS):
        v_ref[j] = pv[:, j * DA_V_DIM:(j + 1) * DA_V_DIM].astype(BF16)
    u = jnp.dot(h, w_ref[:, 2 * QK_WIDTH + DA_WIDTH:], preferred_element_type=F32)

    @pl.when(i == 0)
    def _():
        halo_ref[...] = jnp.zeros_like(halo_ref)

    prev = halo_ref[...]
    halo_ref[...] = u[tm - POOL_HALO:, :]
    tpos = i * tm + lax.broadcasted_iota(jnp.int32, (tm, 1), 0)
    for g, w in enumerate(POOL_WINDOWS):
        sl = slice(g * POOL_GROUP_DIM, (g + 1) * POOL_GROUP_DIM)
        ug = u[:, sl]
        s = jnp.concatenate([prev[:, sl], ug], axis=0)
        d = 1
        while d < w:
            s = s + pltpu.roll(s, d, 0)
            d *= 2
        inv_count = 1.0 / jnp.minimum(tpos + 1, w).astype(F32)
        pooled = s[POOL_HALO:, :] * inv_count - ug
        po = jnp.dot(pooled.astype(BF16), wp_ref[g], preferred_element_type=F32) * ps_ref[:, sl]
        po_ref[0, :, sl] = po.astype(BF16)


def _inproj_call(x, g, w_in, pos, freq, w_pool, pool_scale, w_xkv, tm):
    b, s, d = x.shape
    grid = (b, s // tm)
    cast_in, cast_out, cast_shape = _slab_specs([w_xkv], grid)
    tok = lambda n: pl.BlockSpec((1, tm, n), lambda bi, i: (bi, i, 0))
    heads = pl.BlockSpec((None, DA_HEADS, tm, DA_V_DIM), lambda bi, i: (bi, 0, i, 0))
    head_major = jax.ShapeDtypeStruct((b, DA_HEADS, s, DA_V_DIM), BF16)
    return pl.pallas_call(
        functools.partial(_inproj_kernel, tm=tm),
        grid=grid,
        in_specs=[tok(d), _const_spec((1, d)), _HBM,
                  pl.BlockSpec((1, 1, tm), lambda bi, i: (bi, 0, i)), _const_spec(freq.shape),
                  _const_spec(w_pool.shape), _const_spec((1, POOL_WIDTH)), *cast_in],
        out_specs=[heads, heads, heads, tok(POOL_WIDTH), *cast_out],
        out_shape=[head_major, head_major, head_major,
                   jax.ShapeDtypeStruct((b, s, POOL_WIDTH), BF16), *cast_shape],
        scratch_shapes=[pltpu.VMEM((POOL_HALO, POOL_WIDTH), F32), pltpu.VMEM(w_in.shape, BF16),
                        *_stage_scratch(d, w_in.shape[1])],
        compiler_params=pltpu.CompilerParams(
            dimension_semantics=("arbitrary", "arbitrary"), vmem_limit_bytes=VMEM_LIMIT),
        name="inproj",
    )(x, g, w_in, pos, freq, w_pool, pool_scale, w_xkv)


def _interleave(*streams):
    tagged = [((k + 0.5) / len(ops), n, op) for n, ops in enumerate(streams) for k, op in enumerate(ops)]
    return [op for _, _, op in sorted(tagged, key=lambda e: e[:2])]


def _tile_order(nq):
    def length(order):
        total = 0.0
        for step in range(nq + 2):
            scores = order[step] + 1 if step < nq else 0
            probs = order[step - 1] + 1 if 0 <= step - 1 < nq else 0
            values = (order[step - 2] + 1) / 2 if 0 <= step - 2 < nq else 0
            total += max(scores + values, probs)
        return total

    return min(itertools.permutations(range(nq)), key=length)


def _attn_kernel(q_ref, k_ref, v_ref, lq1_ref, lk1_ref, lq2_ref, lk2_ref, gs_ref, wo_in, wxq_in, wxo_in,
                 o_ref, wo_out, wxq_out, wxo_out, *scratch, t, nq, heads):
    _cast_slabs([(wo_in, wo_out), (wxq_in, wxq_out), (wxo_in, wxo_out)])

    h = t // 2
    lane = lax.broadcasted_iota(jnp.int32, (h, DA_V_DIM), 1)
    causal = (lax.broadcasted_iota(jnp.int32, (h, h), 0) <= lax.broadcasted_iota(jnp.int32, (h, h), 1))
    causal = jnp.concatenate([causal, causal], axis=1)
    lam = (jnp.exp(jnp.sum(lq1_ref[...] * lk1_ref[...], axis=-1, keepdims=True))
           - jnp.exp(jnp.sum(lq2_ref[...] * lk2_ref[...], axis=-1, keepdims=True))
           + LAMBDA_INIT)
    gain = gs_ref[...] * (1.0 - LAMBDA_INIT)
    nt = (((1,), (1,)), ((), ()))
    tn = (((0,), (0,)), ((), ()))
    state = {}

    def chunks(i):
        n = 2 * (i + 1)
        return [(slice(c * h, (c + 1) * h), t if c == n - 1 else 0) for c in range(n)]

    def both(st, name, lo, value, combine):
        for half, part in ((0, value[:, :t]), (1, value[:, t:])) if lo == 0 else ((1, value),):
            key = (name, half)
            st[key] = part if key not in st else combine(st[key], part)

    def score_ops(hd, i):
        s_ref, st = scratch[i], state.setdefault((hd, i), {})

        def start():
            stacked = []
            for half in range(2):
                q = q_ref[hd, i * t + half * h:i * t + (half + 1) * h, :]
                zero = jnp.zeros_like(q)
                stacked += [jnp.where(lane < DA_QK_DIM, q, zero), jnp.where(lane >= DA_QK_DIM, q, zero)]
            st["qs"] = jnp.concatenate(stacked, axis=0)

        def one(c, rows, lo):
            s = lax.dot_general(k_ref[hd, rows, :], st["qs"][lo:, :], nt, preferred_element_type=F32)
            first_diag = 2 * i
            if c == first_diag:
                s = jnp.concatenate([jnp.where(causal, s[:, :t], -jnp.inf), s[:, t:]], axis=1)
            elif c == first_diag + 1:
                s = jnp.where(causal, s, -jnp.inf)
            s_ref[rows, lo:] = s
            both(st, "m", lo, jnp.max(s, axis=0, keepdims=True), jnp.maximum)

        return [start] + [functools.partial(one, c, rows, lo) for c, (rows, lo) in enumerate(chunks(i))]

    def prob_ops(hd, i):
        s_ref, st = scratch[i], state[hd, i]

        def one(rows, lo):
            m = st[("m", 1)] if lo else jnp.concatenate([st[("m", 0)], st[("m", 1)]], axis=1)
            p = jnp.exp2(s_ref[rows, lo:] - m)
            s_ref[rows, lo:] = p
            both(st, "l", lo, jnp.sum(p, axis=0, keepdims=True), jnp.add)

        return [functools.partial(one, rows, lo) for rows, lo in chunks(i)]

    def value_ops(hd, i):
        s_ref, st = scratch[i], state[hd, i]

        def combine(rows, half):
            l = st[("l", half)]
            ratio = lam * l[:, :h] / l[:, h:]
            base = half * t
            return s_ref[rows, base:base + h] - ratio * s_ref[rows, base + h:base + t]

        def one(rows, lo):
            halves = (1,) if lo else (0, 1)
            a = jnp.concatenate([combine(rows, half) for half in halves], axis=1).astype(BF16)
            pv = lax.dot_general(v_ref[hd, rows, :], a, tn, preferred_element_type=F32)
            for n, half in enumerate(halves):
                part = pv[:, n * h:(n + 1) * h]
                st["acc", half] = part if ("acc", half) not in st else st["acc", half] + part

        def finish():
            o = jnp.concatenate([st[("acc", half)] * (1.0 / st[("l", half)][:, :h]) for half in range(2)], axis=1)
            inv = lax.rsqrt(jnp.mean(o * o, axis=0, keepdims=True) + EPS)
            o_ref[hd, i * t:(i + 1) * t, :] = (o * inv * gain).T.astype(BF16)

        return [functools.partial(one, rows, lo) for rows, lo in chunks(i)] + [finish]

    order = [(hd, i) for hd in range(heads) for i in _tile_order(nq)]
    for step in range(len(order) + 2):
        streams = []
        if step < len(order):
            streams.append(score_ops(*order[step]))
        if 0 <= step - 1 < len(order):
            streams.append(prob_ops(*order[step - 1]))
        if 0 <= step - 2 < len(order):
            streams.append(value_ops(*order[step - 2]))
        for op in _interleave(*streams):
            op()


def _attn_call(q, k, v, lq1, lk1, lq2, lk2, g_subln_col, later_weights, t):
    b, _, s, _ = q.shape
    nq = s // t
    heads = ATTN_HEADS_PER_STEP
    steps = DA_HEADS // heads
    cast_in, cast_out, cast_shape = _slab_specs(later_weights, (b, steps))
    head = pl.BlockSpec((None, heads, s, DA_V_DIM), lambda bi, h: (bi, h, 0, 0))
    vec = _const_spec((1, DA_QK_DIM))
    return pl.pallas_call(
        functools.partial(_attn_kernel, t=t, nq=nq, heads=heads),
        grid=(b, steps),
        in_specs=[head, head, head, vec, vec, vec, vec, _const_spec((DA_V_DIM, 1)), *cast_in],
        out_specs=[head, *cast_out],
        out_shape=[jax.ShapeDtypeStruct((b, DA_HEADS, s, DA_V_DIM), BF16), *cast_shape],
        scratch_shapes=[pltpu.VMEM(((i + 1) * t, 2 * t), F32) for i in range(nq)],
        compiler_params=pltpu.CompilerParams(
            dimension_semantics=("arbitrary", "arbitrary"), vmem_limit_bytes=VMEM_LIMIT),
        name="diff_attn",
    )(q, k, v, lq1, lk1, lq2, lk2, g_subln_col, *later_weights)


def _mixout_kernel(da_ref, po_ref, x_ref, wo_ref, gpost_ref, gxpre_ref, wxq_ref, kv_ref, wxo_ref,
                   gxpost_ref, wg_in, wu_in, wd_in, o_ref, wg_out, wu_out, wd_out):
    _cast_slabs([(wg_in, wg_out), (wu_in, wu_out), (wd_in, wd_out)])

    tm = x_ref.shape[1]
    groups = [slice(r, r + MIXOUT_GROUP) for r in range(0, tm, MIXOUT_GROUP)]
    x_scale = X_HEAD_DIM ** -0.5

    def phases(rows):
        st = {}

        def mix():
            da = jnp.concatenate([da_ref[h, rows, :] for h in range(DA_HEADS)], axis=-1)
            st["mix"] = (jnp.dot(da, wo_ref[0:DA_WIDTH, :], preferred_element_type=F32)
                         + jnp.dot(po_ref[0, rows, :], wo_ref[DA_WIDTH:, :], preferred_element_type=F32))

        def norms():
            st["x1"] = x_ref[0, rows, :] + _rms(st.pop("mix"), gpost_ref[...])
            st["hq"], st["inv"] = _prenorm_operand(st["x1"], gxpre_ref[...])

        def queries():
            st["xq"] = (jnp.dot(st.pop("hq"), wxq_ref[...], preferred_element_type=F32)
                        * (st.pop("inv") * x_scale)).astype(BF16)

        def attend():
            heads = []
            for h in range(X_HEADS):
                sl = slice(h * X_HEAD_DIM, (h + 1) * X_HEAD_DIM)
                kh = kv_ref[0, :, sl]
                vh = kv_ref[0, :, D_MODEL + h * X_HEAD_DIM:D_MODEL + (h + 1) * X_HEAD_DIM]
                sc = lax.dot_general(st["xq"][:, sl], kh, (((1,), (1,)), ((), ())), preferred_element_type=F32)
                p = jnp.exp(sc - jnp.max(sc, axis=-1, keepdims=True))
                pm = (p * (1.0 / jnp.sum(p, axis=-1, keepdims=True))).astype(BF16)
                heads.append(jnp.dot(pm, vh, preferred_element_type=F32).astype(BF16))
            st["xo"] = jnp.concatenate(heads, axis=-1)

        def project():
            st["y"] = jnp.dot(st.pop("xo"), wxo_ref[...], preferred_element_type=F32)

        def finish():
            o_ref[0, rows, :] = st["x1"] + _rms(st["y"], gxpost_ref[...])

        return [mix, norms, queries, attend, project, finish]

    _run_staggered([phases(rows) for rows in groups])


def _mixout_call(da, po, x, w_out, g_mix_post, g_x_pre, w_xq, kv, w_xo, g_x_post, ffn_weights, tm):
    b, s, d = x.shape
    grid = (b, s // tm)
    cast_in, cast_out, cast_shape = _slab_specs(ffn_weights, grid)
    tok = lambda n: pl.BlockSpec((1, tm, n), lambda bi, i: (bi, i, 0))
    gvec = _const_spec((1, d))
    return pl.pallas_call(
        _mixout_kernel,
        grid=grid,
        in_specs=[pl.BlockSpec((None, DA_HEADS, tm, DA_V_DIM), lambda bi, i: (bi, 0, i, 0)),
                  tok(POOL_WIDTH), tok(d), _const_spec(w_out.shape), gvec, gvec, _const_spec(w_xq.shape),
                  pl.BlockSpec((1,) + kv.shape[1:], lambda bi, i: (bi, 0, 0)),
                  _const_spec(w_xo.shape), gvec, *cast_in],
        out_specs=[tok(d), *cast_out],
        out_shape=[jax.ShapeDtypeStruct((b, s, d), F32), *cast_shape],
        compiler_params=pltpu.CompilerParams(
            dimension_semantics=("arbitrary", "arbitrary"), vmem_limit_bytes=VMEM_LIMIT),
        name="mixout_xattn",
    )(da, po, x, w_out, g_mix_post, g_x_pre, w_xq, kv, w_xo, g_x_post, *ffn_weights)


def _ffn_kernel(x_ref, gpre_ref, wg_ref, wu_ref, wd_ref, gpost_ref, o_ref):
    tm = x_ref.shape[1]

    def phases(rows):
        st = {}

        def prenorm():
            st["x"] = x_ref[0, rows, :]
            st["hf"], st["inv"] = _prenorm_operand(st["x"], gpre_ref[...])

        def chunk(lo, hi):
            gate = jnp.dot(st["hf"], wg_ref[:, lo:hi], preferred_element_type=F32) * st["inv"]
            up = jnp.dot(st["hf"], wu_ref[:, lo:hi], preferred_element_type=F32) * st["inv"]
            act = (gate * (1.0 / (1.0 + jnp.exp(-gate))) * up).astype(BF16)
            part = jnp.dot(act, wd_ref[lo:hi, :], preferred_element_type=F32)
            st["ff"] = part if "ff" not in st else st["ff"] + part

        def finish():
            o_ref[0, rows, :] = st["x"] + _rms(st["ff"], gpost_ref[...])

        return [prenorm] + [functools.partial(chunk, lo, hi) for lo, hi in FF_CHUNKS] + [finish]

    _run_staggered([phases(slice(r, r + MIXOUT_GROUP)) for r in range(0, tm, MIXOUT_GROUP)])


def _ffn_call(x, g_pre, w_gate, w_up, w_down, g_post, tm):
    b, s, d = x.shape
    tok = pl.BlockSpec((1, tm, d), lambda bi, i: (bi, i, 0))
    gvec = _const_spec((1, d))
    return pl.pallas_call(
        _ffn_kernel,
        grid=(b, s // tm),
        in_specs=[tok, gvec, _const_spec(w_gate.shape), _const_spec(w_up.shape),
                  _const_spec(w_down.shape), gvec],
        out_specs=tok,
        out_shape=jax.ShapeDtypeStruct((b, s, d), F32),
        compiler_params=pltpu.CompilerParams(
            dimension_semantics=("arbitrary", "arbitrary"), vmem_limit_bytes=VMEM_LIMIT),
        name="swiglu",
    )(x, g_pre, w_gate, w_up, w_down, g_post)


def _rope_lane_constants():
    inv_freq = ROPE_THETA ** (-jnp.arange(0, ROPE_DIM, 2, dtype=F32) / ROPE_DIM)
    packed_freq = jnp.tile(inv_freq, LANES // ROPE_HALF)
    one_half = jnp.ones((ROPE_HALF,), F32)
    per_map = jnp.concatenate([-one_half, one_half, jnp.zeros((DA_QK_DIM - ROPE_DIM,), F32)])
    sign = jnp.concatenate([per_map, per_map])
    return jnp.concatenate([jnp.stack([packed_freq, sign]), jnp.zeros((6, LANES), F32)], axis=0)


def kernel(x, mem, positions, g_mix_pre, w_in, lambda_q1, lambda_k1, lambda_q2, lambda_k2,
           g_subln, w_pool, pool_scale, w_out, g_mix_post, g_x_pre, g_mem, w_xq, w_xkv, w_xo,
           g_x_post, g_ffn_pre, w_gate, w_up, w_down, g_ffn_post):
    b, s, d = x.shape
    assert d == D_MODEL and s % TOKEN_TILE == 0 and s % ATTN_TILE == 0
    row = lambda a: a.reshape(1, -1).astype(F32)
    bf = lambda a: a.astype(BF16)

    pos = positions.reshape(b, 1, s)
    q, k, v, po, w_xkv = _inproj_call(x, row(g_mix_pre), w_in, pos, _rope_lane_constants(), bf(w_pool),
                                      row(pool_scale), w_xkv, 2 * TOKEN_TILE)
    kv = _kv_call(mem, row(g_mem), w_xkv, 2 * TOKEN_TILE)
    da, w_out, w_xq, w_xo = _attn_call(q, k, v, row(lambda_q1), row(lambda_k1), row(lambda_q2),
                                       row(lambda_k2), g_subln.reshape(-1, 1).astype(F32),
                                       (w_out, w_xq, w_xo), ATTN_TILE)
    x2, *ffn_weights = _mixout_call(da, po, x, w_out, row(g_mix_post), row(g_x_pre), w_xq, kv, w_xo,
                                    row(g_x_post), (w_gate, w_up, w_down), 2 * TOKEN_TILE)
    return _ffn_call(x2, row(g_ffn_pre), *ffn_weights, row(g_ffn_post), 2 * TOKEN_TILE)
```

```python
import functools
import itertools
import math

import jax
import jax.numpy as jnp
from jax import lax
from jax.experimental import pallas as pl
from jax.experimental.pallas import tpu as pltpu

F32 = jnp.float32
BF16 = jnp.bfloat16

D_MODEL = 1024
MEM_LEN = 256
EPS = 1e-6
DA_HEADS = 4
DA_QK_DIM = 64
DA_V_DIM = 2 * DA_QK_DIM
DA_WIDTH = DA_HEADS * DA_V_DIM
QK_WIDTH = DA_HEADS * 2 * DA_QK_DIM
POOL_WINDOWS = (2, 4, 8, 16)
POOL_WIDTH = D_MODEL - DA_WIDTH
POOL_GROUP_DIM = POOL_WIDTH // len(POOL_WINDOWS)
ROPE_THETA = 500000.0
ROPE_DIM = DA_QK_DIM // 4
ROPE_HALF = ROPE_DIM // 2
X_HEADS = 4
X_HEAD_DIM = D_MODEL // X_HEADS
D_FF = -(-(8 * D_MODEL) // (3 * 256)) * 256
LAMBDA_INIT = 0.8 - 0.6 * math.exp(-0.3 * 0)
LOG2_E = math.log2(math.e)

LANES = 128
POOL_HALO = 16
VMEM_LIMIT = 56 * 1024 * 1024

TOKEN_TILE = 1024
ROW_GROUP = 256
ATTN_TILE = 512
ATTN_HEADS_PER_STEP = 2
FF_CHUNKS = ((0, 1536), (1536, 2816))
STAGE_BYTES = 4 * 1024 * 1024


def _inv_rms(x):
    return lax.rsqrt(jnp.mean(x * x, axis=-1, keepdims=True) + EPS)


def _rms(x, g):
    return x * _inv_rms(x) * g


def _prenorm_operand(x, g):
    return (x * g).astype(BF16), _inv_rms(x)


def _const_spec(shape):
    zeros = (0,) * len(shape)
    return pl.BlockSpec(shape, lambda *_: zeros, pipeline_mode=pl.Buffered(1))


_HBM = pl.BlockSpec(memory_space=pl.ANY)


def _stage_scratch(rows, cols):
    chunks = -(-(rows * cols * 4) // STAGE_BYTES)
    while rows % chunks or (rows // chunks) % 16:
        chunks += 1
    return [pltpu.VMEM((2, rows // chunks, cols), F32), pltpu.SemaphoreType.DMA((2,))]


def _load_weights(weights, stage_ref, sem_ref):
    rows = stage_ref.shape[1]
    jobs = []
    for w_hbm, w_vmem in weights:
        assert w_hbm.shape[0] % rows == 0 and stage_ref.shape[2] == w_hbm.shape[1]
        jobs += [(w_hbm, w_vmem, r) for r in range(0, w_hbm.shape[0], rows)]

    def copy(k):
        w_hbm, _, r = jobs[k]
        return pltpu.make_async_copy(w_hbm.at[pl.ds(r, rows), :], stage_ref.at[k % 2], sem_ref.at[k % 2])

    copy(0).start()
    for k, (_, w_vmem, r) in enumerate(jobs):
        if k + 1 < len(jobs):
            copy(k + 1).start()
        copy(k).wait()
        w_vmem[r:r + rows, :] = stage_ref[k % 2].astype(BF16)


def _cast_slabs(pairs):
    for w_in, w_out in pairs:
        w_out[...] = w_in[...].astype(BF16)


def _slab_specs(weights, grid):
    n_steps = grid[0] * grid[1]
    specs, shapes = [], []
    for w in weights:
        rows = w.shape[0] // n_steps
        assert rows * n_steps == w.shape[0] and rows % 16 == 0
        specs.append(pl.BlockSpec((rows, w.shape[1]), lambda bi, i: (bi * grid[1] + i, 0)))
        shapes.append(jax.ShapeDtypeStruct(w.shape, BF16))
    return specs, specs, shapes


def _run_staggered(chains):
    depth = len(chains[0])
    for time in range(depth + len(chains) - 1):
        for g, chain in enumerate(chains):
            if 0 <= time - g < depth:
                chain[time - g]()


def _first_step(grid_rank):
    first = pl.program_id(0) == 0
    for axis in range(1, grid_rank):
        first = first & (pl.program_id(axis) == 0)
    return first


def _kv_kernel(mem_ref, g_ref, w_ref, kv_ref):
    mn = _rms(mem_ref[...], g_ref[...]).astype(BF16)
    kv_ref[...] = jnp.dot(mn, w_ref[...], preferred_element_type=F32).astype(BF16)


def _kv_call(mem, g_mem, w_xkv, tm):
    b, m, d = mem.shape
    n = w_xkv.shape[1]
    rows = b * m
    tm = min(tm, rows)
    assert rows % tm == 0
    kv = pl.pallas_call(
        _kv_kernel,
        grid=(rows // tm,),
        in_specs=[pl.BlockSpec((tm, d), lambda i: (i, 0)), _const_spec((1, d)), _const_spec(w_xkv.shape)],
        out_specs=pl.BlockSpec((tm, n), lambda i: (i, 0)),
        out_shape=jax.ShapeDtypeStruct((rows, n), BF16),
        compiler_params=pltpu.CompilerParams(
            dimension_semantics=("arbitrary",), vmem_limit_bytes=VMEM_LIMIT),
        name="kv_proj",
    )(mem.reshape(rows, d), g_mem, w_xkv)
    return kv.reshape(b, m, n)


def _inproj_kernel(x_ref, g_ref, w_hbm, pos_ref, freq_ref, wp_ref, ps_ref, wkv_in,
                   q_ref, k_ref, v_ref, po_ref, wkv_out, halo_ref, w_ref, stage, sem, *, tm):
    @pl.when(_first_step(2))
    def _():
        _load_weights([(w_hbm, w_ref)], stage, sem)

    _cast_slabs([(wkv_in, wkv_out)])

    i = pl.program_id(1)
    h = _rms(x_ref[0], g_ref[...]).astype(BF16)

    groups = LANES // ROPE_DIM
    rows = tm // groups
    lane = lax.broadcasted_iota(jnp.int32, (rows, LANES), 1)
    lane_group = lax.shift_right_logical(lane, int(math.log2(ROPE_DIM)))
    pos = jnp.broadcast_to(pos_ref[0].astype(F32), (LANES, tm)).T
    packed = jnp.zeros((rows, LANES), F32)
    for g in range(groups):
        packed = jnp.where(lane_group == g, pos[g * rows:(g + 1) * rows], packed)
    ang = packed * freq_ref[0:1, :]
    cos8, sin8 = jnp.cos(ang), jnp.sin(ang)
    first_map = lane < DA_QK_DIM
    rotary = freq_ref[1:2, :] != 0.0

    def spread(packed_table, g):
        shift = (LANES - ROPE_DIM * g) % LANES
        lo = pltpu.roll(packed_table, shift, 1) if shift else packed_table
        hi_shift = (shift + DA_QK_DIM) % LANES
        hi = pltpu.roll(packed_table, hi_shift, 1) if hi_shift else packed_table
        return jnp.where(first_map, lo, hi)

    cos_t = jnp.concatenate([jnp.where(rotary, spread(cos8, g), 1.0) for g in range(groups)], axis=0)
    sin_t = jnp.concatenate([spread(sin8, g) * freq_ref[1:2, :] for g in range(groups)], axis=0)
    first_half = freq_ref[1:2, :] < 0.0

    def rope(t, c, s):
        partner = jnp.where(first_half, pltpu.roll(t, LANES - ROPE_HALF, 1), pltpu.roll(t, ROPE_HALF, 1))
        return t * c + partner * s

    qk_scale = DA_QK_DIM ** -0.5 * LOG2_E
    cos_q, sin_q = cos_t * qk_scale, sin_t * qk_scale
    pq = jnp.dot(h, w_ref[:, 0:QK_WIDTH], preferred_element_type=F32)
    for j in range(DA_HEADS):
        sl = slice(j * LANES, (j + 1) * LANES)
        q_ref[j] = rope(pq[:, sl], cos_q, sin_q).astype(BF16)
    pk = jnp.dot(h, w_ref[:, QK_WIDTH:2 * QK_WIDTH], preferred_element_type=F32)
    for j in range(DA_HEADS):
        sl = slice(j * LANES, (j + 1) * LANES)
        k_ref[j] = rope(pk[:, sl], cos_t, sin_t).astype(BF16)
    pv = jnp.dot(h, w_ref[:, 2 * QK_WIDTH:2 * QK_WIDTH + DA_WIDTH], preferred_element_type=F32)
    for j in range(DA_HEADS):
        v_ref[j] = pv[:, j * DA_V_DIM:(j + 1) * DA_V_DIM].astype(BF16)
    u = jnp.dot(h, w_ref[:, 2 * QK_WIDTH + DA_WIDTH:], preferred_element_type=F32)

    @pl.when(i == 0)
    def _():
        halo_ref[...] = jnp.zeros_like(halo_ref)

    prev = halo_ref[...]
    halo_ref[...] = u[tm - POOL_HALO:, :]
    tpos = i * tm + lax.broadcasted_iota(jnp.int32, (tm, 1), 0)
    for g, w in enumerate(POOL_WINDOWS):
        sl = slice(g * POOL_GROUP_DIM, (g + 1) * POOL_GROUP_DIM)
        ug = u[:, sl]
        s = jnp.concatenate([prev[:, sl], ug], axis=0)
        d = 1
        while d < w:
            s = s + pltpu.roll(s, d, 0)
            d *= 2
        inv_count = 1.0 / jnp.minimum(tpos + 1, w).astype(F32)
        pooled = s[POOL_HALO:, :] * inv_count - ug
        po = jnp.dot(pooled.astype(BF16), wp_ref[g], preferred_element_type=F32) * ps_ref[:, sl]
        po_ref[0, :, sl] = po.astype(BF16)


def _inproj_call(x, g, w_in, pos, freq, w_pool, pool_scale, w_xkv, tm):
    b, s, d = x.shape
    grid = (b, s // tm)
    cast_in, cast_out, cast_shape = _slab_specs([w_xkv], grid)
    tok = lambda n: pl.BlockSpec((1, tm, n), lambda bi, i: (bi, i, 0))
    heads = pl.BlockSpec((None, DA_HEADS, tm, DA_V_DIM), lambda bi, i: (bi, 0, i, 0))
    head_major = jax.ShapeDtypeStruct((b, DA_HEADS, s, DA_V_DIM), BF16)
    return pl.pallas_call(
        functools.partial(_inproj_kernel, tm=tm),
        grid=grid,
        in_specs=[tok(d), _const_spec((1, d)), _HBM,
                  pl.BlockSpec((1, 1, tm), lambda bi, i: (bi, 0, i)), _const_spec(freq.shape),
                  _const_spec(w_pool.shape), _const_spec((1, POOL_WIDTH)), *cast_in],
        out_specs=[heads, heads, heads, tok(POOL_WIDTH), *cast_out],
        out_shape=[head_major, head_major, head_major,
                   jax.ShapeDtypeStruct((b, s, POOL_WIDTH), BF16), *cast_shape],
        scratch_shapes=[pltpu.VMEM((POOL_HALO, POOL_WIDTH), F32), pltpu.VMEM(w_in.shape, BF16),
                        *_stage_scratch(d, w_in.shape[1])],
        compiler_params=pltpu.CompilerParams(
            dimension_semantics=("arbitrary", "arbitrary"), vmem_limit_bytes=VMEM_LIMIT),
        name="inproj",
    )(x, g, w_in, pos, freq, w_pool, pool_scale, w_xkv)


def _interleave(*streams):
    tagged = [((k + 0.5) / len(ops), n, op) for n, ops in enumerate(streams) for k, op in enumerate(ops)]
    return [op for _, _, op in sorted(tagged, key=lambda e: e[:2])]


def _tile_order(nq):
    def length(order):
        total = 0.0
        for step in range(nq + 2):
            scores = order[step] + 1 if step < nq else 0
            probs = order[step - 1] + 1 if 0 <= step - 1 < nq else 0
            values = (order[step - 2] + 1) / 2 if 0 <= step - 2 < nq else 0
            total += max(scores + values, probs)
        return total

    return min(itertools.permutations(range(nq)), key=length)


def _attn_kernel(q_ref, k_ref, v_ref, lq1_ref, lk1_ref, lq2_ref, lk2_ref, gs_ref, wo_in, wxq_in, wxo_in,
                 o_ref, wo_out, wxq_out, wxo_out, *scratch, t, nq, heads):
    _cast_slabs([(wo_in, wo_out), (wxq_in, wxq_out), (wxo_in, wxo_out)])

    h = t // 2
    lane = lax.broadcasted_iota(jnp.int32, (h, DA_V_DIM), 1)
    causal = (lax.broadcasted_iota(jnp.int32, (h, h), 0) <= lax.broadcasted_iota(jnp.int32, (h, h), 1))
    causal = jnp.concatenate([causal, causal], axis=1)
    lam = (jnp.exp(jnp.sum(lq1_ref[...] * lk1_ref[...], axis=-1, keepdims=True))
           - jnp.exp(jnp.sum(lq2_ref[...] * lk2_ref[...], axis=-1, keepdims=True))
           + LAMBDA_INIT)
    gain = gs_ref[...] * (1.0 - LAMBDA_INIT)
    nt = (((1,), (1,)), ((), ()))
    tn = (((0,), (0,)), ((), ()))
    state = {}

    def chunks(i):
        n = 2 * (i + 1)
        return [(slice(c * h, (c + 1) * h), t if c == n - 1 else 0) for c in range(n)]

    def both(st, name, lo, value, combine):
        for half, part in ((0, value[:, :t]), (1, value[:, t:])) if lo == 0 else ((1, value),):
            key = (name, half)
            st[key] = part if key not in st else combine(st[key], part)

    def score_ops(hd, i):
        s_ref, st = scratch[i], state.setdefault((hd, i), {})

        def start():
            stacked = []
            for half in range(2):
                q = q_ref[hd, i * t + half * h:i * t + (half + 1) * h, :]
                zero = jnp.zeros_like(q)
                stacked += [jnp.where(lane < DA_QK_DIM, q, zero), jnp.where(lane >= DA_QK_DIM, q, zero)]
            st["qs"] = jnp.concatenate(stacked, axis=0)

        def one(c, rows, lo):
            s = lax.dot_general(k_ref[hd, rows, :], st["qs"][lo:, :], nt, preferred_element_type=F32)
            first_diag = 2 * i
            if c == first_diag:
                s = jnp.concatenate([jnp.where(causal, s[:, :t], -jnp.inf), s[:, t:]], axis=1)
            elif c == first_diag + 1:
                s = jnp.where(causal, s, -jnp.inf)
            s_ref[rows, lo:] = s
            both(st, "m", lo, jnp.max(s, axis=0, keepdims=True), jnp.maximum)

        return [start] + [functools.partial(one, c, rows, lo) for c, (rows, lo) in enumerate(chunks(i))]

    def prob_ops(hd, i):
        s_ref, st = scratch[i], state[hd, i]

        def one(rows, lo):
            m = st[("m", 1)] if lo else jnp.concatenate([st[("m", 0)], st[("m", 1)]], axis=1)
            p = jnp.exp2(s_ref[rows, lo:] - m)
            s_ref[rows, lo:] = p
            both(st, "l", lo, jnp.sum(p, axis=0, keepdims=True), jnp.add)

        return [functools.partial(one, rows, lo) for rows, lo in chunks(i)]

    def value_ops(hd, i):
        s_ref, st = scratch[i], state[hd, i]

        def combine(rows, half):
            l = st[("l", half)]
            ratio = lam * l[:, :h] / l[:, h:]
            base = half * t
            return s_ref[rows, base:base + h] - ratio * s_ref[rows, base + h:base + t]

        def one(rows, lo):
            halves = (1,) if lo else (0, 1)
            a = jnp.concatenate([combine(rows, half) for half in halves], axis=1).astype(BF16)
            pv = lax.dot_general(v_ref[hd, rows, :], a, tn, preferred_element_type=F32)
            for n, half in enumerate(halves):
                part = pv[:, n * h:(n + 1) * h]
                st["acc", half] = part if ("acc", half) not in st else st["acc", half] + part

        def finish():
            o = jnp.concatenate([st[("acc", half)] * (1.0 / st[("l", half)][:, :h]) for half in range(2)], axis=1)
            inv = lax.rsqrt(jnp.mean(o * o, axis=0, keepdims=True) + EPS)
            o_ref[hd, i * t:(i + 1) * t, :] = (o * inv * gain).T.astype(BF16)

        return [functools.partial(one, rows, lo) for rows, lo in chunks(i)] + [finish]

    order = [(hd, i) for hd in range(heads) for i in _tile_order(nq)]
    for step in range(len(order) + 2):
        streams = []
        if step < len(order):
            streams.append(score_ops(*order[step]))
        if 0 <= step - 1 < len(order):
            streams.append(prob_ops(*order[step - 1]))
        if 0 <= step - 2 < len(order):
            streams.append(value_ops(*order[step - 2]))
        for op in _interleave(*streams):
            op()


def _attn_call(q, k, v, lq1, lk1, lq2, lk2, g_subln_col, later_weights, t):
    b, _, s, _ = q.shape
    nq = s // t
    heads = ATTN_HEADS_PER_STEP
    steps = DA_HEADS // heads
    cast_in, cast_out, cast_shape = _slab_specs(later_weights, (b, steps))
    head = pl.BlockSpec((None, heads, s, DA_V_DIM), lambda bi, h: (bi, h, 0, 0))
    vec = _const_spec((1, DA_QK_DIM))
    return pl.pallas_call(
        functools.partial(_attn_kernel, t=t, nq=nq, heads=heads),
        grid=(b, steps),
        in_specs=[head, head, head, vec, vec, vec, vec, _const_spec((DA_V_DIM, 1)), *cast_in],
        out_specs=[head, *cast_out],
        out_shape=[jax.ShapeDtypeStruct((b, DA_HEADS, s, DA_V_DIM), BF16), *cast_shape],
        scratch_shapes=[pltpu.VMEM(((i + 1) * t, 2 * t), F32) for i in range(nq)],
        compiler_params=pltpu.CompilerParams(
            dimension_semantics=("arbitrary", "arbitrary"), vmem_limit_bytes=VMEM_LIMIT),
        name="diff_attn",
    )(q, k, v, lq1, lk1, lq2, lk2, g_subln_col, *later_weights)


def _mixout_kernel(da_ref, po_ref, x_ref, wo_ref, gpost_ref, gxpre_ref, wxq_ref, kv_ref, wxo_ref,
                   gxpost_ref, wg_in, wu_in, wd_in, o_ref, wg_out, wu_out, wd_out):
    _cast_slabs([(wg_in, wg_out), (wu_in, wu_out), (wd_in, wd_out)])

    tm = x_ref.shape[1]
    groups = [slice(r, r + ROW_GROUP) for r in range(0, tm, ROW_GROUP)]
    x_scale = X_HEAD_DIM ** -0.5

    def phases(rows):
        st = {}

        def mix():
            da = jnp.concatenate([da_ref[h, rows, :] for h in range(DA_HEADS)], axis=-1)
            st["mix"] = (jnp.dot(da, wo_ref[0:DA_WIDTH, :], preferred_element_type=F32)
                         + jnp.dot(po_ref[0, rows, :], wo_ref[DA_WIDTH:, :], preferred_element_type=F32))

        def norms():
            st["x1"] = x_ref[0, rows, :] + _rms(st.pop("mix"), gpost_ref[...])
            st["hq"], st["inv"] = _prenorm_operand(st["x1"], gxpre_ref[...])

        def queries():
            st["xq"] = (jnp.dot(st.pop("hq"), wxq_ref[...], preferred_element_type=F32)
                        * (st.pop("inv") * x_scale)).astype(BF16)

        def attend():
            heads = []
            for h in range(X_HEADS):
                sl = slice(h * X_HEAD_DIM, (h + 1) * X_HEAD_DIM)
                kh = kv_ref[0, :, sl]
                vh = kv_ref[0, :, D_MODEL + h * X_HEAD_DIM:D_MODEL + (h + 1) * X_HEAD_DIM]
                sc = lax.dot_general(st["xq"][:, sl], kh, (((1,), (1,)), ((), ())), preferred_element_type=F32)
                p = jnp.exp(sc - jnp.max(sc, axis=-1, keepdims=True))
                pm = (p * (1.0 / jnp.sum(p, axis=-1, keepdims=True))).astype(BF16)
                heads.append(jnp.dot(pm, vh, preferred_element_type=F32).astype(BF16))
            st["xo"] = jnp.concatenate(heads, axis=-1)

        def project():
            st["y"] = jnp.dot(st.pop("xo"), wxo_ref[...], preferred_element_type=F32)

        def finish():
            o_ref[0, rows, :] = st["x1"] + _rms(st["y"], gxpost_ref[...])

        return [mix, norms, queries, attend, project, finish]

    _run_staggered([phases(rows) for rows in groups])


def _mixout_call(da, po, x, w_out, g_mix_post, g_x_pre, w_xq, kv, w_xo, g_x_post, ffn_weights, tm):
    b, s, d = x.shape
    grid = (b, s // tm)
    cast_in, cast_out, cast_shape = _slab_specs(ffn_weights, grid)
    tok = lambda n: pl.BlockSpec((1, tm, n), lambda bi, i: (bi, i, 0))
    gvec = _const_spec((1, d))
    return pl.pallas_call(
        _mixout_kernel,
        grid=grid,
        in_specs=[pl.BlockSpec((None, DA_HEADS, tm, DA_V_DIM), lambda bi, i: (bi, 0, i, 0)),
                  tok(POOL_WIDTH), tok(d), _const_spec(w_out.shape), gvec, gvec, _const_spec(w_xq.shape),
                  pl.BlockSpec((1,) + kv.shape[1:], lambda bi, i: (bi, 0, 0)),
                  _const_spec(w_xo.shape), gvec, *cast_in],
        out_specs=[tok(d), *cast_out],
        out_shape=[jax.ShapeDtypeStruct((b, s, d), F32), *cast_shape],
        compiler_params=pltpu.CompilerParams(
            dimension_semantics=("arbitrary", "arbitrary"), vmem_limit_bytes=VMEM_LIMIT),
        name="mixout_xattn",
    )(da, po, x, w_out, g_mix_post, g_x_pre, w_xq, kv, w_xo, g_x_post, *ffn_weights)


def _ffn_kernel(x_ref, gpre_ref, wg_ref, wu_ref, wd_ref, gpost_ref, o_ref):
    tm = x_ref.shape[1]

    def phases(rows):
        st = {}

        def prenorm():
            st["x"] = x_ref[0, rows, :]
            st["hf"], st["inv"] = _prenorm_operand(st["x"], gpre_ref[...])

        def chunk(lo, hi):
            gate = jnp.dot(st["hf"], wg_ref[:, lo:hi], preferred_element_type=F32) * st["inv"]
            up = jnp.dot(st["hf"], wu_ref[:, lo:hi], preferred_element_type=F32) * st["inv"]
            act = (gate * (1.0 / (1.0 + jnp.exp(-gate))) * up).astype(BF16)
            part = jnp.dot(act, wd_ref[lo:hi, :], preferred_element_type=F32)
            st["ff"] = part if "ff" not in st else st["ff"] + part

        def finish():
            o_ref[0, rows, :] = st["x"] + _rms(st["ff"], gpost_ref[...])

        return [prenorm] + [functools.partial(chunk, lo, hi) for lo, hi in FF_CHUNKS] + [finish]

    _run_staggered([phases(slice(r, r + ROW_GROUP)) for r in range(0, tm, ROW_GROUP)])


def _ffn_call(x, g_pre, w_gate, w_up, w_down, g_post, tm):
    b, s, d = x.shape
    tok = pl.BlockSpec((1, tm, d), lambda bi, i: (bi, i, 0))
    gvec = _const_spec((1, d))
    return pl.pallas_call(
        _ffn_kernel,
        grid=(b, s // tm),
        in_specs=[tok, gvec, _const_spec(w_gate.shape), _const_spec(w_up.shape),
                  _const_spec(w_down.shape), gvec],
        out_specs=tok,
        out_shape=jax.ShapeDtypeStruct((b, s, d), F32),
        compiler_params=pltpu.CompilerParams(
            dimension_semantics=("arbitrary", "arbitrary"), vmem_limit_bytes=VMEM_LIMIT),
        name="swiglu",
    )(x, g_pre, w_gate, w_up, w_down, g_post)


def _rope_lane_constants():
    inv_freq = ROPE_THETA ** (-jnp.arange(0, ROPE_DIM, 2, dtype=F32) / ROPE_DIM)
    packed_freq = jnp.tile(inv_freq, LANES // ROPE_HALF)
    one_half = jnp.ones((ROPE_HALF,), F32)
    per_map = jnp.concatenate([-one_half, one_half, jnp.zeros((DA_QK_DIM - ROPE_DIM,), F32)])
    sign = jnp.concatenate([per_map, per_map])
    return jnp.concatenate([jnp.stack([packed_freq, sign]), jnp.zeros((6, LANES), F32)], axis=0)


def kernel(x, mem, positions, g_mix_pre, w_in, lambda_q1, lambda_k1, lambda_q2, lambda_k2,
           g_subln, w_pool, pool_scale, w_out, g_mix_post, g_x_pre, g_mem, w_xq, w_xkv, w_xo,
           g_x_post, g_ffn_pre, w_gate, w_up, w_down, g_ffn_post):
    b, s, d = x.shape
    assert d == D_MODEL and s % TOKEN_TILE == 0 and s % ATTN_TILE == 0
    row = lambda a: a.reshape(1, -1).astype(F32)
    bf = lambda a: a.astype(BF16)

    pos = positions.reshape(b, 1, s)
    q, k, v, po, w_xkv = _inproj_call(x, row(g_mix_pre), w_in, pos, _rope_lane_constants(), bf(w_pool),
                                      row(pool_scale), w_xkv, TOKEN_TILE)
    kv = _kv_call(mem, row(g_mem), w_xkv, TOKEN_TILE)
    da, w_out, w_xq, w_xo = _attn_call(q, k, v, row(lambda_q1), row(lambda_k1), row(lambda_q2),
                                       row(lambda_k2), g_subln.reshape(-1, 1).astype(F32),
                                       (w_out, w_xq, w_xo), ATTN_TILE)
    x2, *ffn_weights = _mixout_call(da, po, x, w_out, row(g_mix_post), row(g_x_pre), w_xq, kv, w_xo,
                                    row(g_x_post), (w_gate, w_up, w_down), TOKEN_TILE)
    return _ffn_call(x2, row(g_ffn_pre), *ffn_weights, row(g_ffn_post), TOKEN_TILE)
```

```python
import functools
import itertools
import math

import jax
import jax.numpy as jnp
from jax import lax
from jax.experimental import pallas as pl
from jax.experimental.pallas import tpu as pltpu

F32 = jnp.float32
BF16 = jnp.bfloat16

D_MODEL = 1024
MEM_LEN = 256
EPS = 1e-6
DA_HEADS = 4
DA_QK_DIM = 64
DA_V_DIM = 2 * DA_QK_DIM
DA_WIDTH = DA_HEADS * DA_V_DIM
QK_WIDTH = DA_HEADS * 2 * DA_QK_DIM
POOL_WINDOWS = (2, 4, 8, 16)
POOL_WIDTH = D_MODEL - DA_WIDTH
POOL_GROUP_DIM = POOL_WIDTH // len(POOL_WINDOWS)
ROPE_THETA = 500000.0
ROPE_DIM = DA_QK_DIM // 4
ROPE_HALF = ROPE_DIM // 2
X_HEADS = 4
X_HEAD_DIM = D_MODEL // X_HEADS
D_FF = -(-(8 * D_MODEL) // (3 * 256)) * 256
LAMBDA_INIT = 0.8 - 0.6 * math.exp(-0.3 * 0)
LOG2_E = math.log2(math.e)

LANES = 128
POOL_HALO = 16
VMEM_LIMIT = 56 * 1024 * 1024

TOKEN_TILE = 1024
ROW_GROUP = 256
ATTN_TILE = 512
ATTN_HEADS_PER_STEP = 2
FF_CHUNKS = ((0, 1536), (1536, 2816))
STAGE_BYTES = 4 * 1024 * 1024


def _inv_rms(x):
    return lax.rsqrt(jnp.mean(x * x, axis=-1, keepdims=True) + EPS)


def _rms(x, g):
    return x * _inv_rms(x) * g


def _prenorm_operand(x, g):
    return (x * g).astype(BF16), _inv_rms(x)


def _const_spec(shape):
    zeros = (0,) * len(shape)
    return pl.BlockSpec(shape, lambda *_: zeros, pipeline_mode=pl.Buffered(1))


_HBM = pl.BlockSpec(memory_space=pl.ANY)


def _stage_scratch(rows, cols):
    chunks = -(-(rows * cols * 4) // STAGE_BYTES)
    while rows % chunks or (rows // chunks) % 16:
        chunks += 1
    return [pltpu.VMEM((2, rows // chunks, cols), F32), pltpu.SemaphoreType.DMA((2,))]


def _load_weights(weights, stage_ref, sem_ref):
    rows = stage_ref.shape[1]
    jobs = []
    for w_hbm, w_vmem in weights:
        assert w_hbm.shape[0] % rows == 0 and stage_ref.shape[2] == w_hbm.shape[1]
        jobs += [(w_hbm, w_vmem, r) for r in range(0, w_hbm.shape[0], rows)]

    def copy(k):
        w_hbm, _, r = jobs[k]
        return pltpu.make_async_copy(w_hbm.at[pl.ds(r, rows), :], stage_ref.at[k % 2], sem_ref.at[k % 2])

    copy(0).start()
    for k, (_, w_vmem, r) in enumerate(jobs):
        if k + 1 < len(jobs):
            copy(k + 1).start()
        copy(k).wait()
        w_vmem[r:r + rows, :] = stage_ref[k % 2].astype(BF16)


def _cast_slabs(pairs):
    for w_in, w_out in pairs:
        w_out[...] = w_in[...].astype(BF16)


def _slab_specs(weights, grid):
    n_steps = grid[0] * grid[1]
    specs, shapes = [], []
    for w in weights:
        rows = w.shape[0] // n_steps
        assert rows * n_steps == w.shape[0] and rows % 16 == 0
        specs.append(pl.BlockSpec((rows, w.shape[1]), lambda bi, i: (bi * grid[1] + i, 0)))
        shapes.append(jax.ShapeDtypeStruct(w.shape, BF16))
    return specs, specs, shapes


def _run_staggered(chains):
    depth = len(chains[0])
    for time in range(depth + len(chains) - 1):
        for g, chain in enumerate(chains):
            if 0 <= time - g < depth:
                chain[time - g]()


def _first_step(grid_rank):
    first = pl.program_id(0) == 0
    for axis in range(1, grid_rank):
        first = first & (pl.program_id(axis) == 0)
    return first


def _kv_kernel(mem_ref, g_ref, w_ref, kv_ref):
    mn = _rms(mem_ref[...], g_ref[...]).astype(BF16)
    kv_ref[...] = jnp.dot(mn, w_ref[...], preferred_element_type=F32).astype(BF16)


def _kv_call(mem, g_mem, w_xkv, tm):
    b, m, d = mem.shape
    n = w_xkv.shape[1]
    rows = b * m
    tm = min(tm, rows)
    assert rows % tm == 0
    kv = pl.pallas_call(
        _kv_kernel,
        grid=(rows // tm,),
        in_specs=[pl.BlockSpec((tm, d), lambda i: (i, 0)), _const_spec((1, d)), _const_spec(w_xkv.shape)],
        out_specs=pl.BlockSpec((tm, n), lambda i: (i, 0)),
        out_shape=jax.ShapeDtypeStruct((rows, n), BF16),
        compiler_params=pltpu.CompilerParams(
            dimension_semantics=("arbitrary",), vmem_limit_bytes=VMEM_LIMIT),
        name="kv_proj",
    )(mem.reshape(rows, d), g_mem, w_xkv)
    return kv.reshape(b, m, n)


def _inproj_kernel(x_ref, g_ref, w_hbm, pos_ref, freq_ref, wp_ref, ps_ref, wkv_in,
                   q_ref, k_ref, v_ref, po_ref, wkv_out, halo_ref, w_ref, stage, sem, *, tm):
    @pl.when(_first_step(2))
    def _():
        _load_weights([(w_hbm, w_ref)], stage, sem)

    _cast_slabs([(wkv_in, wkv_out)])

    i = pl.program_id(1)
    h = _rms(x_ref[0], g_ref[...]).astype(BF16)

    groups = LANES // ROPE_DIM
    rows = tm // groups
    lane = lax.broadcasted_iota(jnp.int32, (rows, LANES), 1)
    lane_group = lax.shift_right_logical(lane, int(math.log2(ROPE_DIM)))
    pos = jnp.broadcast_to(pos_ref[0].astype(F32), (LANES, tm)).T
    packed = jnp.zeros((rows, LANES), F32)
    for g in range(groups):
        packed = jnp.where(lane_group == g, pos[g * rows:(g + 1) * rows], packed)
    ang = packed * freq_ref[0:1, :]
    cos8, sin8 = jnp.cos(ang), jnp.sin(ang)
    first_map = lane < DA_QK_DIM
    rotary = freq_ref[1:2, :] != 0.0

    def spread(packed_table, g):
        shift = (LANES - ROPE_DIM * g) % LANES
        lo = pltpu.roll(packed_table, shift, 1) if shift else packed_table
        hi_shift = (shift + DA_QK_DIM) % LANES
        hi = pltpu.roll(packed_table, hi_shift, 1) if hi_shift else packed_table
        return jnp.where(first_map, lo, hi)

    cos_t = jnp.concatenate([jnp.where(rotary, spread(cos8, g), 1.0) for g in range(groups)], axis=0)
    sin_t = jnp.concatenate([spread(sin8, g) * freq_ref[1:2, :] for g in range(groups)], axis=0)
    first_half = freq_ref[1:2, :] < 0.0

    def rope(t, c, s):
        partner = jnp.where(first_half, pltpu.roll(t, LANES - ROPE_HALF, 1), pltpu.roll(t, ROPE_HALF, 1))
        return t * c + partner * s

    qk_scale = DA_QK_DIM ** -0.5 * LOG2_E
    cos_q, sin_q = cos_t * qk_scale, sin_t * qk_scale
    pq = jnp.dot(h, w_ref[:, 0:QK_WIDTH], preferred_element_type=F32)
    for j in range(DA_HEADS):
        sl = slice(j * LANES, (j + 1) * LANES)
        q_ref[j] = rope(pq[:, sl], cos_q, sin_q).astype(BF16)
    pk = jnp.dot(h, w_ref[:, QK_WIDTH:2 * QK_WIDTH], preferred_element_type=F32)
    for j in range(DA_HEADS):
        sl = slice(j * LANES, (j + 1) * LANES)
        k_ref[j] = rope(pk[:, sl], cos_t, sin_t).astype(BF16)
    pv = jnp.dot(h, w_ref[:, 2 * QK_WIDTH:2 * QK_WIDTH + DA_WIDTH], preferred_element_type=F32)
    for j in range(DA_HEADS):
        v_ref[j] = pv[:, j * DA_V_DIM:(j + 1) * DA_V_DIM].astype(BF16)
    u = jnp.dot(h, w_ref[:, 2 * QK_WIDTH + DA_WIDTH:], preferred_element_type=F32)

    @pl.when(i == 0)
    def _():
        halo_ref[...] = jnp.zeros_like(halo_ref)

    prev = halo_ref[...]
    halo_ref[...] = u[tm - POOL_HALO:, :]
    tpos = i * tm + lax.broadcasted_iota(jnp.int32, (tm, 1), 0)
    for g, w in enumerate(POOL_WINDOWS):
        sl = slice(g * POOL_GROUP_DIM, (g + 1) * POOL_GROUP_DIM)
        ug = u[:, sl]
        s = jnp.concatenate([prev[:, sl], ug], axis=0)
        d = 1
        while d < w:
            s = s + pltpu.roll(s, d, 0)
            d *= 2
        inv_count = 1.0 / jnp.minimum(tpos + 1, w).astype(F32)
        pooled = s[POOL_HALO:, :] * inv_count - ug
        po = jnp.dot(pooled.astype(BF16), wp_ref[g], preferred_element_type=F32) * ps_ref[:, sl]
        po_ref[0, :, sl] = po.astype(BF16)


def _inproj_call(x, g, w_in, pos, freq, w_pool, pool_scale, w_xkv, tm):
    b, s, d = x.shape
    grid = (b, s // tm)
    cast_in, cast_out, cast_shape = _slab_specs([w_xkv], grid)
    tok = lambda n: pl.BlockSpec((1, tm, n), lambda bi, i: (bi, i, 0))
    heads = pl.BlockSpec((None, DA_HEADS, tm, DA_V_DIM), lambda bi, i: (bi, 0, i, 0))
    head_major = jax.ShapeDtypeStruct((b, DA_HEADS, s, DA_V_DIM), BF16)
    return pl.pallas_call(
        functools.partial(_inproj_kernel, tm=tm),
        grid=grid,
        in_specs=[tok(d), _const_spec((1, d)), _HBM,
                  pl.BlockSpec((1, 1, tm), lambda bi, i: (bi, 0, i)), _const_spec(freq.shape),
                  _const_spec(w_pool.shape), _const_spec((1, POOL_WIDTH)), *cast_in],
        out_specs=[heads, heads, heads, tok(POOL_WIDTH), *cast_out],
        out_shape=[head_major, head_major, head_major,
                   jax.ShapeDtypeStruct((b, s, POOL_WIDTH), BF16), *cast_shape],
        scratch_shapes=[pltpu.VMEM((POOL_HALO, POOL_WIDTH), F32), pltpu.VMEM(w_in.shape, BF16),
                        *_stage_scratch(d, w_in.shape[1])],
        compiler_params=pltpu.CompilerParams(
            dimension_semantics=("arbitrary", "arbitrary"), vmem_limit_bytes=VMEM_LIMIT),
        name="inproj",
    )(x, g, w_in, pos, freq, w_pool, pool_scale, w_xkv)


def _interleave(*streams):
    tagged = [((k + 0.5) / len(ops), n, op) for n, ops in enumerate(streams) for k, op in enumerate(ops)]
    return [op for _, _, op in sorted(tagged, key=lambda e: e[:2])]


def _tile_order(nq):
    def length(order):
        total = 0.0
        for step in range(nq + 2):
            scores = order[step] + 1 if step < nq else 0
            probs = order[step - 1] + 1 if 0 <= step - 1 < nq else 0
            values = (order[step - 2] + 1) / 2 if 0 <= step - 2 < nq else 0
            total += max(scores + values, probs)
        return total

    return min(itertools.permutations(range(nq)), key=length)


def _attn_kernel(q_ref, k_ref, v_ref, lq1_ref, lk1_ref, lq2_ref, lk2_ref, gs_ref, wo_in, wxq_in, wxo_in,
                 o_ref, wo_out, wxq_out, wxo_out, *scratch, t, nq, heads):
    _cast_slabs([(wo_in, wo_out), (wxq_in, wxq_out), (wxo_in, wxo_out)])

    h = t // 2
    lane = lax.broadcasted_iota(jnp.int32, (h, DA_V_DIM), 1)
    causal = (lax.broadcasted_iota(jnp.int32, (h, h), 0) <= lax.broadcasted_iota(jnp.int32, (h, h), 1))
    causal = jnp.concatenate([causal, causal], axis=1)
    lam = (jnp.exp(jnp.sum(lq1_ref[...] * lk1_ref[...], axis=-1, keepdims=True))
           - jnp.exp(jnp.sum(lq2_ref[...] * lk2_ref[...], axis=-1, keepdims=True))
           + LAMBDA_INIT)
    gain = gs_ref[...] * (1.0 - LAMBDA_INIT)
    nt = (((1,), (1,)), ((), ()))
    tn = (((0,), (0,)), ((), ()))
    state = {}

    def chunks(i):
        n = 2 * (i + 1)
        return [(slice(c * h, (c + 1) * h), t if c == n - 1 else 0) for c in range(n)]

    def both(st, name, lo, value, combine):
        for half, part in ((0, value[:, :t]), (1, value[:, t:])) if lo == 0 else ((1, value),):
            key = (name, half)
            st[key] = part if key not in st else combine(st[key], part)

    def score_ops(hd, i):
        s_ref, st = scratch[i], state.setdefault((hd, i), {})

        def start():
            stacked = []
            for half in range(2):
                q = q_ref[hd, i * t + half * h:i * t + (half + 1) * h, :]
                zero = jnp.zeros_like(q)
                stacked += [jnp.where(lane < DA_QK_DIM, q, zero), jnp.where(lane >= DA_QK_DIM, q, zero)]
            st["qs"] = jnp.concatenate(stacked, axis=0)

        def one(c, rows, lo):
            s = lax.dot_general(k_ref[hd, rows, :], st["qs"][lo:, :], nt, preferred_element_type=F32)
            first_diag = 2 * i
            if c == first_diag:
                s = jnp.concatenate([jnp.where(causal, s[:, :t], -jnp.inf), s[:, t:]], axis=1)
            elif c == first_diag + 1:
                s = jnp.where(causal, s, -jnp.inf)
            s_ref[rows, lo:] = s
            both(st, "m", lo, jnp.max(s, axis=0, keepdims=True), jnp.maximum)

        return [start] + [functools.partial(one, c, rows, lo) for c, (rows, lo) in enumerate(chunks(i))]

    def prob_ops(hd, i):
        s_ref, st = scratch[i], state[hd, i]

        def one(rows, lo):
            m = st[("m", 1)] if lo else jnp.concatenate([st[("m", 0)], st[("m", 1)]], axis=1)
            p = jnp.exp2(s_ref[rows, lo:] - m)
            s_ref[rows, lo:] = p
            both(st, "l", lo, jnp.sum(p, axis=0, keepdims=True), jnp.add)

        return [functools.partial(one, rows, lo) for rows, lo in chunks(i)]

    def value_ops(hd, i):
        s_ref, st = scratch[i], state[hd, i]

        def combine(rows, half):
            l = st[("l", half)]
            ratio = lam * l[:, :h] / l[:, h:]
            base = half * t
            return s_ref[rows, base:base + h] - ratio * s_ref[rows, base + h:base + t]

        def one(rows, lo):
            halves = (1,) if lo else (0, 1)
            a = jnp.concatenate([combine(rows, half) for half in halves], axis=1).astype(BF16)
            pv = lax.dot_general(v_ref[hd, rows, :], a, tn, preferred_element_type=F32)
            for n, half in enumerate(halves):
                part = pv[:, n * h:(n + 1) * h]
                st["acc", half] = part if ("acc", half) not in st else st["acc", half] + part

        def finish():
            o = jnp.concatenate([st[("acc", half)] * (1.0 / st[("l", half)][:, :h]) for half in range(2)], axis=1)
            inv = lax.rsqrt(jnp.mean(o * o, axis=0, keepdims=True) + EPS)
            o_ref[hd, i * t:(i + 1) * t, :] = (o * inv * gain).T.astype(BF16)

        return [functools.partial(one, rows, lo) for rows, lo in chunks(i)] + [finish]

    order = [(hd, i) for hd in range(heads) for i in _tile_order(nq)]
    for step in range(len(order) + 2):
        streams = []
        if step < len(order):
            streams.append(score_ops(*order[step]))
        if 0 <= step - 1 < len(order):
            streams.append(prob_ops(*order[step - 1]))
        if 0 <= step - 2 < len(order):
            streams.append(value_ops(*order[step - 2]))
        for op in _interleave(*streams):
            op()


def _attn_call(q, k, v, lq1, lk1, lq2, lk2, g_subln_col, later_weights, t):
    b, _, s, _ = q.shape
    nq = s // t
    heads = ATTN_HEADS_PER_STEP
    steps = DA_HEADS // heads
    cast_in, cast_out, cast_shape = _slab_specs(later_weights, (b, steps))
    head = pl.BlockSpec((None, heads, s, DA_V_DIM), lambda bi, h: (bi, h, 0, 0))
    vec = _const_spec((1, DA_QK_DIM))
    return pl.pallas_call(
        functools.partial(_attn_kernel, t=t, nq=nq, heads=heads),
        grid=(b, steps),
        in_specs=[head, head, head, vec, vec, vec, vec, _const_spec((DA_V_DIM, 1)), *cast_in],
        out_specs=[head, *cast_out],
        out_shape=[jax.ShapeDtypeStruct((b, DA_HEADS, s, DA_V_DIM), BF16), *cast_shape],
        scratch_shapes=[pltpu.VMEM(((i + 1) * t, 2 * t), F32) for i in range(nq)],
        compiler_params=pltpu.CompilerParams(
            dimension_semantics=("arbitrary", "arbitrary"), vmem_limit_bytes=VMEM_LIMIT),
        name="diff_attn",
    )(q, k, v, lq1, lk1, lq2, lk2, g_subln_col, *later_weights)


def _mixout_kernel(da_ref, po_ref, x_ref, wo_ref, gpost_ref, gxpre_ref, wxq_ref, kv_ref, wxo_ref,
                   gxpost_ref, wg_in, wu_in, wd_in, o_ref, wg_out, wu_out, wd_out):
    _cast_slabs([(wg_in, wg_out), (wu_in, wu_out), (wd_in, wd_out)])

    tm = x_ref.shape[1]
    groups = [slice(r, r + ROW_GROUP) for r in range(0, tm, ROW_GROUP)]
    x_scale = X_HEAD_DIM ** -0.5

    def phases(rows):
        st = {}

        def mix():
            da = jnp.concatenate([da_ref[h, rows, :] for h in range(DA_HEADS)], axis=-1)
            st["mix"] = (jnp.dot(da, wo_ref[0:DA_WIDTH, :], preferred_element_type=F32)
                         + jnp.dot(po_ref[0, rows, :], wo_ref[DA_WIDTH:, :], preferred_element_type=F32))

        def norms():
            st["x1"] = x_ref[0, rows, :] + _rms(st.pop("mix"), gpost_ref[...])
            st["hq"], st["inv"] = _prenorm_operand(st["x1"], gxpre_ref[...])

        def queries():
            st["xq"] = (jnp.dot(st.pop("hq"), wxq_ref[...], preferred_element_type=F32)
                        * (st.pop("inv") * x_scale)).astype(BF16)

        def attend():
            heads = []
            for h in range(X_HEADS):
                sl = slice(h * X_HEAD_DIM, (h + 1) * X_HEAD_DIM)
                kh = kv_ref[0, :, sl]
                vh = kv_ref[0, :, D_MODEL + h * X_HEAD_DIM:D_MODEL + (h + 1) * X_HEAD_DIM]
                sc = lax.dot_general(st["xq"][:, sl], kh, (((1,), (1,)), ((), ())), preferred_element_type=F32)
                p = jnp.exp(sc - jnp.max(sc, axis=-1, keepdims=True))
                pm = (p * (1.0 / jnp.sum(p, axis=-1, keepdims=True))).astype(BF16)
                heads.append(jnp.dot(pm, vh, preferred_element_type=F32).astype(BF16))
            st["xo"] = jnp.concatenate(heads, axis=-1)

        def project():
            st["y"] = jnp.dot(st.pop("xo"), wxo_ref[...], preferred_element_type=F32)

        def finish():
            o_ref[0, rows, :] = st["x1"] + _rms(st["y"], gxpost_ref[...])

        return [mix, norms, queries, attend, project, finish]

    _run_staggered([phases(rows) for rows in groups])


def _mixout_call(da, po, x, w_out, g_mix_post, g_x_pre, w_xq, kv, w_xo, g_x_post, ffn_weights, tm):
    b, s, d = x.shape
    grid = (b, s // tm)
    cast_in, cast_out, cast_shape = _slab_specs(ffn_weights, grid)
    tok = lambda n: pl.BlockSpec((1, tm, n), lambda bi, i: (bi, i, 0))
    gvec = _const_spec((1, d))
    return pl.pallas_call(
        _mixout_kernel,
        grid=grid,
        in_specs=[pl.BlockSpec((None, DA_HEADS, tm, DA_V_DIM), lambda bi, i: (bi, 0, i, 0)),
                  tok(POOL_WIDTH), tok(d), _const_spec(w_out.shape), gvec, gvec, _const_spec(w_xq.shape),
                  pl.BlockSpec((1,) + kv.shape[1:], lambda bi, i: (bi, 0, 0)),
                  _const_spec(w_xo.shape), gvec, *cast_in],
        out_specs=[tok(d), *cast_out],
        out_shape=[jax.ShapeDtypeStruct((b, s, d), F32), *cast_shape],
        compiler_params=pltpu.CompilerParams(
            dimension_semantics=("arbitrary", "arbitrary"), vmem_limit_bytes=VMEM_LIMIT),
        name="mixout_xattn",
    )(da, po, x, w_out, g_mix_post, g_x_pre, w_xq, kv, w_xo, g_x_post, *ffn_weights)


def _ffn_kernel(x_ref, gpre_ref, wg_ref, wu_ref, wd_ref, gpost_ref, o_ref):
    tm = x_ref.shape[1]

    def phases(rows):
        st = {}

        def prenorm():
            st["x"] = x_ref[0, rows, :]
            st["hf"], st["inv"] = _prenorm_operand(st["x"], gpre_ref[...])

        def chunk(lo, hi):
            gate = jnp.dot(st["hf"], wg_ref[:, lo:hi], preferred_element_type=F32) * st["inv"]
            up = jnp.dot(st["hf"], wu_ref[:, lo:hi], preferred_element_type=F32) * st["inv"]
            act = (gate * (1.0 / (1.0 + jnp.exp(-gate))) * up).astype(BF16)
            part = jnp.dot(act, wd_ref[lo:hi, :], preferred_element_type=F32)
            st["ff"] = part if "ff" not in st else st["ff"] + part

        def finish():
            o_ref[0, rows, :] = st["x"] + _rms(st["ff"], gpost_ref[...])

        return [prenorm] + [functools.partial(chunk, lo, hi) for lo, hi in FF_CHUNKS] + [finish]

    _run_staggered([phases(slice(r, r + ROW_GROUP)) for r in range(0, tm, ROW_GROUP)])


def _ffn_call(x, g_pre, w_gate, w_up, w_down, g_post, tm):
    b, s, d = x.shape
    tok = pl.BlockSpec((1, tm, d), lambda bi, i: (bi, i, 0))
    gvec = _const_spec((1, d))
    return pl.pallas_call(
        _ffn_kernel,
        grid=(b, s // tm),
        in_specs=[tok, gvec, _const_spec(w_gate.shape), _const_spec(w_up.shape),
                  _const_spec(w_down.shape), gvec],
        out_specs=tok,
        out_shape=jax.ShapeDtypeStruct((b, s, d), F32),
        compiler_params=pltpu.CompilerParams(
            dimension_semantics=("arbitrary", "arbitrary"), vmem_limit_bytes=VMEM_LIMIT),
        name="swiglu",
    )(x, g_pre, w_gate, w_up, w_down, g_post)


def _rope_lane_constants():
    inv_freq = ROPE_THETA ** (-jnp.arange(0, ROPE_DIM, 2, dtype=F32) / ROPE_DIM)
    packed_freq = jnp.tile(inv_freq, LANES // ROPE_HALF)
    one_half = jnp.ones((ROPE_HALF,), F32)
    per_map = jnp.concatenate([-one_half, one_half, jnp.zeros((DA_QK_DIM - ROPE_DIM,), F32)])
    sign = jnp.concatenate([per_map, per_map])
    return jnp.concatenate([jnp.stack([packed_freq, sign]), jnp.zeros((6, LANES), F32)], axis=0)


def kernel(x, mem, positions, g_mix_pre, w_in, lambda_q1, lambda_k1, lambda_q2, lambda_k2,
           g_subln, w_pool, pool_scale, w_out, g_mix_post, g_x_pre, g_mem, w_xq, w_xkv, w_xo,
           g_x_post, g_ffn_pre, w_gate, w_up, w_down, g_ffn_post):
    b, s, d = x.shape
    assert d == D_MODEL and s % TOKEN_TILE == 0 and s % ATTN_TILE == 0
    row = lambda a: a.reshape(1, -1).astype(F32)
    bf = lambda a: a.astype(BF16)

    pos = positions.reshape(b, 1, s)
    q, k, v, po, w_xkv = _inproj_call(x, row(g_mix_pre), w_in, pos, _rope_lane_constants(), bf(w_pool),
                                      row(pool_scale), w_xkv, TOKEN_TILE)
    kv = _kv_call(mem, row(g_mem), w_xkv, TOKEN_TILE // 2)
    da, w_out, w_xq, w_xo = _attn_call(q, k, v, row(lambda_q1), row(lambda_k1), row(lambda_q2),
                                       row(lambda_k2), g_subln.reshape(-1, 1).astype(F32),
                                       (w_out, w_xq, w_xo), ATTN_TILE)
    x2, *ffn_weights = _mixout_call(da, po, x, w_out, row(g_mix_post), row(g_x_pre), w_xq, kv, w_xo,
                                    row(g_x_post), (w_gate, w_up, w_down), TOKEN_TILE)
    return _ffn_call(x2, row(g_ffn_pre), *ffn_weights, row(g_ffn_post), TOKEN_TILE)
```
